```python
import math
import jax
import jax.numpy as jnp
from jax import lax
import numpy as np

D_MODEL = 2048
BATCH = 4
SEQ = 4096
DEPTH = 2

GRID_W = 64
CTX_LEN = 256
N_EVEN = (DEPTH + 1) // 2
N_ODD = DEPTH // 2
MOD_CHUNKS = 6
FFN_HIDDEN = 4 * D_MODEL
NORM_EPS = 1e-6
SCAN_CHUNK = 128

S5_WIDTH = D_MODEL // 2
S5_P = 16
S5_G = S5_WIDTH // S5_P
S5_N = 64
DT_MIN = 0.001
DT_MAX = 0.1
LAMBDA_RE_MAX = -1e-4
RET_DK = 256
RET_H = (D_MODEL // 2) // RET_DK
RET_DV = (D_MODEL // 2) // RET_H
RET_DECAY_BASE = -5.0
EVEN_IN = S5_WIDTH + 2 * RET_H * RET_DK + 2 * RET_H * RET_DV
EVEN_CAT = S5_WIDTH + RET_H * RET_DV

ATT_HD = 64
ATT_H = D_MODEL // ATT_HD
ATT_KVH = ATT_H // 8
ODD_IN = (ATT_H + 2 * ATT_KVH) * ATT_HD
WINDOW = 128
ATT_BLOCK = 128
ROPE_BASE = 10000.0
NEG_INF = -1e30

kernel_name = 'hybrid_s5_retention_swa_dit'


def rms_norm(x, g):
    xf = x.astype(jnp.float32)
    y = xf * lax.rsqrt(jnp.mean(xf * xf, axis=-1, keepdims=True) + NORM_EPS)
    return (y * g.astype(jnp.float32)).astype(x.dtype)


def head_rms(o):
    of = o.astype(jnp.float32)
    return of * lax.rsqrt(jnp.mean(of * of, axis=-1, keepdims=True) + NORM_EPS)


def ada_modulation(cvec, w, b):
    m = jax.nn.silu(cvec) @ w + b
    return jnp.split(m, MOD_CHUNKS, axis=-1)


def modulate(h, shift, scale):
    return h * (1.0 + scale[:, None, :]) + shift[:, None, :]


def gated_residual(x, out, g_post, gate):
    return x + gate[:, None, :] * rms_norm(out, g_post)


def sq_relu_mlp(h, w1, w2):
    a = jax.nn.relu(h @ w1)
    return (a * a) @ w2


def _flip(t):
    return jnp.flip(t, axis=1)


def _identity(t):
    return t


def _linear_recurrence_combine(left, right):
    a1r, a1i, b1r, b1i = left
    a2r, a2i, b2r, b2i = right
    ar = a2r * a1r - a2i * a1i
    ai = a2r * a1i + a2i * a1r
    br = a2r * b1r - a2i * b1i + b2r
    bi = a2r * b1i + a2i * b1r + b2i
    return ar, ai, br, bi


def s5_scan(u, h0, lam_re, lam_im, log_dt, b_re, b_im, c_re, c_im):
    f32 = jnp.float32
    bsz, n, g, p = u.shape
    t = SCAN_CHUNK
    nc = n // t
    lr = jnp.minimum(lam_re.astype(f32), LAMBDA_RE_MAX)
    li = lam_im.astype(f32)
    dt = jnp.exp(log_dt.astype(f32))[:, None]
    zr, zi = lr * dt, li * dt
    ab_mag = jnp.exp(zr)
    ab_re, ab_im = ab_mag * jnp.cos(zi), ab_mag * jnp.sin(zi)
    den = lr * lr + li * li
    nr = ab_re - 1.0
    f_re = (nr * lr + ab_im * li) / den
    f_im = (ab_im * lr - nr * li) / den
    br, bi = b_re.astype(f32), b_im.astype(f32)
    bb_re = f_re[..., None] * br - f_im[..., None] * bi
    bb_im = f_re[..., None] * bi + f_im[..., None] * br
    steps = jnp.arange(1, t + 1, dtype=f32)[:, None, None]
    pw_mag = jnp.exp(steps * zr)
    pw_re, pw_im = pw_mag * jnp.cos(steps * zi), pw_mag * jnp.sin(steps * zi)
    a_re = jnp.broadcast_to(ab_re, (bsz, t) + ab_re.shape)
    a_im = jnp.broadcast_to(ab_im, (bsz, t) + ab_im.shape)
    cr, ci = c_re.astype(f32), c_im.astype(f32)

    def step(h, u_c):
        hr, hi = h
        bu_re = jnp.einsum('btgp,gnp->btgn', u_c, bb_re)
        bu_im = jnp.einsum('btgp,gnp->btgn', u_c, bb_im)
        _, _, sr, si = lax.associative_scan(_linear_recurrence_combine, (a_re, a_im, bu_re, bu_im), axis=1)
        st_re = sr + pw_re * hr[:, None] - pw_im * hi[:, None]
        st_im = si + pw_re * hi[:, None] + pw_im * hr[:, None]
        y = jnp.einsum('btgn,gpn->btgp', st_re, cr) - jnp.einsum('btgn,gpn->btgp', st_im, ci)
        return (st_re[:, -1], st_im[:, -1]), y

    u_chunks = u.astype(f32).reshape(bsz, nc, t, g, p).swapaxes(0, 1)
    h_last, y = lax.scan(step, h0, u_chunks)
    return h_last, y.swapaxes(0, 1).reshape(bsz, n, g, p)


def s5_bidirectional(u_ctx, u_lat, lam_re, lam_im, log_dt, b_re, b_im, c_re, c_im):
    bsz = u_lat.shape[0]
    ys_c, ys_l = [], []
    for direction in range(2):
        rev = _flip if direction == 1 else _identity
        prm = (lam_re[direction], lam_im[direction], log_dt[direction],
               b_re[direction], b_im[direction], c_re[direction], c_im[direction])
        h0 = (jnp.zeros((bsz, S5_G, S5_N), jnp.float32), jnp.zeros((bsz, S5_G, S5_N), jnp.float32))
        h_ctx, yc = s5_scan(rev(u_ctx), h0, *prm)
        _, yl = s5_scan(rev(u_lat), h_ctx, *prm)
        ys_c.append(rev(yc))
        ys_l.append(rev(yl))
    return ys_c[0] + ys_c[1], ys_l[0] + ys_l[1]


def retention_log_decay(direction):
    e = RET_DECAY_BASE - (2.0 * jnp.arange(RET_H, dtype=jnp.float32) + direction)
    return jnp.log1p(-jnp.exp2(e))


def retention_scan(q, k, v, r0, log_g):
    f32 = jnp.float32
    bsz, n = q.shape[:2]
    t = SCAN_CHUNK
    nc = n // t
    pos = jnp.arange(t, dtype=f32)
    diff = pos[:, None] - pos[None, :]
    lower = diff >= 0
    decay_in = jnp.where(lower[None], jnp.exp(jnp.maximum(diff, 0.0)[None] * log_g[:, None, None]), 0.0)
    q_dec = jnp.exp((pos[:, None] + 1.0) * log_g[None, :])
    k_dec = jnp.exp((t - 1.0 - pos)[:, None] * log_g[None, :])
    chunk_dec = jnp.exp(t * log_g)

    def to_chunks(a):
        return a.astype(f32).reshape((bsz, nc, t) + a.shape[2:]).swapaxes(0, 1)

    def step(r, xs):
        qc, kc, vc = xs
        s = jnp.einsum('bnhd,bmhd->bhnm', qc, kc) * decay_in[None]
        inner = jnp.einsum('bhnm,bmhe->bnhe', s, vc)
        cross = jnp.einsum('bnhd,bhde->bnhe', qc, r) * q_dec[None, :, :, None]
        r_new = r * chunk_dec[None, :, None, None] + jnp.einsum('bmhd,bmhe->bhde', kc * k_dec[None, :, :, None], vc)
        return r_new, inner + cross

    r_last, o = lax.scan(step, r0, (to_chunks(q), to_chunks(k), to_chunks(v)))
    return r_last, o.swapaxes(0, 1).reshape(bsz, n, RET_H, RET_DV)


def retention_bidirectional(q_c, k_c, v_c, q_l, k_l, v_l):
    bsz = q_l.shape[0]
    os_c, os_l = [], []
    for direction in range(2):
        rev = _flip if direction == 1 else _identity
        log_g = retention_log_decay(direction)
        r0 = jnp.zeros((bsz, RET_H, RET_DK, RET_DV), jnp.float32)
        r_ctx, oc = retention_scan(rev(q_c), rev(k_c), rev(v_c), r0, log_g)
        _, ol = retention_scan(rev(q_l), rev(k_l), rev(v_l), r_ctx, log_g)
        os_c.append(rev(oc))
        os_l.append(rev(ol))
    return os_c[0] + os_c[1], os_l[0] + os_l[1]


def s5_retention_mixer(h_ctx, h_lat, w_in, w_out, lam_re, lam_im, log_dt, b_re, b_im,
                       c_re, c_im, d_skip, glu_w, glu_b, need_ctx):
    qk_w = RET_H * RET_DK
    v_w = RET_H * RET_DV
    cuts = [S5_WIDTH, S5_WIDTH + qk_w, S5_WIDTH + 2 * qk_w, S5_WIDTH + 2 * qk_w + v_w]

    def project(h):
        bsz, n, _ = h.shape
        u, q, k, v, g = jnp.split(h @ w_in, cuts, axis=-1)
        return (u, u.reshape(bsz, n, S5_G, S5_P),
                q.reshape(bsz, n, RET_H, RET_DK),
                k.reshape(bsz, n, RET_H, RET_DK) * (RET_DK ** -0.5),
                v.reshape(bsz, n, RET_H, RET_DV), g)

    u_c, u4_c, q_c, k_c, v_c, g_c = project(h_ctx)
    u_l, u4_l, q_l, k_l, v_l, g_l = project(h_lat)
    ys_c, ys_l = s5_bidirectional(u4_c, u4_l, lam_re, lam_im, log_dt, b_re, b_im, c_re, c_im)
    r_c, r_l = retention_bidirectional(q_c, k_c, v_c, q_l, k_l, v_l)

    def merge(y_s5, u, r, g):
        bsz, n = u.shape[:2]
        y = jax.nn.gelu(y_s5.reshape(bsz, n, S5_WIDTH).astype(u.dtype) + d_skip * u)
        s5_out = y * jax.nn.sigmoid(y @ glu_w + glu_b)
        ret_out = head_rms(r).reshape(bsz, n, v_w).astype(g.dtype) * jax.nn.silu(g)
        return jnp.concatenate([s5_out, ret_out], axis=-1) @ w_out

    out_l = merge(ys_l, u_l, r_l, g_l)
    out_c = merge(ys_c, u_c, r_c, g_c) if need_ctx else None
    return out_c, out_l


def axial_rope_tables(n_tokens):
    f32 = jnp.float32
    rows = n_tokens // GRID_W
    row = jnp.repeat(jnp.arange(rows, dtype=f32), GRID_W)
    col = jnp.tile(jnp.arange(GRID_W, dtype=f32), rows)
    n_freq = ATT_HD // 4
    inv_freq = ROPE_BASE ** (-jnp.arange(n_freq, dtype=f32) / n_freq)
    ang = jnp.concatenate([row[:, None] * inv_freq[None], col[:, None] * inv_freq[None]], axis=-1)
    return jnp.cos(ang), jnp.sin(ang)


def apply_rope(x, cos, sin):
    half = ATT_HD // 2
    c = cos[None, :, None, :].astype(x.dtype)
    s = sin[None, :, None, :].astype(x.dtype)
    x1, x2 = x[..., :half], x[..., half:]
    return jnp.concatenate([x1 * c - x2 * s, x2 * c + x1 * s], axis=-1)


def sink_probs(sink_kg, parts):
    lead = parts[0].shape[:-1]
    s_sink = jnp.broadcast_to(sink_kg[None, :, :, None, None], lead + (1,))
    p = jax.nn.softmax(jnp.concatenate([s_sink] + list(parts), axis=-1), axis=-1)
    return p[..., 1:]


def window_attention(h_ctx, h_lat, w_in, w_out, sink, need_ctx):
    f32 = jnp.float32
    bsz, n, _ = h_lat.shape
    n_ctx = h_ctx.shape[1]
    grp = ATT_H // ATT_KVH
    qw = ATT_H * ATT_HD
    kvw = ATT_KVH * ATT_HD
    scale = ATT_HD ** -0.5
    p_l = h_lat @ w_in
    q_l = p_l[..., :qw].reshape(bsz, n, ATT_H, ATT_HD)
    k_l = p_l[..., qw:qw + kvw].reshape(bsz, n, ATT_KVH, ATT_HD)
    v_l = p_l[..., qw + kvw:].reshape(bsz, n, ATT_KVH, ATT_HD)
    cos, sin = axial_rope_tables(n)
    q_l = (apply_rope(q_l, cos, sin) * scale).reshape(bsz, n, ATT_KVH, grp, ATT_HD)
    k_l = apply_rope(k_l, cos, sin)
    p_c = h_ctx @ (w_in if need_ctx else w_in[:, qw:])
    k_c = p_c[..., -2 * kvw:-kvw].reshape(bsz, n_ctx, ATT_KVH, ATT_HD)
    v_c = p_c[..., -kvw:].reshape(bsz, n_ctx, ATT_KVH, ATT_HD)
    sink_kg = sink.astype(f32).reshape(ATT_KVH, grp)

    t = ATT_BLOCK
    band = t + 2 * WINDOW
    pad = ((0, 0), (WINDOW, WINDOW), (0, 0), (0, 0))
    k_pad = jnp.pad(k_l, pad)
    v_pad = jnp.pad(v_l, pad)
    offs_q = jnp.arange(t)
    offs_k = jnp.arange(band)
    in_window = jnp.abs(offs_k[None, :] - WINDOW - offs_q[:, None]) <= WINDOW

    def attend_block(bi):
        start = bi * t
        qb = lax.dynamic_slice_in_dim(q_l, start, t, axis=1)
        kb = lax.dynamic_slice_in_dim(k_pad, start, band, axis=1)
        vb = lax.dynamic_slice_in_dim(v_pad, start, band, axis=1)
        kpos = start - WINDOW + offs_k
        valid = in_window & ((kpos >= 0) & (kpos < n))[None, :]
        s_ctx = jnp.einsum('btkgd,bskd->bkgts', qb, k_c).astype(f32)
        s_loc = jnp.where(valid, jnp.einsum('btkgd,bskd->bkgts', qb, kb).astype(f32), NEG_INF)
        p = sink_probs(sink_kg, [s_ctx, s_loc]).astype(vb.dtype)
        return (jnp.einsum('bkgts,bskd->btkgd', p[..., :n_ctx], v_c)
                + jnp.einsum('bkgts,bskd->btkgd', p[..., n_ctx:], vb))

    o_l = lax.map(attend_block, jnp.arange(n // t))
    out_l = o_l.swapaxes(0, 1).reshape(bsz, n, qw) @ w_out
    out_c = None
    if need_ctx:
        q_c = (p_c[..., :qw] * scale).reshape(bsz, n_ctx, ATT_KVH, grp, ATT_HD)
        s = jnp.einsum('btkgd,bskd->bkgts', q_c, k_c).astype(f32)
        p = sink_probs(sink_kg, [s]).astype(v_c.dtype)
        out_c = jnp.einsum('bkgts,bskd->btkgd', p, v_c).reshape(bsz, n_ctx, qw) @ w_out
    return out_c, out_l


def setup_inputs(seed: int = 0) -> dict:
    key = jax.random.key(seed)
    ks = jax.random.split(key, 24)
    f32 = jnp.float32
    d = D_MODEL

    def nrm(k, shape, s):
        return jax.random.normal(k, shape, f32) * s

    n_idx = jnp.arange(S5_N, dtype=f32)
    s5_shape = (N_EVEN, 2, S5_G, S5_N)
    return {
        'x': nrm(ks[0], (BATCH, SEQ, d), 1.0),
        'c': nrm(ks[1], (BATCH, d), 1.0),
        'ctx': nrm(ks[2], (BATCH, CTX_LEN, d), 1.0),
        'c_ctx': nrm(ks[3], (d,), 1.0),
        'mod_w': nrm(ks[4], (DEPTH, d, MOD_CHUNKS * d), 0.5 * d ** -0.5),
        'mod_b': nrm(ks[5], (DEPTH, MOD_CHUNKS * d), 0.02),
        'norm_g': 1.0 + nrm(ks[6], (DEPTH, 4, d), 0.05),
        'mlp_w1': nrm(ks[7], (DEPTH, d, FFN_HIDDEN), d ** -0.5),
        'mlp_w2': nrm(ks[8], (DEPTH, FFN_HIDDEN, d), FFN_HIDDEN ** -0.5),
        'even_w_in': nrm(ks[9], (N_EVEN, d, EVEN_IN), d ** -0.5),
        'even_w_out': nrm(ks[10], (N_EVEN, EVEN_CAT, d), EVEN_CAT ** -0.5),
        's5_lam_re': -0.5 + nrm(ks[11], s5_shape, 0.01),
        's5_lam_im': math.pi * n_idx + nrm(ks[12], s5_shape, 0.01),
        's5_log_dt': jax.random.uniform(ks[13], (N_EVEN, 2, S5_G), f32, math.log(DT_MIN), math.log(DT_MAX)),
        's5_b_re': nrm(ks[14], (N_EVEN, 2, S5_G, S5_N, S5_P), (2 * S5_P) ** -0.5),
        's5_b_im': nrm(ks[15], (N_EVEN, 2, S5_G, S5_N, S5_P), (2 * S5_P) ** -0.5),
        's5_c_re': nrm(ks[16], (N_EVEN, 2, S5_G, S5_P, S5_N), S5_N ** -0.5),
        's5_c_im': nrm(ks[17], (N_EVEN, 2, S5_G, S5_P, S5_N), S5_N ** -0.5),
        's5_d': nrm(ks[18], (N_EVEN, S5_WIDTH), 1.0),
        's5_glu_w': nrm(ks[19], (N_EVEN, S5_WIDTH, S5_WIDTH), S5_WIDTH ** -0.5),
        's5_glu_b': nrm(ks[20], (N_EVEN, S5_WIDTH), 0.02),
        'odd_w_in': nrm(ks[21], (N_ODD, d, ODD_IN), d ** -0.5),
        'odd_w_out': nrm(ks[22], (N_ODD, ATT_H * ATT_HD, d), (ATT_H * ATT_HD) ** -0.5),
        'odd_sink': nrm(ks[23], (N_ODD, ATT_H), 1.0),
    }


def reference(x, c, ctx, c_ctx, mod_w, mod_b, norm_g, mlp_w1, mlp_w2, even_w_in, even_w_out,
              s5_lam_re, s5_lam_im, s5_log_dt, s5_b_re, s5_b_im, s5_c_re, s5_c_im, s5_d,
              s5_glu_w, s5_glu_b, odd_w_in, odd_w_out, odd_sink):
    xc = ctx
    c_ctx_b = c_ctx[None, :]
    for i in range(DEPTH):
        last = i == DEPTH - 1
        j = i // 2
        sh1, sc1, gt1, sh2, sc2, gt2 = ada_modulation(c, mod_w[i], mod_b[i])
        csh1, csc1, cgt1, csh2, csc2, cgt2 = ada_modulation(c_ctx_b, mod_w[i], mod_b[i])
        g_pre1, g_post1, g_pre2, g_post2 = norm_g[i, 0], norm_g[i, 1], norm_g[i, 2], norm_g[i, 3]

        h_l = modulate(rms_norm(x, g_pre1), sh1, sc1)
        h_c = modulate(rms_norm(xc, g_pre1), csh1, csc1)
        if i % 2 == 0:
            o_c, o_l = s5_retention_mixer(h_c, h_l, even_w_in[j], even_w_out[j],
                                          s5_lam_re[j], s5_lam_im[j], s5_log_dt[j], s5_b_re[j], s5_b_im[j],
                                          s5_c_re[j], s5_c_im[j], s5_d[j], s5_glu_w[j], s5_glu_b[j],
                                          need_ctx=not last)
        else:
            o_c, o_l = window_attention(h_c, h_l, odd_w_in[j], odd_w_out[j], odd_sink[j], need_ctx=not last)
        x = gated_residual(x, o_l, g_post1, gt1)

        m_l = sq_relu_mlp(modulate(rms_norm(x, g_pre2), sh2, sc2), mlp_w1[i], mlp_w2[i])
        x = gated_residual(x, m_l, g_post2, gt2)

        if not last:
            xc = gated_residual(xc, o_c, g_post1, cgt1)
            m_c = sq_relu_mlp(modulate(rms_norm(xc, g_pre2), csh2, csc2), mlp_w1[i], mlp_w2[i])
            xc = gated_residual(xc, m_c, g_post2, cgt2)
    return x
```

```python
import functools
import math

import jax
import jax.numpy as jnp
from jax import lax
from jax.experimental import pallas as pl
from jax.experimental.pallas import tpu as pltpu

D_MODEL = 2048
BATCH = 4
SEQ = 4096
DEPTH = 2
GRID_W = 64
CTX_LEN = 256
MOD_CHUNKS = 6
FFN_HIDDEN = 4 * D_MODEL
NORM_EPS = 1e-6

S5_WIDTH = D_MODEL // 2
S5_P = 16
S5_G = S5_WIDTH // S5_P
S5_N = 64
LAMBDA_RE_MAX = -1e-4
RET_DK = 256
RET_H = (D_MODEL // 2) // RET_DK
RET_DV = (D_MODEL // 2) // RET_H
RET_DECAY_BASE = -5.0
EVEN_IN = S5_WIDTH + 2 * RET_H * RET_DK + 2 * RET_H * RET_DV

ATT_HD = 64
ATT_H = D_MODEL // ATT_HD
ATT_KVH = ATT_H // 8
ATT_GRP = ATT_H // ATT_KVH
ODD_IN = (ATT_H + 2 * ATT_KVH) * ATT_HD
WINDOW = 128
ATT_BLOCK = 128
ROPE_BASE = 10000.0
NEG_INF = -1e30

R_LAT = BATCH * SEQ
R_CTX = BATCH * CTX_LEN
R_ALL = R_LAT + R_CTX
MOD_ROWS = 8
CTX_MOD_ROW = BATCH

S5_T = 16
S5_TP = S5_T * S5_P
S5_NC = (CTX_LEN + SEQ) // S5_T
S5_NC_CTX = CTX_LEN // S5_T
S5_BPAD = 8
S5_GB = 4
RET_T = 256
RET_STEPS = (CTX_LEN + SEQ) // RET_T

VMEM_LIMIT = 56 * 1024 * 1024

f32 = jnp.float32
bf16 = jnp.bfloat16


def _cparams(*sem):
    return pltpu.CompilerParams(dimension_semantics=sem, vmem_limit_bytes=VMEM_LIMIT)


def _rms(xf, g):
    ms = jnp.mean(xf * xf, axis=-1, keepdims=True)
    return xf * lax.rsqrt(ms + NORM_EPS) * g


def _mod_row(i, tm, n_rows):
    n_lat_tiles = R_LAT // tm
    if n_rows == R_LAT:
        return i // (SEQ // tm)
    return jnp.where(i < n_lat_tiles, i // (SEQ // tm), CTX_MOD_ROW)


def _mod_spec(chunk, tm, n_rows):
    return pl.BlockSpec((None, 1, D_MODEL), lambda i, *_: (_mod_row(i, tm, n_rows) * MOD_CHUNKS + chunk, 0, 0))


def _gain_spec(k):
    return pl.BlockSpec((None, 1, D_MODEL), lambda *_: (k, 0, 0))


def _mod_kernel(c_ref, w_ref, b_ref, o_ref):
    c = c_ref[...]
    s = c * jax.nn.sigmoid(c)
    o_ref[...] = jnp.dot(s.astype(bf16), w_ref[...].astype(bf16), preferred_element_type=f32) + b_ref[...]


def _modulation(c_rows, mod_w, mod_b):
    tn = 1024
    n = MOD_CHUNKS * D_MODEL
    return pl.pallas_call(
        _mod_kernel,
        grid=(DEPTH, n // tn),
        in_specs=[pl.BlockSpec((MOD_ROWS, D_MODEL), lambda l, j: (0, 0)),
                  pl.BlockSpec((None, D_MODEL, tn), lambda l, j: (l, 0, j)),
                  pl.BlockSpec((None, 1, tn), lambda l, j: (l, 0, j))],
        out_specs=pl.BlockSpec((None, MOD_ROWS, tn), lambda l, j: (l, 0, j)),
        out_shape=jax.ShapeDtypeStruct((DEPTH, MOD_ROWS, n), f32),
        compiler_params=_cparams("parallel", "parallel"),
        name="ada_modulation",
    )(c_rows, mod_w, mod_b.reshape(DEPTH, 1, n))


def _even_in_kernel(x_ref, g_ref, sh_ref, sc_ref, w_ref, o_ref, h_ref):
    @pl.when(pl.program_id(1) == 0)
    def _():
        h = _rms(x_ref[...], g_ref[...]) * (1.0 + sc_ref[...]) + sh_ref[...]
        h_ref[...] = h.astype(bf16)

    o_ref[...] = jnp.dot(h_ref[...], w_ref[...], preferred_element_type=f32).astype(o_ref.dtype)


def _even_in_proj(x, gains, mods, w_in):
    tm, tn = 512, 1024
    return pl.pallas_call(
        _even_in_kernel,
        grid=(R_ALL // tm, EVEN_IN // tn),
        in_specs=[pl.BlockSpec((tm, D_MODEL), lambda i, j: (i, 0)),
                  _gain_spec(0), _mod_spec(0, tm, R_ALL), _mod_spec(1, tm, R_ALL),
                  pl.BlockSpec((D_MODEL, tn), lambda i, j: (0, j))],
        out_specs=pl.BlockSpec((tm, tn), lambda i, j: (i, j)),
        out_shape=jax.ShapeDtypeStruct((R_ALL, EVEN_IN), bf16),
        scratch_shapes=[pltpu.VMEM((tm, D_MODEL), bf16)],
        compiler_params=_cparams("parallel", "arbitrary"),
        name="even_in_proj",
    )(x, gains, mods, mods, w_in)


def _s5_tables(lam_re, lam_im, log_dt, b_re, b_im, c_re, c_im):
    hp = lax.Precision.HIGHEST
    lr = jnp.minimum(lam_re.astype(f32), LAMBDA_RE_MAX)
    li = lam_im.astype(f32)
    dt = jnp.exp(log_dt.astype(f32))[..., None]
    zr, zi = lr * dt, li * dt
    ab_mag = jnp.exp(zr)
    ab_re, ab_im = ab_mag * jnp.cos(zi), ab_mag * jnp.sin(zi)
    den = lr * lr + li * li
    nr = ab_re - 1.0
    f_re = (nr * lr + ab_im * li) / den
    f_im = (ab_im * lr - nr * li) / den
    br, bi = b_re.astype(f32), b_im.astype(f32)
    bb_re = f_re[..., None] * br - f_im[..., None] * bi
    bb_im = f_re[..., None] * bi + f_im[..., None] * br
    steps = jnp.arange(S5_T + 1, dtype=f32)[:, None, None, None]
    pw_mag = jnp.exp(steps * zr)
    pw_re, pw_im = pw_mag * jnp.cos(steps * zi), pw_mag * jnp.sin(steps * zi)
    cr, ci = c_re.astype(f32), c_im.astype(f32)
    ca_re = cr[None] * pw_re[:, :, :, None, :] - ci[None] * pw_im[:, :, :, None, :]
    ca_im = cr[None] * pw_im[:, :, :, None, :] + ci[None] * pw_re[:, :, :, None, :]
    taps = (jnp.einsum('kdgpn,dgnq->kdgpq', ca_re, bb_re, precision=hp)
            - jnp.einsum('kdgpn,dgnq->kdgpq', ca_im, bb_im, precision=hp))
    s_idx = jnp.arange(S5_T)[:, None]
    t_idx = jnp.arange(S5_T)[None, :]
    lag_f = t_idx - s_idx
    lag_b = s_idx - t_idx
    tf = jnp.where((lag_f >= 0)[:, :, None, None, None], taps[jnp.clip(lag_f, 0, S5_T), 0], 0.0)
    tb = jnp.where((lag_b >= 0)[:, :, None, None, None], taps[jnp.clip(lag_b, 0, S5_T), 1], 0.0)
    w_intra = (tf + tb).transpose(2, 0, 4, 1, 3).reshape(S5_G, S5_TP, S5_TP)

    def ab(k_of_s, d):
        pr, pi = pw_re[k_of_s, d][..., None], pw_im[k_of_s, d][..., None]
        return pr * bb_re[d][None] - pi * bb_im[d][None], pr * bb_im[d][None] + pi * bb_re[d][None]

    s_ar = jnp.arange(S5_T)
    sf_re, sf_im = ab(S5_T - 1 - s_ar, 0)
    sb_re, sb_im = ab(s_ar, 1)

    def rows(a):
        return a.transpose(1, 0, 3, 2).reshape(S5_G, S5_TP, S5_N)

    w_state = jnp.concatenate([rows(sf_re), rows(sb_re), rows(sf_im), rows(sb_im)], axis=-1)

    def cols(a):
        return a.transpose(1, 3, 0, 2).reshape(S5_G, S5_N, S5_TP)

    t_ar = jnp.arange(S5_T)
    w_cross = jnp.concatenate([cols(ca_re[t_ar + 1, 0]), cols(ca_re[S5_T - t_ar, 1]),
                               -cols(ca_im[t_ar + 1, 0]), -cols(ca_im[S5_T - t_ar, 1])], axis=1)
    a_re = jnp.concatenate([pw_re[S5_T, 0], pw_re[S5_T, 1]], axis=-1).reshape(S5_G, 1, 2 * S5_N)
    a_im = jnp.concatenate([pw_im[S5_T, 0], pw_im[S5_T, 1]], axis=-1).reshape(S5_G, 1, 2 * S5_N)
    return w_intra.astype(bf16), w_state.astype(bf16), w_cross.astype(bf16), a_re, a_im


def _s5_kernel(u_ref, wi_ref, ws_ref, wc_ref, are_ref, aim_ref, y_ref, l_ref, s_ref):
    half = 2 * S5_N
    for g in range(S5_GB):
        l_ref[g] = jnp.dot(u_ref[g], ws_ref[g], preferred_element_type=f32)

    fwd_lane = lax.broadcasted_iota(jnp.int32, (S5_BPAD, half), 1) < S5_N
    a_re = [jnp.broadcast_to(are_ref[g], (S5_BPAD, half)) for g in range(S5_GB)]
    a_im = [jnp.broadcast_to(aim_ref[g], (S5_BPAD, half)) for g in range(S5_GB)]

    def step(i, carry):
        cb = jnp.where(i < S5_NC_CTX, S5_NC_CTX - 1 - i, S5_NC + S5_NC_CTX - 1 - i)
        rf = pl.multiple_of(i * S5_BPAD, S5_BPAD)
        rb = pl.multiple_of(cb * S5_BPAD, S5_BPAD)
        out = []
        for g in range(S5_GB):
            h_re, h_im = carry[2 * g], carry[2 * g + 1]
            s_ref[g, pl.ds(rf, S5_BPAD), 0:S5_N] = h_re[:, 0:S5_N]
            s_ref[g, pl.ds(rb, S5_BPAD), S5_N:half] = h_re[:, S5_N:half]
            s_ref[g, pl.ds(rf, S5_BPAD), half:half + S5_N] = h_im[:, 0:S5_N]
            s_ref[g, pl.ds(rb, S5_BPAD), half + S5_N:2 * half] = h_im[:, S5_N:half]
            l_re = jnp.where(fwd_lane, l_ref[g, pl.ds(rf, S5_BPAD), 0:half], l_ref[g, pl.ds(rb, S5_BPAD), 0:half])
            l_im = jnp.where(fwd_lane, l_ref[g, pl.ds(rf, S5_BPAD), half:2 * half],
                             l_ref[g, pl.ds(rb, S5_BPAD), half:2 * half])
            out.append(a_re[g] * h_re - a_im[g] * h_im + l_re)
            out.append(a_re[g] * h_im + a_im[g] * h_re + l_im)
        return tuple(out)

    zero = jnp.zeros((S5_BPAD, half), f32)
    lax.fori_loop(0, S5_NC, step, (zero,) * (2 * S5_GB))

    for g in range(S5_GB):
        y = jnp.dot(u_ref[g], wi_ref[g], preferred_element_type=f32)
        y = y + jnp.dot(s_ref[g].astype(bf16), wc_ref[g], preferred_element_type=f32)
        y_ref[g] = y.astype(y_ref.dtype)


def _s5_mix(u_blocks, tables):
    w_intra, w_state, w_cross, a_re, a_im = tables
    rows = S5_NC * S5_BPAD
    big = pl.BlockSpec((S5_GB, rows, S5_TP), lambda g: (g, 0, 0))
    wsp = pl.BlockSpec((S5_GB, S5_TP, S5_TP), lambda g: (g, 0, 0))
    asp = pl.BlockSpec((S5_GB, 1, 2 * S5_N), lambda g: (g, 0, 0))
    return pl.pallas_call(
        _s5_kernel,
        grid=(S5_G // S5_GB,),
        in_specs=[big, wsp, wsp, wsp, asp, asp],
        out_specs=big,
        out_shape=jax.ShapeDtypeStruct((S5_G, rows, S5_TP), bf16),
        scratch_shapes=[pltpu.VMEM((S5_GB, rows, S5_TP), f32), pltpu.VMEM((S5_GB, rows, S5_TP), f32)],
        compiler_params=_cparams("parallel"),
        name="s5_mix",
    )(u_blocks, w_intra, w_state, w_cross, a_re, a_im)


def _s5_to_blocks(u_rows):
    lat = u_rows[:R_LAT].reshape(BATCH, SEQ // S5_T, S5_T, S5_G, S5_P)
    ctx = u_rows[R_LAT:].reshape(BATCH, CTX_LEN // S5_T, S5_T, S5_G, S5_P)
    seq = jnp.concatenate([ctx, lat], axis=1).transpose(3, 1, 0, 2, 4)
    seq = jnp.pad(seq, ((0, 0), (0, 0), (0, S5_BPAD - BATCH), (0, 0), (0, 0)))
    return seq.reshape(S5_G, S5_NC * S5_BPAD, S5_TP)


def _s5_from_blocks(y_blocks):
    y = y_blocks.reshape(S5_G, S5_NC, S5_BPAD, S5_T, S5_P)[:, :, :BATCH].transpose(2, 1, 3, 0, 4)
    ctx = y[:, :S5_NC_CTX].reshape(R_CTX, S5_WIDTH)
    lat = y[:, S5_NC_CTX:].reshape(R_LAT, S5_WIDTH)
    return jnp.concatenate([lat, ctx], axis=0)


def _ret_kernel(lg_ref, q_ref, k_ref, v_ref, o_ref, r_ref):
    d = pl.program_id(2)

    @pl.when(pl.program_id(3) == 0)
    def _():
        r_ref[...] = jnp.zeros_like(r_ref)

    t = RET_T
    lg = lg_ref[...][:, 0:1]
    n = lax.broadcasted_iota(jnp.int32, (t, t), 0)
    m = lax.broadcasted_iota(jnp.int32, (t, t), 1)
    diff = jnp.where(d == 0, n - m, m - n).astype(f32)
    decay_in = jnp.where(diff >= 0, jnp.exp(jnp.maximum(diff, 0.0) * lg), 0.0)
    pos = lax.broadcasted_iota(jnp.int32, (t, 1), 0)
    pos = jnp.where(d == 0, pos, t - 1 - pos).astype(f32)
    q_dec = jnp.exp((pos + 1.0) * lg)
    k_dec = jnp.exp((t - 1.0 - pos) * lg)
    chunk_dec = jnp.exp(t * lg)

    q = q_ref[...]
    k = k_ref[...].astype(f32) * (RET_DK ** -0.5)
    v = v_ref[...]
    r = r_ref[...]
    s = lax.dot_general(q, k.astype(bf16), (((1,), (1,)), ((), ())), preferred_element_type=f32) * decay_in
    inner = jnp.dot(s.astype(bf16), v, preferred_element_type=f32)
    cross = jnp.dot(q, r.astype(bf16), preferred_element_type=f32) * q_dec
    o_ref[...] = (inner + cross).astype(o_ref.dtype)
    kd = (k * k_dec).astype(bf16)
    r_ref[...] = r * chunk_dec + lax.dot_general(kd, v, (((0,), (0,)), ((), ())), preferred_element_type=f32)


def _ret_row_block(b, d, i):
    lat_steps = SEQ // RET_T
    lat = jnp.where(d == 0, i - 1, lat_steps - i)
    return jnp.where(i == 0, R_LAT // RET_T + b, b * lat_steps + lat)


def _retention(p, log_g):
    col0 = S5_WIDTH // RET_DK

    def spec(off):
        return pl.BlockSpec((RET_T, RET_DK), lambda b, h, d, i: (_ret_row_block(b, d, i), col0 + off * RET_H + h))

    return pl.pallas_call(
        _ret_kernel,
        grid=(BATCH, RET_H, 2, RET_STEPS),
        in_specs=[pl.BlockSpec((None, 1, 128), lambda b, h, d, i: (d * RET_H + h, 0, 0)),
                  spec(0), spec(1), spec(2)],
        out_specs=pl.BlockSpec((None, RET_T, RET_DV), lambda b, h, d, i: (d, _ret_row_block(b, d, i), h)),
        out_shape=jax.ShapeDtypeStruct((2, R_ALL, RET_H * RET_DV), bf16),
        scratch_shapes=[pltpu.VMEM((RET_DK, RET_DV), f32)],
        compiler_params=_cparams("parallel", "parallel", "parallel", "arbitrary"),
        name="retention",
    )(log_g, p, p, p)


def _gelu_tanh(x):
    return 0.5 * x * (1.0 + jnp.tanh(math.sqrt(2.0 / math.pi) * (x + 0.044715 * (x * x * x))))


def _even_out_kernel(u_ref, gate_ref, ys_ref, of_ref, ob_ref, x_ref, gt_ref, gpost_ref, dsk_ref, glub_ref,
                     gluw_ref, wo_ref, o_ref):
    u = u_ref[...].astype(f32)
    y = _gelu_tanh(ys_ref[...].astype(f32) + dsk_ref[...] * u)
    z = jnp.dot(y.astype(bf16), gluw_ref[...], preferred_element_type=f32) + glub_ref[...]
    s5_out = y * jax.nn.sigmoid(z)
    r = of_ref[...].astype(f32) + ob_ref[...].astype(f32)
    heads = []
    for h in range(RET_H):
        rh = r[:, h * RET_DV:(h + 1) * RET_DV]
        heads.append(rh * lax.rsqrt(jnp.mean(rh * rh, axis=-1, keepdims=True) + NORM_EPS))
    g = gate_ref[...].astype(f32)
    ret_out = jnp.concatenate(heads, axis=-1) * (g * jax.nn.sigmoid(g))
    out = jnp.dot(s5_out.astype(bf16), wo_ref[0:S5_WIDTH, :], preferred_element_type=f32)
    out = out + jnp.dot(ret_out.astype(bf16), wo_ref[S5_WIDTH:, :], preferred_element_type=f32)
    o_ref[...] = x_ref[...] + gt_ref[...] * _rms(out, gpost_ref[...])


def _even_out_proj(p, ys, o_ret, x, mods, gains, d_skip, glu_b, glu_w, w_out):
    tm = 256
    half = S5_WIDTH
    vec = pl.BlockSpec((1, half), lambda i: (0, 0))
    return pl.pallas_call(
        _even_out_kernel,
        grid=(R_ALL // tm,),
        in_specs=[pl.BlockSpec((tm, half), lambda i: (i, 0)),
                  pl.BlockSpec((tm, half), lambda i: (i, EVEN_IN // half - 1)),
                  pl.BlockSpec((tm, half), lambda i: (i, 0)),
                  pl.BlockSpec((None, tm, half), lambda i: (0, i, 0)),
                  pl.BlockSpec((None, tm, half), lambda i: (1, i, 0)),
                  pl.BlockSpec((tm, D_MODEL), lambda i: (i, 0)),
                  _mod_spec(2, tm, R_ALL), _gain_spec(1), vec, vec,
                  pl.BlockSpec((half, half), lambda i: (0, 0)),
                  pl.BlockSpec((D_MODEL, D_MODEL), lambda i: (0, 0))],
        out_specs=pl.BlockSpec((tm, D_MODEL), lambda i: (i, 0)),
        out_shape=jax.ShapeDtypeStruct((R_ALL, D_MODEL), f32),
        compiler_params=_cparams("parallel"),
        name="even_out_proj",
    )(p, p, ys, o_ret, o_ret, x, mods, gains, d_skip, glu_b, glu_w, w_out)


def _mlp_kernel(x_ref, gpre_ref, sh_ref, sc_ref, gt_ref, gpost_ref, w1_ref, w2_ref, o_ref, h_ref, acc_ref):
    f = pl.program_id(1)

    @pl.when(f == 0)
    def _():
        h = _rms(x_ref[...], gpre_ref[...]) * (1.0 + sc_ref[...]) + sh_ref[...]
        h_ref[...] = h.astype(bf16)
        acc_ref[...] = jnp.zeros_like(acc_ref)

    a = jnp.maximum(jnp.dot(h_ref[...], w1_ref[...], preferred_element_type=f32), 0.0)
    acc_ref[...] += jnp.dot((a * a).astype(bf16), w2_ref[...], preferred_element_type=f32)

    @pl.when(f == pl.num_programs(1) - 1)
    def _():
        o_ref[...] = x_ref[...] + gt_ref[...] * _rms(acc_ref[...], gpost_ref[...])


def _mlp(x, n_rows, mods, gains, w1, w2):
    tm, tf = 512, 512
    return pl.pallas_call(
        _mlp_kernel,
        grid=(n_rows // tm, FFN_HIDDEN // tf),
        in_specs=[pl.BlockSpec((tm, D_MODEL), lambda i, f: (i, 0)),
                  _gain_spec(2), _mod_spec(3, tm, n_rows), _mod_spec(4, tm, n_rows), _mod_spec(5, tm, n_rows),
                  _gain_spec(3),
                  pl.BlockSpec((D_MODEL, tf), lambda i, f: (0, f)),
                  pl.BlockSpec((tf, D_MODEL), lambda i, f: (f, 0))],
        out_specs=pl.BlockSpec((tm, D_MODEL), lambda i, f: (i, 0)),
        out_shape=jax.ShapeDtypeStruct((n_rows, D_MODEL), f32),
        scratch_shapes=[pltpu.VMEM((tm, D_MODEL), bf16), pltpu.VMEM((tm, D_MODEL), f32)],
        compiler_params=_cparams("parallel", "arbitrary"),
        name="sq_relu_mlp",
    )(x, gains, mods, mods, mods, gains, w1, w2)


def _rope_tables():
    rows = SEQ // GRID_W
    row = jnp.repeat(jnp.arange(rows, dtype=f32), GRID_W)
    col = jnp.tile(jnp.arange(GRID_W, dtype=f32), rows)
    n_freq = ATT_HD // 4
    inv_freq = ROPE_BASE ** (-jnp.arange(n_freq, dtype=f32) / n_freq)
    ang = jnp.concatenate([row[:, None] * inv_freq[None], col[:, None] * inv_freq[None]], axis=-1)
    cos, sin = jnp.cos(ang), jnp.sin(ang)
    return jnp.tile(cos, (1, 4)), jnp.concatenate([-sin, sin, -sin, sin], axis=-1)


def _rope(x, cos, sin):
    half = ATT_HD // 2
    lane = lax.broadcasted_iota(jnp.int32, x.shape, 1)
    partner = jnp.where((lane & (ATT_HD - 1)) < half, pltpu.roll(x, 128 - half, 1), pltpu.roll(x, half, 1))
    return x * cos + partner * sin


def _odd_in_kernel(x_ref, g_ref, sh_ref, sc_ref, w_ref, cos_ref, sin_ref, q_ref, kv_ref, h_ref, *, tm, tn):
    i = pl.program_id(0)
    j = pl.program_id(1)
    n_q_tiles = ATT_H * ATT_HD // tn
    kw = ATT_KVH * ATT_HD

    @pl.when(j == 0)
    def _():
        h = _rms(x_ref[...], g_ref[...]) * (1.0 + sc_ref[...]) + sh_ref[...]
        h_ref[...] = h.astype(bf16)

    acc = jnp.dot(h_ref[...], w_ref[...], preferred_element_type=f32)
    cos = cos_ref[...]
    sin = sin_ref[...]

    @pl.when(j < n_q_tiles)
    def _():
        for c in range(tn // 128):
            q_ref[:, c * 128:(c + 1) * 128] = (_rope(acc[:, c * 128:(c + 1) * 128], cos, sin)
                                               * (ATT_HD ** -0.5)).astype(q_ref.dtype)

    @pl.when(j == n_q_tiles)
    def _():
        is_lat = i < R_LAT // tm
        for c in range(kw // 128):
            kc = acc[:, c * 128:(c + 1) * 128]
            kv_ref[:, c * 128:(c + 1) * 128] = jnp.where(is_lat, _rope(kc, cos, sin), kc).astype(kv_ref.dtype)
        kv_ref[:, kw:] = acc[:, kw:].astype(kv_ref.dtype)


def _odd_in_proj(x, gains, mods, w_in, cos, sin):
    tm, tn = 512, 512
    qw = ATT_H * ATT_HD
    n_q_tiles = qw // tn
    pos_tiles = SEQ // tm
    return pl.pallas_call(
        functools.partial(_odd_in_kernel, tm=tm, tn=tn),
        grid=(R_ALL // tm, ODD_IN // tn),
        in_specs=[pl.BlockSpec((tm, D_MODEL), lambda i, j: (i, 0)),
                  _gain_spec(0), _mod_spec(0, tm, R_ALL), _mod_spec(1, tm, R_ALL),
                  pl.BlockSpec((D_MODEL, tn), lambda i, j: (0, j)),
                  pl.BlockSpec((tm, 128), lambda i, j: (i % pos_tiles, 0)),
                  pl.BlockSpec((tm, 128), lambda i, j: (i % pos_tiles, 0))],
        out_specs=[pl.BlockSpec((tm, tn), lambda i, j: (i, jnp.minimum(j, n_q_tiles - 1))),
                   pl.BlockSpec((tm, 2 * ATT_KVH * ATT_HD), lambda i, j: (i, 0))],
        out_shape=[jax.ShapeDtypeStruct((R_ALL, qw), bf16),
                   jax.ShapeDtypeStruct((R_ALL, 2 * ATT_KVH * ATT_HD), bf16)],
        scratch_shapes=[pltpu.VMEM((tm, D_MODEL), bf16)],
        compiler_params=_cparams("parallel", "arbitrary"),
        name="odd_in_proj",
    )(x, gains, mods, mods, w_in, cos, sin)


def _attn_kernel(sink_ref, q_ref, kvp_ref, kvc_ref, kvn_ref, kvx_ref, o_ref):
    qb = pl.program_id(1)
    t = ATT_BLOCK
    n_keys = CTX_LEN + 3 * t
    kw = ATT_KVH * ATT_HD
    row = lax.broadcasted_iota(jnp.int32, (t, n_keys), 0)
    col = lax.broadcasted_iota(jnp.int32, (t, n_keys), 1) - CTX_LEN
    in_window = jnp.abs(col - WINDOW - row) <= WINDOW
    kpos = qb * t - WINDOW + col
    valid = (col < 0) | (in_window & (kpos >= 0) & (kpos < SEQ))

    for kh in range(ATT_KVH):
        ks = slice(kh * ATT_HD, (kh + 1) * ATT_HD)
        vs = slice(kw + kh * ATT_HD, kw + (kh + 1) * ATT_HD)
        k_all = jnp.concatenate([kvx_ref[:, ks], kvp_ref[:, ks], kvc_ref[:, ks], kvn_ref[:, ks]], axis=0)
        v_all = jnp.concatenate([kvx_ref[:, vs], kvp_ref[:, vs], kvc_ref[:, vs], kvn_ref[:, vs]], axis=0)
        q_all = jnp.concatenate([q_ref[:, (kh * ATT_GRP + g) * ATT_HD:(kh * ATT_GRP + g + 1) * ATT_HD]
                                 for g in range(ATT_GRP)], axis=0)
        s_all = lax.dot_general(q_all, k_all, (((1,), (1,)), ((), ())), preferred_element_type=f32)
        probs = []
        for g in range(ATT_GRP):
            sink = sink_ref[kh * ATT_GRP + g]
            s = jnp.where(valid, s_all[g * t:(g + 1) * t], NEG_INF)
            m = jnp.maximum(jnp.max(s, axis=-1, keepdims=True), sink)
            e = jnp.exp(s - m)
            den = jnp.sum(e, axis=-1, keepdims=True) + jnp.exp(sink - m)
            probs.append((e * (1.0 / den)).astype(bf16))
        o_all = jnp.dot(jnp.concatenate(probs, axis=0), v_all, preferred_element_type=f32)
        for g in range(ATT_GRP):
            h = kh * ATT_GRP + g
            o_ref[:, h * ATT_HD:(h + 1) * ATT_HD] = o_all[g * t:(g + 1) * t].astype(o_ref.dtype)


def _attention(q, kv, sink):
    t = ATT_BLOCK
    nb = SEQ // t
    kvw = 2 * ATT_KVH * ATT_HD
    return pl.pallas_call(
        _attn_kernel,
        grid=(BATCH, nb),
        in_specs=[pl.BlockSpec(memory_space=pltpu.SMEM),
                  pl.BlockSpec((t, ATT_H * ATT_HD), lambda b, i: (b * nb + i, 0)),
                  pl.BlockSpec((t, kvw), lambda b, i: (b * nb + jnp.maximum(i - 1, 0), 0)),
                  pl.BlockSpec((t, kvw), lambda b, i: (b * nb + i, 0)),
                  pl.BlockSpec((t, kvw), lambda b, i: (b * nb + jnp.minimum(i + 1, nb - 1), 0)),
                  pl.BlockSpec((CTX_LEN, kvw), lambda b, i: (R_LAT // CTX_LEN + b, 0))],
        out_specs=pl.BlockSpec((t, ATT_H * ATT_HD), lambda b, i: (b * nb + i, 0)),
        out_shape=jax.ShapeDtypeStruct((R_LAT, ATT_H * ATT_HD), bf16),
        compiler_params=_cparams("parallel", "parallel"),
        name="window_attention",
    )(sink, q, kv, kv, kv, kv)


def _odd_out_kernel(a_ref, x_ref, gt_ref, gpost_ref, wo_ref, o_ref):
    out = jnp.dot(a_ref[...], wo_ref[...], preferred_element_type=f32)
    o_ref[...] = x_ref[...] + gt_ref[...] * _rms(out, gpost_ref[...])


def _odd_out_proj(a, x, mods, gains, w_out):
    tm = 256
    return pl.pallas_call(
        _odd_out_kernel,
        grid=(R_LAT // tm,),
        in_specs=[pl.BlockSpec((tm, D_MODEL), lambda i: (i, 0)),
                  pl.BlockSpec((tm, D_MODEL), lambda i: (i, 0)),
                  _mod_spec(2, tm, R_LAT), _gain_spec(1),
                  pl.BlockSpec((D_MODEL, D_MODEL), lambda i: (0, 0))],
        out_specs=pl.BlockSpec((tm, D_MODEL), lambda i: (i, 0)),
        out_shape=jax.ShapeDtypeStruct((R_LAT, D_MODEL), f32),
        compiler_params=_cparams("parallel"),
        name="odd_out_proj",
    )(a, x, mods, gains, w_out)


def kernel(x, c, ctx, c_ctx, mod_w, mod_b, norm_g, mlp_w1, mlp_w2, even_w_in, even_w_out, s5_lam_re, s5_lam_im, s5_log_dt, s5_b_re, s5_b_im, s5_c_re, s5_c_im, s5_d, s5_glu_w, s5_glu_b, odd_w_in, odd_w_out, odd_sink):
    rows = jnp.concatenate([x.reshape(R_LAT, D_MODEL), ctx.reshape(R_CTX, D_MODEL)], axis=0)
    c_rows = jnp.concatenate([c, c_ctx[None, :], jnp.zeros((MOD_ROWS - BATCH - 1, D_MODEL), f32)], axis=0)
    mods_all = _modulation(c_rows, mod_w, mod_b).reshape(DEPTH, MOD_ROWS * MOD_CHUNKS, 1, D_MODEL)
    gains_all = norm_g.reshape(DEPTH, 4, 1, D_MODEL)

    mods, gains = mods_all[0], gains_all[0]
    p = _even_in_proj(rows, gains, mods, even_w_in[0].astype(bf16))
    tables = _s5_tables(s5_lam_re[0], s5_lam_im[0], s5_log_dt[0], s5_b_re[0], s5_b_im[0], s5_c_re[0], s5_c_im[0])
    ys = _s5_from_blocks(_s5_mix(_s5_to_blocks(p[:, :S5_WIDTH]), tables))
    e = RET_DECAY_BASE - (2.0 * jnp.arange(RET_H, dtype=f32)[None, :] + jnp.arange(2, dtype=f32)[:, None])
    log_g = jnp.broadcast_to(jnp.log1p(-jnp.exp2(e)).reshape(2 * RET_H, 1, 1), (2 * RET_H, 1, 128))
    o_ret = _retention(p, log_g)
    rows = _even_out_proj(p, ys, o_ret, rows, mods, gains, s5_d[0].reshape(1, S5_WIDTH),
                          s5_glu_b[0].reshape(1, S5_WIDTH), s5_glu_w[0].astype(bf16), even_w_out[0].astype(bf16))
    rows = _mlp(rows, R_ALL, mods, gains, mlp_w1[0].astype(bf16), mlp_w2[0].astype(bf16))

    mods, gains = mods_all[1], gains_all[1]
    cos, sin = _rope_tables()
    q, kv = _odd_in_proj(rows, gains, mods, odd_w_in[0].astype(bf16), cos, sin)
    a = _attention(q, kv, odd_sink[0].astype(f32))
    lat = _odd_out_proj(a, rows, mods, gains, odd_w_out[0].astype(bf16))
    lat = _mlp(lat, R_LAT, mods, gains, mlp_w1[1].astype(bf16), mlp_w2[1].astype(bf16))
    return lat.reshape(BATCH, SEQ, D_MODEL)
```

```python
import functools
import math

import jax
import jax.numpy as jnp
from jax import lax
from jax.experimental import pallas as pl
from jax.experimental.pallas import tpu as pltpu

D_MODEL = 2048
BATCH = 4
SEQ = 4096
DEPTH = 2
GRID_W = 64
CTX_LEN = 256
MOD_CHUNKS = 6
FFN_HIDDEN = 4 * D_MODEL
NORM_EPS = 1e-6

S5_WIDTH = D_MODEL // 2
S5_P = 16
S5_G = S5_WIDTH // S5_P
S5_N = 64
LAMBDA_RE_MAX = -1e-4
RET_DK = 256
RET_H = (D_MODEL // 2) // RET_DK
RET_DV = (D_MODEL // 2) // RET_H
RET_DECAY_BASE = -5.0
EVEN_IN = S5_WIDTH + 2 * RET_H * RET_DK + 2 * RET_H * RET_DV

ATT_HD = 64
ATT_H = D_MODEL // ATT_HD
ATT_KVH = ATT_H // 8
ATT_GRP = ATT_H // ATT_KVH
ODD_IN = (ATT_H + 2 * ATT_KVH) * ATT_HD
WINDOW = 128
ATT_BLOCK = 128
ROPE_BASE = 10000.0
NEG_INF = -1e30

LANES = 128
SUBLANES = 8

L_ALL = CTX_LEN + SEQ
TT = 128
N_TILES = L_ALL // TT
CTX_TILES = CTX_LEN // TT
MOD_ROWS = 8
CTX_MOD_ROW = BATCH

S5_T = 16
S5_TP = S5_T * S5_P
S5_NC = L_ALL // S5_T
S5_NC_CTX = CTX_LEN // S5_T
S5_BPAD = SUBLANES
S5_GB = 4
S5_TILE_BLOCKS = TT // S5_T
S5_LANE_GROUPS = LANES // S5_P
RET_T = 256
RET_STEPS = L_ALL // RET_T
RET_CHAINS = 2 * RET_H

VMEM_LIMIT = 56 * 1024 * 1024

f32 = jnp.float32
bf16 = jnp.bfloat16


def _cparams(*sem):
    return pltpu.CompilerParams(dimension_semantics=sem, vmem_limit_bytes=VMEM_LIMIT)


def _rms(xf, g):
    ms = jnp.mean(xf * xf, axis=-1, keepdims=True)
    return xf * lax.rsqrt(ms + NORM_EPS) * g


def _mod_spec(chunk, ctx_tiles):
    return pl.BlockSpec((None, None, BATCH, 1, D_MODEL),
                        lambda i, *_: (chunk, jnp.where(i < ctx_tiles, 1, 0), 0, 0, 0))


def _gain_spec(k):
    return pl.BlockSpec((None, 1, D_MODEL), lambda *_: (k, 0, 0))


def _const_spec(shape):
    return pl.BlockSpec(shape, lambda *_: (0,) * len(shape), pipeline_mode=pl.Buffered(1))


def _chunk_transpose(vs):
    chunk = lax.broadcasted_iota(jnp.int32, vs[0].shape, 1) // S5_P
    vs = list(vs)
    for d in (4, 2, 1):
        keep = (chunk & d) == 0
        for i in range(S5_LANE_GROUPS):
            if i & d:
                continue
            a, b = vs[i], vs[i + d]
            vs[i] = jnp.where(keep, a, pltpu.roll(b, d * S5_P, 1))
            vs[i + d] = jnp.where(keep, pltpu.roll(a, LANES - d * S5_P, 1), b)
    return vs


def _mod_kernel(c_ref, w_ref, b_ref, o_ref):
    c = c_ref[...]
    s = c * jax.nn.sigmoid(c)
    o_ref[...] = jnp.dot(s.astype(bf16), w_ref[...].astype(bf16), preferred_element_type=f32) + b_ref[...]


def _modulation(c_rows, mod_w, mod_b):
    tn = 1024
    n = MOD_CHUNKS * D_MODEL
    return pl.pallas_call(
        _mod_kernel,
        grid=(DEPTH, n // tn),
        in_specs=[pl.BlockSpec((MOD_ROWS, D_MODEL), lambda l, j: (0, 0)),
                  pl.BlockSpec((None, D_MODEL, tn), lambda l, j: (l, 0, j)),
                  pl.BlockSpec((None, 1, tn), lambda l, j: (l, 0, j))],
        out_specs=pl.BlockSpec((None, MOD_ROWS, tn), lambda l, j: (l, 0, j)),
        out_shape=jax.ShapeDtypeStruct((DEPTH, MOD_ROWS, n), f32),
        compiler_params=_cparams("parallel", "parallel"),
        name="ada_modulation",
    )(c_rows, mod_w, mod_b.reshape(DEPTH, 1, n))


def _even_in_kernel(x_ref, c_ref, g_ref, sh_ref, sc_ref, w_ref, p_ref, rows_ref, ub_ref, h_ref, a_ref, st_ref):
    i = pl.program_id(0)
    j = pl.program_id(1)

    def fill(src_ref):
        xf = src_ref[...]
        rows_ref[...] = xf
        h = _rms(xf, g_ref[...]) * (1.0 + sc_ref[...]) + sh_ref[...]
        h_ref[...] = h.reshape(BATCH * TT, D_MODEL).astype(bf16)

    @pl.when(jnp.logical_and(j == 0, i < CTX_TILES))
    def _():
        fill(c_ref)

    @pl.when(jnp.logical_and(j == 0, i >= CTX_TILES))
    def _():
        fill(x_ref)

    acc = jnp.dot(h_ref[...], w_ref[...], preferred_element_type=f32)
    p_ref[...] = acc.reshape(BATCH, TT, acc.shape[-1]).astype(p_ref.dtype)

    @pl.when(j == 0)
    def _():
        for s in range(S5_WIDTH // LANES):
            a_ref[s] = acc[:, s * LANES:(s + 1) * LANES]
        st_ref[...] = jnp.zeros_like(st_ref)
        for b in range(BATCH):
            for s in range(S5_WIDTH // LANES):
                for half in range(S5_TP // LANES):
                    vs = [a_ref[s, pl.ds(b * TT + half * S5_LANE_GROUPS + tt, S5_TILE_BLOCKS, stride=S5_T), :]
                          for tt in range(S5_LANE_GROUPS)]
                    vs = _chunk_transpose(vs)
                    for gg in range(S5_LANE_GROUPS):
                        st_ref[s * S5_LANE_GROUPS + gg, half, pl.ds(b, S5_TILE_BLOCKS, stride=S5_BPAD), :] = vs[gg]
        for g in range(S5_G):
            ub_ref[g] = jnp.concatenate([st_ref[g, 0], st_ref[g, 1]], axis=-1).astype(ub_ref.dtype)


def _even_in_proj(x, ctx, gains, mods, w_in):
    tn = S5_WIDTH
    blk_rows = S5_TILE_BLOCKS * S5_BPAD
    return pl.pallas_call(
        _even_in_kernel,
        grid=(N_TILES, EVEN_IN // tn),
        in_specs=[pl.BlockSpec((BATCH, TT, D_MODEL), lambda i, j: (0, jnp.maximum(i - CTX_TILES, 0), 0)),
                  pl.BlockSpec((BATCH, TT, D_MODEL), lambda i, j: (0, jnp.minimum(i, CTX_TILES - 1), 0)),
                  _gain_spec(0), _mod_spec(0, CTX_TILES), _mod_spec(1, CTX_TILES),
                  pl.BlockSpec((D_MODEL, tn), lambda i, j: (0, j))],
        out_specs=[pl.BlockSpec((BATCH, TT, tn), lambda i, j: (0, i, j)),
                   pl.BlockSpec((BATCH, TT, D_MODEL), lambda i, j: (0, i, 0)),
                   pl.BlockSpec((S5_G, blk_rows, S5_TP), lambda i, j: (0, i, 0))],
        out_shape=[jax.ShapeDtypeStruct((BATCH, L_ALL, EVEN_IN), bf16),
                   jax.ShapeDtypeStruct((BATCH, L_ALL, D_MODEL), f32),
                   jax.ShapeDtypeStruct((S5_G, S5_NC * S5_BPAD, S5_TP), bf16)],
        scratch_shapes=[pltpu.VMEM((BATCH * TT, D_MODEL), bf16),
                        pltpu.VMEM((S5_WIDTH // LANES, BATCH * TT, LANES), f32),
                        pltpu.VMEM((S5_G, S5_TP // LANES, blk_rows, LANES), f32)],
        compiler_params=_cparams("parallel", "arbitrary"),
        name="even_in_proj",
    )(x, ctx, gains, mods, mods, w_in)


def _s5_tables(lam_re, lam_im, log_dt, b_re, b_im, c_re, c_im):
    hp = lax.Precision.HIGHEST
    lr = jnp.minimum(lam_re.astype(f32), LAMBDA_RE_MAX)
    li = lam_im.astype(f32)
    dt = jnp.exp(log_dt.astype(f32))[..., None]
    zr, zi = lr * dt, li * dt
    ab_mag = jnp.exp(zr)
    ab_re, ab_im = ab_mag * jnp.cos(zi), ab_mag * jnp.sin(zi)
    den = lr * lr + li * li
    nr = ab_re - 1.0
    f_re = (nr * lr + ab_im * li) / den
    f_im = (ab_im * lr - nr * li) / den
    br, bi = b_re.astype(f32), b_im.astype(f32)
    bb_re = f_re[..., None] * br - f_im[..., None] * bi
    bb_im = f_re[..., None] * bi + f_im[..., None] * br
    steps = jnp.arange(S5_T + 1, dtype=f32)[:, None, None, None]
    pw_mag = jnp.exp(steps * zr)
    pw_re, pw_im = pw_mag * jnp.cos(steps * zi), pw_mag * jnp.sin(steps * zi)
    cr, ci = c_re.astype(f32), c_im.astype(f32)
    ca_re = cr[None] * pw_re[:, :, :, None, :] - ci[None] * pw_im[:, :, :, None, :]
    ca_im = cr[None] * pw_im[:, :, :, None, :] + ci[None] * pw_re[:, :, :, None, :]
    taps = (jnp.einsum('kdgpn,dgnq->kdgpq', ca_re, bb_re, precision=hp)
            - jnp.einsum('kdgpn,dgnq->kdgpq', ca_im, bb_im, precision=hp))
    s_idx = jnp.arange(S5_T)[:, None]
    t_idx = jnp.arange(S5_T)[None, :]
    lag_f = t_idx - s_idx
    lag_b = s_idx - t_idx
    tf = jnp.where((lag_f >= 0)[:, :, None, None, None], taps[jnp.clip(lag_f, 0, S5_T), 0], 0.0)
    tb = jnp.where((lag_b >= 0)[:, :, None, None, None], taps[jnp.clip(lag_b, 0, S5_T), 1], 0.0)
    w_intra = (tf + tb).transpose(2, 0, 4, 1, 3).reshape(S5_G, S5_TP, S5_TP)

    def ab(k_of_s, d):
        pr, pi = pw_re[k_of_s, d][..., None], pw_im[k_of_s, d][..., None]
        return pr * bb_re[d][None] - pi * bb_im[d][None], pr * bb_im[d][None] + pi * bb_re[d][None]

    s_ar = jnp.arange(S5_T)
    sf_re, sf_im = ab(S5_T - 1 - s_ar, 0)
    sb_re, sb_im = ab(s_ar, 1)

    def rows(a):
        return a.transpose(1, 0, 3, 2).reshape(S5_G, S5_TP, S5_N)

    w_state = jnp.concatenate([rows(sf_re), rows(sb_re), rows(sf_im), rows(sb_im)], axis=-1)

    def cols(a):
        return a.transpose(1, 3, 0, 2).reshape(S5_G, S5_N, S5_TP)

    t_ar = jnp.arange(S5_T)
    w_cross = jnp.concatenate([cols(ca_re[t_ar + 1, 0]), cols(ca_re[S5_T - t_ar, 1]),
                               -cols(ca_im[t_ar + 1, 0]), -cols(ca_im[S5_T - t_ar, 1])], axis=1)
    a_re = jnp.concatenate([pw_re[S5_T, 0], pw_re[S5_T, 1]], axis=-1).reshape(S5_G, 1, 2 * S5_N)
    a_im = jnp.concatenate([pw_im[S5_T, 0], pw_im[S5_T, 1]], axis=-1).reshape(S5_G, 1, 2 * S5_N)
    return w_intra.astype(bf16), w_state.astype(bf16), w_cross.astype(bf16), a_re, a_im


def _s5_kernel(u_ref, wi_ref, ws_ref, wc_ref, are_ref, aim_ref, y_ref, s_ref):
    half = 2 * S5_N
    for g in range(S5_GB):
        s_ref[g] = jnp.dot(u_ref[g], ws_ref[g], preferred_element_type=f32)

    fwd_lane = lax.broadcasted_iota(jnp.int32, (S5_BPAD, half), 1) < S5_N
    a_re = [jnp.broadcast_to(are_ref[g], (S5_BPAD, half)) for g in range(S5_GB)]
    a_im = [jnp.broadcast_to(aim_ref[g], (S5_BPAD, half)) for g in range(S5_GB)]

    def step(i, carry):
        cb = jnp.where(i < S5_NC_CTX, S5_NC_CTX - 1 - i, S5_NC + S5_NC_CTX - 1 - i)
        rf = pl.multiple_of(i * S5_BPAD, S5_BPAD)
        rb = pl.multiple_of(cb * S5_BPAD, S5_BPAD)
        out = []
        for g in range(S5_GB):
            h_re, h_im = carry[2 * g], carry[2 * g + 1]
            l_re = jnp.where(fwd_lane, s_ref[g, pl.ds(rf, S5_BPAD), 0:half], s_ref[g, pl.ds(rb, S5_BPAD), 0:half])
            l_im = jnp.where(fwd_lane, s_ref[g, pl.ds(rf, S5_BPAD), half:2 * half],
                             s_ref[g, pl.ds(rb, S5_BPAD), half:2 * half])
            s_ref[g, pl.ds(rf, S5_BPAD), 0:S5_N] = h_re[:, 0:S5_N]
            s_ref[g, pl.ds(rb, S5_BPAD), S5_N:half] = h_re[:, S5_N:half]
            s_ref[g, pl.ds(rf, S5_BPAD), half:half + S5_N] = h_im[:, 0:S5_N]
            s_ref[g, pl.ds(rb, S5_BPAD), half + S5_N:2 * half] = h_im[:, S5_N:half]
            out.append(a_re[g] * h_re - a_im[g] * h_im + l_re)
            out.append(a_re[g] * h_im + a_im[g] * h_re + l_im)
        return tuple(out)

    zero = jnp.zeros((S5_BPAD, half), f32)
    lax.fori_loop(0, S5_NC, step, (zero,) * (2 * S5_GB))

    for g in range(S5_GB):
        y = jnp.dot(u_ref[g], wi_ref[g], preferred_element_type=f32)
        y = y + jnp.dot(s_ref[g].astype(bf16), wc_ref[g], preferred_element_type=f32)
        for hf in range(S5_TP // LANES):
            y_ref[g, hf] = y[:, hf * LANES:(hf + 1) * LANES]


def _s5_mix(u_blocks, tables):
    w_intra, w_state, w_cross, a_re, a_im = tables
    rows = S5_NC * S5_BPAD
    wsp = pl.BlockSpec((S5_GB, S5_TP, S5_TP), lambda g: (g, 0, 0))
    asp = pl.BlockSpec((S5_GB, 1, 2 * S5_N), lambda g: (g, 0, 0))
    return pl.pallas_call(
        _s5_kernel,
        grid=(S5_G // S5_GB,),
        in_specs=[pl.BlockSpec((S5_GB, rows, S5_TP), lambda g: (g, 0, 0)), wsp, wsp, wsp, asp, asp],
        out_specs=pl.BlockSpec((S5_GB, S5_TP // LANES, rows, LANES), lambda g: (g, 0, 0, 0)),
        out_shape=jax.ShapeDtypeStruct((S5_G, S5_TP // LANES, rows, LANES), f32),
        scratch_shapes=[pltpu.VMEM((S5_GB, rows, S5_TP), f32)],
        compiler_params=_cparams("parallel"),
        name="s5_mix",
    )(u_blocks, w_intra, w_state, w_cross, a_re, a_im)


def _ret_kernel(lg_ref, cd_ref, qf_ref, kf_ref, vf_ref, qb_ref, kb_ref, vb_ref, of_ref, ob_ref,
                r_ref, dec_ref, qdec_ref, kdec_ref):
    t = RET_T

    @pl.when(pl.program_id(1) == 0)
    def _():
        r_ref[...] = jnp.zeros_like(r_ref)
        n = lax.broadcasted_iota(jnp.int32, (t, t), 0)
        m = lax.broadcasted_iota(jnp.int32, (t, t), 1)
        pos = lax.broadcasted_iota(jnp.int32, (t, 1), 0)
        for d in range(2):
            diff = (n - m if d == 0 else m - n).astype(f32)
            p = (pos if d == 0 else t - 1 - pos).astype(f32)
            for h in range(RET_H):
                c = d * RET_H + h
                lg = lg_ref[c]
                dec_ref[c] = jnp.where(diff >= 0, jnp.exp(jnp.maximum(diff, 0.0) * lg), 0.0)
                qdec_ref[c] = jnp.exp((p + 1.0) * lg)
                kdec_ref[c] = jnp.exp((t - 1.0 - p) * lg)

    for d, (q_ref, k_ref, v_ref, o_ref) in enumerate(((qf_ref, kf_ref, vf_ref, of_ref),
                                                        (qb_ref, kb_ref, vb_ref, ob_ref))):
        for h in range(RET_H):
            c = d * RET_H + h
            cols = slice(h * RET_DK, (h + 1) * RET_DK)
            q = q_ref[:, cols]
            ks = k_ref[:, cols] * (RET_DK ** -0.5)
            v = v_ref[:, cols]
            r = r_ref[c]
            s = lax.dot_general(q, ks, (((1,), (1,)), ((), ())), preferred_element_type=f32) * dec_ref[c]
            inner = jnp.dot(s.astype(bf16), v, preferred_element_type=f32)
            cross = jnp.dot(q, r.astype(bf16), preferred_element_type=f32) * qdec_ref[c]
            o_ref[:, cols] = (inner + cross).astype(o_ref.dtype)
            kd = (ks.astype(f32) * kdec_ref[c]).astype(bf16)
            r_ref[c] = r * cd_ref[c] + lax.dot_general(kd, v, (((0,), (0,)), ((), ())), preferred_element_type=f32)


def _retention(p):
    e = RET_DECAY_BASE - (2.0 * jnp.arange(RET_H, dtype=f32)[None, :] + jnp.arange(2, dtype=f32)[:, None])
    log_g = jnp.log1p(-jnp.exp2(e)).reshape(RET_CHAINS)
    chunk_dec = jnp.exp(RET_T * log_g)
    width = RET_H * RET_DK
    col0 = S5_WIDTH // width

    def bwd_chunk(i):
        return jnp.where(i == 0, 0, RET_STEPS - i)

    def fwd(off):
        return pl.BlockSpec((None, RET_T, width), lambda b, i: (b, i, col0 + off))

    def bwd(off):
        return pl.BlockSpec((None, RET_T, width), lambda b, i: (b, bwd_chunk(i), col0 + off))

    smem = pl.BlockSpec(memory_space=pltpu.SMEM)
    out = jax.ShapeDtypeStruct((BATCH, L_ALL, RET_H * RET_DV), bf16)
    return pl.pallas_call(
        _ret_kernel,
        grid=(BATCH, RET_STEPS),
        in_specs=[smem, smem, fwd(0), fwd(1), fwd(2), bwd(0), bwd(1), bwd(2)],
        out_specs=[pl.BlockSpec((None, RET_T, RET_H * RET_DV), lambda b, i: (b, i, 0)),
                   pl.BlockSpec((None, RET_T, RET_H * RET_DV), lambda b, i: (b, bwd_chunk(i), 0))],
        out_shape=[out, out],
        scratch_shapes=[pltpu.VMEM((RET_CHAINS, RET_DK, RET_DV), f32),
                        pltpu.VMEM((RET_CHAINS, RET_T, RET_T), f32),
                        pltpu.VMEM((RET_CHAINS, RET_T, 1), f32),
                        pltpu.VMEM((RET_CHAINS, RET_T, 1), f32)],
        compiler_params=_cparams("parallel", "arbitrary"),
        name="retention",
    )(log_g, chunk_dec, p, p, p, p, p, p)


def _gelu_tanh(x):
    return 0.5 * x * (1.0 + jnp.tanh(math.sqrt(2.0 / math.pi) * (x + 0.044715 * (x * x * x))))


def _even_out_kernel(u_ref, gate_ref, y_ref, of_ref, ob_ref, x_ref, gt_ref, gpost_ref, dsk_ref, glub_ref,
                     gluw_ref, wo_ref, o_ref, z_ref, yf_ref):
    rows = BATCH * TT
    for b in range(BATCH):
        for s in range(S5_WIDTH // LANES):
            for half in range(S5_TP // LANES):
                vs = [y_ref[s * S5_LANE_GROUPS + gg, half, pl.ds(b, S5_TILE_BLOCKS, stride=S5_BPAD), :]
                      for gg in range(S5_LANE_GROUPS)]
                vs = _chunk_transpose(vs)
                for tt in range(S5_LANE_GROUPS):
                    z_ref[s, pl.ds(b * TT + half * S5_LANE_GROUPS + tt, S5_TILE_BLOCKS, stride=S5_T), :] = vs[tt]
    u = u_ref[...].reshape(rows, S5_WIDTH)
    for s in range(S5_WIDTH // LANES):
        cols = slice(s * LANES, (s + 1) * LANES)
        yf_ref[:, cols] = _gelu_tanh(z_ref[s] + dsk_ref[:, cols] * u[:, cols].astype(f32))
    y = yf_ref[...]
    z = jnp.dot(y.astype(bf16), gluw_ref[...], preferred_element_type=f32) + glub_ref[...]
    s5_out = y * jax.nn.sigmoid(z)
    r = (of_ref[...].astype(f32) + ob_ref[...].astype(f32)).reshape(rows, RET_H * RET_DV)
    heads = []
    for h in range(RET_H):
        rh = r[:, h * RET_DV:(h + 1) * RET_DV]
        heads.append(rh * lax.rsqrt(jnp.mean(rh * rh, axis=-1, keepdims=True) + NORM_EPS))
    g = gate_ref[...].reshape(rows, RET_H * RET_DV).astype(f32)
    ret_out = jnp.concatenate(heads, axis=-1) * (g * jax.nn.sigmoid(g))
    out = jnp.dot(s5_out.astype(bf16), wo_ref[0:S5_WIDTH, :], preferred_element_type=f32)
    out = out + jnp.dot(ret_out.astype(bf16), wo_ref[S5_WIDTH:, :], preferred_element_type=f32)
    out = out.reshape(BATCH, TT, D_MODEL)
    o_ref[...] = x_ref[...] + gt_ref[...] * _rms(out, gpost_ref[...])


def _even_out_proj(p, y_blocks, o_f, o_b, rows, mods, gains, d_skip, glu_b, glu_w, w_out):
    half = S5_WIDTH
    blk_rows = S5_TILE_BLOCKS * S5_BPAD
    tile = pl.BlockSpec((BATCH, TT, half), lambda i: (0, i, 0))
    return pl.pallas_call(
        _even_out_kernel,
        grid=(N_TILES,),
        in_specs=[tile,
                  pl.BlockSpec((BATCH, TT, half), lambda i: (0, i, EVEN_IN // half - 1)),
                  pl.BlockSpec((S5_G, S5_TP // LANES, blk_rows, LANES), lambda i: (0, 0, i, 0)),
                  tile, tile,
                  pl.BlockSpec((BATCH, TT, D_MODEL), lambda i: (0, i, 0)),
                  _mod_spec(2, CTX_TILES), _gain_spec(1),
                  _const_spec((1, half)), _const_spec((1, half)),
                  _const_spec((half, half)), _const_spec((D_MODEL, D_MODEL))],
        out_specs=pl.BlockSpec((BATCH, TT, D_MODEL), lambda i: (0, i, 0)),
        out_shape=jax.ShapeDtypeStruct((BATCH, L_ALL, D_MODEL), f32),
        scratch_shapes=[pltpu.VMEM((S5_WIDTH // LANES, BATCH * TT, LANES), f32),
                        pltpu.VMEM((BATCH * TT, S5_WIDTH), f32)],
        compiler_params=_cparams("parallel"),
        name="even_out_proj",
    )(p, p, y_blocks, o_f, o_b, rows, mods, gains, d_skip, glu_b, glu_w, w_out)


def _mlp_kernel(x_ref, gpre_ref, sh_ref, sc_ref, gt_ref, gpost_ref, w1_ref, w2_ref, o_ref, h_ref, acc_ref):
    f = pl.program_id(1)

    @pl.when(f == 0)
    def _():
        h = _rms(x_ref[...], gpre_ref[...]) * (1.0 + sc_ref[...]) + sh_ref[...]
        h_ref[...] = h.reshape(BATCH * TT, D_MODEL).astype(bf16)
        acc_ref[...] = jnp.zeros_like(acc_ref)

    a = jnp.maximum(jnp.dot(h_ref[...], w1_ref[...], preferred_element_type=f32), 0.0)
    acc_ref[...] += jnp.dot((a * a).astype(bf16), w2_ref[...], preferred_element_type=f32)

    @pl.when(f == pl.num_programs(1) - 1)
    def _():
        m = acc_ref[...].reshape(BATCH, TT, D_MODEL)
        o_ref[...] = x_ref[...] + gt_ref[...] * _rms(m, gpost_ref[...])


def _mlp(x, ctx_tiles, mods, gains, w1, w2):
    tf = 512
    n_tiles = x.shape[1] // TT
    return pl.pallas_call(
        _mlp_kernel,
        grid=(n_tiles, FFN_HIDDEN // tf),
        in_specs=[pl.BlockSpec((BATCH, TT, D_MODEL), lambda i, f: (0, i, 0)),
                  _gain_spec(2), _mod_spec(3, ctx_tiles), _mod_spec(4, ctx_tiles), _mod_spec(5, ctx_tiles),
                  _gain_spec(3),
                  pl.BlockSpec((D_MODEL, tf), lambda i, f: (0, f)),
                  pl.BlockSpec((tf, D_MODEL), lambda i, f: (f, 0))],
        out_specs=pl.BlockSpec((BATCH, TT, D_MODEL), lambda i, f: (0, i, 0)),
        out_shape=jax.ShapeDtypeStruct((BATCH, n_tiles * TT, D_MODEL), f32),
        scratch_shapes=[pltpu.VMEM((BATCH * TT, D_MODEL), bf16), pltpu.VMEM((BATCH * TT, D_MODEL), f32)],
        compiler_params=_cparams("parallel", "arbitrary"),
        name="sq_relu_mlp",
    )(x, gains, mods, mods, mods, gains, w1, w2)


def _rope_tables():
    rows = SEQ // GRID_W
    row = jnp.repeat(jnp.arange(rows, dtype=f32), GRID_W)
    col = jnp.tile(jnp.arange(GRID_W, dtype=f32), rows)
    n_freq = ATT_HD // 4
    inv_freq = ROPE_BASE ** (-jnp.arange(n_freq, dtype=f32) / n_freq)
    ang = jnp.concatenate([row[:, None] * inv_freq[None], col[:, None] * inv_freq[None]], axis=-1)
    cos, sin = jnp.cos(ang), jnp.sin(ang)
    return jnp.tile(cos, (1, 4)), jnp.concatenate([-sin, sin, -sin, sin], axis=-1)


def _rope(x, cos, sin):
    half = ATT_HD // 2
    lane = lax.broadcasted_iota(jnp.int32, x.shape, x.ndim - 1)
    partner = jnp.where((lane & (ATT_HD - 1)) < half,
                        pltpu.roll(x, LANES - half, x.ndim - 1), pltpu.roll(x, half, x.ndim - 1))
    return x * cos + partner * sin


def _odd_in_kernel(x_ref, g_ref, sh_ref, sc_ref, w_ref, cos_ref, sin_ref, q_ref, kv_ref, h_ref, *, tn):
    i = pl.program_id(0)
    j = pl.program_id(1)
    n_q_tiles = ATT_H * ATT_HD // tn
    kw = ATT_KVH * ATT_HD

    @pl.when(j == 0)
    def _():
        h = _rms(x_ref[...], g_ref[...]) * (1.0 + sc_ref[...]) + sh_ref[...]
        h_ref[...] = h.reshape(BATCH * TT, D_MODEL).astype(bf16)

    acc = jnp.dot(h_ref[...], w_ref[...], preferred_element_type=f32)
    cos = jnp.concatenate([cos_ref[...]] * BATCH, axis=0)
    sin = jnp.concatenate([sin_ref[...]] * BATCH, axis=0)

    @pl.when(j < n_q_tiles)
    def _():
        for c in range(tn // LANES):
            cols = slice(c * LANES, (c + 1) * LANES)
            qc = _rope(acc[:, cols], cos, sin) * (ATT_HD ** -0.5)
            q_ref[:, :, cols] = qc.reshape(BATCH, TT, LANES).astype(q_ref.dtype)

    @pl.when(j == n_q_tiles)
    def _():
        is_lat = i >= CTX_TILES
        for c in range(kw // LANES):
            cols = slice(c * LANES, (c + 1) * LANES)
            kc = acc[:, cols]
            kc = jnp.where(is_lat, _rope(kc, cos, sin), kc)
            kv_ref[:, :, cols] = kc.reshape(BATCH, TT, LANES).astype(kv_ref.dtype)
        kv_ref[:, :, kw:] = acc[:, kw:].reshape(BATCH, TT, kw).astype(kv_ref.dtype)


def _odd_in_proj(x, gains, mods, w_in, cos, sin):
    tn = 512
    qw = ATT_H * ATT_HD
    n_q_tiles = qw // tn
    pos = pl.BlockSpec((TT, LANES), lambda i, j: (jnp.maximum(i - CTX_TILES, 0), 0))
    return pl.pallas_call(
        functools.partial(_odd_in_kernel, tn=tn),
        grid=(N_TILES, ODD_IN // tn),
        in_specs=[pl.BlockSpec((BATCH, TT, D_MODEL), lambda i, j: (0, i, 0)),
                  _gain_spec(0), _mod_spec(0, CTX_TILES), _mod_spec(1, CTX_TILES),
                  pl.BlockSpec((D_MODEL, tn), lambda i, j: (0, j)), pos, pos],
        out_specs=[pl.BlockSpec((BATCH, TT, tn), lambda i, j: (0, i, jnp.minimum(j, n_q_tiles - 1))),
                   pl.BlockSpec((BATCH, TT, 2 * ATT_KVH * ATT_HD), lambda i, j: (0, i, 0))],
        out_shape=[jax.ShapeDtypeStruct((BATCH, L_ALL, qw), bf16),
                   jax.ShapeDtypeStruct((BATCH, L_ALL, 2 * ATT_KVH * ATT_HD), bf16)],
        scratch_shapes=[pltpu.VMEM((BATCH * TT, D_MODEL), bf16)],
        compiler_params=_cparams("parallel", "arbitrary"),
        name="odd_in_proj",
    )(x, gains, mods, mods, w_in, cos, sin)


def _attn_kernel(sink_ref, q_ref, kvp_ref, kvc_ref, kvn_ref, kvx_ref, o_ref):
    qb = pl.program_id(1)
    t = ATT_BLOCK
    n_keys = CTX_LEN + 3 * t
    kw = ATT_KVH * ATT_HD
    row = lax.broadcasted_iota(jnp.int32, (t, n_keys), 0)
    col = lax.broadcasted_iota(jnp.int32, (t, n_keys), 1) - CTX_LEN
    in_window = jnp.abs(col - WINDOW - row) <= WINDOW
    kpos = qb * t - WINDOW + col
    valid = (col < 0) | (in_window & (kpos >= 0) & (kpos < SEQ))

    for kh in range(ATT_KVH):
        ks = slice(kh * ATT_HD, (kh + 1) * ATT_HD)
        vs = slice(kw + kh * ATT_HD, kw + (kh + 1) * ATT_HD)
        k_all = jnp.concatenate([kvx_ref[:, ks], kvp_ref[:, ks], kvc_ref[:, ks], kvn_ref[:, ks]], axis=0)
        v_all = jnp.concatenate([kvx_ref[:, vs], kvp_ref[:, vs], kvc_ref[:, vs], kvn_ref[:, vs]], axis=0)
        q_all = jnp.concatenate([q_ref[:, (kh * ATT_GRP + g) * ATT_HD:(kh * ATT_GRP + g + 1) * ATT_HD]
                                 for g in range(ATT_GRP)], axis=0)
        s_all = lax.dot_general(q_all, k_all, (((1,), (1,)), ((), ())), preferred_element_type=f32)
        probs = []
        for g in range(ATT_GRP):
            sink = sink_ref[kh * ATT_GRP + g]
            s = jnp.where(valid, s_all[g * t:(g + 1) * t], NEG_INF)
            m = jnp.maximum(jnp.max(s, axis=-1, keepdims=True), sink)
            e = jnp.exp(s - m)
            den = jnp.sum(e, axis=-1, keepdims=True) + jnp.exp(sink - m)
            probs.append((e * (1.0 / den)).astype(bf16))
        o_all = jnp.dot(jnp.concatenate(probs, axis=0), v_all, preferred_element_type=f32)
        for g in range(ATT_GRP):
            h = kh * ATT_GRP + g
            o_ref[:, h * ATT_HD:(h + 1) * ATT_HD] = o_all[g * t:(g + 1) * t].astype(o_ref.dtype)


def _attention(q, kv, sink):
    t = ATT_BLOCK
    nb = SEQ // t
    off = CTX_LEN // t
    kvw = 2 * ATT_KVH * ATT_HD
    return pl.pallas_call(
        _attn_kernel,
        grid=(BATCH, nb),
        in_specs=[pl.BlockSpec(memory_space=pltpu.SMEM),
                  pl.BlockSpec((None, t, ATT_H * ATT_HD), lambda b, i: (b, off + i, 0)),
                  pl.BlockSpec((None, t, kvw), lambda b, i: (b, off + jnp.maximum(i - 1, 0), 0)),
                  pl.BlockSpec((None, t, kvw), lambda b, i: (b, off + i, 0)),
                  pl.BlockSpec((None, t, kvw), lambda b, i: (b, off + jnp.minimum(i + 1, nb - 1), 0)),
                  pl.BlockSpec((None, CTX_LEN, kvw), lambda b, i: (b, 0, 0))],
        out_specs=pl.BlockSpec((None, t, ATT_H * ATT_HD), lambda b, i: (b, i, 0)),
        out_shape=jax.ShapeDtypeStruct((BATCH, SEQ, ATT_H * ATT_HD), bf16),
        compiler_params=_cparams("parallel", "parallel"),
        name="window_attention",
    )(sink, q, kv, kv, kv, kv)


def _odd_out_kernel(a_ref, x_ref, gt_ref, gpost_ref, wo_ref, o_ref):
    a = a_ref[...].reshape(BATCH * TT, D_MODEL)
    out = jnp.dot(a, wo_ref[...], preferred_element_type=f32).reshape(BATCH, TT, D_MODEL)
    o_ref[...] = x_ref[...] + gt_ref[...] * _rms(out, gpost_ref[...])


def _odd_out_proj(a, rows, mods, gains, w_out):
    return pl.pallas_call(
        _odd_out_kernel,
        grid=(SEQ // TT,),
        in_specs=[pl.BlockSpec((BATCH, TT, D_MODEL), lambda i: (0, i, 0)),
                  pl.BlockSpec((BATCH, TT, D_MODEL), lambda i: (0, i + CTX_TILES, 0)),
                  _mod_spec(2, 0), _gain_spec(1),
                  _const_spec((D_MODEL, D_MODEL))],
        out_specs=pl.BlockSpec((BATCH, TT, D_MODEL), lambda i: (0, i, 0)),
        out_shape=jax.ShapeDtypeStruct((BATCH, SEQ, D_MODEL), f32),
        compiler_params=_cparams("parallel"),
        name="odd_out_proj",
    )(a, rows, mods, gains, w_out)


def _layer_mods(m):
    m = m.reshape(MOD_ROWS, MOD_CHUNKS, D_MODEL)
    lat = m[:BATCH]
    ctx = jnp.broadcast_to(m[CTX_MOD_ROW:CTX_MOD_ROW + 1], lat.shape)
    return jnp.stack([lat, ctx], axis=0).transpose(2, 0, 1, 3)[:, :, :, None, :]


def kernel(x, c, ctx, c_ctx, mod_w, mod_b, norm_g, mlp_w1, mlp_w2, even_w_in, even_w_out, s5_lam_re, s5_lam_im, s5_log_dt, s5_b_re, s5_b_im, s5_c_re, s5_c_im, s5_d, s5_glu_w, s5_glu_b, odd_w_in, odd_w_out, odd_sink):
    c_rows = jnp.concatenate([c, c_ctx[None, :], jnp.zeros((MOD_ROWS - BATCH - 1, D_MODEL), f32)], axis=0)
    mods_all = _modulation(c_rows, mod_w, mod_b)
    gains_all = norm_g.reshape(DEPTH, 4, 1, D_MODEL)

    mods, gains = _layer_mods(mods_all[0]), gains_all[0]
    p, rows, u_blocks = _even_in_proj(x, ctx, gains, mods, even_w_in[0].astype(bf16))
    tables = _s5_tables(s5_lam_re[0], s5_lam_im[0], s5_log_dt[0], s5_b_re[0], s5_b_im[0], s5_c_re[0], s5_c_im[0])
    y_blocks = _s5_mix(u_blocks, tables)
    o_f, o_b = _retention(p)
    rows = _even_out_proj(p, y_blocks, o_f, o_b, rows, mods, gains, s5_d[0].reshape(1, S5_WIDTH),
                          s5_glu_b[0].reshape(1, S5_WIDTH), s5_glu_w[0].astype(bf16), even_w_out[0].astype(bf16))
    rows = _mlp(rows, CTX_TILES, mods, gains, mlp_w1[0].astype(bf16), mlp_w2[0].astype(bf16))

    mods, gains = _layer_mods(mods_all[1]), gains_all[1]
    cos, sin = _rope_tables()
    q, kv = _odd_in_proj(rows, gains, mods, odd_w_in[0].astype(bf16), cos, sin)
    a = _attention(q, kv, odd_sink[0].astype(f32))
    lat = _odd_out_proj(a, rows, mods, gains, odd_w_out[0].astype(bf16))
    return _mlp(lat, 0, mods, gains, mlp_w1[1].astype(bf16), mlp_w2[1].astype(bf16))
```

```python
import functools
import math

import jax
import jax.numpy as jnp
from jax import lax
from jax.experimental import pallas as pl
from jax.experimental.pallas import tpu as pltpu

D_MODEL = 2048
BATCH = 4
SEQ = 4096
DEPTH = 2
GRID_W = 64
CTX_LEN = 256
MOD_CHUNKS = 6
FFN_HIDDEN = 4 * D_MODEL
NORM_EPS = 1e-6

S5_WIDTH = D_MODEL // 2
S5_P = 16
S5_G = S5_WIDTH // S5_P
S5_N = 64
LAMBDA_RE_MAX = -1e-4
RET_DK = 256
RET_H = (D_MODEL // 2) // RET_DK
RET_DV = (D_MODEL // 2) // RET_H
RET_DECAY_BASE = -5.0
EVEN_IN = S5_WIDTH + 2 * RET_H * RET_DK + 2 * RET_H * RET_DV

ATT_HD = 64
ATT_H = D_MODEL // ATT_HD
ATT_KVH = ATT_H // 8
ATT_GRP = ATT_H // ATT_KVH
ODD_IN = (ATT_H + 2 * ATT_KVH) * ATT_HD
WINDOW = 128
ATT_BLOCK = 128
ROPE_BASE = 10000.0
NEG_INF = -1e30

LANES = 128
SUBLANES = 8

L_ALL = CTX_LEN + SEQ
TT = 128
N_TILES = L_ALL // TT
CTX_TILES = CTX_LEN // TT
MOD_ROWS = 8
CTX_MOD_ROW = BATCH

S5_T = 16
S5_TP = S5_T * S5_P
S5_NC = L_ALL // S5_T
S5_NC_CTX = CTX_LEN // S5_T
S5_BPAD = SUBLANES
S5_GB = 4
S5_GB_TABLES = 8
S5_TILE_BLOCKS = TT // S5_T
S5_LANE_GROUPS = LANES // S5_P
RET_T = 256
RET_STEPS = L_ALL // RET_T
RET_CHAINS = 2 * RET_H

VMEM_LIMIT = 56 * 1024 * 1024

f32 = jnp.float32
bf16 = jnp.bfloat16


def _cparams(*sem):
    return pltpu.CompilerParams(dimension_semantics=sem, vmem_limit_bytes=VMEM_LIMIT)


def _rms(xf, g):
    ms = jnp.mean(xf * xf, axis=-1, keepdims=True)
    return xf * lax.rsqrt(ms + NORM_EPS) * g


def _mod_spec(chunk, ctx_tiles):
    return pl.BlockSpec((None, None, BATCH, 1, D_MODEL),
                        lambda i, *_: (chunk, jnp.where(i < ctx_tiles, 1, 0), 0, 0, 0))


def _gain_spec(k):
    return pl.BlockSpec((None, 1, D_MODEL), lambda *_: (k, 0, 0))


def _const_spec(shape):
    return pl.BlockSpec(shape, lambda *_: (0,) * len(shape), pipeline_mode=pl.Buffered(1))


def _norm_mod(xf, g, sc, sh):
    ms = jnp.mean(xf * xf, axis=-1, keepdims=True)
    return xf * lax.rsqrt(ms + NORM_EPS) * (g * (1.0 + sc)) + sh


def _chunk_swap_matrix():
    n = S5_LANE_GROUPS * LANES
    src = lax.broadcasted_iota(jnp.int32, (n, n), 0)
    dst = lax.broadcasted_iota(jnp.int32, (n, n), 1)
    i, j, p = src // LANES, (src // S5_P) % S5_LANE_GROUPS, src % S5_P
    return (dst == j * LANES + i * S5_P + p).astype(bf16)


def _mod_kernel(c_ref, w_ref, b_ref, o_ref):
    c = c_ref[...]
    s = c * jax.nn.sigmoid(c)
    o_ref[...] = jnp.dot(s.astype(bf16), w_ref[...].astype(bf16), preferred_element_type=f32) + b_ref[...]


def _modulation(c_rows, mod_w, mod_b):
    tn = 1024
    n = MOD_CHUNKS * D_MODEL
    return pl.pallas_call(
        _mod_kernel,
        grid=(DEPTH, n // tn),
        in_specs=[pl.BlockSpec((MOD_ROWS, D_MODEL), lambda l, j: (0, 0)),
                  pl.BlockSpec((None, D_MODEL, tn), lambda l, j: (l, 0, j)),
                  pl.BlockSpec((None, 1, tn), lambda l, j: (l, 0, j))],
        out_specs=pl.BlockSpec((None, MOD_ROWS, tn), lambda l, j: (l, 0, j)),
        out_shape=jax.ShapeDtypeStruct((DEPTH, MOD_ROWS, n), f32),
        compiler_params=_cparams("parallel", "parallel"),
        name="ada_modulation",
    )(c_rows, mod_w, mod_b.reshape(DEPTH, 1, n))


def _even_in_kernel(x_ref, c_ref, g_ref, sh_ref, sc_ref, w_ref, perm_ref, p_ref, rows_ref, ub_ref,
                    h_ref, a_ref, xs_ref, st_ref):
    i = pl.program_id(0)
    j = pl.program_id(1)

    def fill(src_ref):
        xf = src_ref[...]
        rows_ref[...] = xf
        h = _norm_mod(xf, g_ref[...], sc_ref[...], sh_ref[...])
        h_ref[...] = h.reshape(BATCH * TT, D_MODEL).astype(bf16)

    @pl.when(jnp.logical_and(j == 0, i < CTX_TILES))
    def _():
        fill(c_ref)

    @pl.when(jnp.logical_and(j == 0, i >= CTX_TILES))
    def _():
        fill(x_ref)

    acc = jnp.dot(h_ref[...], w_ref[...], preferred_element_type=f32)
    p_ref[...] = acc.reshape(BATCH, TT, acc.shape[-1]).astype(p_ref.dtype)

    @pl.when(j == 0)
    def _():
        n_slab, n_half = S5_WIDTH // LANES, S5_TP // LANES
        for s in range(n_slab):
            a_ref[s] = acc[:, s * LANES:(s + 1) * LANES]
        for s in range(n_slab):
            for half in range(n_half):
                for b in range(BATCH):
                    r0 = ((s * n_half + half) * BATCH + b) * S5_TILE_BLOCKS
                    for tt in range(S5_LANE_GROUPS):
                        xs_ref[tt, r0:r0 + S5_TILE_BLOCKS, :] = a_ref[
                            s, pl.ds(b * TT + half * S5_LANE_GROUPS + tt, S5_TILE_BLOCKS, stride=S5_T), :]
        xs = jnp.concatenate([xs_ref[tt] for tt in range(S5_LANE_GROUPS)], axis=-1).astype(bf16)
        ys = jnp.dot(xs, perm_ref[...], preferred_element_type=f32)
        st_ref[...] = jnp.zeros_like(st_ref)
        for s in range(n_slab):
            for half in range(n_half):
                for b in range(BATCH):
                    r0 = ((s * n_half + half) * BATCH + b) * S5_TILE_BLOCKS
                    for gg in range(S5_LANE_GROUPS):
                        st_ref[s * S5_LANE_GROUPS + gg, half, pl.ds(b, S5_TILE_BLOCKS, stride=S5_BPAD), :] = (
                            ys[r0:r0 + S5_TILE_BLOCKS, gg * LANES:(gg + 1) * LANES])
        for g in range(S5_G):
            ub_ref[g] = jnp.concatenate([st_ref[g, 0], st_ref[g, 1]], axis=-1).astype(ub_ref.dtype)


def _even_in_proj(x, ctx, gains, mods, w_in, perm):
    tn = S5_WIDTH
    blk_rows = S5_TILE_BLOCKS * S5_BPAD
    return pl.pallas_call(
        _even_in_kernel,
        grid=(N_TILES, EVEN_IN // tn),
        in_specs=[pl.BlockSpec((BATCH, TT, D_MODEL), lambda i, j: (0, jnp.maximum(i - CTX_TILES, 0), 0)),
                  pl.BlockSpec((BATCH, TT, D_MODEL), lambda i, j: (0, jnp.minimum(i, CTX_TILES - 1), 0)),
                  _gain_spec(0), _mod_spec(0, CTX_TILES), _mod_spec(1, CTX_TILES),
                  pl.BlockSpec((D_MODEL, tn), lambda i, j: (0, j)),
                  _const_spec(perm.shape)],
        out_specs=[pl.BlockSpec((BATCH, TT, tn), lambda i, j: (0, i, j)),
                   pl.BlockSpec((BATCH, TT, D_MODEL), lambda i, j: (0, i, 0)),
                   pl.BlockSpec((S5_G, blk_rows, S5_TP), lambda i, j: (0, i, 0))],
        out_shape=[jax.ShapeDtypeStruct((BATCH, L_ALL, EVEN_IN), bf16),
                   jax.ShapeDtypeStruct((BATCH, L_ALL, D_MODEL), f32),
                   jax.ShapeDtypeStruct((S5_G, S5_NC * S5_BPAD, S5_TP), bf16)],
        scratch_shapes=[pltpu.VMEM((BATCH * TT, D_MODEL), bf16),
                        pltpu.VMEM((S5_WIDTH // LANES, BATCH * TT, LANES), f32),
                        pltpu.VMEM((S5_LANE_GROUPS, BATCH * TT, LANES), f32),
                        pltpu.VMEM((S5_G, S5_TP // LANES, blk_rows, LANES), f32)],
        compiler_params=_cparams("parallel", "arbitrary"),
        name="even_in_proj",
    )(x, ctx, gains, mods, mods, w_in, perm)


def _s5_tables(lam_re, lam_im, log_dt, b_re, b_im, c_re, c_im):
    lr = jnp.minimum(lam_re.astype(f32), LAMBDA_RE_MAX)
    li = lam_im.astype(f32)
    dt = jnp.exp(log_dt.astype(f32))[..., None]
    zr, zi = lr * dt, li * dt
    ab_mag = jnp.exp(zr)
    ab_re, ab_im = ab_mag * jnp.cos(zi), ab_mag * jnp.sin(zi)
    den = lr * lr + li * li
    nr = ab_re - 1.0
    f_re = (nr * lr + ab_im * li) / den
    f_im = (ab_im * lr - nr * li) / den
    br, bi = b_re.astype(f32), b_im.astype(f32)
    bb_re = f_re[..., None] * br - f_im[..., None] * bi
    bb_im = f_re[..., None] * bi + f_im[..., None] * br
    steps = jnp.arange(S5_T + 1, dtype=f32)[:, None]
    zr_k, zi_k = zr[:, :, None, :] * steps, zi[:, :, None, :] * steps
    pw_mag = jnp.exp(zr_k)
    pw_re, pw_im = pw_mag * jnp.cos(zi_k), pw_mag * jnp.sin(zi_k)
    cr, ci = c_re.astype(f32), c_im.astype(f32)
    bt_re, bt_im = bb_re.transpose(0, 1, 3, 2), bb_im.transpose(0, 1, 3, 2)

    kf, kb = slice(0, S5_T), slice(S5_T - 1, None, -1)
    ca_re = cr[:, :, None] * pw_re[:, :, :, None, :] - ci[:, :, None] * pw_im[:, :, :, None, :]
    ca_im = cr[:, :, None] * pw_im[:, :, :, None, :] + ci[:, :, None] * pw_re[:, :, :, None, :]
    ca_f = jnp.concatenate([ca_re[0, :, kf], ca_im[0, :, kf]], axis=-1).reshape(S5_G, S5_TP, 2 * S5_N)
    ca_b = jnp.concatenate([ca_re[1, :, kb], ca_im[1, :, kb]], axis=-1).reshape(S5_G, S5_TP, 2 * S5_N)
    bcat = jnp.concatenate([bt_re, -bt_im], axis=-1)
    w_intra = _s5_intra_matrix(bcat[0], bcat[1], ca_f, ca_b)

    def state_in(d, k):
        pr, pi = pw_re[d][:, k][:, :, None, :], pw_im[d][:, k][:, :, None, :]
        br_, bi_ = bt_re[d][:, None], bt_im[d][:, None]
        return ((pr * br_ - pi * bi_).reshape(S5_G, S5_TP, S5_N), (pr * bi_ + pi * br_).reshape(S5_G, S5_TP, S5_N))

    sf_re, sf_im = state_in(0, kb)
    sb_re, sb_im = state_in(1, kf)
    w_state = jnp.concatenate([sf_re, sb_re, sf_im, sb_im], axis=-1)

    ct_re, ct_im = cr.transpose(0, 1, 3, 2), ci.transpose(0, 1, 3, 2)
    pt_re, pt_im = pw_re.transpose(0, 1, 3, 2), pw_im.transpose(0, 1, 3, 2)

    def state_out(d, k):
        pr, pi = pt_re[d][:, :, k][..., None], pt_im[d][:, :, k][..., None]
        cr_, ci_ = ct_re[d][:, :, None, :], ct_im[d][:, :, None, :]
        return ((cr_ * pr - ci_ * pi).reshape(S5_G, S5_N, S5_TP), (cr_ * pi + ci_ * pr).reshape(S5_G, S5_N, S5_TP))

    xf_re, xf_im = state_out(0, slice(1, S5_T + 1))
    xb_re, xb_im = state_out(1, slice(S5_T, 0, -1))
    w_cross = jnp.concatenate([xf_re, xb_re, -xf_im, -xb_im], axis=1)
    a_re = jnp.concatenate([pw_re[0, :, S5_T], pw_re[1, :, S5_T]], axis=-1).reshape(S5_G, 1, 2 * S5_N)
    a_im = jnp.concatenate([pw_im[0, :, S5_T], pw_im[1, :, S5_T]], axis=-1).reshape(S5_G, 1, 2 * S5_N)
    return w_intra, w_state.astype(bf16), w_cross.astype(bf16), a_re, a_im


def _s5_intra_kernel(bf_ref, bb_ref, caf_ref, cab_ref, w_ref):
    lane = lax.broadcasted_iota(jnp.int32, (S5_P, S5_TP), 1)
    dn = (((1,), (1,)), ((), ()))
    for g in range(S5_GB_TABLES):
        tq_f = lax.dot_general(bf_ref[g], caf_ref[g], dn, precision=lax.Precision.HIGHEST, preferred_element_type=f32)
        tq_b = lax.dot_general(bb_ref[g], cab_ref[g], dn, precision=lax.Precision.HIGHEST, preferred_element_type=f32)
        for s in range(S5_T):
            sf, sb = S5_P * s, S5_P * (S5_T - 1 - s)
            row_f = tq_f if s == 0 else jnp.where(lane >= sf, pltpu.roll(tq_f, sf, 1), 0.0)
            row_b = tq_b if sb == 0 else jnp.where(lane < S5_TP - sb, pltpu.roll(tq_b, S5_TP - sb, 1), 0.0)
            w_ref[g, s * S5_P:(s + 1) * S5_P, :] = (row_f + row_b).astype(w_ref.dtype)


def _s5_intra_matrix(b_f, b_b, ca_f, ca_b):
    def spec(rows, cols):
        return pl.BlockSpec((S5_GB_TABLES, rows, cols), lambda i: (i, 0, 0))

    return pl.pallas_call(
        _s5_intra_kernel,
        grid=(S5_G // S5_GB_TABLES,),
        in_specs=[spec(S5_P, 2 * S5_N), spec(S5_P, 2 * S5_N), spec(S5_TP, 2 * S5_N), spec(S5_TP, 2 * S5_N)],
        out_specs=spec(S5_TP, S5_TP),
        out_shape=jax.ShapeDtypeStruct((S5_G, S5_TP, S5_TP), bf16),
        compiler_params=_cparams("parallel"),
        name="s5_intra_matrix",
    )(b_f, b_b, ca_f, ca_b)


def _s5_kernel(u_ref, wi_ref, ws_ref, wc_ref, are_ref, aim_ref, y_ref, s_ref):
    half = 2 * S5_N
    for g in range(S5_GB):
        s_ref[g] = jnp.dot(u_ref[g], ws_ref[g], preferred_element_type=f32)

    fwd_lane = lax.broadcasted_iota(jnp.int32, (S5_BPAD, half), 1) < S5_N
    a_re = [jnp.broadcast_to(are_ref[g], (S5_BPAD, half)) for g in range(S5_GB)]
    a_im = [jnp.broadcast_to(aim_ref[g], (S5_BPAD, half)) for g in range(S5_GB)]

    def step(i, carry):
        cb = jnp.where(i < S5_NC_CTX, S5_NC_CTX - 1 - i, S5_NC + S5_NC_CTX - 1 - i)
        rf = pl.multiple_of(i * S5_BPAD, S5_BPAD)
        rb = pl.multiple_of(cb * S5_BPAD, S5_BPAD)
        out = []
        for g in range(S5_GB):
            h_re, h_im = carry[2 * g], carry[2 * g + 1]
            l_re = jnp.where(fwd_lane, s_ref[g, pl.ds(rf, S5_BPAD), 0:half], s_ref[g, pl.ds(rb, S5_BPAD), 0:half])
            l_im = jnp.where(fwd_lane, s_ref[g, pl.ds(rf, S5_BPAD), half:2 * half],
                             s_ref[g, pl.ds(rb, S5_BPAD), half:2 * half])
            s_ref[g, pl.ds(rf, S5_BPAD), 0:S5_N] = h_re[:, 0:S5_N]
            s_ref[g, pl.ds(rb, S5_BPAD), S5_N:half] = h_re[:, S5_N:half]
            s_ref[g, pl.ds(rf, S5_BPAD), half:half + S5_N] = h_im[:, 0:S5_N]
            s_ref[g, pl.ds(rb, S5_BPAD), half + S5_N:2 * half] = h_im[:, S5_N:half]
            out.append(a_re[g] * h_re - a_im[g] * h_im + l_re)
            out.append(a_re[g] * h_im + a_im[g] * h_re + l_im)
        return tuple(out)

    zero = jnp.zeros((S5_BPAD, half), f32)
    lax.fori_loop(0, S5_NC, step, (zero,) * (2 * S5_GB))

    for g in range(S5_GB):
        y = jnp.dot(u_ref[g], wi_ref[g], preferred_element_type=f32)
        y = y + jnp.dot(s_ref[g].astype(bf16), wc_ref[g], preferred_element_type=f32)
        for hf in range(S5_TP // LANES):
            y_ref[g, hf] = y[:, hf * LANES:(hf + 1) * LANES]


def _s5_mix(u_blocks, tables):
    w_intra, w_state, w_cross, a_re, a_im = tables
    rows = S5_NC * S5_BPAD
    wsp = pl.BlockSpec((S5_GB, S5_TP, S5_TP), lambda g: (g, 0, 0))
    asp = pl.BlockSpec((S5_GB, 1, 2 * S5_N), lambda g: (g, 0, 0))
    return pl.pallas_call(
        _s5_kernel,
        grid=(S5_G // S5_GB,),
        in_specs=[pl.BlockSpec((S5_GB, rows, S5_TP), lambda g: (g, 0, 0)), wsp, wsp, wsp, asp, asp],
        out_specs=pl.BlockSpec((S5_GB, S5_TP // LANES, rows, LANES), lambda g: (g, 0, 0, 0)),
        out_shape=jax.ShapeDtypeStruct((S5_G, S5_TP // LANES, rows, LANES), f32),
        scratch_shapes=[pltpu.VMEM((S5_GB, rows, S5_TP), f32)],
        compiler_params=_cparams("parallel"),
        name="s5_mix",
    )(u_blocks, w_intra, w_state, w_cross, a_re, a_im)


def _ret_kernel(lg_ref, cd_ref, qf_ref, kf_ref, vf_ref, qb_ref, kb_ref, vb_ref, of_ref, ob_ref,
                r_ref, dec_ref, qdec_ref, kdec_ref):
    t = RET_T

    @pl.when(pl.program_id(1) == 0)
    def _():
        r_ref[...] = jnp.zeros_like(r_ref)
        n = lax.broadcasted_iota(jnp.int32, (t, t), 0)
        m = lax.broadcasted_iota(jnp.int32, (t, t), 1)
        pos = lax.broadcasted_iota(jnp.int32, (t, 1), 0)
        for d in range(2):
            diff = (n - m if d == 0 else m - n).astype(f32)
            p = (pos if d == 0 else t - 1 - pos).astype(f32)
            for h in range(RET_H):
                c = d * RET_H + h
                lg = lg_ref[c]
                dec_ref[c] = jnp.where(diff >= 0, jnp.exp(jnp.maximum(diff, 0.0) * lg), 0.0)
                qdec_ref[c] = jnp.exp((p + 1.0) * lg)
                kdec_ref[c] = jnp.exp((t - 1.0 - p) * lg)

    for d, (q_ref, k_ref, v_ref, o_ref) in enumerate(((qf_ref, kf_ref, vf_ref, of_ref),
                                                        (qb_ref, kb_ref, vb_ref, ob_ref))):
        for h in range(RET_H):
            c = d * RET_H + h
            cols = slice(h * RET_DK, (h + 1) * RET_DK)
            q = q_ref[:, cols]
            ks = k_ref[:, cols] * (RET_DK ** -0.5)
            v = v_ref[:, cols]
            r = r_ref[c]
            s = lax.dot_general(q, ks, (((1,), (1,)), ((), ())), preferred_element_type=f32) * dec_ref[c]
            inner = jnp.dot(s.astype(bf16), v, preferred_element_type=f32)
            cross = jnp.dot(q, r.astype(bf16), preferred_element_type=f32) * qdec_ref[c]
            o_ref[:, cols] = (inner + cross).astype(o_ref.dtype)
            kd = (ks.astype(f32) * kdec_ref[c]).astype(bf16)
            r_ref[c] = r * cd_ref[c] + lax.dot_general(kd, v, (((0,), (0,)), ((), ())), preferred_element_type=f32)


def _retention(p):
    e = RET_DECAY_BASE - (2.0 * jnp.arange(RET_H, dtype=f32)[None, :] + jnp.arange(2, dtype=f32)[:, None])
    log_g = jnp.log1p(-jnp.exp2(e)).reshape(RET_CHAINS)
    chunk_dec = jnp.exp(RET_T * log_g)
    width = RET_H * RET_DK
    col0 = S5_WIDTH // width

    def bwd_chunk(i):
        return jnp.where(i == 0, 0, RET_STEPS - i)

    def fwd(off):
        return pl.BlockSpec((None, RET_T, width), lambda b, i: (b, i, col0 + off))

    def bwd(off):
        return pl.BlockSpec((None, RET_T, width), lambda b, i: (b, bwd_chunk(i), col0 + off))

    smem = pl.BlockSpec(memory_space=pltpu.SMEM)
    out = jax.ShapeDtypeStruct((BATCH, L_ALL, RET_H * RET_DV), bf16)
    return pl.pallas_call(
        _ret_kernel,
        grid=(BATCH, RET_STEPS),
        in_specs=[smem, smem, fwd(0), fwd(1), fwd(2), bwd(0), bwd(1), bwd(2)],
        out_specs=[pl.BlockSpec((None, RET_T, RET_H * RET_DV), lambda b, i: (b, i, 0)),
                   pl.BlockSpec((None, RET_T, RET_H * RET_DV), lambda b, i: (b, bwd_chunk(i), 0))],
        out_shape=[out, out],
        scratch_shapes=[pltpu.VMEM((RET_CHAINS, RET_DK, RET_DV), f32),
                        pltpu.VMEM((RET_CHAINS, RET_T, RET_T), f32),
                        pltpu.VMEM((RET_CHAINS, RET_T, 1), f32),
                        pltpu.VMEM((RET_CHAINS, RET_T, 1), f32)],
        compiler_params=_cparams("parallel", "arbitrary"),
        name="retention",
    )(log_g, chunk_dec, p, p, p, p, p, p)


def _gelu_tanh(x):
    return 0.5 * x * (1.0 + jnp.tanh(math.sqrt(2.0 / math.pi) * (x + 0.044715 * (x * x * x))))


def _even_out_kernel(u_ref, gate_ref, y_ref, of_ref, ob_ref, x_ref, gt_ref, gpost_ref, dsk_ref, glub_ref,
                     gluw_ref, wo_ref, perm_ref, o_ref, z_ref, xs_ref, yf_ref):
    rows = BATCH * TT
    n_slab, n_half = S5_WIDTH // LANES, S5_TP // LANES
    for s in range(n_slab):
        for half in range(n_half):
            for b in range(BATCH):
                r0 = ((s * n_half + half) * BATCH + b) * S5_TILE_BLOCKS
                for gg in range(S5_LANE_GROUPS):
                    xs_ref[gg, r0:r0 + S5_TILE_BLOCKS, :] = y_ref[
                        s * S5_LANE_GROUPS + gg, half, pl.ds(b, S5_TILE_BLOCKS, stride=S5_BPAD), :]
    xs = jnp.concatenate([xs_ref[gg] for gg in range(S5_LANE_GROUPS)], axis=-1).astype(bf16)
    ys = jnp.dot(xs, perm_ref[...], preferred_element_type=f32)
    for s in range(n_slab):
        for half in range(n_half):
            for b in range(BATCH):
                r0 = ((s * n_half + half) * BATCH + b) * S5_TILE_BLOCKS
                for tt in range(S5_LANE_GROUPS):
                    z_ref[s, pl.ds(b * TT + half * S5_LANE_GROUPS + tt, S5_TILE_BLOCKS, stride=S5_T), :] = (
                        ys[r0:r0 + S5_TILE_BLOCKS, tt * LANES:(tt + 1) * LANES])
    u = u_ref[...].reshape(rows, S5_WIDTH)
    for s in range(S5_WIDTH // LANES):
        cols = slice(s * LANES, (s + 1) * LANES)
        yf_ref[:, cols] = _gelu_tanh(z_ref[s] + dsk_ref[:, cols] * u[:, cols].astype(f32))
    y = yf_ref[...]
    z = jnp.dot(y.astype(bf16), gluw_ref[...], preferred_element_type=f32) + glub_ref[...]
    s5_out = y * jax.nn.sigmoid(z)
    r = (of_ref[...].astype(f32) + ob_ref[...].astype(f32)).reshape(rows, RET_H * RET_DV)
    heads = []
    for h in range(RET_H):
        rh = r[:, h * RET_DV:(h + 1) * RET_DV]
        heads.append(rh * lax.rsqrt(jnp.mean(rh * rh, axis=-1, keepdims=True) + NORM_EPS))
    g = gate_ref[...].reshape(rows, RET_H * RET_DV).astype(f32)
    ret_out = jnp.concatenate(heads, axis=-1) * (g * jax.nn.sigmoid(g))
    out = jnp.dot(s5_out.astype(bf16), wo_ref[0:S5_WIDTH, :], preferred_element_type=f32)
    out = out + jnp.dot(ret_out.astype(bf16), wo_ref[S5_WIDTH:, :], preferred_element_type=f32)
    out = out.reshape(BATCH, TT, D_MODEL)
    o_ref[...] = x_ref[...] + gt_ref[...] * _rms(out, gpost_ref[...])


def _even_out_proj(p, y_blocks, o_f, o_b, rows, mods, gains, d_skip, glu_b, glu_w, w_out, perm):
    half = S5_WIDTH
    blk_rows = S5_TILE_BLOCKS * S5_BPAD
    tile = pl.BlockSpec((BATCH, TT, half), lambda i: (0, i, 0))
    return pl.pallas_call(
        _even_out_kernel,
        grid=(N_TILES,),
        in_specs=[tile,
                  pl.BlockSpec((BATCH, TT, half), lambda i: (0, i, EVEN_IN // half - 1)),
                  pl.BlockSpec((S5_G, S5_TP // LANES, blk_rows, LANES), lambda i: (0, 0, i, 0)),
                  tile, tile,
                  pl.BlockSpec((BATCH, TT, D_MODEL), lambda i: (0, i, 0)),
                  _mod_spec(2, CTX_TILES), _gain_spec(1),
                  _const_spec((1, half)), _const_spec((1, half)),
                  _const_spec((half, half)), _const_spec((D_MODEL, D_MODEL)), _const_spec(perm.shape)],
        out_specs=pl.BlockSpec((BATCH, TT, D_MODEL), lambda i: (0, i, 0)),
        out_shape=jax.ShapeDtypeStruct((BATCH, L_ALL, D_MODEL), f32),
        scratch_shapes=[pltpu.VMEM((S5_WIDTH // LANES, BATCH * TT, LANES), f32),
                        pltpu.VMEM((S5_LANE_GROUPS, BATCH * TT, LANES), f32),
                        pltpu.VMEM((BATCH * TT, S5_WIDTH), f32)],
        compiler_params=_cparams("parallel"),
        name="even_out_proj",
    )(p, p, y_blocks, o_f, o_b, rows, mods, gains, d_skip, glu_b, glu_w, w_out, perm)


def _mlp_kernel(x_ref, gpre_ref, sh_ref, sc_ref, gt_ref, gpost_ref, w1_ref, w2_ref, o_ref, h_ref, acc_ref):
    f = pl.program_id(1)

    @pl.when(f == 0)
    def _():
        h = _norm_mod(x_ref[...], gpre_ref[...], sc_ref[...], sh_ref[...])
        h_ref[...] = h.reshape(BATCH * TT, D_MODEL).astype(bf16)
        acc_ref[...] = jnp.zeros_like(acc_ref)

    a = jnp.maximum(jnp.dot(h_ref[...], w1_ref[...], preferred_element_type=f32), 0.0)
    acc_ref[...] += jnp.dot((a * a).astype(bf16), w2_ref[...], preferred_element_type=f32)

    @pl.when(f == pl.num_programs(1) - 1)
    def _():
        m = acc_ref[...].reshape(BATCH, TT, D_MODEL)
        o_ref[...] = x_ref[...] + gt_ref[...] * _rms(m, gpost_ref[...])


def _mlp(x, ctx_tiles, mods, gains, w1, w2):
    tf = 512
    n_tiles = x.shape[1] // TT
    return pl.pallas_call(
        _mlp_kernel,
        grid=(n_tiles, FFN_HIDDEN // tf),
        in_specs=[pl.BlockSpec((BATCH, TT, D_MODEL), lambda i, f: (0, i, 0)),
                  _gain_spec(2), _mod_spec(3, ctx_tiles), _mod_spec(4, ctx_tiles), _mod_spec(5, ctx_tiles),
                  _gain_spec(3),
                  pl.BlockSpec((D_MODEL, tf), lambda i, f: (0, f)),
                  pl.BlockSpec((tf, D_MODEL), lambda i, f: (f, 0))],
        out_specs=pl.BlockSpec((BATCH, TT, D_MODEL), lambda i, f: (0, i, 0)),
        out_shape=jax.ShapeDtypeStruct((BATCH, n_tiles * TT, D_MODEL), f32),
        scratch_shapes=[pltpu.VMEM((BATCH * TT, D_MODEL), bf16), pltpu.VMEM((BATCH * TT, D_MODEL), f32)],
        compiler_params=_cparams("parallel", "arbitrary"),
        name="sq_relu_mlp",
    )(x, gains, mods, mods, mods, gains, w1, w2)


def _rope_tables():
    rows = SEQ // GRID_W
    row = jnp.repeat(jnp.arange(rows, dtype=f32), GRID_W)
    col = jnp.tile(jnp.arange(GRID_W, dtype=f32), rows)
    n_freq = ATT_HD // 4
    inv_freq = ROPE_BASE ** (-jnp.arange(n_freq, dtype=f32) / n_freq)
    ang = jnp.concatenate([row[:, None] * inv_freq[None], col[:, None] * inv_freq[None]], axis=-1)
    cos, sin = jnp.cos(ang), jnp.sin(ang)
    return jnp.tile(cos, (1, 4)), jnp.concatenate([-sin, sin, -sin, sin], axis=-1)


def _rope(x, cos, sin):
    half = ATT_HD // 2
    lane = lax.broadcasted_iota(jnp.int32, x.shape, x.ndim - 1)
    partner = jnp.where((lane & (ATT_HD - 1)) < half,
                        pltpu.roll(x, LANES - half, x.ndim - 1), pltpu.roll(x, half, x.ndim - 1))
    return x * cos + partner * sin


def _odd_in_kernel(x_ref, g_ref, sh_ref, sc_ref, w_ref, cos_ref, sin_ref, q_ref, kv_ref, h_ref, *, tn):
    i = pl.program_id(0)
    j = pl.program_id(1)
    n_q_tiles = ATT_H * ATT_HD // tn
    kw = ATT_KVH * ATT_HD

    @pl.when(j == 0)
    def _():
        h = _norm_mod(x_ref[...], g_ref[...], sc_ref[...], sh_ref[...])
        h_ref[...] = h.reshape(BATCH * TT, D_MODEL).astype(bf16)

    acc = jnp.dot(h_ref[...], w_ref[...], preferred_element_type=f32)
    cos = jnp.concatenate([cos_ref[...]] * BATCH, axis=0)
    sin = jnp.concatenate([sin_ref[...]] * BATCH, axis=0)

    @pl.when(j < n_q_tiles)
    def _():
        for c in range(tn // LANES):
            cols = slice(c * LANES, (c + 1) * LANES)
            qc = _rope(acc[:, cols], cos, sin) * (ATT_HD ** -0.5)
            q_ref[:, :, cols] = qc.reshape(BATCH, TT, LANES).astype(q_ref.dtype)

    @pl.when(j == n_q_tiles)
    def _():
        is_lat = i >= CTX_TILES
        for c in range(kw // LANES):
            cols = slice(c * LANES, (c + 1) * LANES)
            kc = acc[:, cols]
            kc = jnp.where(is_lat, _rope(kc, cos, sin), kc)
            kv_ref[:, :, cols] = kc.reshape(BATCH, TT, LANES).astype(kv_ref.dtype)
        kv_ref[:, :, kw:] = acc[:, kw:].reshape(BATCH, TT, kw).astype(kv_ref.dtype)


def _odd_in_proj(x, gains, mods, w_in, cos, sin):
    tn = 512
    qw = ATT_H * ATT_HD
    n_q_tiles = qw // tn
    pos = pl.BlockSpec((TT, LANES), lambda i, j: (jnp.maximum(i - CTX_TILES, 0), 0))
    return pl.pallas_call(
        functools.partial(_odd_in_kernel, tn=tn),
        grid=(N_TILES, ODD_IN // tn),
        in_specs=[pl.BlockSpec((BATCH, TT, D_MODEL), lambda i, j: (0, i, 0)),
                  _gain_spec(0), _mod_spec(0, CTX_TILES), _mod_spec(1, CTX_TILES),
                  pl.BlockSpec((D_MODEL, tn), lambda i, j: (0, j)), pos, pos],
        out_specs=[pl.BlockSpec((BATCH, TT, tn), lambda i, j: (0, i, jnp.minimum(j, n_q_tiles - 1))),
                   pl.BlockSpec((BATCH, TT, 2 * ATT_KVH * ATT_HD), lambda i, j: (0, i, 0))],
        out_shape=[jax.ShapeDtypeStruct((BATCH, L_ALL, qw), bf16),
                   jax.ShapeDtypeStruct((BATCH, L_ALL, 2 * ATT_KVH * ATT_HD), bf16)],
        scratch_shapes=[pltpu.VMEM((BATCH * TT, D_MODEL), bf16)],
        compiler_params=_cparams("parallel", "arbitrary"),
        name="odd_in_proj",
    )(x, gains, mods, mods, w_in, cos, sin)


def _attn_kernel(sink_ref, q_ref, kvp_ref, kvc_ref, kvn_ref, kvx_ref, o_ref):
    qb = pl.program_id(1)
    t = ATT_BLOCK
    kw = ATT_KVH * ATT_HD
    row = lax.broadcasted_iota(jnp.int32, (t, t), 0)
    col = lax.broadcasted_iota(jnp.int32, (t, t), 1)

    def band_valid(blk):
        off = col + blk * t
        kpos = qb * t - WINDOW + off
        return (jnp.abs(off - WINDOW - row) <= WINDOW) & (kpos >= 0) & (kpos < SEQ)

    valid_prev, valid_next = band_valid(0), band_valid(2)
    c0, c1, c2, c3 = CTX_LEN, CTX_LEN + t, CTX_LEN + 2 * t, CTX_LEN + 3 * t

    for kh in range(ATT_KVH):
        ks = slice(kh * ATT_HD, (kh + 1) * ATT_HD)
        vs = slice(kw + kh * ATT_HD, kw + (kh + 1) * ATT_HD)
        k_all = jnp.concatenate([kvx_ref[:, ks], kvp_ref[:, ks], kvc_ref[:, ks], kvn_ref[:, ks]], axis=0)
        v_all = jnp.concatenate([kvx_ref[:, vs], kvp_ref[:, vs], kvc_ref[:, vs], kvn_ref[:, vs]], axis=0)
        q_all = jnp.concatenate([q_ref[:, (kh * ATT_GRP + g) * ATT_HD:(kh * ATT_GRP + g + 1) * ATT_HD]
                                 for g in range(ATT_GRP)], axis=0)
        s_all = lax.dot_general(q_all, k_all, (((1,), (1,)), ((), ())), preferred_element_type=f32)
        probs, inv_den = [], []
        for g in range(ATT_GRP):
            sink = sink_ref[kh * ATT_GRP + g]
            s = s_all[g * t:(g + 1) * t]
            parts = [s[:, :c0], jnp.where(valid_prev, s[:, c0:c1], NEG_INF), s[:, c1:c2],
                     jnp.where(valid_next, s[:, c2:c3], NEG_INF)]
            m = sink
            for part in parts:
                m = jnp.maximum(m, jnp.max(part, axis=-1, keepdims=True))
            es = [jnp.exp(part - m) for part in parts]
            den = jnp.exp(sink - m)
            for e in es:
                den = den + jnp.sum(e, axis=-1, keepdims=True)
            inv_den.append(1.0 / den)
            probs.append(jnp.concatenate(es, axis=-1).astype(bf16))
        o_all = jnp.dot(jnp.concatenate(probs, axis=0), v_all, preferred_element_type=f32)
        for g in range(ATT_GRP):
            h = kh * ATT_GRP + g
            o_ref[:, h * ATT_HD:(h + 1) * ATT_HD] = (o_all[g * t:(g + 1) * t] * inv_den[g]).astype(o_ref.dtype)


def _attention(q, kv, sink):
    t = ATT_BLOCK
    nb = SEQ // t
    off = CTX_LEN // t
    kvw = 2 * ATT_KVH * ATT_HD
    return pl.pallas_call(
        _attn_kernel,
        grid=(BATCH, nb),
        in_specs=[pl.BlockSpec(memory_space=pltpu.SMEM),
                  pl.BlockSpec((None, t, ATT_H * ATT_HD), lambda b, i: (b, off + i, 0)),
                  pl.BlockSpec((None, t, kvw), lambda b, i: (b, off + jnp.maximum(i - 1, 0), 0)),
                  pl.BlockSpec((None, t, kvw), lambda b, i: (b, off + i, 0)),
                  pl.BlockSpec((None, t, kvw), lambda b, i: (b, off + jnp.minimum(i + 1, nb - 1), 0)),
                  pl.BlockSpec((None, CTX_LEN, kvw), lambda b, i: (b, 0, 0))],
        out_specs=pl.BlockSpec((None, t, ATT_H * ATT_HD), lambda b, i: (b, i, 0)),
        out_shape=jax.ShapeDtypeStruct((BATCH, SEQ, ATT_H * ATT_HD), bf16),
        compiler_params=_cparams("parallel", "parallel"),
        name="window_attention",
    )(sink, q, kv, kv, kv, kv)


def _odd_out_kernel(a_ref, x_ref, gt_ref, gpost_ref, wo_ref, o_ref):
    a = a_ref[...].reshape(BATCH * TT, D_MODEL)
    out = jnp.dot(a, wo_ref[...], preferred_element_type=f32).reshape(BATCH, TT, D_MODEL)
    o_ref[...] = x_ref[...] + gt_ref[...] * _rms(out, gpost_ref[...])


def _odd_out_proj(a, rows, mods, gains, w_out):
    return pl.pallas_call(
        _odd_out_kernel,
        grid=(SEQ // TT,),
        in_specs=[pl.BlockSpec((BATCH, TT, D_MODEL), lambda i: (0, i, 0)),
                  pl.BlockSpec((BATCH, TT, D_MODEL), lambda i: (0, i + CTX_TILES, 0)),
                  _mod_spec(2, 0), _gain_spec(1),
                  _const_spec((D_MODEL, D_MODEL))],
        out_specs=pl.BlockSpec((BATCH, TT, D_MODEL), lambda i: (0, i, 0)),
        out_shape=jax.ShapeDtypeStruct((BATCH, SEQ, D_MODEL), f32),
        compiler_params=_cparams("parallel"),
        name="odd_out_proj",
    )(a, rows, mods, gains, w_out)


def _layer_mods(m):
    m = m.reshape(MOD_ROWS, MOD_CHUNKS, D_MODEL)
    lat = m[:BATCH]
    ctx = jnp.broadcast_to(m[CTX_MOD_ROW:CTX_MOD_ROW + 1], lat.shape)
    return jnp.stack([lat, ctx], axis=0).transpose(2, 0, 1, 3)[:, :, :, None, :]


def kernel(x, c, ctx, c_ctx, mod_w, mod_b, norm_g, mlp_w1, mlp_w2, even_w_in, even_w_out, s5_lam_re, s5_lam_im, s5_log_dt, s5_b_re, s5_b_im, s5_c_re, s5_c_im, s5_d, s5_glu_w, s5_glu_b, odd_w_in, odd_w_out, odd_sink):
    c_rows = jnp.concatenate([c, c_ctx[None, :], jnp.zeros((MOD_ROWS - BATCH - 1, D_MODEL), f32)], axis=0)
    mods_all = _modulation(c_rows, mod_w, mod_b)
    gains_all = norm_g.reshape(DEPTH, 4, 1, D_MODEL)

    mods, gains = _layer_mods(mods_all[0]), gains_all[0]
    perm = _chunk_swap_matrix()
    p, rows, u_blocks = _even_in_proj(x, ctx, gains, mods, even_w_in[0].astype(bf16), perm)
    tables = _s5_tables(s5_lam_re[0], s5_lam_im[0], s5_log_dt[0], s5_b_re[0], s5_b_im[0], s5_c_re[0], s5_c_im[0])
    y_blocks = _s5_mix(u_blocks, tables)
    o_f, o_b = _retention(p)
    rows = _even_out_proj(p, y_blocks, o_f, o_b, rows, mods, gains, s5_d[0].reshape(1, S5_WIDTH),
                          s5_glu_b[0].reshape(1, S5_WIDTH), s5_glu_w[0].astype(bf16), even_w_out[0].astype(bf16),
                          perm)
    rows = _mlp(rows, CTX_TILES, mods, gains, mlp_w1[0].astype(bf16), mlp_w2[0].astype(bf16))

    mods, gains = _layer_mods(mods_all[1]), gains_all[1]
    cos, sin = _rope_tables()
    q, kv = _odd_in_proj(rows, gains, mods, odd_w_in[0].astype(bf16), cos, sin)
    a = _attention(q, kv, odd_sink[0].astype(f32))
    lat = _odd_out_proj(a, rows, mods, gains, odd_w_out[0].astype(bf16))
    return _mlp(lat, 0, mods, gains, mlp_w1[1].astype(bf16), mlp_w2[1].astype(bf16))
```

```python
import functools
import math

import jax
import jax.numpy as jnp
from jax import lax
from jax.experimental import pallas as pl
from jax.experimental.pallas import tpu as pltpu

D_MODEL = 2048
BATCH = 4
SEQ = 4096
DEPTH = 2
GRID_W = 64
CTX_LEN = 256
MOD_CHUNKS = 6
FFN_HIDDEN = 4 * D_MODEL
NORM_EPS = 1e-6

S5_WIDTH = D_MODEL // 2
S5_P = 16
S5_G = S5_WIDTH // S5_P
S5_N = 64
LAMBDA_RE_MAX = -1e-4
RET_DK = 256
RET_H = (D_MODEL // 2) // RET_DK
RET_DV = (D_MODEL // 2) // RET_H
RET_DECAY_BASE = -5.0
EVEN_IN = S5_WIDTH + 2 * RET_H * RET_DK + 2 * RET_H * RET_DV

ATT_HD = 64
ATT_H = D_MODEL // ATT_HD
ATT_KVH = ATT_H // 8
ATT_GRP = ATT_H // ATT_KVH
ODD_IN = (ATT_H + 2 * ATT_KVH) * ATT_HD
WINDOW = 128
ATT_BLOCK = 128
ROPE_BASE = 10000.0
NEG_INF = -1e30
LOG2_E = math.log2(math.e)
Q_SCALE = ATT_HD ** -0.5 * LOG2_E

LANES = 128
SUBLANES = 8

L_ALL = CTX_LEN + SEQ
TT = 128
N_TILES = L_ALL // TT
CTX_TILES = CTX_LEN // TT
NORM_ROWS = 16
NORM_UNROLL = 4
MOD_ROWS = 8
CTX_MOD_ROW = BATCH

S5_T = 16
S5_TP = S5_T * S5_P
S5_NC = L_ALL // S5_T
S5_NC_CTX = CTX_LEN // S5_T
S5_BPAD = SUBLANES
S5_GB = 4
S5_GB_TABLES = 8
S5_TILE_BLOCKS = TT // S5_T
S5_LANE_GROUPS = LANES // S5_P
RET_T = 256
RET_STEPS = L_ALL // RET_T
RET_CHAINS = 2 * RET_H

VMEM_LIMIT = 56 * 1024 * 1024

f32 = jnp.float32
bf16 = jnp.bfloat16


def _cparams(*sem):
    return pltpu.CompilerParams(dimension_semantics=sem, vmem_limit_bytes=VMEM_LIMIT)


def _mod_spec(chunk, ctx_tiles):
    return pl.BlockSpec((None, None, BATCH, 1, D_MODEL),
                        lambda i, *_: (chunk, jnp.where(i < ctx_tiles, 1, 0), 0, 0, 0))


def _gain_spec(k):
    return pl.BlockSpec((None, 1, D_MODEL), lambda *_: (k, 0, 0))


def _const_spec(shape):
    return pl.BlockSpec(shape, lambda *_: (0,) * len(shape), pipeline_mode=pl.Buffered(1))


def _norm_mod_rows(x_ref, g_ref, sc_ref, sh_ref, h_ref, copy_ref=None):
    for b in range(BATCH):
        gain = g_ref[...] * (1.0 + sc_ref[b])
        shift = sh_ref[b]

        def body(r, carry):
            r0 = pl.multiple_of(r * NORM_ROWS, NORM_ROWS)
            xf = x_ref[b, pl.ds(r0, NORM_ROWS), :]
            if copy_ref is not None:
                copy_ref[b, pl.ds(r0, NORM_ROWS), :] = xf
            ms = jnp.mean(xf * xf, axis=-1, keepdims=True)
            h = xf * lax.rsqrt(ms + NORM_EPS) * gain + shift
            h_ref[pl.ds(pl.multiple_of(b * TT + r0, NORM_ROWS), NORM_ROWS), :] = h.astype(h_ref.dtype)
            return carry

        lax.fori_loop(0, TT // NORM_ROWS, body, 0, unroll=NORM_UNROLL)


def _gated_residual_rows(load_rows, x_ref, gt_ref, gpost_ref, o_ref):
    for b in range(BATCH):
        gain = gpost_ref[...] * gt_ref[b]

        def body(r, carry):
            r0 = pl.multiple_of(r * NORM_ROWS, NORM_ROWS)
            m = load_rows(b, r0, NORM_ROWS)
            ms = jnp.mean(m * m, axis=-1, keepdims=True)
            o_ref[b, pl.ds(r0, NORM_ROWS), :] = x_ref[b, pl.ds(r0, NORM_ROWS), :] + m * lax.rsqrt(ms + NORM_EPS) * gain
            return carry

        lax.fori_loop(0, TT // NORM_ROWS, body, 0, unroll=NORM_UNROLL)


def _chunk_swap_matrix():
    n = S5_LANE_GROUPS * LANES
    src = lax.broadcasted_iota(jnp.int32, (n, n), 0)
    dst = lax.broadcasted_iota(jnp.int32, (n, n), 1)
    i, j, p = src // LANES, (src // S5_P) % S5_LANE_GROUPS, src % S5_P
    return (dst == j * LANES + i * S5_P + p).astype(bf16)


def _mod_kernel(c_ref, w_ref, b_ref, o_ref):
    c = c_ref[...]
    s = c * jax.nn.sigmoid(c)
    o_ref[...] = jnp.dot(s.astype(bf16), w_ref[...].astype(bf16), preferred_element_type=f32) + b_ref[...]


def _modulation(c_rows, mod_w, mod_b):
    tn = 1024
    n = MOD_CHUNKS * D_MODEL
    return pl.pallas_call(
        _mod_kernel,
        grid=(DEPTH, n // tn),
        in_specs=[pl.BlockSpec((MOD_ROWS, D_MODEL), lambda l, j: (0, 0)),
                  pl.BlockSpec((None, D_MODEL, tn), lambda l, j: (l, 0, j)),
                  pl.BlockSpec((None, 1, tn), lambda l, j: (l, 0, j))],
        out_specs=pl.BlockSpec((None, MOD_ROWS, tn), lambda l, j: (l, 0, j)),
        out_shape=jax.ShapeDtypeStruct((DEPTH, MOD_ROWS, n), f32),
        compiler_params=_cparams("parallel", "parallel"),
        name="ada_modulation",
    )(c_rows, mod_w, mod_b.reshape(DEPTH, 1, n))


def _even_in_kernel(x_ref, c_ref, g_ref, sh_ref, sc_ref, w_ref, perm_ref, p_ref, rows_ref, ub_ref,
                    h_ref, a_ref, xs_ref, st_ref):
    i = pl.program_id(0)
    j = pl.program_id(1)

    @pl.when(jnp.logical_and(j == 0, i < CTX_TILES))
    def _():
        _norm_mod_rows(c_ref, g_ref, sc_ref, sh_ref, h_ref, copy_ref=rows_ref)

    @pl.when(jnp.logical_and(j == 0, i >= CTX_TILES))
    def _():
        _norm_mod_rows(x_ref, g_ref, sc_ref, sh_ref, h_ref, copy_ref=rows_ref)

    acc = jnp.dot(h_ref[...], w_ref[...], preferred_element_type=f32)
    p_ref[...] = acc.reshape(BATCH, TT, acc.shape[-1]).astype(p_ref.dtype)

    @pl.when(j == 0)
    def _():
        n_slab, n_half = S5_WIDTH // LANES, S5_TP // LANES
        for s in range(n_slab):
            a_ref[s] = acc[:, s * LANES:(s + 1) * LANES]
        for s in range(n_slab):
            for half in range(n_half):
                for b in range(BATCH):
                    r0 = ((s * n_half + half) * BATCH + b) * S5_TILE_BLOCKS
                    for tt in range(S5_LANE_GROUPS):
                        xs_ref[tt, r0:r0 + S5_TILE_BLOCKS, :] = a_ref[
                            s, pl.ds(b * TT + half * S5_LANE_GROUPS + tt, S5_TILE_BLOCKS, stride=S5_T), :]
        xs = jnp.concatenate([xs_ref[tt] for tt in range(S5_LANE_GROUPS)], axis=-1).astype(bf16)
        ys = jnp.dot(xs, perm_ref[...], preferred_element_type=f32)
        st_ref[...] = jnp.zeros_like(st_ref)
        for s in range(n_slab):
            for half in range(n_half):
                for b in range(BATCH):
                    r0 = ((s * n_half + half) * BATCH + b) * S5_TILE_BLOCKS
                    for gg in range(S5_LANE_GROUPS):
                        st_ref[s * S5_LANE_GROUPS + gg, half, pl.ds(b, S5_TILE_BLOCKS, stride=S5_BPAD), :] = (
                            ys[r0:r0 + S5_TILE_BLOCKS, gg * LANES:(gg + 1) * LANES])
        for g in range(S5_G):
            ub_ref[g] = jnp.concatenate([st_ref[g, 0], st_ref[g, 1]], axis=-1).astype(ub_ref.dtype)


def _even_in_proj(x, ctx, gains, mods, w_in, perm):
    tn = S5_WIDTH
    blk_rows = S5_TILE_BLOCKS * S5_BPAD
    return pl.pallas_call(
        _even_in_kernel,
        grid=(N_TILES, EVEN_IN // tn),
        in_specs=[pl.BlockSpec((BATCH, TT, D_MODEL), lambda i, j: (0, jnp.maximum(i - CTX_TILES, 0), 0)),
                  pl.BlockSpec((BATCH, TT, D_MODEL), lambda i, j: (0, jnp.minimum(i, CTX_TILES - 1), 0)),
                  _gain_spec(0), _mod_spec(0, CTX_TILES), _mod_spec(1, CTX_TILES),
                  pl.BlockSpec((D_MODEL, tn), lambda i, j: (0, j)),
                  _const_spec(perm.shape)],
        out_specs=[pl.BlockSpec((BATCH, TT, tn), lambda i, j: (0, i, j)),
                   pl.BlockSpec((BATCH, TT, D_MODEL), lambda i, j: (0, i, 0)),
                   pl.BlockSpec((S5_G, blk_rows, S5_TP), lambda i, j: (0, i, 0))],
        out_shape=[jax.ShapeDtypeStruct((BATCH, L_ALL, EVEN_IN), bf16),
                   jax.ShapeDtypeStruct((BATCH, L_ALL, D_MODEL), f32),
                   jax.ShapeDtypeStruct((S5_G, S5_NC * S5_BPAD, S5_TP), bf16)],
        scratch_shapes=[pltpu.VMEM((BATCH * TT, D_MODEL), bf16),
                        pltpu.VMEM((S5_WIDTH // LANES, BATCH * TT, LANES), f32),
                        pltpu.VMEM((S5_LANE_GROUPS, BATCH * TT, LANES), f32),
                        pltpu.VMEM((S5_G, S5_TP // LANES, blk_rows, LANES), f32)],
        compiler_params=_cparams("parallel", "arbitrary"),
        name="even_in_proj",
    )(x, ctx, gains, mods, mods, w_in, perm)


def _s5_tables(lam_re, lam_im, log_dt, b_re, b_im, c_re, c_im):
    lr = jnp.minimum(lam_re.astype(f32), LAMBDA_RE_MAX)
    li = lam_im.astype(f32)
    dt = jnp.exp(log_dt.astype(f32))[..., None]
    zr, zi = lr * dt, li * dt
    ab_mag = jnp.exp(zr)
    ab_re, ab_im = ab_mag * jnp.cos(zi), ab_mag * jnp.sin(zi)
    den = lr * lr + li * li
    nr = ab_re - 1.0
    f_re = (nr * lr + ab_im * li) / den
    f_im = (ab_im * lr - nr * li) / den
    br, bi = b_re.astype(f32), b_im.astype(f32)
    bb_re = f_re[..., None] * br - f_im[..., None] * bi
    bb_im = f_re[..., None] * bi + f_im[..., None] * br
    steps = jnp.arange(S5_T + 1, dtype=f32)[:, None]
    zr_k, zi_k = zr[:, :, None, :] * steps, zi[:, :, None, :] * steps
    pw_mag = jnp.exp(zr_k)
    pw_re, pw_im = pw_mag * jnp.cos(zi_k), pw_mag * jnp.sin(zi_k)
    cr, ci = c_re.astype(f32), c_im.astype(f32)
    bt_re, bt_im = bb_re.transpose(0, 1, 3, 2), bb_im.transpose(0, 1, 3, 2)

    kf, kb = slice(0, S5_T), slice(S5_T - 1, None, -1)
    ca_re = cr[:, :, None] * pw_re[:, :, :, None, :] - ci[:, :, None] * pw_im[:, :, :, None, :]
    ca_im = cr[:, :, None] * pw_im[:, :, :, None, :] + ci[:, :, None] * pw_re[:, :, :, None, :]
    ca_f = jnp.concatenate([ca_re[0, :, kf], ca_im[0, :, kf]], axis=-1).reshape(S5_G, S5_TP, 2 * S5_N)
    ca_b = jnp.concatenate([ca_re[1, :, kb], ca_im[1, :, kb]], axis=-1).reshape(S5_G, S5_TP, 2 * S5_N)
    bcat = jnp.concatenate([bt_re, -bt_im], axis=-1)
    w_intra = _s5_intra_matrix(bcat[0], bcat[1], ca_f, ca_b)

    def state_in(d, k):
        pr, pi = pw_re[d][:, k][:, :, None, :], pw_im[d][:, k][:, :, None, :]
        br_, bi_ = bt_re[d][:, None], bt_im[d][:, None]
        return ((pr * br_ - pi * bi_).reshape(S5_G, S5_TP, S5_N), (pr * bi_ + pi * br_).reshape(S5_G, S5_TP, S5_N))

    sf_re, sf_im = state_in(0, kb)
    sb_re, sb_im = state_in(1, kf)
    w_state = jnp.concatenate([sf_re, sb_re, sf_im, sb_im], axis=-1)

    ct_re, ct_im = cr.transpose(0, 1, 3, 2), ci.transpose(0, 1, 3, 2)
    pt_re, pt_im = pw_re.transpose(0, 1, 3, 2), pw_im.transpose(0, 1, 3, 2)

    def state_out(d, k):
        pr, pi = pt_re[d][:, :, k][..., None], pt_im[d][:, :, k][..., None]
        cr_, ci_ = ct_re[d][:, :, None, :], ct_im[d][:, :, None, :]
        return ((cr_ * pr - ci_ * pi).reshape(S5_G, S5_N, S5_TP), (cr_ * pi + ci_ * pr).reshape(S5_G, S5_N, S5_TP))

    xf_re, xf_im = state_out(0, slice(1, S5_T + 1))
    xb_re, xb_im = state_out(1, slice(S5_T, 0, -1))
    w_cross = jnp.concatenate([xf_re, xb_re, -xf_im, -xb_im], axis=1)
    a_re = jnp.concatenate([pw_re[0, :, S5_T], pw_re[1, :, S5_T]], axis=-1).reshape(S5_G, 1, 2 * S5_N)
    a_im = jnp.concatenate([pw_im[0, :, S5_T], pw_im[1, :, S5_T]], axis=-1).reshape(S5_G, 1, 2 * S5_N)
    return w_intra, w_state.astype(bf16), w_cross.astype(bf16), a_re, a_im


def _s5_intra_kernel(bf_ref, bb_ref, caf_ref, cab_ref, w_ref):
    lane = lax.broadcasted_iota(jnp.int32, (S5_P, S5_TP), 1)
    dn = (((1,), (1,)), ((), ()))
    for g in range(S5_GB_TABLES):
        tq_f = lax.dot_general(bf_ref[g], caf_ref[g], dn, precision=lax.Precision.HIGHEST, preferred_element_type=f32)
        tq_b = lax.dot_general(bb_ref[g], cab_ref[g], dn, precision=lax.Precision.HIGHEST, preferred_element_type=f32)
        for s in range(S5_T):
            sf, sb = S5_P * s, S5_P * (S5_T - 1 - s)
            row_f = tq_f if s == 0 else jnp.where(lane >= sf, pltpu.roll(tq_f, sf, 1), 0.0)
            row_b = tq_b if sb == 0 else jnp.where(lane < S5_TP - sb, pltpu.roll(tq_b, S5_TP - sb, 1), 0.0)
            w_ref[g, s * S5_P:(s + 1) * S5_P, :] = (row_f + row_b).astype(w_ref.dtype)


def _s5_intra_matrix(b_f, b_b, ca_f, ca_b):
    def spec(rows, cols):
        return pl.BlockSpec((S5_GB_TABLES, rows, cols), lambda i: (i, 0, 0))

    return pl.pallas_call(
        _s5_intra_kernel,
        grid=(S5_G // S5_GB_TABLES,),
        in_specs=[spec(S5_P, 2 * S5_N), spec(S5_P, 2 * S5_N), spec(S5_TP, 2 * S5_N), spec(S5_TP, 2 * S5_N)],
        out_specs=spec(S5_TP, S5_TP),
        out_shape=jax.ShapeDtypeStruct((S5_G, S5_TP, S5_TP), bf16),
        compiler_params=_cparams("parallel"),
        name="s5_intra_matrix",
    )(b_f, b_b, ca_f, ca_b)


def _s5_kernel(u_ref, wi_ref, ws_ref, wc_ref, are_ref, aim_ref, y_ref, s_ref):
    half = 2 * S5_N
    for g in range(S5_GB):
        s_ref[g] = jnp.dot(u_ref[g], ws_ref[g], preferred_element_type=f32)

    fwd_lane = lax.broadcasted_iota(jnp.int32, (S5_BPAD, half), 1) < S5_N
    a_re = [jnp.broadcast_to(are_ref[g], (S5_BPAD, half)) for g in range(S5_GB)]
    a_im = [jnp.broadcast_to(aim_ref[g], (S5_BPAD, half)) for g in range(S5_GB)]

    def step(i, carry):
        cb = jnp.where(i < S5_NC_CTX, S5_NC_CTX - 1 - i, S5_NC + S5_NC_CTX - 1 - i)
        rf = pl.multiple_of(i * S5_BPAD, S5_BPAD)
        rb = pl.multiple_of(cb * S5_BPAD, S5_BPAD)
        out = []
        for g in range(S5_GB):
            h_re, h_im = carry[2 * g], carry[2 * g + 1]
            l_re = jnp.where(fwd_lane, s_ref[g, pl.ds(rf, S5_BPAD), 0:half], s_ref[g, pl.ds(rb, S5_BPAD), 0:half])
            l_im = jnp.where(fwd_lane, s_ref[g, pl.ds(rf, S5_BPAD), half:2 * half],
                             s_ref[g, pl.ds(rb, S5_BPAD), half:2 * half])
            s_ref[g, pl.ds(rf, S5_BPAD), 0:S5_N] = h_re[:, 0:S5_N]
            s_ref[g, pl.ds(rb, S5_BPAD), S5_N:half] = h_re[:, S5_N:half]
            s_ref[g, pl.ds(rf, S5_BPAD), half:half + S5_N] = h_im[:, 0:S5_N]
            s_ref[g, pl.ds(rb, S5_BPAD), half + S5_N:2 * half] = h_im[:, S5_N:half]
            out.append(a_re[g] * h_re - a_im[g] * h_im + l_re)
            out.append(a_re[g] * h_im + a_im[g] * h_re + l_im)
        return tuple(out)

    zero = jnp.zeros((S5_BPAD, half), f32)
    lax.fori_loop(0, S5_NC, step, (zero,) * (2 * S5_GB))

    for g in range(S5_GB):
        y = jnp.dot(u_ref[g], wi_ref[g], preferred_element_type=f32)
        y = y + jnp.dot(s_ref[g].astype(bf16), wc_ref[g], preferred_element_type=f32)
        for hf in range(S5_TP // LANES):
            y_ref[g, hf] = y[:, hf * LANES:(hf + 1) * LANES]


def _s5_mix(u_blocks, tables):
    w_intra, w_state, w_cross, a_re, a_im = tables
    rows = S5_NC * S5_BPAD
    wsp = pl.BlockSpec((S5_GB, S5_TP, S5_TP), lambda g: (g, 0, 0))
    asp = pl.BlockSpec((S5_GB, 1, 2 * S5_N), lambda g: (g, 0, 0))
    return pl.pallas_call(
        _s5_kernel,
        grid=(S5_G // S5_GB,),
        in_specs=[pl.BlockSpec((S5_GB, rows, S5_TP), lambda g: (g, 0, 0)), wsp, wsp, wsp, asp, asp],
        out_specs=pl.BlockSpec((S5_GB, S5_TP // LANES, rows, LANES), lambda g: (g, 0, 0, 0)),
        out_shape=jax.ShapeDtypeStruct((S5_G, S5_TP // LANES, rows, LANES), f32),
        scratch_shapes=[pltpu.VMEM((S5_GB, rows, S5_TP), f32)],
        compiler_params=_cparams("parallel"),
        name="s5_mix",
    )(u_blocks, w_intra, w_state, w_cross, a_re, a_im)


def _ret_kernel(lg_ref, cd_ref, qf_ref, kf_ref, vf_ref, qb_ref, kb_ref, vb_ref, of_ref, ob_ref,
                r_ref, dec_ref, qdec_ref, kdec_ref):
    t = RET_T

    @pl.when(pl.program_id(1) == 0)
    def _():
        r_ref[...] = jnp.zeros_like(r_ref)
        n = lax.broadcasted_iota(jnp.int32, (t, t), 0)
        m = lax.broadcasted_iota(jnp.int32, (t, t), 1)
        pos = lax.broadcasted_iota(jnp.int32, (t, 1), 0)
        for d in range(2):
            diff = (n - m if d == 0 else m - n).astype(f32)
            p = (pos if d == 0 else t - 1 - pos).astype(f32)
            for h in range(RET_H):
                c = d * RET_H + h
                lg = lg_ref[c]
                dec_ref[c] = jnp.where(diff >= 0, jnp.exp(jnp.maximum(diff, 0.0) * lg), 0.0)
                qdec_ref[c] = jnp.exp((p + 1.0) * lg)
                kdec_ref[c] = jnp.exp((t - 1.0 - p) * lg)

    for d, (q_ref, k_ref, v_ref, o_ref) in enumerate(((qf_ref, kf_ref, vf_ref, of_ref),
                                                        (qb_ref, kb_ref, vb_ref, ob_ref))):
        for h in range(RET_H):
            c = d * RET_H + h
            cols = slice(h * RET_DK, (h + 1) * RET_DK)
            q = q_ref[:, cols]
            ks = k_ref[:, cols] * (RET_DK ** -0.5)
            v = v_ref[:, cols]
            r = r_ref[c]
            s = lax.dot_general(q, ks, (((1,), (1,)), ((), ())), preferred_element_type=f32) * dec_ref[c]
            inner = jnp.dot(s.astype(bf16), v, preferred_element_type=f32)
            cross = jnp.dot(q, r.astype(bf16), preferred_element_type=f32) * qdec_ref[c]
            o_ref[:, cols] = (inner + cross).astype(o_ref.dtype)
            kd = (ks.astype(f32) * kdec_ref[c]).astype(bf16)
            r_ref[c] = r * cd_ref[c] + lax.dot_general(kd, v, (((0,), (0,)), ((), ())), preferred_element_type=f32)


def _retention(p):
    e = RET_DECAY_BASE - (2.0 * jnp.arange(RET_H, dtype=f32)[None, :] + jnp.arange(2, dtype=f32)[:, None])
    log_g = jnp.log1p(-jnp.exp2(e)).reshape(RET_CHAINS)
    chunk_dec = jnp.exp(RET_T * log_g)
    width = RET_H * RET_DK
    col0 = S5_WIDTH // width

    def bwd_chunk(i):
        return jnp.where(i == 0, 0, RET_STEPS - i)

    def fwd(off):
        return pl.BlockSpec((None, RET_T, width), lambda b, i: (b, i, col0 + off))

    def bwd(off):
        return pl.BlockSpec((None, RET_T, width), lambda b, i: (b, bwd_chunk(i), col0 + off))

    smem = pl.BlockSpec(memory_space=pltpu.SMEM)
    out = jax.ShapeDtypeStruct((BATCH, L_ALL, RET_H * RET_DV), bf16)
    return pl.pallas_call(
        _ret_kernel,
        grid=(BATCH, RET_STEPS),
        in_specs=[smem, smem, fwd(0), fwd(1), fwd(2), bwd(0), bwd(1), bwd(2)],
        out_specs=[pl.BlockSpec((None, RET_T, RET_H * RET_DV), lambda b, i: (b, i, 0)),
                   pl.BlockSpec((None, RET_T, RET_H * RET_DV), lambda b, i: (b, bwd_chunk(i), 0))],
        out_shape=[out, out],
        scratch_shapes=[pltpu.VMEM((RET_CHAINS, RET_DK, RET_DV), f32),
                        pltpu.VMEM((RET_CHAINS, RET_T, RET_T), f32),
                        pltpu.VMEM((RET_CHAINS, RET_T, 1), f32),
                        pltpu.VMEM((RET_CHAINS, RET_T, 1), f32)],
        compiler_params=_cparams("parallel", "arbitrary"),
        name="retention",
    )(log_g, chunk_dec, p, p, p, p, p, p)


def _gelu_tanh(x):
    return 0.5 * x * (1.0 + jnp.tanh(math.sqrt(2.0 / math.pi) * (x + 0.044715 * (x * x * x))))


def _even_out_kernel(u_ref, gate_ref, y_ref, of_ref, ob_ref, x_ref, gt_ref, gpost_ref, dsk_ref, glub_ref,
                     gluw_ref, wo_ref, perm_ref, o_ref, z_ref, xs_ref, yf_ref):
    rows = BATCH * TT
    n_slab, n_half = S5_WIDTH // LANES, S5_TP // LANES
    for s in range(n_slab):
        for half in range(n_half):
            for b in range(BATCH):
                r0 = ((s * n_half + half) * BATCH + b) * S5_TILE_BLOCKS
                for gg in range(S5_LANE_GROUPS):
                    xs_ref[gg, r0:r0 + S5_TILE_BLOCKS, :] = y_ref[
                        s * S5_LANE_GROUPS + gg, half, pl.ds(b, S5_TILE_BLOCKS, stride=S5_BPAD), :]
    xs = jnp.concatenate([xs_ref[gg] for gg in range(S5_LANE_GROUPS)], axis=-1).astype(bf16)
    ys = jnp.dot(xs, perm_ref[...], preferred_element_type=f32)
    for s in range(n_slab):
        for half in range(n_half):
            for b in range(BATCH):
                r0 = ((s * n_half + half) * BATCH + b) * S5_TILE_BLOCKS
                for tt in range(S5_LANE_GROUPS):
                    z_ref[s, pl.ds(b * TT + half * S5_LANE_GROUPS + tt, S5_TILE_BLOCKS, stride=S5_T), :] = (
                        ys[r0:r0 + S5_TILE_BLOCKS, tt * LANES:(tt + 1) * LANES])
    u = u_ref[...].reshape(rows, S5_WIDTH)
    for s in range(S5_WIDTH // LANES):
        cols = slice(s * LANES, (s + 1) * LANES)
        yf_ref[:, cols] = _gelu_tanh(z_ref[s] + dsk_ref[:, cols] * u[:, cols].astype(f32))
    y = yf_ref[...]
    z = jnp.dot(y.astype(bf16), gluw_ref[...], preferred_element_type=f32) + glub_ref[...]
    s5_out = y * jax.nn.sigmoid(z)
    r = (of_ref[...].astype(f32) + ob_ref[...].astype(f32)).reshape(rows, RET_H * RET_DV)
    heads = []
    for h in range(RET_H):
        rh = r[:, h * RET_DV:(h + 1) * RET_DV]
        heads.append(rh * lax.rsqrt(jnp.mean(rh * rh, axis=-1, keepdims=True) + NORM_EPS))
    g = gate_ref[...].reshape(rows, RET_H * RET_DV).astype(f32)
    ret_out = jnp.concatenate(heads, axis=-1) * (g * jax.nn.sigmoid(g))
    out = jnp.dot(s5_out.astype(bf16), wo_ref[0:S5_WIDTH, :], preferred_element_type=f32)
    out = out + jnp.dot(ret_out.astype(bf16), wo_ref[S5_WIDTH:, :], preferred_element_type=f32)
    o_ref[...] = out.reshape(BATCH, TT, D_MODEL)
    _gated_residual_rows(lambda b, r0, n: o_ref[b, pl.ds(r0, n), :], x_ref, gt_ref, gpost_ref, o_ref)


def _even_out_proj(p, y_blocks, o_f, o_b, rows, mods, gains, d_skip, glu_b, glu_w, w_out, perm):
    half = S5_WIDTH
    blk_rows = S5_TILE_BLOCKS * S5_BPAD
    tile = pl.BlockSpec((BATCH, TT, half), lambda i: (0, i, 0))
    return pl.pallas_call(
        _even_out_kernel,
        grid=(N_TILES,),
        in_specs=[tile,
                  pl.BlockSpec((BATCH, TT, half), lambda i: (0, i, EVEN_IN // half - 1)),
                  pl.BlockSpec((S5_G, S5_TP // LANES, blk_rows, LANES), lambda i: (0, 0, i, 0)),
                  tile, tile,
                  pl.BlockSpec((BATCH, TT, D_MODEL), lambda i: (0, i, 0)),
                  _mod_spec(2, CTX_TILES), _gain_spec(1),
                  _const_spec((1, half)), _const_spec((1, half)),
                  _const_spec((half, half)), _const_spec((D_MODEL, D_MODEL)), _const_spec(perm.shape)],
        out_specs=pl.BlockSpec((BATCH, TT, D_MODEL), lambda i: (0, i, 0)),
        out_shape=jax.ShapeDtypeStruct((BATCH, L_ALL, D_MODEL), f32),
        scratch_shapes=[pltpu.VMEM((S5_WIDTH // LANES, BATCH * TT, LANES), f32),
                        pltpu.VMEM((S5_LANE_GROUPS, BATCH * TT, LANES), f32),
                        pltpu.VMEM((BATCH * TT, S5_WIDTH), f32)],
        compiler_params=_cparams("parallel"),
        name="even_out_proj",
    )(p, p, y_blocks, o_f, o_b, rows, mods, gains, d_skip, glu_b, glu_w, w_out, perm)


def _mlp_kernel(x_ref, gpre_ref, sh_ref, sc_ref, gt_ref, gpost_ref, w1_ref, w2_ref, o_ref, h_ref, acc_ref):
    f = pl.program_id(1)

    @pl.when(f == 0)
    def _():
        _norm_mod_rows(x_ref, gpre_ref, sc_ref, sh_ref, h_ref)
        acc_ref[...] = jnp.zeros_like(acc_ref)

    a = jnp.maximum(jnp.dot(h_ref[...], w1_ref[...], preferred_element_type=f32), 0.0)
    acc_ref[...] += jnp.dot((a * a).astype(bf16), w2_ref[...], preferred_element_type=f32)

    @pl.when(f == pl.num_programs(1) - 1)
    def _():
        _gated_residual_rows(lambda b, r0, n: acc_ref[pl.ds(pl.multiple_of(b * TT + r0, n), n), :],
                             x_ref, gt_ref, gpost_ref, o_ref)


def _mlp(x, ctx_tiles, mods, gains, w1, w2):
    tf = 1024
    n_tiles = x.shape[1] // TT
    return pl.pallas_call(
        _mlp_kernel,
        grid=(n_tiles, FFN_HIDDEN // tf),
        in_specs=[pl.BlockSpec((BATCH, TT, D_MODEL), lambda i, f: (0, i, 0)),
                  _gain_spec(2), _mod_spec(3, ctx_tiles), _mod_spec(4, ctx_tiles), _mod_spec(5, ctx_tiles),
                  _gain_spec(3),
                  pl.BlockSpec((D_MODEL, tf), lambda i, f: (0, f)),
                  pl.BlockSpec((tf, D_MODEL), lambda i, f: (f, 0))],
        out_specs=pl.BlockSpec((BATCH, TT, D_MODEL), lambda i, f: (0, i, 0)),
        out_shape=jax.ShapeDtypeStruct((BATCH, n_tiles * TT, D_MODEL), f32),
        scratch_shapes=[pltpu.VMEM((BATCH * TT, D_MODEL), bf16), pltpu.VMEM((BATCH * TT, D_MODEL), f32)],
        compiler_params=_cparams("parallel", "arbitrary"),
        name="sq_relu_mlp",
    )(x, gains, mods, mods, mods, gains, w1, w2)


def _rope_tables():
    rows = SEQ // GRID_W
    row = jnp.repeat(jnp.arange(rows, dtype=f32), GRID_W)
    col = jnp.tile(jnp.arange(GRID_W, dtype=f32), rows)
    n_freq = ATT_HD // 4
    inv_freq = ROPE_BASE ** (-jnp.arange(n_freq, dtype=f32) / n_freq)
    ang = jnp.concatenate([row[:, None] * inv_freq[None], col[:, None] * inv_freq[None]], axis=-1)
    cos, sin = jnp.cos(ang), jnp.sin(ang)
    return jnp.tile(cos, (1, 4)), jnp.concatenate([-sin, sin, -sin, sin], axis=-1)


def _rope(x, cos, sin):
    half = ATT_HD // 2
    lane = lax.broadcasted_iota(jnp.int32, x.shape, x.ndim - 1)
    partner = jnp.where((lane & (ATT_HD - 1)) < half,
                        pltpu.roll(x, LANES - half, x.ndim - 1), pltpu.roll(x, half, x.ndim - 1))
    return x * cos + partner * sin


def _odd_in_kernel(x_ref, g_ref, sh_ref, sc_ref, w_ref, cos_ref, sin_ref, q_ref, kv_ref, h_ref, *, tn):
    i = pl.program_id(0)
    j = pl.program_id(1)
    n_q_tiles = ATT_H * ATT_HD // tn
    kw = ATT_KVH * ATT_HD

    @pl.when(j == 0)
    def _():
        _norm_mod_rows(x_ref, g_ref, sc_ref, sh_ref, h_ref)

    acc = jnp.dot(h_ref[...], w_ref[...], preferred_element_type=f32)
    cos = jnp.concatenate([cos_ref[...]] * BATCH, axis=0)
    sin = jnp.concatenate([sin_ref[...]] * BATCH, axis=0)

    @pl.when(j < n_q_tiles)
    def _():
        for c in range(tn // LANES):
            cols = slice(c * LANES, (c + 1) * LANES)
            qc = _rope(acc[:, cols], cos, sin) * Q_SCALE
            q_ref[:, :, cols] = qc.reshape(BATCH, TT, LANES).astype(q_ref.dtype)

    @pl.when(j == n_q_tiles)
    def _():
        is_lat = i >= CTX_TILES
        for c in range(kw // LANES):
            cols = slice(c * LANES, (c + 1) * LANES)
            kc = acc[:, cols]
            kc = jnp.where(is_lat, _rope(kc, cos, sin), kc)
            kv_ref[:, :, cols] = kc.reshape(BATCH, TT, LANES).astype(kv_ref.dtype)
        kv_ref[:, :, kw:] = acc[:, kw:].reshape(BATCH, TT, kw).astype(kv_ref.dtype)


def _odd_in_proj(x, gains, mods, w_in, cos, sin):
    tn = 512
    qw = ATT_H * ATT_HD
    n_q_tiles = qw // tn
    pos = pl.BlockSpec((TT, LANES), lambda i, j: (jnp.maximum(i - CTX_TILES, 0), 0))
    return pl.pallas_call(
        functools.partial(_odd_in_kernel, tn=tn),
        grid=(N_TILES, ODD_IN // tn),
        in_specs=[pl.BlockSpec((BATCH, TT, D_MODEL), lambda i, j: (0, i, 0)),
                  _gain_spec(0), _mod_spec(0, CTX_TILES), _mod_spec(1, CTX_TILES),
                  pl.BlockSpec((D_MODEL, tn), lambda i, j: (0, j)), pos, pos],
        out_specs=[pl.BlockSpec((BATCH, TT, tn), lambda i, j: (0, i, jnp.minimum(j, n_q_tiles - 1))),
                   pl.BlockSpec((BATCH, TT, 2 * ATT_KVH * ATT_HD), lambda i, j: (0, i, 0))],
        out_shape=[jax.ShapeDtypeStruct((BATCH, L_ALL, qw), bf16),
                   jax.ShapeDtypeStruct((BATCH, L_ALL, 2 * ATT_KVH * ATT_HD), bf16)],
        scratch_shapes=[pltpu.VMEM((BATCH * TT, D_MODEL), bf16)],
        compiler_params=_cparams("parallel", "arbitrary"),
        name="odd_in_proj",
    )(x, gains, mods, mods, w_in, cos, sin)


def _attn_kernel(sink_ref, q_ref, kvp_ref, kvc_ref, kvn_ref, kvx_ref, o_ref):
    qb = pl.program_id(1)
    t = ATT_BLOCK
    kw = ATT_KVH * ATT_HD
    row = lax.broadcasted_iota(jnp.int32, (t, t), 0)
    col = lax.broadcasted_iota(jnp.int32, (t, t), 1)

    def band_valid(blk):
        off = col + blk * t
        kpos = qb * t - WINDOW + off
        return (jnp.abs(off - WINDOW - row) <= WINDOW) & (kpos >= 0) & (kpos < SEQ)

    valid_prev, valid_next = band_valid(0), band_valid(2)
    c0, c1, c2, c3 = CTX_LEN, CTX_LEN + t, CTX_LEN + 2 * t, CTX_LEN + 3 * t

    for kh in range(ATT_KVH):
        ks = slice(kh * ATT_HD, (kh + 1) * ATT_HD)
        vs = slice(kw + kh * ATT_HD, kw + (kh + 1) * ATT_HD)
        k_all = jnp.concatenate([kvx_ref[:, ks], kvp_ref[:, ks], kvc_ref[:, ks], kvn_ref[:, ks]], axis=0)
        v_all = jnp.concatenate([kvx_ref[:, vs], kvp_ref[:, vs], kvc_ref[:, vs], kvn_ref[:, vs]], axis=0)
        q_all = jnp.concatenate([q_ref[:, (kh * ATT_GRP + g) * ATT_HD:(kh * ATT_GRP + g + 1) * ATT_HD]
                                 for g in range(ATT_GRP)], axis=0)
        s_all = lax.dot_general(q_all, k_all, (((1,), (1,)), ((), ())), preferred_element_type=f32)
        probs, inv_den = [], []
        for g in range(ATT_GRP):
            sink = sink_ref[kh * ATT_GRP + g] * LOG2_E
            s = s_all[g * t:(g + 1) * t]
            s = jnp.concatenate([s[:, :c0], jnp.where(valid_prev, s[:, c0:c1], NEG_INF), s[:, c1:c2],
                                 jnp.where(valid_next, s[:, c2:c3], NEG_INF)], axis=-1)
            m = jnp.maximum(jnp.max(s, axis=-1, keepdims=True), sink)
            e = jnp.exp2(s - m)
            inv_den.append(1.0 / (jnp.sum(e, axis=-1, keepdims=True) + jnp.exp2(sink - m)))
            probs.append(e.astype(bf16))
        o_all = jnp.dot(jnp.concatenate(probs, axis=0), v_all, preferred_element_type=f32)
        for g in range(ATT_GRP):
            h = kh * ATT_GRP + g
            o_ref[:, h * ATT_HD:(h + 1) * ATT_HD] = (o_all[g * t:(g + 1) * t] * inv_den[g]).astype(o_ref.dtype)


def _attention(q, kv, sink):
    t = ATT_BLOCK
    nb = SEQ // t
    off = CTX_LEN // t
    kvw = 2 * ATT_KVH * ATT_HD
    return pl.pallas_call(
        _attn_kernel,
        grid=(BATCH, nb),
        in_specs=[pl.BlockSpec(memory_space=pltpu.SMEM),
                  pl.BlockSpec((None, t, ATT_H * ATT_HD), lambda b, i: (b, off + i, 0)),
                  pl.BlockSpec((None, t, kvw), lambda b, i: (b, off + jnp.maximum(i - 1, 0), 0)),
                  pl.BlockSpec((None, t, kvw), lambda b, i: (b, off + i, 0)),
                  pl.BlockSpec((None, t, kvw), lambda b, i: (b, off + jnp.minimum(i + 1, nb - 1), 0)),
                  pl.BlockSpec((None, CTX_LEN, kvw), lambda b, i: (b, 0, 0))],
        out_specs=pl.BlockSpec((None, t, ATT_H * ATT_HD), lambda b, i: (b, i, 0)),
        out_shape=jax.ShapeDtypeStruct((BATCH, SEQ, ATT_H * ATT_HD), bf16),
        compiler_params=_cparams("parallel", "parallel"),
        name="window_attention",
    )(sink, q, kv, kv, kv, kv)


def _odd_out_kernel(a_ref, x_ref, gt_ref, gpost_ref, wo_ref, o_ref):
    a = a_ref[...].reshape(BATCH * TT, D_MODEL)
    o_ref[...] = jnp.dot(a, wo_ref[...], preferred_element_type=f32).reshape(BATCH, TT, D_MODEL)
    _gated_residual_rows(lambda b, r0, n: o_ref[b, pl.ds(r0, n), :], x_ref, gt_ref, gpost_ref, o_ref)


def _odd_out_proj(a, rows, mods, gains, w_out):
    return pl.pallas_call(
        _odd_out_kernel,
        grid=(SEQ // TT,),
        in_specs=[pl.BlockSpec((BATCH, TT, D_MODEL), lambda i: (0, i, 0)),
                  pl.BlockSpec((BATCH, TT, D_MODEL), lambda i: (0, i + CTX_TILES, 0)),
                  _mod_spec(2, 0), _gain_spec(1),
                  _const_spec((D_MODEL, D_MODEL))],
        out_specs=pl.BlockSpec((BATCH, TT, D_MODEL), lambda i: (0, i, 0)),
        out_shape=jax.ShapeDtypeStruct((BATCH, SEQ, D_MODEL), f32),
        compiler_params=_cparams("parallel"),
        name="odd_out_proj",
    )(a, rows, mods, gains, w_out)


def _layer_mods(m):
    m = m.reshape(MOD_ROWS, MOD_CHUNKS, D_MODEL)
    lat = m[:BATCH]
    ctx = jnp.broadcast_to(m[CTX_MOD_ROW:CTX_MOD_ROW + 1], lat.shape)
    return jnp.stack([lat, ctx], axis=0).transpose(2, 0, 1, 3)[:, :, :, None, :]


def kernel(x, c, ctx, c_ctx, mod_w, mod_b, norm_g, mlp_w1, mlp_w2, even_w_in, even_w_out, s5_lam_re, s5_lam_im, s5_log_dt, s5_b_re, s5_b_im, s5_c_re, s5_c_im, s5_d, s5_glu_w, s5_glu_b, odd_w_in, odd_w_out, odd_sink):
    c_rows = jnp.concatenate([c, c_ctx[None, :], jnp.zeros((MOD_ROWS - BATCH - 1, D_MODEL), f32)], axis=0)
    mods_all = _modulation(c_rows, mod_w, mod_b)
    gains_all = norm_g.reshape(DEPTH, 4, 1, D_MODEL)

    mods, gains = _layer_mods(mods_all[0]), gains_all[0]
    perm = _chunk_swap_matrix()
    p, rows, u_blocks = _even_in_proj(x, ctx, gains, mods, even_w_in[0].astype(bf16), perm)
    tables = _s5_tables(s5_lam_re[0], s5_lam_im[0], s5_log_dt[0], s5_b_re[0], s5_b_im[0], s5_c_re[0], s5_c_im[0])
    y_blocks = _s5_mix(u_blocks, tables)
    o_f, o_b = _retention(p)
    rows = _even_out_proj(p, y_blocks, o_f, o_b, rows, mods, gains, s5_d[0].reshape(1, S5_WIDTH),
                          s5_glu_b[0].reshape(1, S5_WIDTH), s5_glu_w[0].astype(bf16), even_w_out[0].astype(bf16),
                          perm)
    rows = _mlp(rows, CTX_TILES, mods, gains, mlp_w1[0].astype(bf16), mlp_w2[0].astype(bf16))

    mods, gains = _layer_mods(mods_all[1]), gains_all[1]
    cos, sin = _rope_tables()
    q, kv = _odd_in_proj(rows, gains, mods, odd_w_in[0].astype(bf16), cos, sin)
    a = _attention(q, kv, odd_sink[0].astype(f32))
    lat = _odd_out_proj(a, rows, mods, gains, odd_w_out[0].astype(bf16))
    return _mlp(lat, 0, mods, gains, mlp_w1[1].astype(bf16), mlp_w2[1].astype(bf16))
```

```python
import math

import jax
import jax.numpy as jnp
from jax import lax
from jax.experimental import pallas as pl
from jax.experimental.pallas import tpu as pltpu

D_MODEL = 2048
BATCH = 4
SEQ = 4096
DEPTH = 2
GRID_W = 64
CTX_LEN = 256
MOD_CHUNKS = 6
FFN_HIDDEN = 4 * D_MODEL
NORM_EPS = 1e-6

S5_WIDTH = D_MODEL // 2
S5_P = 16
S5_G = S5_WIDTH // S5_P
S5_N = 64
LAMBDA_RE_MAX = -1e-4
RET_DK = 256
RET_H = (D_MODEL // 2) // RET_DK
RET_DV = (D_MODEL // 2) // RET_H
RET_DECAY_BASE = -5.0
EVEN_IN = S5_WIDTH + 2 * RET_H * RET_DK + 2 * RET_H * RET_DV

ATT_HD = 64
ATT_H = D_MODEL // ATT_HD
ATT_KVH = ATT_H // 8
ATT_GRP = ATT_H // ATT_KVH
ODD_IN = (ATT_H + 2 * ATT_KVH) * ATT_HD
WINDOW = 128
ATT_BLOCK = 128
ROPE_BASE = 10000.0
NEG_INF = -1e30
LOG2_E = math.log2(math.e)
Q_SCALE = ATT_HD ** -0.5 * LOG2_E

LANES = 128
SUBLANES = 8

L_ALL = CTX_LEN + SEQ
TT = 128
N_TILES = L_ALL // TT
CTX_TILES = CTX_LEN // TT
NORM_ROWS = 16
NORM_UNROLL = 4
MOD_ROWS = 8
CTX_MOD_ROW = BATCH

S5_T = 16
S5_TP = S5_T * S5_P
S5_NC = L_ALL // S5_T
S5_NC_CTX = CTX_LEN // S5_T
S5_BPAD = SUBLANES
S5_GB = 4
S5_GB_TABLES = 8
S5_TILE_BLOCKS = TT // S5_T
S5_LANE_GROUPS = LANES // S5_P
RET_T = 256
RET_STEPS = L_ALL // RET_T
RET_CHAINS = 2 * RET_H

CAST_BLOCK_ELEMS = 2 * 1024 * 1024
VMEM_LIMIT = 56 * 1024 * 1024

f32 = jnp.float32
bf16 = jnp.bfloat16


def _cparams(*sem):
    return pltpu.CompilerParams(dimension_semantics=sem, vmem_limit_bytes=VMEM_LIMIT)


def _mod_spec(chunk, ctx_tiles):
    return pl.BlockSpec((None, None, BATCH, 1, D_MODEL),
                        lambda i, *_: (chunk, jnp.where(i < ctx_tiles, 1, 0), 0, 0, 0))


def _gain_spec(k):
    return pl.BlockSpec((None, 1, D_MODEL), lambda *_: (k, 0, 0))


def _const_spec(shape):
    return pl.BlockSpec(shape, lambda *_: (0,) * len(shape), pipeline_mode=pl.Buffered(1))


def _norm_mod_rows(x_ref, g_ref, sc_ref, sh_ref, h_ref, copy_ref=None):
    for b in range(BATCH):
        gain = g_ref[...] * (1.0 + sc_ref[b])
        shift = sh_ref[b]

        def body(r, carry):
            r0 = pl.multiple_of(r * NORM_ROWS, NORM_ROWS)
            xf = x_ref[b, pl.ds(r0, NORM_ROWS), :]
            if copy_ref is not None:
                copy_ref[b, pl.ds(r0, NORM_ROWS), :] = xf
            ms = jnp.mean(xf * xf, axis=-1, keepdims=True)
            h = xf * lax.rsqrt(ms + NORM_EPS) * gain + shift
            h_ref[pl.ds(pl.multiple_of(b * TT + r0, NORM_ROWS), NORM_ROWS), :] = h.astype(h_ref.dtype)
            return carry

        lax.fori_loop(0, TT // NORM_ROWS, body, 0, unroll=NORM_UNROLL)


def _gated_residual(m, x, gate, g_post):
    ms = jnp.mean(m * m, axis=-1, keepdims=True)
    return x + m * lax.rsqrt(ms + NORM_EPS) * (g_post * gate)


def _gated_residual_rows(load_rows, x_ref, gt_ref, gpost_ref, o_ref):
    for b in range(BATCH):
        gain = gpost_ref[...] * gt_ref[b]

        def body(r, carry):
            r0 = pl.multiple_of(r * NORM_ROWS, NORM_ROWS)
            m = load_rows(b, r0, NORM_ROWS)
            ms = jnp.mean(m * m, axis=-1, keepdims=True)
            o_ref[b, pl.ds(r0, NORM_ROWS), :] = x_ref[b, pl.ds(r0, NORM_ROWS), :] + m * lax.rsqrt(ms + NORM_EPS) * gain
            return carry

        lax.fori_loop(0, TT // NORM_ROWS, body, 0, unroll=NORM_UNROLL)


def _chunk_swap_matrix():
    n = S5_LANE_GROUPS * LANES
    src = lax.broadcasted_iota(jnp.int32, (n, n), 0)
    dst = lax.broadcasted_iota(jnp.int32, (n, n), 1)
    i, j, p = src // LANES, (src // S5_P) % S5_LANE_GROUPS, src % S5_P
    return (dst == j * LANES + i * S5_P + p).astype(bf16)


def _cast_kernel(w_ref, o_ref):
    o_ref[...] = w_ref[...].astype(o_ref.dtype)


def _to_bf16(w):
    n_layers, rows, cols = w.shape
    block_rows = min(rows, pl.next_power_of_2(CAST_BLOCK_ELEMS // cols + 1) // 2)
    spec = pl.BlockSpec((None, block_rows, cols), lambda l, r: (l, r, 0))
    return pl.pallas_call(
        _cast_kernel,
        grid=(n_layers, rows // block_rows),
        in_specs=[spec],
        out_specs=spec,
        out_shape=jax.ShapeDtypeStruct(w.shape, bf16),
        compiler_params=_cparams("parallel", "parallel"),
        name="to_bf16",
    )(w)


def _mod_kernel(c_ref, w_ref, b_ref, o_ref):
    c = c_ref[...]
    s = c * jax.nn.sigmoid(c)
    o_ref[...] = jnp.dot(s.astype(bf16), w_ref[...].astype(bf16), preferred_element_type=f32) + b_ref[...]


def _modulation(c_rows, mod_w, mod_b):
    tn = 1024
    n = MOD_CHUNKS * D_MODEL
    return pl.pallas_call(
        _mod_kernel,
        grid=(DEPTH, n // tn),
        in_specs=[pl.BlockSpec((MOD_ROWS, D_MODEL), lambda l, j: (0, 0)),
                  pl.BlockSpec((None, D_MODEL, tn), lambda l, j: (l, 0, j)),
                  pl.BlockSpec((None, 1, tn), lambda l, j: (l, 0, j))],
        out_specs=pl.BlockSpec((None, MOD_ROWS, tn), lambda l, j: (l, 0, j)),
        out_shape=jax.ShapeDtypeStruct((DEPTH, MOD_ROWS, n), f32),
        compiler_params=_cparams("parallel", "parallel"),
        name="ada_modulation",
    )(c_rows, mod_w, mod_b.reshape(DEPTH, 1, n))


def _even_in_kernel(x_ref, c_ref, g_ref, sh_ref, sc_ref, w_ref, perm_ref, p_ref, rows_ref, ub_ref,
                    h_ref, a_ref, xs_ref, st_ref):
    i = pl.program_id(0)
    j = pl.program_id(1)

    @pl.when(jnp.logical_and(j == 0, i < CTX_TILES))
    def _():
        _norm_mod_rows(c_ref, g_ref, sc_ref, sh_ref, h_ref, copy_ref=rows_ref)

    @pl.when(jnp.logical_and(j == 0, i >= CTX_TILES))
    def _():
        _norm_mod_rows(x_ref, g_ref, sc_ref, sh_ref, h_ref, copy_ref=rows_ref)

    acc = jnp.dot(h_ref[...], w_ref[...], preferred_element_type=f32)
    p_ref[...] = acc.reshape(BATCH, TT, acc.shape[-1]).astype(p_ref.dtype)

    @pl.when(j == 0)
    def _():
        n_slab, n_half = S5_WIDTH // LANES, S5_TP // LANES
        for s in range(n_slab):
            a_ref[s] = acc[:, s * LANES:(s + 1) * LANES]
        for s in range(n_slab):
            for half in range(n_half):
                for b in range(BATCH):
                    r0 = ((s * n_half + half) * BATCH + b) * S5_TILE_BLOCKS
                    for tt in range(S5_LANE_GROUPS):
                        xs_ref[tt, r0:r0 + S5_TILE_BLOCKS, :] = a_ref[
                            s, pl.ds(b * TT + half * S5_LANE_GROUPS + tt, S5_TILE_BLOCKS, stride=S5_T), :]
        xs = jnp.concatenate([xs_ref[tt] for tt in range(S5_LANE_GROUPS)], axis=-1).astype(bf16)
        ys = jnp.dot(xs, perm_ref[...], preferred_element_type=f32)
        st_ref[...] = jnp.zeros_like(st_ref)
        for s in range(n_slab):
            for half in range(n_half):
                for b in range(BATCH):
                    r0 = ((s * n_half + half) * BATCH + b) * S5_TILE_BLOCKS
                    for gg in range(S5_LANE_GROUPS):
                        st_ref[s * S5_LANE_GROUPS + gg, half, pl.ds(b, S5_TILE_BLOCKS, stride=S5_BPAD), :] = (
                            ys[r0:r0 + S5_TILE_BLOCKS, gg * LANES:(gg + 1) * LANES])
        for g in range(S5_G):
            ub_ref[g] = jnp.concatenate([st_ref[g, 0], st_ref[g, 1]], axis=-1).astype(ub_ref.dtype)


def _even_in_proj(x, ctx, gains, mods, w_in, perm):
    tn = S5_WIDTH
    blk_rows = S5_TILE_BLOCKS * S5_BPAD
    return pl.pallas_call(
        _even_in_kernel,
        grid=(N_TILES, EVEN_IN // tn),
        in_specs=[pl.BlockSpec((BATCH, TT, D_MODEL), lambda i, j: (0, jnp.maximum(i - CTX_TILES, 0), 0)),
                  pl.BlockSpec((BATCH, TT, D_MODEL), lambda i, j: (0, jnp.minimum(i, CTX_TILES - 1), 0)),
                  _gain_spec(0), _mod_spec(0, CTX_TILES), _mod_spec(1, CTX_TILES),
                  pl.BlockSpec((D_MODEL, tn), lambda i, j: (0, j)),
                  _const_spec(perm.shape)],
        out_specs=[pl.BlockSpec((BATCH, TT, tn), lambda i, j: (0, i, j)),
                   pl.BlockSpec((BATCH, TT, D_MODEL), lambda i, j: (0, i, 0)),
                   pl.BlockSpec((S5_G, blk_rows, S5_TP), lambda i, j: (0, i, 0))],
        out_shape=[jax.ShapeDtypeStruct((BATCH, L_ALL, EVEN_IN), bf16),
                   jax.ShapeDtypeStruct((BATCH, L_ALL, D_MODEL), f32),
                   jax.ShapeDtypeStruct((S5_G, S5_NC * S5_BPAD, S5_TP), bf16)],
        scratch_shapes=[pltpu.VMEM((BATCH * TT, D_MODEL), bf16),
                        pltpu.VMEM((S5_WIDTH // LANES, BATCH * TT, LANES), f32),
                        pltpu.VMEM((S5_LANE_GROUPS, BATCH * TT, LANES), f32),
                        pltpu.VMEM((S5_G, S5_TP // LANES, blk_rows, LANES), f32)],
        compiler_params=_cparams("parallel", "arbitrary"),
        name="even_in_proj",
    )(x, ctx, gains, mods, mods, w_in, perm)


def _s5_tables(lam_re, lam_im, log_dt, b_re, b_im, c_re, c_im):
    lr = jnp.minimum(lam_re.astype(f32), LAMBDA_RE_MAX)
    li = lam_im.astype(f32)
    dt = jnp.exp(log_dt.astype(f32))[..., None]
    zr, zi = lr * dt, li * dt
    ab_mag = jnp.exp(zr)
    ab_re, ab_im = ab_mag * jnp.cos(zi), ab_mag * jnp.sin(zi)
    den = lr * lr + li * li
    nr = ab_re - 1.0
    f_re = (nr * lr + ab_im * li) / den
    f_im = (ab_im * lr - nr * li) / den
    br, bi = b_re.astype(f32), b_im.astype(f32)
    bb_re = f_re[..., None] * br - f_im[..., None] * bi
    bb_im = f_re[..., None] * bi + f_im[..., None] * br
    steps = jnp.arange(S5_T + 1, dtype=f32)[:, None]
    zr_k, zi_k = zr[:, :, None, :] * steps, zi[:, :, None, :] * steps
    pw_mag = jnp.exp(zr_k)
    pw_re, pw_im = pw_mag * jnp.cos(zi_k), pw_mag * jnp.sin(zi_k)
    cr, ci = c_re.astype(f32), c_im.astype(f32)
    bt_re, bt_im = bb_re.transpose(0, 1, 3, 2), bb_im.transpose(0, 1, 3, 2)

    kf, kb = slice(0, S5_T), slice(S5_T - 1, None, -1)
    ca_re = cr[:, :, None] * pw_re[:, :, :, None, :] - ci[:, :, None] * pw_im[:, :, :, None, :]
    ca_im = cr[:, :, None] * pw_im[:, :, :, None, :] + ci[:, :, None] * pw_re[:, :, :, None, :]
    ca_f = jnp.concatenate([ca_re[0, :, kf], ca_im[0, :, kf]], axis=-1).reshape(S5_G, S5_TP, 2 * S5_N)
    ca_b = jnp.concatenate([ca_re[1, :, kb], ca_im[1, :, kb]], axis=-1).reshape(S5_G, S5_TP, 2 * S5_N)
    bcat = jnp.concatenate([bt_re, -bt_im], axis=-1)
    w_intra = _s5_intra_matrix(bcat[0], bcat[1], ca_f, ca_b)

    def state_in(d, k):
        pr, pi = pw_re[d][:, k][:, :, None, :], pw_im[d][:, k][:, :, None, :]
        br_, bi_ = bt_re[d][:, None], bt_im[d][:, None]
        return ((pr * br_ - pi * bi_).reshape(S5_G, S5_TP, S5_N), (pr * bi_ + pi * br_).reshape(S5_G, S5_TP, S5_N))

    sf_re, sf_im = state_in(0, kb)
    sb_re, sb_im = state_in(1, kf)
    w_state = jnp.concatenate([sf_re, sb_re, sf_im, sb_im], axis=-1)

    ct_re, ct_im = cr.transpose(0, 1, 3, 2), ci.transpose(0, 1, 3, 2)
    pt_re, pt_im = pw_re.transpose(0, 1, 3, 2), pw_im.transpose(0, 1, 3, 2)

    def state_out(d, k):
        pr, pi = pt_re[d][:, :, k][..., None], pt_im[d][:, :, k][..., None]
        cr_, ci_ = ct_re[d][:, :, None, :], ct_im[d][:, :, None, :]
        return ((cr_ * pr - ci_ * pi).reshape(S5_G, S5_N, S5_TP), (cr_ * pi + ci_ * pr).reshape(S5_G, S5_N, S5_TP))

    xf_re, xf_im = state_out(0, slice(1, S5_T + 1))
    xb_re, xb_im = state_out(1, slice(S5_T, 0, -1))
    w_cross = jnp.concatenate([xf_re, xb_re, -xf_im, -xb_im], axis=1)
    a_re = jnp.concatenate([pw_re[0, :, S5_T], pw_re[1, :, S5_T]], axis=-1).reshape(S5_G, 1, 2 * S5_N)
    a_im = jnp.concatenate([pw_im[0, :, S5_T], pw_im[1, :, S5_T]], axis=-1).reshape(S5_G, 1, 2 * S5_N)
    return w_intra, w_state.astype(bf16), w_cross.astype(bf16), a_re, a_im


def _s5_intra_kernel(bf_ref, bb_ref, caf_ref, cab_ref, w_ref):
    lane = lax.broadcasted_iota(jnp.int32, (S5_P, S5_TP), 1)
    dn = (((1,), (1,)), ((), ()))
    for g in range(S5_GB_TABLES):
        tq_f = lax.dot_general(bf_ref[g], caf_ref[g], dn, precision=lax.Precision.HIGHEST, preferred_element_type=f32)
        tq_b = lax.dot_general(bb_ref[g], cab_ref[g], dn, precision=lax.Precision.HIGHEST, preferred_element_type=f32)
        for s in range(S5_T):
            sf, sb = S5_P * s, S5_P * (S5_T - 1 - s)
            row_f = tq_f if s == 0 else jnp.where(lane >= sf, pltpu.roll(tq_f, sf, 1), 0.0)
            row_b = tq_b if sb == 0 else jnp.where(lane < S5_TP - sb, pltpu.roll(tq_b, S5_TP - sb, 1), 0.0)
            w_ref[g, s * S5_P:(s + 1) * S5_P, :] = (row_f + row_b).astype(w_ref.dtype)


def _s5_intra_matrix(b_f, b_b, ca_f, ca_b):
    def spec(rows, cols):
        return pl.BlockSpec((S5_GB_TABLES, rows, cols), lambda i: (i, 0, 0))

    return pl.pallas_call(
        _s5_intra_kernel,
        grid=(S5_G // S5_GB_TABLES,),
        in_specs=[spec(S5_P, 2 * S5_N), spec(S5_P, 2 * S5_N), spec(S5_TP, 2 * S5_N), spec(S5_TP, 2 * S5_N)],
        out_specs=spec(S5_TP, S5_TP),
        out_shape=jax.ShapeDtypeStruct((S5_G, S5_TP, S5_TP), bf16),
        compiler_params=_cparams("parallel"),
        name="s5_intra_matrix",
    )(b_f, b_b, ca_f, ca_b)


def _s5_kernel(u_ref, wi_ref, ws_ref, wc_ref, are_ref, aim_ref, y_ref, s_ref):
    half = 2 * S5_N
    for g in range(S5_GB):
        s_ref[g] = jnp.dot(u_ref[g], ws_ref[g], preferred_element_type=f32)

    fwd_lane = lax.broadcasted_iota(jnp.int32, (S5_BPAD, half), 1) < S5_N
    a_re = [jnp.broadcast_to(are_ref[g], (S5_BPAD, half)) for g in range(S5_GB)]
    a_im = [jnp.broadcast_to(aim_ref[g], (S5_BPAD, half)) for g in range(S5_GB)]

    def step(i, carry):
        cb = jnp.where(i < S5_NC_CTX, S5_NC_CTX - 1 - i, S5_NC + S5_NC_CTX - 1 - i)
        rf = pl.multiple_of(i * S5_BPAD, S5_BPAD)
        rb = pl.multiple_of(cb * S5_BPAD, S5_BPAD)
        out = []
        for g in range(S5_GB):
            h_re, h_im = carry[2 * g], carry[2 * g + 1]
            l_re = jnp.where(fwd_lane, s_ref[g, pl.ds(rf, S5_BPAD), 0:half], s_ref[g, pl.ds(rb, S5_BPAD), 0:half])
            l_im = jnp.where(fwd_lane, s_ref[g, pl.ds(rf, S5_BPAD), half:2 * half],
                             s_ref[g, pl.ds(rb, S5_BPAD), half:2 * half])
            s_ref[g, pl.ds(rf, S5_BPAD), 0:S5_N] = h_re[:, 0:S5_N]
            s_ref[g, pl.ds(rb, S5_BPAD), S5_N:half] = h_re[:, S5_N:half]
            s_ref[g, pl.ds(rf, S5_BPAD), half:half + S5_N] = h_im[:, 0:S5_N]
            s_ref[g, pl.ds(rb, S5_BPAD), half + S5_N:2 * half] = h_im[:, S5_N:half]
            out.append(a_re[g] * h_re - a_im[g] * h_im + l_re)
            out.append(a_re[g] * h_im + a_im[g] * h_re + l_im)
        return tuple(out)

    zero = jnp.zeros((S5_BPAD, half), f32)
    lax.fori_loop(0, S5_NC, step, (zero,) * (2 * S5_GB))

    for g in range(S5_GB):
        y = jnp.dot(u_ref[g], wi_ref[g], preferred_element_type=f32)
        y = y + jnp.dot(s_ref[g].astype(bf16), wc_ref[g], preferred_element_type=f32)
        for hf in range(S5_TP // LANES):
            y_ref[g, hf] = y[:, hf * LANES:(hf + 1) * LANES]


def _s5_mix(u_blocks, tables):
    w_intra, w_state, w_cross, a_re, a_im = tables
    rows = S5_NC * S5_BPAD
    wsp = pl.BlockSpec((S5_GB, S5_TP, S5_TP), lambda g: (g, 0, 0))
    asp = pl.BlockSpec((S5_GB, 1, 2 * S5_N), lambda g: (g, 0, 0))
    return pl.pallas_call(
        _s5_kernel,
        grid=(S5_G // S5_GB,),
        in_specs=[pl.BlockSpec((S5_GB, rows, S5_TP), lambda g: (g, 0, 0)), wsp, wsp, wsp, asp, asp],
        out_specs=pl.BlockSpec((S5_GB, S5_TP // LANES, rows, LANES), lambda g: (g, 0, 0, 0)),
        out_shape=jax.ShapeDtypeStruct((S5_G, S5_TP // LANES, rows, LANES), f32),
        scratch_shapes=[pltpu.VMEM((S5_GB, rows, S5_TP), f32)],
        compiler_params=_cparams("parallel"),
        name="s5_mix",
    )(u_blocks, w_intra, w_state, w_cross, a_re, a_im)


def _ret_kernel(lg_ref, cd_ref, qf_ref, kf_ref, vf_ref, qb_ref, kb_ref, vb_ref, of_ref, ob_ref,
                r_ref, dec_ref, qdec_ref, kdec_ref):
    t = RET_T

    @pl.when(pl.program_id(1) == 0)
    def _():
        r_ref[...] = jnp.zeros_like(r_ref)
        n = lax.broadcasted_iota(jnp.int32, (t, t), 0)
        m = lax.broadcasted_iota(jnp.int32, (t, t), 1)
        pos = lax.broadcasted_iota(jnp.int32, (t, 1), 0)
        for d in range(2):
            diff = (n - m if d == 0 else m - n).astype(f32)
            p = (pos if d == 0 else t - 1 - pos).astype(f32)
            for h in range(RET_H):
                c = d * RET_H + h
                lg = lg_ref[c]
                dec_ref[c] = jnp.where(diff >= 0, jnp.exp(jnp.maximum(diff, 0.0) * lg), 0.0)
                qdec_ref[c] = jnp.exp((p + 1.0) * lg)
                kdec_ref[c] = jnp.exp((t - 1.0 - p) * lg)

    for d, (q_ref, k_ref, v_ref, o_ref) in enumerate(((qf_ref, kf_ref, vf_ref, of_ref),
                                                        (qb_ref, kb_ref, vb_ref, ob_ref))):
        for h in range(RET_H):
            c = d * RET_H + h
            cols = slice(h * RET_DK, (h + 1) * RET_DK)
            q = q_ref[:, cols]
            ks = k_ref[:, cols] * (RET_DK ** -0.5)
            v = v_ref[:, cols]
            r = r_ref[c]
            s = lax.dot_general(q, ks, (((1,), (1,)), ((), ())), preferred_element_type=f32) * dec_ref[c]
            inner = jnp.dot(s.astype(bf16), v, preferred_element_type=f32)
            cross = jnp.dot(q, r.astype(bf16), preferred_element_type=f32) * qdec_ref[c]
            o_ref[:, cols] = (inner + cross).astype(o_ref.dtype)
            kd = (ks.astype(f32) * kdec_ref[c]).astype(bf16)
            r_ref[c] = r * cd_ref[c] + lax.dot_general(kd, v, (((0,), (0,)), ((), ())), preferred_element_type=f32)


def _retention(p):
    e = RET_DECAY_BASE - (2.0 * jnp.arange(RET_H, dtype=f32)[None, :] + jnp.arange(2, dtype=f32)[:, None])
    log_g = jnp.log1p(-jnp.exp2(e)).reshape(RET_CHAINS)
    chunk_dec = jnp.exp(RET_T * log_g)
    width = RET_H * RET_DK
    col0 = S5_WIDTH // width

    def bwd_chunk(i):
        return jnp.where(i == 0, 0, RET_STEPS - i)

    def fwd(off):
        return pl.BlockSpec((None, RET_T, width), lambda b, i: (b, i, col0 + off))

    def bwd(off):
        return pl.BlockSpec((None, RET_T, width), lambda b, i: (b, bwd_chunk(i), col0 + off))

    smem = pl.BlockSpec(memory_space=pltpu.SMEM)
    out = jax.ShapeDtypeStruct((BATCH, L_ALL, RET_H * RET_DV), bf16)
    return pl.pallas_call(
        _ret_kernel,
        grid=(BATCH, RET_STEPS),
        in_specs=[smem, smem, fwd(0), fwd(1), fwd(2), bwd(0), bwd(1), bwd(2)],
        out_specs=[pl.BlockSpec((None, RET_T, RET_H * RET_DV), lambda b, i: (b, i, 0)),
                   pl.BlockSpec((None, RET_T, RET_H * RET_DV), lambda b, i: (b, bwd_chunk(i), 0))],
        out_shape=[out, out],
        scratch_shapes=[pltpu.VMEM((RET_CHAINS, RET_DK, RET_DV), f32),
                        pltpu.VMEM((RET_CHAINS, RET_T, RET_T), f32),
                        pltpu.VMEM((RET_CHAINS, RET_T, 1), f32),
                        pltpu.VMEM((RET_CHAINS, RET_T, 1), f32)],
        compiler_params=_cparams("parallel", "arbitrary"),
        name="retention",
    )(log_g, chunk_dec, p, p, p, p, p, p)


def _gelu_tanh(x):
    return 0.5 * x * (1.0 + jnp.tanh(math.sqrt(2.0 / math.pi) * (x + 0.044715 * (x * x * x))))


def _even_out_kernel(u_ref, gate_ref, y_ref, of_ref, ob_ref, x_ref, gt_ref, gpost_ref, dsk_ref, glub_ref,
                     gluw_ref, wo_ref, perm_ref, o_ref, z_ref, xs_ref, yf_ref):
    rows = BATCH * TT
    n_slab, n_half = S5_WIDTH // LANES, S5_TP // LANES
    for s in range(n_slab):
        for half in range(n_half):
            for b in range(BATCH):
                r0 = ((s * n_half + half) * BATCH + b) * S5_TILE_BLOCKS
                for gg in range(S5_LANE_GROUPS):
                    xs_ref[gg, r0:r0 + S5_TILE_BLOCKS, :] = y_ref[
                        s * S5_LANE_GROUPS + gg, half, pl.ds(b, S5_TILE_BLOCKS, stride=S5_BPAD), :]
    xs = jnp.concatenate([xs_ref[gg] for gg in range(S5_LANE_GROUPS)], axis=-1).astype(bf16)
    ys = jnp.dot(xs, perm_ref[...], preferred_element_type=f32)
    for s in range(n_slab):
        for half in range(n_half):
            for b in range(BATCH):
                r0 = ((s * n_half + half) * BATCH + b) * S5_TILE_BLOCKS
                for tt in range(S5_LANE_GROUPS):
                    z_ref[s, pl.ds(b * TT + half * S5_LANE_GROUPS + tt, S5_TILE_BLOCKS, stride=S5_T), :] = (
                        ys[r0:r0 + S5_TILE_BLOCKS, tt * LANES:(tt + 1) * LANES])
    u = u_ref[...].reshape(rows, S5_WIDTH)
    for s in range(S5_WIDTH // LANES):
        cols = slice(s * LANES, (s + 1) * LANES)
        yf_ref[:, cols] = _gelu_tanh(z_ref[s] + dsk_ref[:, cols] * u[:, cols].astype(f32))
    y = yf_ref[...]
    z = jnp.dot(y.astype(bf16), gluw_ref[...], preferred_element_type=f32) + glub_ref[...]
    s5_out = y * jax.nn.sigmoid(z)
    r = (of_ref[...].astype(f32) + ob_ref[...].astype(f32)).reshape(rows, RET_H * RET_DV)
    heads = []
    for h in range(RET_H):
        rh = r[:, h * RET_DV:(h + 1) * RET_DV]
        heads.append(rh * lax.rsqrt(jnp.mean(rh * rh, axis=-1, keepdims=True) + NORM_EPS))
    g = gate_ref[...].reshape(rows, RET_H * RET_DV).astype(f32)
    ret_out = jnp.concatenate(heads, axis=-1) * (g * jax.nn.sigmoid(g))
    out = jnp.dot(s5_out.astype(bf16), wo_ref[0:S5_WIDTH, :], preferred_element_type=f32)
    out = out + jnp.dot(ret_out.astype(bf16), wo_ref[S5_WIDTH:, :], preferred_element_type=f32)
    o_ref[...] = _gated_residual(out.reshape(BATCH, TT, D_MODEL), x_ref[...], gt_ref[...], gpost_ref[...])


def _even_out_proj(p, y_blocks, o_f, o_b, rows, mods, gains, d_skip, glu_b, glu_w, w_out, perm):
    half = S5_WIDTH
    blk_rows = S5_TILE_BLOCKS * S5_BPAD
    tile = pl.BlockSpec((BATCH, TT, half), lambda i: (0, i, 0))
    return pl.pallas_call(
        _even_out_kernel,
        grid=(N_TILES,),
        in_specs=[tile,
                  pl.BlockSpec((BATCH, TT, half), lambda i: (0, i, EVEN_IN // half - 1)),
                  pl.BlockSpec((S5_G, S5_TP // LANES, blk_rows, LANES), lambda i: (0, 0, i, 0)),
                  tile, tile,
                  pl.BlockSpec((BATCH, TT, D_MODEL), lambda i: (0, i, 0)),
                  _mod_spec(2, CTX_TILES), _gain_spec(1),
                  _const_spec((1, half)), _const_spec((1, half)),
                  _const_spec((half, half)), _const_spec((D_MODEL, D_MODEL)), _const_spec(perm.shape)],
        out_specs=pl.BlockSpec((BATCH, TT, D_MODEL), lambda i: (0, i, 0)),
        out_shape=jax.ShapeDtypeStruct((BATCH, L_ALL, D_MODEL), f32),
        scratch_shapes=[pltpu.VMEM((S5_WIDTH // LANES, BATCH * TT, LANES), f32),
                        pltpu.VMEM((S5_LANE_GROUPS, BATCH * TT, LANES), f32),
                        pltpu.VMEM((BATCH * TT, S5_WIDTH), f32)],
        compiler_params=_cparams("parallel"),
        name="even_out_proj",
    )(p, p, y_blocks, o_f, o_b, rows, mods, gains, d_skip, glu_b, glu_w, w_out, perm)


def _mlp_kernel(x_ref, gpre_ref, sh_ref, sc_ref, gt_ref, gpost_ref, w1_ref, w2_ref, o_ref, h_ref, acc_ref):
    f = pl.program_id(1)

    @pl.when(f == 0)
    def _():
        _norm_mod_rows(x_ref, gpre_ref, sc_ref, sh_ref, h_ref)
        acc_ref[...] = jnp.zeros_like(acc_ref)

    a = jnp.maximum(jnp.dot(h_ref[...], w1_ref[...], preferred_element_type=f32), 0.0)
    acc_ref[...] += jnp.dot((a * a).astype(bf16), w2_ref[...], preferred_element_type=f32)

    @pl.when(f == pl.num_programs(1) - 1)
    def _():
        _gated_residual_rows(lambda b, r0, n: acc_ref[pl.ds(pl.multiple_of(b * TT + r0, n), n), :],
                             x_ref, gt_ref, gpost_ref, o_ref)


def _mlp(x, ctx_tiles, mods, gains, w1, w2, layer):
    tf = 1024
    n_tiles = x.shape[1] // TT
    return pl.pallas_call(
        _mlp_kernel,
        grid=(n_tiles, FFN_HIDDEN // tf),
        in_specs=[pl.BlockSpec((BATCH, TT, D_MODEL), lambda i, f: (0, i, 0)),
                  _gain_spec(2), _mod_spec(3, ctx_tiles), _mod_spec(4, ctx_tiles), _mod_spec(5, ctx_tiles),
                  _gain_spec(3),
                  pl.BlockSpec((None, D_MODEL, tf), lambda i, f: (layer, 0, f)),
                  pl.BlockSpec((None, tf, D_MODEL), lambda i, f: (layer, f, 0))],
        out_specs=pl.BlockSpec((BATCH, TT, D_MODEL), lambda i, f: (0, i, 0)),
        out_shape=jax.ShapeDtypeStruct((BATCH, n_tiles * TT, D_MODEL), f32),
        scratch_shapes=[pltpu.VMEM((BATCH * TT, D_MODEL), bf16), pltpu.VMEM((BATCH * TT, D_MODEL), f32)],
        compiler_params=_cparams("parallel", "arbitrary"),
        name="sq_relu_mlp",
    )(x, gains, mods, mods, mods, gains, w1, w2)


def _rope_tables():
    rows = SEQ // GRID_W
    row = jnp.repeat(jnp.arange(rows, dtype=f32), GRID_W)
    col = jnp.tile(jnp.arange(GRID_W, dtype=f32), rows)
    n_freq = ATT_HD // 4
    inv_freq = ROPE_BASE ** (-jnp.arange(n_freq, dtype=f32) / n_freq)
    ang = jnp.concatenate([row[:, None] * inv_freq[None], col[:, None] * inv_freq[None]], axis=-1)
    cos, sin = jnp.cos(ang), jnp.sin(ang)
    cos, sin = jnp.tile(cos, (1, 4)), jnp.concatenate([-sin, sin, -sin, sin], axis=-1)
    return (jnp.stack([cos * Q_SCALE, cos, jnp.ones_like(cos)]),
            jnp.stack([sin * Q_SCALE, sin, jnp.zeros_like(sin)]))


ROPE_Q, ROPE_K, ROPE_NONE = 0, 1, 2


def _rope(x, cos, sin):
    half = ATT_HD // 2
    lane = lax.broadcasted_iota(jnp.int32, x.shape, x.ndim - 1)
    partner = jnp.where((lane & (ATT_HD - 1)) < half,
                        pltpu.roll(x, LANES - half, x.ndim - 1), pltpu.roll(x, half, x.ndim - 1))
    return x * cos + partner * sin


def _odd_in_kernel(x_ref, g_ref, sh_ref, sc_ref, w_ref, cos_lo_ref, sin_lo_ref, cos_hi_ref, sin_hi_ref,
                   o_ref, h_ref):
    @pl.when(pl.program_id(1) == 0)
    def _():
        _norm_mod_rows(x_ref, g_ref, sc_ref, sh_ref, h_ref)

    acc = jnp.dot(h_ref[...], w_ref[...], preferred_element_type=f32)
    n_groups = acc.shape[-1] // LANES
    for c in range(n_groups):
        cos_ref, sin_ref = (cos_lo_ref, sin_lo_ref) if c < n_groups // 2 else (cos_hi_ref, sin_hi_ref)
        xc = acc[:, c * LANES:(c + 1) * LANES].reshape(BATCH, TT, LANES)
        o_ref[:, :, c * LANES:(c + 1) * LANES] = _rope(xc, cos_ref[...], sin_ref[...]).astype(o_ref.dtype)


def _odd_in_proj(x, gains, mods, w_in, cos, sin):
    tn = 2 * ATT_KVH * ATT_HD
    n_q_tiles = ATT_H * ATT_HD // tn

    def table(kind_of):
        return pl.BlockSpec((None, TT, LANES), lambda i, j: (kind_of(i, j), jnp.maximum(i - CTX_TILES, 0), 0))

    def lo_kind(i, j):
        return jnp.where(j < n_q_tiles, ROPE_Q, jnp.where(i >= CTX_TILES, ROPE_K, ROPE_NONE))

    def hi_kind(i, j):
        return jnp.where(j < n_q_tiles, ROPE_Q, ROPE_NONE)

    return pl.pallas_call(
        _odd_in_kernel,
        grid=(N_TILES, ODD_IN // tn),
        in_specs=[pl.BlockSpec((BATCH, TT, D_MODEL), lambda i, j: (0, i, 0)),
                  _gain_spec(0), _mod_spec(0, CTX_TILES), _mod_spec(1, CTX_TILES),
                  pl.BlockSpec((D_MODEL, tn), lambda i, j: (0, j)),
                  table(lo_kind), table(lo_kind), table(hi_kind), table(hi_kind)],
        out_specs=pl.BlockSpec((BATCH, TT, tn), lambda i, j: (0, i, j)),
        out_shape=jax.ShapeDtypeStruct((BATCH, L_ALL, ODD_IN), bf16),
        scratch_shapes=[pltpu.VMEM((BATCH * TT, D_MODEL), bf16)],
        compiler_params=_cparams("parallel", "arbitrary"),
        name="odd_in_proj",
    )(x, gains, mods, mods, w_in, cos, sin, cos, sin)


def _attn_kernel(sink_ref, q_ref, kvp_ref, kvc_ref, kvn_ref, kvx_ref, o_ref, p_ref):
    qb = pl.program_id(1)
    t = ATT_BLOCK
    kw = ATT_KVH * ATT_HD
    row = lax.broadcasted_iota(jnp.int32, (t, t), 0)
    col = lax.broadcasted_iota(jnp.int32, (t, t), 1)

    def band_valid(blk):
        off = col + blk * t
        kpos = qb * t - WINDOW + off
        return (jnp.abs(off - WINDOW - row) <= WINDOW) & (kpos >= 0) & (kpos < SEQ)

    n_ctx_chunks = CTX_LEN // LANES
    masks = {n_ctx_chunks: band_valid(0), n_ctx_chunks + 2: band_valid(2)}
    n_chunks = n_ctx_chunks + 3

    for kh in range(ATT_KVH):
        ks = slice(kh * ATT_HD, (kh + 1) * ATT_HD)
        vs = slice(kw + kh * ATT_HD, kw + (kh + 1) * ATT_HD)
        k_all = jnp.concatenate([kvx_ref[:, ks], kvp_ref[:, ks], kvc_ref[:, ks], kvn_ref[:, ks]], axis=0)
        v_all = jnp.concatenate([kvx_ref[:, vs], kvp_ref[:, vs], kvc_ref[:, vs], kvn_ref[:, vs]], axis=0)
        q_all = jnp.concatenate([q_ref[:, (kh * ATT_GRP + g) * ATT_HD:(kh * ATT_GRP + g + 1) * ATT_HD]
                                 for g in range(ATT_GRP)], axis=0)
        s_all = lax.dot_general(q_all, k_all, (((1,), (1,)), ((), ())), preferred_element_type=f32)
        inv_den = []
        for g in range(ATT_GRP):
            sink = sink_ref[kh * ATT_GRP + g] * LOG2_E
            rows = slice(g * t, (g + 1) * t)

            def chunk(c):
                s = s_all[rows, c * LANES:(c + 1) * LANES]
                return jnp.where(masks[c], s, NEG_INF) if c in masks else s

            m_lanes = chunk(0)
            for c in range(1, n_chunks):
                m_lanes = jnp.maximum(m_lanes, chunk(c))
            m = jnp.maximum(jnp.max(m_lanes, axis=-1, keepdims=True), sink)
            e_lanes = None
            for c in range(n_chunks):
                e = jnp.exp2(chunk(c) - m)
                p_ref[rows, c * LANES:(c + 1) * LANES] = e.astype(p_ref.dtype)
                e_lanes = e if e_lanes is None else e_lanes + e
            inv_den.append(1.0 / (jnp.sum(e_lanes, axis=-1, keepdims=True) + jnp.exp2(sink - m)))
        o_all = jnp.dot(p_ref[...], v_all, preferred_element_type=f32)
        for g in range(ATT_GRP):
            h = kh * ATT_GRP + g
            o_ref[:, h * ATT_HD:(h + 1) * ATT_HD] = (o_all[g * t:(g + 1) * t] * inv_den[g]).astype(o_ref.dtype)


def _attention(qkv, sink):
    t = ATT_BLOCK
    nb = SEQ // t
    off = CTX_LEN // t
    qw = ATT_H * ATT_HD
    kvw = 2 * ATT_KVH * ATT_HD
    kv_col = qw // kvw
    return pl.pallas_call(
        _attn_kernel,
        grid=(BATCH, nb),
        in_specs=[pl.BlockSpec(memory_space=pltpu.SMEM),
                  pl.BlockSpec((None, t, qw), lambda b, i: (b, off + i, 0)),
                  pl.BlockSpec((None, t, kvw), lambda b, i: (b, off + jnp.maximum(i - 1, 0), kv_col)),
                  pl.BlockSpec((None, t, kvw), lambda b, i: (b, off + i, kv_col)),
                  pl.BlockSpec((None, t, kvw), lambda b, i: (b, off + jnp.minimum(i + 1, nb - 1), kv_col)),
                  pl.BlockSpec((None, CTX_LEN, kvw), lambda b, i: (b, 0, kv_col))],
        out_specs=pl.BlockSpec((None, t, qw), lambda b, i: (b, i, 0)),
        out_shape=jax.ShapeDtypeStruct((BATCH, SEQ, qw), bf16),
        scratch_shapes=[pltpu.VMEM((ATT_GRP * t, CTX_LEN + 3 * t), bf16)],
        compiler_params=_cparams("parallel", "parallel"),
        name="window_attention",
    )(sink, qkv, qkv, qkv, qkv, qkv)


def _odd_out_kernel(a_ref, x_ref, gt_ref, gpost_ref, wo_ref, o_ref):
    a = a_ref[...].reshape(BATCH * TT, D_MODEL)
    out = jnp.dot(a, wo_ref[...], preferred_element_type=f32).reshape(BATCH, TT, D_MODEL)
    o_ref[...] = _gated_residual(out, x_ref[...], gt_ref[...], gpost_ref[...])


def _odd_out_proj(a, rows, mods, gains, w_out):
    return pl.pallas_call(
        _odd_out_kernel,
        grid=(SEQ // TT,),
        in_specs=[pl.BlockSpec((BATCH, TT, D_MODEL), lambda i: (0, i, 0)),
                  pl.BlockSpec((BATCH, TT, D_MODEL), lambda i: (0, i + CTX_TILES, 0)),
                  _mod_spec(2, 0), _gain_spec(1),
                  _const_spec((D_MODEL, D_MODEL))],
        out_specs=pl.BlockSpec((BATCH, TT, D_MODEL), lambda i: (0, i, 0)),
        out_shape=jax.ShapeDtypeStruct((BATCH, SEQ, D_MODEL), f32),
        compiler_params=_cparams("parallel"),
        name="odd_out_proj",
    )(a, rows, mods, gains, w_out)


def _layer_mods(m):
    m = m.reshape(MOD_ROWS, MOD_CHUNKS, D_MODEL)
    lat = m[:BATCH]
    ctx = jnp.broadcast_to(m[CTX_MOD_ROW:CTX_MOD_ROW + 1], lat.shape)
    return jnp.stack([lat, ctx], axis=0).transpose(2, 0, 1, 3)[:, :, :, None, :]


def kernel(x, c, ctx, c_ctx, mod_w, mod_b, norm_g, mlp_w1, mlp_w2, even_w_in, even_w_out, s5_lam_re, s5_lam_im, s5_log_dt, s5_b_re, s5_b_im, s5_c_re, s5_c_im, s5_d, s5_glu_w, s5_glu_b, odd_w_in, odd_w_out, odd_sink):
    c_rows = jnp.concatenate([c, c_ctx[None, :], jnp.zeros((MOD_ROWS - BATCH - 1, D_MODEL), f32)], axis=0)
    mods_all = _modulation(c_rows, mod_w, mod_b)
    gains_all = norm_g.reshape(DEPTH, 4, 1, D_MODEL)

    mods, gains = _layer_mods(mods_all[0]), gains_all[0]
    perm = _chunk_swap_matrix()
    w1, w2 = _to_bf16(mlp_w1), _to_bf16(mlp_w2)
    p, rows, u_blocks = _even_in_proj(x, ctx, gains, mods, _to_bf16(even_w_in)[0], perm)
    tables = _s5_tables(s5_lam_re[0], s5_lam_im[0], s5_log_dt[0], s5_b_re[0], s5_b_im[0], s5_c_re[0], s5_c_im[0])
    y_blocks = _s5_mix(u_blocks, tables)
    o_f, o_b = _retention(p)
    rows = _even_out_proj(p, y_blocks, o_f, o_b, rows, mods, gains, s5_d[0].reshape(1, S5_WIDTH),
                          s5_glu_b[0].reshape(1, S5_WIDTH), _to_bf16(s5_glu_w)[0], _to_bf16(even_w_out)[0], perm)
    rows = _mlp(rows, CTX_TILES, mods, gains, w1, w2, 0)

    mods, gains = _layer_mods(mods_all[1]), gains_all[1]
    cos, sin = _rope_tables()
    qkv = _odd_in_proj(rows, gains, mods, _to_bf16(odd_w_in)[0], cos, sin)
    a = _attention(qkv, odd_sink[0].astype(f32))
    lat = _odd_out_proj(a, rows, mods, gains, _to_bf16(odd_w_out)[0])
    return _mlp(lat, 0, mods, gains, w1, w2, 1)
```

```python
import math

import jax
import jax.numpy as jnp
from jax import lax
from jax.experimental import pallas as pl
from jax.experimental.pallas import tpu as pltpu

D_MODEL = 2048
BATCH = 4
SEQ = 4096
DEPTH = 2
GRID_W = 64
CTX_LEN = 256
MOD_CHUNKS = 6
FFN_HIDDEN = 4 * D_MODEL
NORM_EPS = 1e-6

S5_WIDTH = D_MODEL // 2
S5_P = 16
S5_G = S5_WIDTH // S5_P
S5_N = 64
LAMBDA_RE_MAX = -1e-4
RET_DK = 256
RET_H = (D_MODEL // 2) // RET_DK
RET_DV = (D_MODEL // 2) // RET_H
RET_DECAY_BASE = -5.0
EVEN_IN = S5_WIDTH + 2 * RET_H * RET_DK + 2 * RET_H * RET_DV

ATT_HD = 64
ATT_H = D_MODEL // ATT_HD
ATT_KVH = ATT_H // 8
ATT_GRP = ATT_H // ATT_KVH
ODD_IN = (ATT_H + 2 * ATT_KVH) * ATT_HD
WINDOW = 128
ATT_BLOCK = 128
ROPE_BASE = 10000.0
NEG_INF = -1e30
LOG2_E = math.log2(math.e)
Q_SCALE = ATT_HD ** -0.5 * LOG2_E

LANES = 128
SUBLANES = 8

L_ALL = CTX_LEN + SEQ
TT = 128
N_TILES = L_ALL // TT
CTX_TILES = CTX_LEN // TT
NORM_ROWS = 16
NORM_UNROLL = 4
MOD_ROWS = 8
CTX_MOD_ROW = BATCH

S5_T = 16
S5_TP = S5_T * S5_P
S5_NC = L_ALL // S5_T
S5_NC_CTX = CTX_LEN // S5_T
S5_BPAD = SUBLANES
S5_GB = 4
S5_GB_TABLES = 8
S5_TILE_BLOCKS = TT // S5_T
S5_LANE_GROUPS = LANES // S5_P
RET_T = 256
RET_STEPS = L_ALL // RET_T
RET_CHAINS = 2 * RET_H

CAST_BLOCK_ELEMS = 2 * 1024 * 1024
VMEM_LIMIT = 56 * 1024 * 1024

f32 = jnp.float32
bf16 = jnp.bfloat16


def _cparams(*sem):
    return pltpu.CompilerParams(dimension_semantics=sem, vmem_limit_bytes=VMEM_LIMIT)


def _mod_spec(chunk, ctx_tiles):
    return pl.BlockSpec((None, None, BATCH, 1, D_MODEL),
                        lambda i, *_: (chunk, jnp.where(i < ctx_tiles, 1, 0), 0, 0, 0))


def _gain_spec(k):
    return pl.BlockSpec((None, 1, D_MODEL), lambda *_: (k, 0, 0))


def _const_spec(shape):
    return pl.BlockSpec(shape, lambda *_: (0,) * len(shape), pipeline_mode=pl.Buffered(1))


def _norm_mod_rows(x_ref, g_ref, sc_ref, sh_ref, h_ref, copy_ref=None):
    for b in range(BATCH):
        gain = g_ref[...] * (1.0 + sc_ref[b])
        shift = sh_ref[b]

        def body(r, carry):
            r0 = pl.multiple_of(r * NORM_ROWS, NORM_ROWS)
            xf = x_ref[b, pl.ds(r0, NORM_ROWS), :]
            if copy_ref is not None:
                copy_ref[b, pl.ds(r0, NORM_ROWS), :] = xf
            ms = jnp.mean(xf * xf, axis=-1, keepdims=True)
            h = (xf * lax.rsqrt(ms + NORM_EPS) * gain + shift).astype(h_ref.dtype)
            if len(h_ref.shape) == 3:
                h_ref[b, pl.ds(r0, NORM_ROWS), :] = h
            else:
                h_ref[pl.ds(pl.multiple_of(b * TT + r0, NORM_ROWS), NORM_ROWS), :] = h
            return carry

        lax.fori_loop(0, TT // NORM_ROWS, body, 0, unroll=NORM_UNROLL)


def _gated_residual(m, x, gate, g_post):
    ms = jnp.mean(m * m, axis=-1, keepdims=True)
    return x + m * lax.rsqrt(ms + NORM_EPS) * (g_post * gate)


def _gated_residual_rows(load_rows, x_ref, gt_ref, gpost_ref, o_ref):
    for b in range(BATCH):
        gain = gpost_ref[...] * gt_ref[b]

        def body(r, carry):
            r0 = pl.multiple_of(r * NORM_ROWS, NORM_ROWS)
            m = load_rows(b, r0, NORM_ROWS)
            ms = jnp.mean(m * m, axis=-1, keepdims=True)
            o_ref[b, pl.ds(r0, NORM_ROWS), :] = x_ref[b, pl.ds(r0, NORM_ROWS), :] + m * lax.rsqrt(ms + NORM_EPS) * gain
            return carry

        lax.fori_loop(0, TT // NORM_ROWS, body, 0, unroll=NORM_UNROLL)


def _chunk_swap_matrix():
    n = S5_LANE_GROUPS * LANES
    src = lax.broadcasted_iota(jnp.int32, (n, n), 0)
    dst = lax.broadcasted_iota(jnp.int32, (n, n), 1)
    i, j, p = src // LANES, (src // S5_P) % S5_LANE_GROUPS, src % S5_P
    return (dst == j * LANES + i * S5_P + p).astype(bf16)


def _cast_kernel(w_ref, o_ref):
    o_ref[...] = w_ref[...].astype(o_ref.dtype)


def _to_bf16(w):
    n_layers, rows, cols = w.shape
    block_rows = min(rows, pl.next_power_of_2(CAST_BLOCK_ELEMS // cols + 1) // 2)
    spec = pl.BlockSpec((None, block_rows, cols), lambda l, r: (l, r, 0))
    return pl.pallas_call(
        _cast_kernel,
        grid=(n_layers, rows // block_rows),
        in_specs=[spec],
        out_specs=spec,
        out_shape=jax.ShapeDtypeStruct(w.shape, bf16),
        compiler_params=_cparams("parallel", "parallel"),
        name="to_bf16",
    )(w)


def _mod_kernel(c_ref, w_ref, b_ref, o_ref):
    c = c_ref[...]
    s = c * jax.nn.sigmoid(c)
    o_ref[...] = jnp.dot(s.astype(bf16), w_ref[...].astype(bf16), preferred_element_type=f32) + b_ref[...]


def _modulation(c_rows, mod_w, mod_b):
    tn = 1024
    n = MOD_CHUNKS * D_MODEL
    return pl.pallas_call(
        _mod_kernel,
        grid=(DEPTH, n // tn),
        in_specs=[pl.BlockSpec((MOD_ROWS, D_MODEL), lambda l, j: (0, 0)),
                  pl.BlockSpec((None, D_MODEL, tn), lambda l, j: (l, 0, j)),
                  pl.BlockSpec((None, 1, tn), lambda l, j: (l, 0, j))],
        out_specs=pl.BlockSpec((None, MOD_ROWS, tn), lambda l, j: (l, 0, j)),
        out_shape=jax.ShapeDtypeStruct((DEPTH, MOD_ROWS, n), f32),
        compiler_params=_cparams("parallel", "parallel"),
        name="ada_modulation",
    )(c_rows, mod_w, mod_b.reshape(DEPTH, 1, n))


def _even_u_kernel(x_ref, c_ref, g_ref, sh_ref, sc_ref, w_ref, perm_ref, h_ref, u_ref, rows_ref, ub_ref,
                   a_ref, xs_ref, st_ref):
    i = pl.program_id(0)

    @pl.when(i < CTX_TILES)
    def _():
        _norm_mod_rows(c_ref, g_ref, sc_ref, sh_ref, h_ref, copy_ref=rows_ref)

    @pl.when(i >= CTX_TILES)
    def _():
        _norm_mod_rows(x_ref, g_ref, sc_ref, sh_ref, h_ref, copy_ref=rows_ref)

    acc = jnp.dot(h_ref[...].reshape(BATCH * TT, D_MODEL), w_ref[...], preferred_element_type=f32)
    u_ref[...] = acc.reshape(BATCH, TT, S5_WIDTH).astype(u_ref.dtype)

    n_slab, n_half = S5_WIDTH // LANES, S5_TP // LANES
    for s in range(n_slab):
        a_ref[s] = acc[:, s * LANES:(s + 1) * LANES]
    for s in range(n_slab):
        for half in range(n_half):
            for b in range(BATCH):
                r0 = ((s * n_half + half) * BATCH + b) * S5_TILE_BLOCKS
                for tt in range(S5_LANE_GROUPS):
                    xs_ref[tt, r0:r0 + S5_TILE_BLOCKS, :] = a_ref[
                        s, pl.ds(b * TT + half * S5_LANE_GROUPS + tt, S5_TILE_BLOCKS, stride=S5_T), :]
    xs = jnp.concatenate([xs_ref[tt] for tt in range(S5_LANE_GROUPS)], axis=-1).astype(bf16)
    ys = jnp.dot(xs, perm_ref[...], preferred_element_type=f32)
    st_ref[...] = jnp.zeros_like(st_ref)
    for s in range(n_slab):
        for half in range(n_half):
            for b in range(BATCH):
                r0 = ((s * n_half + half) * BATCH + b) * S5_TILE_BLOCKS
                for gg in range(S5_LANE_GROUPS):
                    st_ref[s * S5_LANE_GROUPS + gg, half, pl.ds(b, S5_TILE_BLOCKS, stride=S5_BPAD), :] = (
                        ys[r0:r0 + S5_TILE_BLOCKS, gg * LANES:(gg + 1) * LANES])
    for g in range(S5_G):
        ub_ref[g] = jnp.concatenate([st_ref[g, 0], st_ref[g, 1]], axis=-1).astype(ub_ref.dtype)


def _even_u_proj(x, ctx, gains, mods, w_in, perm):
    blk_rows = S5_TILE_BLOCKS * S5_BPAD
    tile = lambda width: pl.BlockSpec((BATCH, TT, width), lambda i: (0, i, 0))
    return pl.pallas_call(
        _even_u_kernel,
        grid=(N_TILES,),
        in_specs=[pl.BlockSpec((BATCH, TT, D_MODEL), lambda i: (0, jnp.maximum(i - CTX_TILES, 0), 0)),
                  pl.BlockSpec((BATCH, TT, D_MODEL), lambda i: (0, jnp.minimum(i, CTX_TILES - 1), 0)),
                  _gain_spec(0), _mod_spec(0, CTX_TILES), _mod_spec(1, CTX_TILES),
                  pl.BlockSpec((D_MODEL, S5_WIDTH), lambda i: (0, 0), pipeline_mode=pl.Buffered(1)),
                  _const_spec(perm.shape)],
        out_specs=[tile(D_MODEL), tile(S5_WIDTH), tile(D_MODEL),
                   pl.BlockSpec((S5_G, blk_rows, S5_TP), lambda i: (0, i, 0))],
        out_shape=[jax.ShapeDtypeStruct((BATCH, L_ALL, D_MODEL), bf16),
                   jax.ShapeDtypeStruct((BATCH, L_ALL, S5_WIDTH), bf16),
                   jax.ShapeDtypeStruct((BATCH, L_ALL, D_MODEL), f32),
                   jax.ShapeDtypeStruct((S5_G, S5_NC * S5_BPAD, S5_TP), bf16)],
        scratch_shapes=[pltpu.VMEM((S5_WIDTH // LANES, BATCH * TT, LANES), f32),
                        pltpu.VMEM((S5_LANE_GROUPS, BATCH * TT, LANES), f32),
                        pltpu.VMEM((S5_G, S5_TP // LANES, blk_rows, LANES), f32)],
        compiler_params=_cparams("parallel"),
        name="even_u_proj",
    )(x, ctx, gains, mods, mods, w_in, perm)


EVEN_REST_TILES = (EVEN_IN - S5_WIDTH) // S5_WIDTH


def _even_rest_kernel(h_ref, *refs):
    w_refs, o_ref = refs[:EVEN_REST_TILES], refs[EVEN_REST_TILES]
    h = h_ref[...].reshape(BATCH * TT, D_MODEL)
    for jt, w_ref in enumerate(w_refs):
        acc = jnp.dot(h, w_ref[...], preferred_element_type=f32)
        o_ref[:, :, jt * S5_WIDTH:(jt + 1) * S5_WIDTH] = acc.reshape(BATCH, TT, S5_WIDTH).astype(o_ref.dtype)


def _even_rest_proj(h, w_in):
    width = EVEN_IN - S5_WIDTH
    w_specs = [pl.BlockSpec((D_MODEL, S5_WIDTH), lambda i, jt=jt: (0, 1 + jt), pipeline_mode=pl.Buffered(1))
               for jt in range(EVEN_REST_TILES)]
    return pl.pallas_call(
        _even_rest_kernel,
        grid=(N_TILES,),
        in_specs=[pl.BlockSpec((BATCH, TT, D_MODEL), lambda i: (0, i, 0))] + w_specs,
        out_specs=pl.BlockSpec((BATCH, TT, width), lambda i: (0, i, 0)),
        out_shape=jax.ShapeDtypeStruct((BATCH, L_ALL, width), bf16),
        compiler_params=_cparams("parallel"),
        name="even_rest_proj",
    )(h, *([w_in] * EVEN_REST_TILES))


def _s5_tables(lam_re, lam_im, log_dt, b_re, b_im, c_re, c_im):
    lr = jnp.minimum(lam_re.astype(f32), LAMBDA_RE_MAX)
    li = lam_im.astype(f32)
    dt = jnp.exp(log_dt.astype(f32))[..., None]
    zr, zi = lr * dt, li * dt
    ab_mag = jnp.exp(zr)
    ab_re, ab_im = ab_mag * jnp.cos(zi), ab_mag * jnp.sin(zi)
    den = lr * lr + li * li
    nr = ab_re - 1.0
    f_re = (nr * lr + ab_im * li) / den
    f_im = (ab_im * lr - nr * li) / den
    br, bi = b_re.astype(f32), b_im.astype(f32)
    bb_re = f_re[..., None] * br - f_im[..., None] * bi
    bb_im = f_re[..., None] * bi + f_im[..., None] * br
    steps = jnp.arange(S5_T + 1, dtype=f32)[:, None]
    zr_k, zi_k = zr[:, :, None, :] * steps, zi[:, :, None, :] * steps
    pw_mag = jnp.exp(zr_k)
    pw_re, pw_im = pw_mag * jnp.cos(zi_k), pw_mag * jnp.sin(zi_k)
    cr, ci = c_re.astype(f32), c_im.astype(f32)
    bt_re, bt_im = bb_re.transpose(0, 1, 3, 2), bb_im.transpose(0, 1, 3, 2)

    kf, kb = slice(0, S5_T), slice(S5_T - 1, None, -1)
    ca_re = cr[:, :, None] * pw_re[:, :, :, None, :] - ci[:, :, None] * pw_im[:, :, :, None, :]
    ca_im = cr[:, :, None] * pw_im[:, :, :, None, :] + ci[:, :, None] * pw_re[:, :, :, None, :]
    ca_f = jnp.concatenate([ca_re[0, :, kf], ca_im[0, :, kf]], axis=-1).reshape(S5_G, S5_TP, 2 * S5_N)
    ca_b = jnp.concatenate([ca_re[1, :, kb], ca_im[1, :, kb]], axis=-1).reshape(S5_G, S5_TP, 2 * S5_N)
    bcat = jnp.concatenate([bt_re, -bt_im], axis=-1)
    w_intra = _s5_intra_matrix(bcat[0], bcat[1], ca_f, ca_b)

    def state_in(d, k):
        pr, pi = pw_re[d][:, k][:, :, None, :], pw_im[d][:, k][:, :, None, :]
        br_, bi_ = bt_re[d][:, None], bt_im[d][:, None]
        return ((pr * br_ - pi * bi_).reshape(S5_G, S5_TP, S5_N), (pr * bi_ + pi * br_).reshape(S5_G, S5_TP, S5_N))

    sf_re, sf_im = state_in(0, kb)
    sb_re, sb_im = state_in(1, kf)
    w_state = jnp.concatenate([sf_re, sb_re, sf_im, sb_im], axis=-1)

    ct_re, ct_im = cr.transpose(0, 1, 3, 2), ci.transpose(0, 1, 3, 2)
    pt_re, pt_im = pw_re.transpose(0, 1, 3, 2), pw_im.transpose(0, 1, 3, 2)

    def state_out(d, k):
        pr, pi = pt_re[d][:, :, k][..., None], pt_im[d][:, :, k][..., None]
        cr_, ci_ = ct_re[d][:, :, None, :], ct_im[d][:, :, None, :]
        return ((cr_ * pr - ci_ * pi).reshape(S5_G, S5_N, S5_TP), (cr_ * pi + ci_ * pr).reshape(S5_G, S5_N, S5_TP))

    xf_re, xf_im = state_out(0, slice(1, S5_T + 1))
    xb_re, xb_im = state_out(1, slice(S5_T, 0, -1))
    w_cross = jnp.concatenate([xf_re, xb_re, -xf_im, -xb_im], axis=1)
    a_re = jnp.concatenate([pw_re[0, :, S5_T], pw_re[1, :, S5_T]], axis=-1).reshape(S5_G, 1, 2 * S5_N)
    a_im = jnp.concatenate([pw_im[0, :, S5_T], pw_im[1, :, S5_T]], axis=-1).reshape(S5_G, 1, 2 * S5_N)
    return w_intra, w_state.astype(bf16), w_cross.astype(bf16), a_re, a_im


def _s5_intra_kernel(bf_ref, bb_ref, caf_ref, cab_ref, w_ref):
    lane = lax.broadcasted_iota(jnp.int32, (S5_P, S5_TP), 1)
    dn = (((1,), (1,)), ((), ()))
    for g in range(S5_GB_TABLES):
        tq_f = lax.dot_general(bf_ref[g], caf_ref[g], dn, precision=lax.Precision.HIGHEST, preferred_element_type=f32)
        tq_b = lax.dot_general(bb_ref[g], cab_ref[g], dn, precision=lax.Precision.HIGHEST, preferred_element_type=f32)
        for s in range(S5_T):
            sf, sb = S5_P * s, S5_P * (S5_T - 1 - s)
            row_f = tq_f if s == 0 else jnp.where(lane >= sf, pltpu.roll(tq_f, sf, 1), 0.0)
            row_b = tq_b if sb == 0 else jnp.where(lane < S5_TP - sb, pltpu.roll(tq_b, S5_TP - sb, 1), 0.0)
            w_ref[g, s * S5_P:(s + 1) * S5_P, :] = (row_f + row_b).astype(w_ref.dtype)


def _s5_intra_matrix(b_f, b_b, ca_f, ca_b):
    def spec(rows, cols):
        return pl.BlockSpec((S5_GB_TABLES, rows, cols), lambda i: (i, 0, 0))

    return pl.pallas_call(
        _s5_intra_kernel,
        grid=(S5_G // S5_GB_TABLES,),
        in_specs=[spec(S5_P, 2 * S5_N), spec(S5_P, 2 * S5_N), spec(S5_TP, 2 * S5_N), spec(S5_TP, 2 * S5_N)],
        out_specs=spec(S5_TP, S5_TP),
        out_shape=jax.ShapeDtypeStruct((S5_G, S5_TP, S5_TP), bf16),
        compiler_params=_cparams("parallel"),
        name="s5_intra_matrix",
    )(b_f, b_b, ca_f, ca_b)


def _s5_kernel(u_ref, wi_ref, ws_ref, wc_ref, are_ref, aim_ref, y_ref, s_ref):
    half = 2 * S5_N
    for g in range(S5_GB):
        s_ref[g] = jnp.dot(u_ref[g], ws_ref[g], preferred_element_type=f32)

    fwd_lane = lax.broadcasted_iota(jnp.int32, (S5_BPAD, half), 1) < S5_N
    a_re = [jnp.broadcast_to(are_ref[g], (S5_BPAD, half)) for g in range(S5_GB)]
    a_im = [jnp.broadcast_to(aim_ref[g], (S5_BPAD, half)) for g in range(S5_GB)]

    def step(i, carry):
        cb = jnp.where(i < S5_NC_CTX, S5_NC_CTX - 1 - i, S5_NC + S5_NC_CTX - 1 - i)
        rf = pl.multiple_of(i * S5_BPAD, S5_BPAD)
        rb = pl.multiple_of(cb * S5_BPAD, S5_BPAD)
        out = []
        for g in range(S5_GB):
            h_re, h_im = carry[2 * g], carry[2 * g + 1]
            l_re = jnp.where(fwd_lane, s_ref[g, pl.ds(rf, S5_BPAD), 0:half], s_ref[g, pl.ds(rb, S5_BPAD), 0:half])
            l_im = jnp.where(fwd_lane, s_ref[g, pl.ds(rf, S5_BPAD), half:2 * half],
                             s_ref[g, pl.ds(rb, S5_BPAD), half:2 * half])
            s_ref[g, pl.ds(rf, S5_BPAD), 0:S5_N] = h_re[:, 0:S5_N]
            s_ref[g, pl.ds(rb, S5_BPAD), S5_N:half] = h_re[:, S5_N:half]
            s_ref[g, pl.ds(rf, S5_BPAD), half:half + S5_N] = h_im[:, 0:S5_N]
            s_ref[g, pl.ds(rb, S5_BPAD), half + S5_N:2 * half] = h_im[:, S5_N:half]
            out.append(a_re[g] * h_re - a_im[g] * h_im + l_re)
            out.append(a_re[g] * h_im + a_im[g] * h_re + l_im)
        return tuple(out)

    zero = jnp.zeros((S5_BPAD, half), f32)
    lax.fori_loop(0, S5_NC, step, (zero,) * (2 * S5_GB))

    for g in range(S5_GB):
        y = jnp.dot(u_ref[g], wi_ref[g], preferred_element_type=f32)
        y = y + jnp.dot(s_ref[g].astype(bf16), wc_ref[g], preferred_element_type=f32)
        for hf in range(S5_TP // LANES):
            y_ref[g, hf] = y[:, hf * LANES:(hf + 1) * LANES]


def _s5_mix(u_blocks, tables):
    w_intra, w_state, w_cross, a_re, a_im = tables
    rows = S5_NC * S5_BPAD
    wsp = pl.BlockSpec((S5_GB, S5_TP, S5_TP), lambda g: (g, 0, 0))
    asp = pl.BlockSpec((S5_GB, 1, 2 * S5_N), lambda g: (g, 0, 0))
    return pl.pallas_call(
        _s5_kernel,
        grid=(S5_G // S5_GB,),
        in_specs=[pl.BlockSpec((S5_GB, rows, S5_TP), lambda g: (g, 0, 0)), wsp, wsp, wsp, asp, asp],
        out_specs=pl.BlockSpec((S5_GB, S5_TP // LANES, rows, LANES), lambda g: (g, 0, 0, 0)),
        out_shape=jax.ShapeDtypeStruct((S5_G, S5_TP // LANES, rows, LANES), f32),
        scratch_shapes=[pltpu.VMEM((S5_GB, rows, S5_TP), f32)],
        compiler_params=_cparams("parallel"),
        name="s5_mix",
    )(u_blocks, w_intra, w_state, w_cross, a_re, a_im)


def _ret_kernel(lg_ref, cd_ref, qf_ref, kf_ref, vf_ref, qb_ref, kb_ref, vb_ref, of_ref, ob_ref,
                r_ref, dec_ref, qdec_ref, kdec_ref):
    t = RET_T

    @pl.when(pl.program_id(1) == 0)
    def _():
        r_ref[...] = jnp.zeros_like(r_ref)
        n = lax.broadcasted_iota(jnp.int32, (t, t), 0)
        m = lax.broadcasted_iota(jnp.int32, (t, t), 1)
        pos = lax.broadcasted_iota(jnp.int32, (t, 1), 0)
        for d in range(2):
            diff = (n - m if d == 0 else m - n).astype(f32)
            p = (pos if d == 0 else t - 1 - pos).astype(f32)
            for h in range(RET_H):
                c = d * RET_H + h
                lg = lg_ref[c]
                dec_ref[c] = jnp.where(diff >= 0, jnp.exp(jnp.maximum(diff, 0.0) * lg), 0.0)
                qdec_ref[c] = jnp.exp((p + 1.0) * lg)
                kdec_ref[c] = jnp.exp((t - 1.0 - p) * lg)

    for d, (q_ref, k_ref, v_ref, o_ref) in enumerate(((qf_ref, kf_ref, vf_ref, of_ref),
                                                        (qb_ref, kb_ref, vb_ref, ob_ref))):
        for h in range(RET_H):
            c = d * RET_H + h
            cols = slice(h * RET_DK, (h + 1) * RET_DK)
            q = q_ref[:, cols]
            ks = k_ref[:, cols] * (RET_DK ** -0.5)
            v = v_ref[:, cols]
            r = r_ref[c]
            s = lax.dot_general(q, ks, (((1,), (1,)), ((), ())), preferred_element_type=f32) * dec_ref[c]
            inner = jnp.dot(s.astype(bf16), v, preferred_element_type=f32)
            cross = jnp.dot(q, r.astype(bf16), preferred_element_type=f32) * qdec_ref[c]
            o_ref[:, cols] = (inner + cross).astype(o_ref.dtype)
            kd = (ks.astype(f32) * kdec_ref[c]).astype(bf16)
            r_ref[c] = r * cd_ref[c] + lax.dot_general(kd, v, (((0,), (0,)), ((), ())), preferred_element_type=f32)


def _retention(p):
    e = RET_DECAY_BASE - (2.0 * jnp.arange(RET_H, dtype=f32)[None, :] + jnp.arange(2, dtype=f32)[:, None])
    log_g = jnp.log1p(-jnp.exp2(e)).reshape(RET_CHAINS)
    chunk_dec = jnp.exp(RET_T * log_g)
    width = RET_H * RET_DK
    col0 = 0

    def bwd_chunk(i):
        return jnp.where(i == 0, 0, RET_STEPS - i)

    def fwd(off):
        return pl.BlockSpec((None, RET_T, width), lambda b, i: (b, i, col0 + off))

    def bwd(off):
        return pl.BlockSpec((None, RET_T, width), lambda b, i: (b, bwd_chunk(i), col0 + off))

    smem = pl.BlockSpec(memory_space=pltpu.SMEM)
    out = jax.ShapeDtypeStruct((BATCH, L_ALL, RET_H * RET_DV), bf16)
    return pl.pallas_call(
        _ret_kernel,
        grid=(BATCH, RET_STEPS),
        in_specs=[smem, smem, fwd(0), fwd(1), fwd(2), bwd(0), bwd(1), bwd(2)],
        out_specs=[pl.BlockSpec((None, RET_T, RET_H * RET_DV), lambda b, i: (b, i, 0)),
                   pl.BlockSpec((None, RET_T, RET_H * RET_DV), lambda b, i: (b, bwd_chunk(i), 0))],
        out_shape=[out, out],
        scratch_shapes=[pltpu.VMEM((RET_CHAINS, RET_DK, RET_DV), f32),
                        pltpu.VMEM((RET_CHAINS, RET_T, RET_T), f32),
                        pltpu.VMEM((RET_CHAINS, RET_T, 1), f32),
                        pltpu.VMEM((RET_CHAINS, RET_T, 1), f32)],
        compiler_params=_cparams("parallel", "arbitrary"),
        name="retention",
    )(log_g, chunk_dec, p, p, p, p, p, p)


def _gelu_tanh(x):
    return 0.5 * x * (1.0 + jnp.tanh(math.sqrt(2.0 / math.pi) * (x + 0.044715 * (x * x * x))))


def _even_out_kernel(u_ref, gate_ref, y_ref, of_ref, ob_ref, x_ref, gt_ref, gpost_ref, dsk_ref, glub_ref,
                     gluw_ref, wo_ref, perm_ref, o_ref, z_ref, xs_ref, yf_ref):
    rows = BATCH * TT
    n_slab, n_half = S5_WIDTH // LANES, S5_TP // LANES
    for s in range(n_slab):
        for half in range(n_half):
            for b in range(BATCH):
                r0 = ((s * n_half + half) * BATCH + b) * S5_TILE_BLOCKS
                for gg in range(S5_LANE_GROUPS):
                    xs_ref[gg, r0:r0 + S5_TILE_BLOCKS, :] = y_ref[
                        s * S5_LANE_GROUPS + gg, half, pl.ds(b, S5_TILE_BLOCKS, stride=S5_BPAD), :]
    xs = jnp.concatenate([xs_ref[gg] for gg in range(S5_LANE_GROUPS)], axis=-1).astype(bf16)
    ys = jnp.dot(xs, perm_ref[...], preferred_element_type=f32)
    for s in range(n_slab):
        for half in range(n_half):
            for b in range(BATCH):
                r0 = ((s * n_half + half) * BATCH + b) * S5_TILE_BLOCKS
                for tt in range(S5_LANE_GROUPS):
                    z_ref[s, pl.ds(b * TT + half * S5_LANE_GROUPS + tt, S5_TILE_BLOCKS, stride=S5_T), :] = (
                        ys[r0:r0 + S5_TILE_BLOCKS, tt * LANES:(tt + 1) * LANES])
    u = u_ref[...].reshape(rows, S5_WIDTH)
    for s in range(S5_WIDTH // LANES):
        cols = slice(s * LANES, (s + 1) * LANES)
        yf_ref[:, cols] = _gelu_tanh(z_ref[s] + dsk_ref[:, cols] * u[:, cols].astype(f32))
    y = yf_ref[...]
    z = jnp.dot(y.astype(bf16), gluw_ref[...], preferred_element_type=f32) + glub_ref[...]
    s5_out = y * jax.nn.sigmoid(z)
    r = (of_ref[...].astype(f32) + ob_ref[...].astype(f32)).reshape(rows, RET_H * RET_DV)
    heads = []
    for h in range(RET_H):
        rh = r[:, h * RET_DV:(h + 1) * RET_DV]
        heads.append(rh * lax.rsqrt(jnp.mean(rh * rh, axis=-1, keepdims=True) + NORM_EPS))
    g = gate_ref[...].reshape(rows, RET_H * RET_DV).astype(f32)
    ret_out = jnp.concatenate(heads, axis=-1) * (g * jax.nn.sigmoid(g))
    out = jnp.dot(s5_out.astype(bf16), wo_ref[0:S5_WIDTH, :], preferred_element_type=f32)
    out = out + jnp.dot(ret_out.astype(bf16), wo_ref[S5_WIDTH:, :], preferred_element_type=f32)
    o_ref[...] = _gated_residual(out.reshape(BATCH, TT, D_MODEL), x_ref[...], gt_ref[...], gpost_ref[...])


def _even_out_proj(u, p, y_blocks, o_f, o_b, rows, mods, gains, d_skip, glu_b, glu_w, w_out, perm):
    half = S5_WIDTH
    blk_rows = S5_TILE_BLOCKS * S5_BPAD
    tile = pl.BlockSpec((BATCH, TT, half), lambda i: (0, i, 0))
    return pl.pallas_call(
        _even_out_kernel,
        grid=(N_TILES,),
        in_specs=[tile,
                  pl.BlockSpec((BATCH, TT, half), lambda i: (0, i, EVEN_REST_TILES - 1)),
                  pl.BlockSpec((S5_G, S5_TP // LANES, blk_rows, LANES), lambda i: (0, 0, i, 0)),
                  tile, tile,
                  pl.BlockSpec((BATCH, TT, D_MODEL), lambda i: (0, i, 0)),
                  _mod_spec(2, CTX_TILES), _gain_spec(1),
                  _const_spec((1, half)), _const_spec((1, half)),
                  _const_spec((half, half)), _const_spec((D_MODEL, D_MODEL)), _const_spec(perm.shape)],
        out_specs=pl.BlockSpec((BATCH, TT, D_MODEL), lambda i: (0, i, 0)),
        out_shape=jax.ShapeDtypeStruct((BATCH, L_ALL, D_MODEL), f32),
        scratch_shapes=[pltpu.VMEM((S5_WIDTH // LANES, BATCH * TT, LANES), f32),
                        pltpu.VMEM((S5_LANE_GROUPS, BATCH * TT, LANES), f32),
                        pltpu.VMEM((BATCH * TT, S5_WIDTH), f32)],
        compiler_params=_cparams("parallel"),
        name="even_out_proj",
    )(u, p, y_blocks, o_f, o_b, rows, mods, gains, d_skip, glu_b, glu_w, w_out, perm)


def _mlp_kernel(x_ref, gpre_ref, sh_ref, sc_ref, gt_ref, gpost_ref, w1_ref, w2_ref, o_ref, h_ref, acc_ref):
    f = pl.program_id(1)

    @pl.when(f == 0)
    def _():
        _norm_mod_rows(x_ref, gpre_ref, sc_ref, sh_ref, h_ref)
        acc_ref[...] = jnp.zeros_like(acc_ref)

    a = jnp.maximum(jnp.dot(h_ref[...], w1_ref[...], preferred_element_type=f32), 0.0)
    acc_ref[...] += jnp.dot((a * a).astype(bf16), w2_ref[...], preferred_element_type=f32)

    @pl.when(f == pl.num_programs(1) - 1)
    def _():
        _gated_residual_rows(lambda b, r0, n: acc_ref[pl.ds(pl.multiple_of(b * TT + r0, n), n), :],
                             x_ref, gt_ref, gpost_ref, o_ref)


def _mlp(x, ctx_tiles, mods, gains, w1, w2, layer):
    tf = 1024
    n_tiles = x.shape[1] // TT
    return pl.pallas_call(
        _mlp_kernel,
        grid=(n_tiles, FFN_HIDDEN // tf),
        in_specs=[pl.BlockSpec((BATCH, TT, D_MODEL), lambda i, f: (0, i, 0)),
                  _gain_spec(2), _mod_spec(3, ctx_tiles), _mod_spec(4, ctx_tiles), _mod_spec(5, ctx_tiles),
                  _gain_spec(3),
                  pl.BlockSpec((None, D_MODEL, tf), lambda i, f: (layer, 0, f)),
                  pl.BlockSpec((None, tf, D_MODEL), lambda i, f: (layer, f, 0))],
        out_specs=pl.BlockSpec((BATCH, TT, D_MODEL), lambda i, f: (0, i, 0)),
        out_shape=jax.ShapeDtypeStruct((BATCH, n_tiles * TT, D_MODEL), f32),
        scratch_shapes=[pltpu.VMEM((BATCH * TT, D_MODEL), bf16), pltpu.VMEM((BATCH * TT, D_MODEL), f32)],
        compiler_params=_cparams("parallel", "arbitrary"),
        name="sq_relu_mlp",
    )(x, gains, mods, mods, mods, gains, w1, w2)


def _rope_tables():
    rows = SEQ // GRID_W
    row = jnp.repeat(jnp.arange(rows, dtype=f32), GRID_W)
    col = jnp.tile(jnp.arange(GRID_W, dtype=f32), rows)
    n_freq = ATT_HD // 4
    inv_freq = ROPE_BASE ** (-jnp.arange(n_freq, dtype=f32) / n_freq)
    ang = jnp.concatenate([row[:, None] * inv_freq[None], col[:, None] * inv_freq[None]], axis=-1)
    cos, sin = jnp.cos(ang), jnp.sin(ang)
    cos, sin = jnp.tile(cos, (1, 4)), jnp.concatenate([-sin, sin, -sin, sin], axis=-1)
    return (jnp.stack([cos * Q_SCALE, cos, jnp.ones_like(cos)]),
            jnp.stack([sin * Q_SCALE, sin, jnp.zeros_like(sin)]))


ROPE_Q, ROPE_K, ROPE_NONE = 0, 1, 2


def _rope(x, cos, sin):
    half = ATT_HD // 2
    lane = lax.broadcasted_iota(jnp.int32, x.shape, x.ndim - 1)
    partner = jnp.where((lane & (ATT_HD - 1)) < half,
                        pltpu.roll(x, LANES - half, x.ndim - 1), pltpu.roll(x, half, x.ndim - 1))
    return x * cos + partner * sin


QKV_TILE = 2 * ATT_KVH * ATT_HD
QKV_TILES = ODD_IN // QKV_TILE


def _odd_in_kernel(x_ref, g_ref, sh_ref, sc_ref, w_ref, cos_q_ref, sin_q_ref, cos_k_ref, sin_k_ref, o_ref, h_ref):
    _norm_mod_rows(x_ref, g_ref, sc_ref, sh_ref, h_ref)
    kw = ATT_KVH * ATT_HD
    for jt in range(QKV_TILES):
        acc = jnp.dot(h_ref[...], w_ref[:, jt * QKV_TILE:(jt + 1) * QKV_TILE], preferred_element_type=f32)
        for c in range(QKV_TILE // LANES):
            cols = slice(c * LANES, (c + 1) * LANES)
            xc = acc[:, cols].reshape(BATCH, TT, LANES)
            if jt < QKV_TILES - 1:
                xc = _rope(xc, cos_q_ref[...], sin_q_ref[...])
            elif c * LANES < kw:
                xc = _rope(xc, cos_k_ref[...], sin_k_ref[...])
            o_ref[jt, :, :, cols] = xc.astype(o_ref.dtype)


def _odd_in_proj(x, gains, mods, w_in, cos, sin):
    def table(kind_of):
        return pl.BlockSpec((None, TT, LANES), lambda i: (kind_of(i), jnp.maximum(i - CTX_TILES, 0), 0))

    q_table = table(lambda i: ROPE_Q)
    k_table = table(lambda i: jnp.where(i >= CTX_TILES, ROPE_K, ROPE_NONE))
    return pl.pallas_call(
        _odd_in_kernel,
        grid=(N_TILES,),
        in_specs=[pl.BlockSpec((BATCH, TT, D_MODEL), lambda i: (0, i, 0)),
                  _gain_spec(0), _mod_spec(0, CTX_TILES), _mod_spec(1, CTX_TILES),
                  _const_spec((D_MODEL, ODD_IN)), q_table, q_table, k_table, k_table],
        out_specs=pl.BlockSpec((QKV_TILES, BATCH, TT, QKV_TILE), lambda i: (0, 0, i, 0)),
        out_shape=jax.ShapeDtypeStruct((QKV_TILES, BATCH, L_ALL, QKV_TILE), bf16),
        scratch_shapes=[pltpu.VMEM((BATCH * TT, D_MODEL), bf16)],
        compiler_params=_cparams("parallel"),
        name="odd_in_proj",
    )(x, gains, mods, mods, w_in, cos, sin, cos, sin)


def _attn_kernel(sink_ref, q_ref, kvp_ref, kvc_ref, kvn_ref, kvx_ref, o_ref, p_ref):
    qb = pl.program_id(1)
    t = ATT_BLOCK
    kw = ATT_KVH * ATT_HD
    row = lax.broadcasted_iota(jnp.int32, (t, t), 0)
    col = lax.broadcasted_iota(jnp.int32, (t, t), 1)

    def band_valid(blk):
        off = col + blk * t
        kpos = qb * t - WINDOW + off
        return (jnp.abs(off - WINDOW - row) <= WINDOW) & (kpos >= 0) & (kpos < SEQ)

    n_ctx_chunks = CTX_LEN // LANES
    masks = {n_ctx_chunks: band_valid(0), n_ctx_chunks + 2: band_valid(2)}
    n_chunks = n_ctx_chunks + 3

    for kh in range(ATT_KVH):
        ks = slice(kh * ATT_HD, (kh + 1) * ATT_HD)
        vs = slice(kw + kh * ATT_HD, kw + (kh + 1) * ATT_HD)
        k_all = jnp.concatenate([kvx_ref[:, ks], kvp_ref[:, ks], kvc_ref[:, ks], kvn_ref[:, ks]], axis=0)
        v_all = jnp.concatenate([kvx_ref[:, vs], kvp_ref[:, vs], kvc_ref[:, vs], kvn_ref[:, vs]], axis=0)
        heads = [kh * ATT_GRP + g for g in range(ATT_GRP)]
        per_tile = QKV_TILE // ATT_HD
        q_all = jnp.concatenate([q_ref[h // per_tile, :, (h % per_tile) * ATT_HD:(h % per_tile + 1) * ATT_HD]
                                 for h in heads], axis=0)
        s_all = lax.dot_general(q_all, k_all, (((1,), (1,)), ((), ())), preferred_element_type=f32)
        inv_den = []
        for g in range(ATT_GRP):
            sink = sink_ref[kh * ATT_GRP + g] * LOG2_E
            rows = slice(g * t, (g + 1) * t)

            def chunk(c):
                s = s_all[rows, c * LANES:(c + 1) * LANES]
                return jnp.where(masks[c], s, NEG_INF) if c in masks else s

            m_lanes = chunk(0)
            for c in range(1, n_chunks):
                m_lanes = jnp.maximum(m_lanes, chunk(c))
            m = jnp.maximum(jnp.max(m_lanes, axis=-1, keepdims=True), sink)
            e_lanes = None
            for c in range(n_chunks):
                e = jnp.exp2(chunk(c) - m)
                p_ref[rows, c * LANES:(c + 1) * LANES] = e.astype(p_ref.dtype)
                e_lanes = e if e_lanes is None else e_lanes + e
            inv_den.append(1.0 / (jnp.sum(e_lanes, axis=-1, keepdims=True) + jnp.exp2(sink - m)))
        o_all = jnp.dot(p_ref[...], v_all, preferred_element_type=f32)
        for g in range(ATT_GRP):
            h = kh * ATT_GRP + g
            o_ref[:, h * ATT_HD:(h + 1) * ATT_HD] = (o_all[g * t:(g + 1) * t] * inv_den[g]).astype(o_ref.dtype)


def _attention(qkv, sink):
    t = ATT_BLOCK
    nb = SEQ // t
    off = CTX_LEN // t
    qw = ATT_H * ATT_HD
    kvw = QKV_TILE
    kv_tile = QKV_TILES - 1

    def kv_spec(rows, row_block):
        return pl.BlockSpec((None, None, rows, kvw), lambda b, i: (kv_tile, b, row_block(i), 0))

    return pl.pallas_call(
        _attn_kernel,
        grid=(BATCH, nb),
        in_specs=[pl.BlockSpec(memory_space=pltpu.SMEM),
                  pl.BlockSpec((kv_tile, None, t, QKV_TILE), lambda b, i: (0, b, off + i, 0)),
                  kv_spec(t, lambda i: off + jnp.maximum(i - 1, 0)),
                  kv_spec(t, lambda i: off + i),
                  kv_spec(t, lambda i: off + jnp.minimum(i + 1, nb - 1)),
                  kv_spec(CTX_LEN, lambda i: 0)],
        out_specs=pl.BlockSpec((None, t, qw), lambda b, i: (b, i, 0)),
        out_shape=jax.ShapeDtypeStruct((BATCH, SEQ, qw), bf16),
        scratch_shapes=[pltpu.VMEM((ATT_GRP * t, CTX_LEN + 3 * t), bf16)],
        compiler_params=_cparams("parallel", "parallel"),
        name="window_attention",
    )(sink, qkv, qkv, qkv, qkv, qkv)


def _odd_out_kernel(a_ref, x_ref, gt_ref, gpost_ref, wo_ref, o_ref):
    a = a_ref[...].reshape(BATCH * TT, D_MODEL)
    out = jnp.dot(a, wo_ref[...], preferred_element_type=f32).reshape(BATCH, TT, D_MODEL)
    o_ref[...] = _gated_residual(out, x_ref[...], gt_ref[...], gpost_ref[...])


def _odd_out_proj(a, rows, mods, gains, w_out):
    return pl.pallas_call(
        _odd_out_kernel,
        grid=(SEQ // TT,),
        in_specs=[pl.BlockSpec((BATCH, TT, D_MODEL), lambda i: (0, i, 0)),
                  pl.BlockSpec((BATCH, TT, D_MODEL), lambda i: (0, i + CTX_TILES, 0)),
                  _mod_spec(2, 0), _gain_spec(1),
                  _const_spec((D_MODEL, D_MODEL))],
        out_specs=pl.BlockSpec((BATCH, TT, D_MODEL), lambda i: (0, i, 0)),
        out_shape=jax.ShapeDtypeStruct((BATCH, SEQ, D_MODEL), f32),
        compiler_params=_cparams("parallel"),
        name="odd_out_proj",
    )(a, rows, mods, gains, w_out)


def _layer_mods(m):
    m = m.reshape(MOD_ROWS, MOD_CHUNKS, D_MODEL)
    lat = m[:BATCH]
    ctx = jnp.broadcast_to(m[CTX_MOD_ROW:CTX_MOD_ROW + 1], lat.shape)
    return jnp.stack([lat, ctx], axis=0).transpose(2, 0, 1, 3)[:, :, :, None, :]


def kernel(x, c, ctx, c_ctx, mod_w, mod_b, norm_g, mlp_w1, mlp_w2, even_w_in, even_w_out, s5_lam_re, s5_lam_im, s5_log_dt, s5_b_re, s5_b_im, s5_c_re, s5_c_im, s5_d, s5_glu_w, s5_glu_b, odd_w_in, odd_w_out, odd_sink):
    c_rows = jnp.concatenate([c, c_ctx[None, :], jnp.zeros((MOD_ROWS - BATCH - 1, D_MODEL), f32)], axis=0)
    mods_all = _modulation(c_rows, mod_w, mod_b)
    gains_all = norm_g.reshape(DEPTH, 4, 1, D_MODEL)

    mods, gains = _layer_mods(mods_all[0]), gains_all[0]
    perm = _chunk_swap_matrix()
    w1, w2 = _to_bf16(mlp_w1), _to_bf16(mlp_w2)
    w_in = _to_bf16(even_w_in)[0]
    h, u, rows, u_blocks = _even_u_proj(x, ctx, gains, mods, w_in, perm)
    p = _even_rest_proj(h, w_in)
    tables = _s5_tables(s5_lam_re[0], s5_lam_im[0], s5_log_dt[0], s5_b_re[0], s5_b_im[0], s5_c_re[0], s5_c_im[0])
    y_blocks = _s5_mix(u_blocks, tables)
    o_f, o_b = _retention(p)
    rows = _even_out_proj(u, p, y_blocks, o_f, o_b, rows, mods, gains, s5_d[0].reshape(1, S5_WIDTH),
                          s5_glu_b[0].reshape(1, S5_WIDTH), _to_bf16(s5_glu_w)[0], _to_bf16(even_w_out)[0], perm)
    rows = _mlp(rows, CTX_TILES, mods, gains, w1, w2, 0)

    mods, gains = _layer_mods(mods_all[1]), gains_all[1]
    cos, sin = _rope_tables()
    qkv = _odd_in_proj(rows, gains, mods, _to_bf16(odd_w_in)[0], cos, sin)
    a = _attention(qkv, odd_sink[0].astype(f32))
    lat = _odd_out_proj(a, rows, mods, gains, _to_bf16(odd_w_out)[0])
    return _mlp(lat, 0, mods, gains, w1, w2, 1)
```

```python
import math

import jax
import jax.numpy as jnp
import numpy as np
from jax import lax
from jax.experimental import pallas as pl
from jax.experimental.pallas import tpu as pltpu

D_MODEL = 2048
BATCH = 4
SEQ = 4096
DEPTH = 2
GRID_W = 64
CTX_LEN = 256
MOD_CHUNKS = 6
FFN_HIDDEN = 4 * D_MODEL
NORM_EPS = 1e-6

S5_WIDTH = D_MODEL // 2
S5_P = 16
S5_G = S5_WIDTH // S5_P
S5_N = 64
LAMBDA_RE_MAX = -1e-4
RET_DK = 256
RET_H = (D_MODEL // 2) // RET_DK
RET_DV = (D_MODEL // 2) // RET_H
RET_DECAY_BASE = -5.0
EVEN_IN = S5_WIDTH + 2 * RET_H * RET_DK + 2 * RET_H * RET_DV

ATT_HD = 64
ATT_H = D_MODEL // ATT_HD
ATT_KVH = ATT_H // 8
ATT_GRP = ATT_H // ATT_KVH
ODD_IN = (ATT_H + 2 * ATT_KVH) * ATT_HD
WINDOW = 128
ATT_BLOCK = 128
ROPE_BASE = 10000.0
NEG_INF = -1e30
LOG2_E = math.log2(math.e)
Q_SCALE = ATT_HD ** -0.5 * LOG2_E

LANES = 128
SUBLANES = 8

L_ALL = CTX_LEN + SEQ
TT = 128
N_TILES = L_ALL // TT
CTX_TILES = CTX_LEN // TT
NORM_ROWS = 16
NORM_UNROLL = 4
MOD_ROWS = 8
CTX_MOD_ROW = BATCH

S5_T = 16
S5_TP = S5_T * S5_P
S5_NC = L_ALL // S5_T
S5_NC_CTX = CTX_LEN // S5_T
S5_BPAD = SUBLANES
S5_GB = 4
S5_GB_TABLES = 8
S5_SCAN_UNROLL = 4
S5_TILE_BLOCKS = TT // S5_T
S5_LANE_GROUPS = LANES // S5_P
RET_T = 256
RET_STEPS = L_ALL // RET_T
RET_CHAINS = 2 * RET_H

CAST_BLOCK_ELEMS = 2 * 1024 * 1024
VMEM_LIMIT = 56 * 1024 * 1024

f32 = jnp.float32
bf16 = jnp.bfloat16


def _cparams(*sem):
    return pltpu.CompilerParams(dimension_semantics=sem, vmem_limit_bytes=VMEM_LIMIT)


def _mod_spec(chunk, ctx_tiles):
    return pl.BlockSpec((None, None, BATCH, 1, D_MODEL),
                        lambda i, *_: (chunk, jnp.where(i < ctx_tiles, 1, 0), 0, 0, 0))


def _gain_spec(k):
    return pl.BlockSpec((None, 1, D_MODEL), lambda *_: (k, 0, 0))


def _const_spec(shape):
    return pl.BlockSpec(shape, lambda *_: (0,) * len(shape), pipeline_mode=pl.Buffered(1))


def _norm_mod_rows(x_ref, g_ref, sc_ref, sh_ref, h_ref, copy_ref=None):
    for b in range(BATCH):
        gain = g_ref[...] * (1.0 + sc_ref[b])
        shift = sh_ref[b]

        def body(r, carry):
            r0 = pl.multiple_of(r * NORM_ROWS, NORM_ROWS)
            xf = x_ref[b, pl.ds(r0, NORM_ROWS), :]
            if copy_ref is not None:
                copy_ref[b, pl.ds(r0, NORM_ROWS), :] = xf
            ms = jnp.mean(xf * xf, axis=-1, keepdims=True)
            h = (xf * lax.rsqrt(ms + NORM_EPS) * gain + shift).astype(h_ref.dtype)
            if len(h_ref.shape) == 3:
                h_ref[b, pl.ds(r0, NORM_ROWS), :] = h
            else:
                h_ref[pl.ds(pl.multiple_of(b * TT + r0, NORM_ROWS), NORM_ROWS), :] = h
            return carry

        lax.fori_loop(0, TT // NORM_ROWS, body, 0, unroll=NORM_UNROLL)


def _gated_residual(m, x, gate, g_post):
    ms = jnp.mean(m * m, axis=-1, keepdims=True)
    return x + m * lax.rsqrt(ms + NORM_EPS) * (g_post * gate)


def _mix_epilogue(out, x_ref, gt_ref, gpost_ref, g2_ref, sh2_ref, sc2_ref, o_ref, h_ref):
    x1 = _gated_residual(out, x_ref[...], gt_ref[...], gpost_ref[...])
    o_ref[...] = x1
    ms = jnp.mean(x1 * x1, axis=-1, keepdims=True)
    h_ref[...] = (x1 * lax.rsqrt(ms + NORM_EPS) * (g2_ref[...] * (1.0 + sc2_ref[...])) + sh2_ref[...]).astype(h_ref.dtype)


def _gated_residual_rows(load_rows, x_ref, gt_ref, gpost_ref, o_ref):
    for b in range(BATCH):
        gain = gpost_ref[...] * gt_ref[b]

        def body(r, carry):
            r0 = pl.multiple_of(r * NORM_ROWS, NORM_ROWS)
            m = load_rows(b, r0, NORM_ROWS)
            ms = jnp.mean(m * m, axis=-1, keepdims=True)
            o_ref[b, pl.ds(r0, NORM_ROWS), :] = x_ref[b, pl.ds(r0, NORM_ROWS), :] + m * lax.rsqrt(ms + NORM_EPS) * gain
            return carry

        lax.fori_loop(0, TT // NORM_ROWS, body, 0, unroll=NORM_UNROLL)


def _chunk_swap_matrix():
    n = S5_LANE_GROUPS * LANES
    src = np.arange(n)
    i, j, p = src // LANES, (src // S5_P) % S5_LANE_GROUPS, src % S5_P
    return jnp.asarray(np.arange(n)[None, :] == (j * LANES + i * S5_P + p)[:, None], dtype=bf16)


def _cast_kernel(w_ref, o_ref):
    o_ref[...] = w_ref[...].astype(o_ref.dtype)


def _to_bf16(w):
    n_layers, rows, cols = w.shape
    block_rows = min(rows, pl.next_power_of_2(CAST_BLOCK_ELEMS // cols + 1) // 2)
    spec = pl.BlockSpec((None, block_rows, cols), lambda l, r: (l, r, 0))
    return pl.pallas_call(
        _cast_kernel,
        grid=(n_layers, rows // block_rows),
        in_specs=[spec],
        out_specs=spec,
        out_shape=jax.ShapeDtypeStruct(w.shape, bf16),
        compiler_params=_cparams("parallel", "parallel"),
        name="to_bf16",
    )(w)


def _mod_kernel(c_ref, w_ref, b_ref, o_ref):
    c = c_ref[...]
    s = c * jax.nn.sigmoid(c)
    o_ref[...] = jnp.dot(s.astype(bf16), w_ref[...].astype(bf16), preferred_element_type=f32) + b_ref[...]


def _modulation(c_rows, mod_w, mod_b):
    tn = 1024
    n = MOD_CHUNKS * D_MODEL
    return pl.pallas_call(
        _mod_kernel,
        grid=(DEPTH, n // tn),
        in_specs=[pl.BlockSpec((MOD_ROWS, D_MODEL), lambda l, j: (0, 0)),
                  pl.BlockSpec((None, D_MODEL, tn), lambda l, j: (l, 0, j)),
                  pl.BlockSpec((None, 1, tn), lambda l, j: (l, 0, j))],
        out_specs=pl.BlockSpec((None, MOD_ROWS, tn), lambda l, j: (l, 0, j)),
        out_shape=jax.ShapeDtypeStruct((DEPTH, MOD_ROWS, n), f32),
        compiler_params=_cparams("parallel", "parallel"),
        name="ada_modulation",
    )(c_rows, mod_w, mod_b.reshape(DEPTH, 1, n))


def _even_u_kernel(x_ref, c_ref, g_ref, sh_ref, sc_ref, w_ref, perm_ref, h_ref, u_ref, rows_ref, ub_ref,
                   a_ref, xs_ref, st_ref):
    i = pl.program_id(0)

    @pl.when(i < CTX_TILES)
    def _():
        _norm_mod_rows(c_ref, g_ref, sc_ref, sh_ref, h_ref, copy_ref=rows_ref)

    @pl.when(i >= CTX_TILES)
    def _():
        _norm_mod_rows(x_ref, g_ref, sc_ref, sh_ref, h_ref, copy_ref=rows_ref)

    acc = jnp.dot(h_ref[...].reshape(BATCH * TT, D_MODEL), w_ref[...], preferred_element_type=f32)
    u_ref[...] = acc.reshape(BATCH, TT, S5_WIDTH).astype(u_ref.dtype)

    n_slab, n_half = S5_WIDTH // LANES, S5_TP // LANES
    for s in range(n_slab):
        a_ref[s] = acc[:, s * LANES:(s + 1) * LANES]
    for s in range(n_slab):
        for half in range(n_half):
            for b in range(BATCH):
                r0 = ((s * n_half + half) * BATCH + b) * S5_TILE_BLOCKS
                for tt in range(S5_LANE_GROUPS):
                    xs_ref[tt, r0:r0 + S5_TILE_BLOCKS, :] = a_ref[
                        s, pl.ds(b * TT + half * S5_LANE_GROUPS + tt, S5_TILE_BLOCKS, stride=S5_T), :]
    xs = jnp.concatenate([xs_ref[tt] for tt in range(S5_LANE_GROUPS)], axis=-1).astype(bf16)
    ys = jnp.dot(xs, perm_ref[...], preferred_element_type=f32)
    @pl.when(i == 0)
    def _():
        st_ref[...] = jnp.zeros_like(st_ref)
    for s in range(n_slab):
        for half in range(n_half):
            for b in range(BATCH):
                r0 = ((s * n_half + half) * BATCH + b) * S5_TILE_BLOCKS
                for gg in range(S5_LANE_GROUPS):
                    st_ref[s * S5_LANE_GROUPS + gg, half, pl.ds(b, S5_TILE_BLOCKS, stride=S5_BPAD), :] = (
                        ys[r0:r0 + S5_TILE_BLOCKS, gg * LANES:(gg + 1) * LANES])
    for g in range(S5_G):
        ub_ref[g] = jnp.concatenate([st_ref[g, 0], st_ref[g, 1]], axis=-1).astype(ub_ref.dtype)


def _even_u_proj(x, ctx, gains, mods, w_in, perm):
    blk_rows = S5_TILE_BLOCKS * S5_BPAD
    tile = lambda width: pl.BlockSpec((BATCH, TT, width), lambda i: (0, i, 0))
    return pl.pallas_call(
        _even_u_kernel,
        grid=(N_TILES,),
        in_specs=[pl.BlockSpec((BATCH, TT, D_MODEL), lambda i: (0, jnp.maximum(i - CTX_TILES, 0), 0)),
                  pl.BlockSpec((BATCH, TT, D_MODEL), lambda i: (0, jnp.minimum(i, CTX_TILES - 1), 0)),
                  _gain_spec(0), _mod_spec(0, CTX_TILES), _mod_spec(1, CTX_TILES),
                  pl.BlockSpec((D_MODEL, S5_WIDTH), lambda i: (0, 0), pipeline_mode=pl.Buffered(1)),
                  _const_spec(perm.shape)],
        out_specs=[tile(D_MODEL), tile(S5_WIDTH), tile(D_MODEL),
                   pl.BlockSpec((S5_G, blk_rows, S5_TP), lambda i: (0, i, 0))],
        out_shape=[jax.ShapeDtypeStruct((BATCH, L_ALL, D_MODEL), bf16),
                   jax.ShapeDtypeStruct((BATCH, L_ALL, S5_WIDTH), bf16),
                   jax.ShapeDtypeStruct((BATCH, L_ALL, D_MODEL), f32),
                   jax.ShapeDtypeStruct((S5_G, S5_NC * S5_BPAD, S5_TP), bf16)],
        scratch_shapes=[pltpu.VMEM((S5_WIDTH // LANES, BATCH * TT, LANES), f32),
                        pltpu.VMEM((S5_LANE_GROUPS, BATCH * TT, LANES), f32),
                        pltpu.VMEM((S5_G, S5_TP // LANES, blk_rows, LANES), f32)],
        compiler_params=_cparams("arbitrary"),
        name="even_u_proj",
    )(x, ctx, gains, mods, mods, w_in, perm)


EVEN_REST_TILES = (EVEN_IN - S5_WIDTH) // S5_WIDTH


def _even_rest_kernel(h_ref, *refs):
    w_refs, o_ref = refs[:EVEN_REST_TILES], refs[EVEN_REST_TILES]
    h = h_ref[...].reshape(BATCH * TT, D_MODEL)
    for jt, w_ref in enumerate(w_refs):
        acc = jnp.dot(h, w_ref[...], preferred_element_type=f32)
        o_ref[:, :, jt * S5_WIDTH:(jt + 1) * S5_WIDTH] = acc.reshape(BATCH, TT, S5_WIDTH).astype(o_ref.dtype)


def _even_rest_proj(h, w_in):
    width = EVEN_IN - S5_WIDTH
    w_specs = [pl.BlockSpec((D_MODEL, S5_WIDTH), lambda i, jt=jt: (0, 1 + jt), pipeline_mode=pl.Buffered(1))
               for jt in range(EVEN_REST_TILES)]
    return pl.pallas_call(
        _even_rest_kernel,
        grid=(N_TILES,),
        in_specs=[pl.BlockSpec((BATCH, TT, D_MODEL), lambda i: (0, i, 0))] + w_specs,
        out_specs=pl.BlockSpec((BATCH, TT, width), lambda i: (0, i, 0)),
        out_shape=jax.ShapeDtypeStruct((BATCH, L_ALL, width), bf16),
        compiler_params=_cparams("parallel"),
        name="even_rest_proj",
    )(h, *([w_in] * EVEN_REST_TILES))


def _s5_tables(lam_re, lam_im, log_dt, b_re, b_im, c_re, c_im):
    lr = jnp.minimum(lam_re.astype(f32), LAMBDA_RE_MAX)
    li = lam_im.astype(f32)
    dt = jnp.exp(log_dt.astype(f32))[..., None]
    zr, zi = lr * dt, li * dt
    ab_mag = jnp.exp(zr)
    ab_re, ab_im = ab_mag * jnp.cos(zi), ab_mag * jnp.sin(zi)
    den = lr * lr + li * li
    nr = ab_re - 1.0
    f_re = (nr * lr + ab_im * li) / den
    f_im = (ab_im * lr - nr * li) / den
    br, bi = b_re.astype(f32), b_im.astype(f32)
    bb_re = f_re[..., None] * br - f_im[..., None] * bi
    bb_im = f_re[..., None] * bi + f_im[..., None] * br
    steps = jnp.arange(S5_T + 1, dtype=f32)[:, None]
    zr_k, zi_k = zr[:, :, None, :] * steps, zi[:, :, None, :] * steps
    pw_mag = jnp.exp(zr_k)
    pw_re, pw_im = pw_mag * jnp.cos(zi_k), pw_mag * jnp.sin(zi_k)
    cr, ci = c_re.astype(f32), c_im.astype(f32)
    bt_re, bt_im = bb_re.transpose(0, 1, 3, 2), bb_im.transpose(0, 1, 3, 2)

    kf, kb = slice(0, S5_T), slice(S5_T - 1, None, -1)
    ca_re = cr[:, :, None] * pw_re[:, :, :, None, :] - ci[:, :, None] * pw_im[:, :, :, None, :]
    ca_im = cr[:, :, None] * pw_im[:, :, :, None, :] + ci[:, :, None] * pw_re[:, :, :, None, :]
    ca_f = jnp.concatenate([ca_re[0, :, kf], ca_im[0, :, kf]], axis=-1).reshape(S5_G, S5_TP, 2 * S5_N)
    ca_b = jnp.concatenate([ca_re[1, :, kb], ca_im[1, :, kb]], axis=-1).reshape(S5_G, S5_TP, 2 * S5_N)
    bcat = jnp.concatenate([bt_re, -bt_im], axis=-1)
    w_intra = _s5_intra_matrix(bcat[0], bcat[1], ca_f, ca_b)

    def state_in(d, k):
        pr, pi = pw_re[d][:, k][:, :, None, :], pw_im[d][:, k][:, :, None, :]
        br_, bi_ = bt_re[d][:, None], bt_im[d][:, None]
        return ((pr * br_ - pi * bi_).reshape(S5_G, S5_TP, S5_N), (pr * bi_ + pi * br_).reshape(S5_G, S5_TP, S5_N))

    sf_re, sf_im = state_in(0, kb)
    sb_re, sb_im = state_in(1, kf)
    w_state = jnp.concatenate([sf_re, sb_re, sf_im, sb_im], axis=-1)

    ct_re, ct_im = cr.transpose(0, 1, 3, 2), ci.transpose(0, 1, 3, 2)
    pt_re, pt_im = pw_re.transpose(0, 1, 3, 2), pw_im.transpose(0, 1, 3, 2)

    def state_out(d, k):
        pr, pi = pt_re[d][:, :, k][..., None], pt_im[d][:, :, k][..., None]
        cr_, ci_ = ct_re[d][:, :, None, :], ct_im[d][:, :, None, :]
        return ((cr_ * pr - ci_ * pi).reshape(S5_G, S5_N, S5_TP), (cr_ * pi + ci_ * pr).reshape(S5_G, S5_N, S5_TP))

    xf_re, xf_im = state_out(0, slice(1, S5_T + 1))
    xb_re, xb_im = state_out(1, slice(S5_T, 0, -1))
    w_cross = jnp.concatenate([xf_re, xb_re, -xf_im, -xb_im], axis=1)
    a_re = jnp.concatenate([pw_re[0, :, S5_T], pw_re[1, :, S5_T]], axis=-1).reshape(S5_G, 1, 2 * S5_N)
    a_im = jnp.concatenate([pw_im[0, :, S5_T], pw_im[1, :, S5_T]], axis=-1).reshape(S5_G, 1, 2 * S5_N)
    return w_intra, w_state.astype(bf16), w_cross.astype(bf16), a_re, a_im


def _s5_intra_kernel(bf_ref, bb_ref, caf_ref, cab_ref, w_ref):
    lane = lax.broadcasted_iota(jnp.int32, (S5_P, S5_TP), 1)
    dn = (((1,), (1,)), ((), ()))
    for g in range(S5_GB_TABLES):
        tq_f = lax.dot_general(bf_ref[g], caf_ref[g], dn, precision=lax.Precision.HIGHEST, preferred_element_type=f32)
        tq_b = lax.dot_general(bb_ref[g], cab_ref[g], dn, precision=lax.Precision.HIGHEST, preferred_element_type=f32)
        for s in range(S5_T):
            sf, sb = S5_P * s, S5_P * (S5_T - 1 - s)
            row_f = tq_f if s == 0 else jnp.where(lane >= sf, pltpu.roll(tq_f, sf, 1), 0.0)
            row_b = tq_b if sb == 0 else jnp.where(lane < S5_TP - sb, pltpu.roll(tq_b, S5_TP - sb, 1), 0.0)
            w_ref[g, s * S5_P:(s + 1) * S5_P, :] = (row_f + row_b).astype(w_ref.dtype)


def _s5_intra_matrix(b_f, b_b, ca_f, ca_b):
    def spec(rows, cols):
        return pl.BlockSpec((S5_GB_TABLES, rows, cols), lambda i: (i, 0, 0))

    return pl.pallas_call(
        _s5_intra_kernel,
        grid=(S5_G // S5_GB_TABLES,),
        in_specs=[spec(S5_P, 2 * S5_N), spec(S5_P, 2 * S5_N), spec(S5_TP, 2 * S5_N), spec(S5_TP, 2 * S5_N)],
        out_specs=spec(S5_TP, S5_TP),
        out_shape=jax.ShapeDtypeStruct((S5_G, S5_TP, S5_TP), bf16),
        compiler_params=_cparams("parallel"),
        name="s5_intra_matrix",
    )(b_f, b_b, ca_f, ca_b)


def _s5_kernel(u_ref, wi_ref, ws_ref, wc_ref, are_ref, aim_ref, y_ref, s_ref):
    half = 2 * S5_N
    for g in range(S5_GB):
        s_ref[g] = jnp.dot(u_ref[g], ws_ref[g], preferred_element_type=f32)

    fwd_lane = lax.broadcasted_iota(jnp.int32, (S5_BPAD, half), 1) < S5_N
    a_re = [jnp.broadcast_to(are_ref[g], (S5_BPAD, half)) for g in range(S5_GB)]
    a_im = [jnp.broadcast_to(aim_ref[g], (S5_BPAD, half)) for g in range(S5_GB)]

    def step(i, carry):
        cb = jnp.where(i < S5_NC_CTX, S5_NC_CTX - 1 - i, S5_NC + S5_NC_CTX - 1 - i)
        rf = pl.multiple_of(i * S5_BPAD, S5_BPAD)
        rb = pl.multiple_of(cb * S5_BPAD, S5_BPAD)
        out = []
        for g in range(S5_GB):
            h_re, h_im = carry[2 * g], carry[2 * g + 1]
            l_re = jnp.where(fwd_lane, s_ref[g, pl.ds(rf, S5_BPAD), 0:half], s_ref[g, pl.ds(rb, S5_BPAD), 0:half])
            l_im = jnp.where(fwd_lane, s_ref[g, pl.ds(rf, S5_BPAD), half:2 * half],
                             s_ref[g, pl.ds(rb, S5_BPAD), half:2 * half])
            s_ref[g, pl.ds(rf, S5_BPAD), 0:S5_N] = h_re[:, 0:S5_N]
            s_ref[g, pl.ds(rb, S5_BPAD), S5_N:half] = h_re[:, S5_N:half]
            s_ref[g, pl.ds(rf, S5_BPAD), half:half + S5_N] = h_im[:, 0:S5_N]
            s_ref[g, pl.ds(rb, S5_BPAD), half + S5_N:2 * half] = h_im[:, S5_N:half]
            out.append(a_re[g] * h_re - a_im[g] * h_im + l_re)
            out.append(a_re[g] * h_im + a_im[g] * h_re + l_im)
        return tuple(out)

    zero = jnp.zeros((S5_BPAD, half), f32)
    lax.fori_loop(0, S5_NC, step, (zero,) * (2 * S5_GB), unroll=S5_SCAN_UNROLL)

    for g in range(S5_GB):
        y = jnp.dot(u_ref[g], wi_ref[g], preferred_element_type=f32)
        y = y + jnp.dot(s_ref[g].astype(bf16), wc_ref[g], preferred_element_type=f32)
        for hf in range(S5_TP // LANES):
            y_ref[g, hf] = y[:, hf * LANES:(hf + 1) * LANES]


def _s5_mix(u_blocks, tables):
    w_intra, w_state, w_cross, a_re, a_im = tables
    rows = S5_NC * S5_BPAD
    wsp = pl.BlockSpec((S5_GB, S5_TP, S5_TP), lambda g: (g, 0, 0))
    asp = pl.BlockSpec((S5_GB, 1, 2 * S5_N), lambda g: (g, 0, 0))
    return pl.pallas_call(
        _s5_kernel,
        grid=(S5_G // S5_GB,),
        in_specs=[pl.BlockSpec((S5_GB, rows, S5_TP), lambda g: (g, 0, 0)), wsp, wsp, wsp, asp, asp],
        out_specs=pl.BlockSpec((S5_GB, S5_TP // LANES, rows, LANES), lambda g: (g, 0, 0, 0)),
        out_shape=jax.ShapeDtypeStruct((S5_G, S5_TP // LANES, rows, LANES), f32),
        scratch_shapes=[pltpu.VMEM((S5_GB, rows, S5_TP), f32)],
        compiler_params=_cparams("parallel"),
        name="s5_mix",
    )(u_blocks, w_intra, w_state, w_cross, a_re, a_im)


def _ret_kernel(lg_ref, cd_ref, qf_ref, kf_ref, vf_ref, qb_ref, kb_ref, vb_ref, of_ref, ob_ref,
                r_ref, dec_ref, qdec_ref, kdec_ref):
    t = RET_T

    @pl.when(pl.program_id(1) == 0)
    def _():
        r_ref[...] = jnp.zeros_like(r_ref)
        n = lax.broadcasted_iota(jnp.int32, (t, t), 0)
        m = lax.broadcasted_iota(jnp.int32, (t, t), 1)
        pos = lax.broadcasted_iota(jnp.int32, (t, 1), 0)
        for d in range(2):
            diff = (n - m if d == 0 else m - n).astype(f32)
            p = (pos if d == 0 else t - 1 - pos).astype(f32)
            for h in range(RET_H):
                c = d * RET_H + h
                lg = lg_ref[c]
                dec_ref[c] = jnp.where(diff >= 0, jnp.exp(jnp.maximum(diff, 0.0) * lg), 0.0)
                qdec_ref[c] = jnp.exp((p + 1.0) * lg)
                kdec_ref[c] = jnp.exp((t - 1.0 - p) * lg)

    for d, (q_ref, k_ref, v_ref, o_ref) in enumerate(((qf_ref, kf_ref, vf_ref, of_ref),
                                                        (qb_ref, kb_ref, vb_ref, ob_ref))):
        for h in range(RET_H):
            c = d * RET_H + h
            cols = slice(h * RET_DK, (h + 1) * RET_DK)
            q = q_ref[:, cols]
            ks = k_ref[:, cols] * (RET_DK ** -0.5)
            v = v_ref[:, cols]
            r = r_ref[c]
            s = lax.dot_general(q, ks, (((1,), (1,)), ((), ())), preferred_element_type=f32) * dec_ref[c]
            inner = jnp.dot(s.astype(bf16), v, preferred_element_type=f32)
            cross = jnp.dot(q, r.astype(bf16), preferred_element_type=f32) * qdec_ref[c]
            o_ref[:, cols] = (inner + cross).astype(o_ref.dtype)
            kd = (ks.astype(f32) * kdec_ref[c]).astype(bf16)
            r_ref[c] = r * cd_ref[c] + lax.dot_general(kd, v, (((0,), (0,)), ((), ())), preferred_element_type=f32)


def _retention(p):
    e = RET_DECAY_BASE - (2.0 * np.arange(RET_H, dtype=np.float64)[None, :] + np.arange(2, dtype=np.float64)[:, None])
    log_g64 = np.log1p(-np.exp2(e)).reshape(RET_CHAINS)
    log_g = jnp.asarray(log_g64, dtype=f32)
    chunk_dec = jnp.asarray(np.exp(RET_T * log_g64), dtype=f32)
    width = RET_H * RET_DK
    col0 = 0

    def bwd_chunk(i):
        return jnp.where(i == 0, 0, RET_STEPS - i)

    def fwd(off):
        return pl.BlockSpec((None, RET_T, width), lambda b, i: (b, i, col0 + off))

    def bwd(off):
        return pl.BlockSpec((None, RET_T, width), lambda b, i: (b, bwd_chunk(i), col0 + off))

    smem = pl.BlockSpec(memory_space=pltpu.SMEM)
    out = jax.ShapeDtypeStruct((BATCH, L_ALL, RET_H * RET_DV), bf16)
    return pl.pallas_call(
        _ret_kernel,
        grid=(BATCH, RET_STEPS),
        in_specs=[smem, smem, fwd(0), fwd(1), fwd(2), bwd(0), bwd(1), bwd(2)],
        out_specs=[pl.BlockSpec((None, RET_T, RET_H * RET_DV), lambda b, i: (b, i, 0)),
                   pl.BlockSpec((None, RET_T, RET_H * RET_DV), lambda b, i: (b, bwd_chunk(i), 0))],
        out_shape=[out, out],
        scratch_shapes=[pltpu.VMEM((RET_CHAINS, RET_DK, RET_DV), f32),
                        pltpu.VMEM((RET_CHAINS, RET_T, RET_T), f32),
                        pltpu.VMEM((RET_CHAINS, RET_T, 1), f32),
                        pltpu.VMEM((RET_CHAINS, RET_T, 1), f32)],
        compiler_params=_cparams("parallel", "arbitrary"),
        name="retention",
    )(log_g, chunk_dec, p, p, p, p, p, p)


def _gelu_tanh(x):
    return 0.5 * x * (1.0 + jnp.tanh(math.sqrt(2.0 / math.pi) * (x + 0.044715 * (x * x * x))))


def _even_out_kernel(u_ref, gate_ref, y_ref, of_ref, ob_ref, x_ref, gt_ref, gpost_ref, g2_ref, sh2_ref, sc2_ref,
                     dsk_ref, glub_ref, gluw_ref, wo_ref, perm_ref, o_ref, h_ref, z_ref, xs_ref, yf_ref):
    rows = BATCH * TT
    n_slab, n_half = S5_WIDTH // LANES, S5_TP // LANES
    for s in range(n_slab):
        for half in range(n_half):
            for b in range(BATCH):
                r0 = ((s * n_half + half) * BATCH + b) * S5_TILE_BLOCKS
                for gg in range(S5_LANE_GROUPS):
                    xs_ref[gg, r0:r0 + S5_TILE_BLOCKS, :] = y_ref[
                        s * S5_LANE_GROUPS + gg, half, pl.ds(b, S5_TILE_BLOCKS, stride=S5_BPAD), :]
    xs = jnp.concatenate([xs_ref[gg] for gg in range(S5_LANE_GROUPS)], axis=-1).astype(bf16)
    ys = jnp.dot(xs, perm_ref[...], preferred_element_type=f32)
    for s in range(n_slab):
        for half in range(n_half):
            for b in range(BATCH):
                r0 = ((s * n_half + half) * BATCH + b) * S5_TILE_BLOCKS
                for tt in range(S5_LANE_GROUPS):
                    z_ref[s, pl.ds(b * TT + half * S5_LANE_GROUPS + tt, S5_TILE_BLOCKS, stride=S5_T), :] = (
                        ys[r0:r0 + S5_TILE_BLOCKS, tt * LANES:(tt + 1) * LANES])
    u = u_ref[...].reshape(rows, S5_WIDTH)
    for s in range(S5_WIDTH // LANES):
        cols = slice(s * LANES, (s + 1) * LANES)
        yf_ref[:, cols] = _gelu_tanh(z_ref[s] + dsk_ref[:, cols] * u[:, cols].astype(f32))
    y = yf_ref[...]
    z = jnp.dot(y.astype(bf16), gluw_ref[...], preferred_element_type=f32) + glub_ref[...]
    s5_out = y * jax.nn.sigmoid(z)
    r = (of_ref[...].astype(f32) + ob_ref[...].astype(f32)).reshape(rows, RET_H * RET_DV)
    heads = []
    for h in range(RET_H):
        rh = r[:, h * RET_DV:(h + 1) * RET_DV]
        heads.append(rh * lax.rsqrt(jnp.mean(rh * rh, axis=-1, keepdims=True) + NORM_EPS))
    g = gate_ref[...].reshape(rows, RET_H * RET_DV).astype(f32)
    ret_out = jnp.concatenate(heads, axis=-1) * (g * jax.nn.sigmoid(g))
    out = jnp.dot(s5_out.astype(bf16), wo_ref[0:S5_WIDTH, :], preferred_element_type=f32)
    out = out + jnp.dot(ret_out.astype(bf16), wo_ref[S5_WIDTH:, :], preferred_element_type=f32)
    _mix_epilogue(out.reshape(BATCH, TT, D_MODEL), x_ref, gt_ref, gpost_ref, g2_ref, sh2_ref, sc2_ref, o_ref, h_ref)


def _even_out_proj(u, p, y_blocks, o_f, o_b, rows, mods, gains, d_skip, glu_b, glu_w, w_out, perm):
    half = S5_WIDTH
    blk_rows = S5_TILE_BLOCKS * S5_BPAD
    tile = pl.BlockSpec((BATCH, TT, half), lambda i: (0, i, 0))
    return pl.pallas_call(
        _even_out_kernel,
        grid=(N_TILES,),
        in_specs=[tile,
                  pl.BlockSpec((BATCH, TT, half), lambda i: (0, i, EVEN_REST_TILES - 1)),
                  pl.BlockSpec((S5_G, S5_TP // LANES, blk_rows, LANES), lambda i: (0, 0, i, 0)),
                  tile, tile,
                  pl.BlockSpec((BATCH, TT, D_MODEL), lambda i: (0, i, 0)),
                  _mod_spec(2, CTX_TILES), _gain_spec(1),
                  _gain_spec(2), _mod_spec(3, CTX_TILES), _mod_spec(4, CTX_TILES),
                  _const_spec((1, half)), _const_spec((1, half)),
                  _const_spec((half, half)), _const_spec((D_MODEL, D_MODEL)), _const_spec(perm.shape)],
        out_specs=[pl.BlockSpec((BATCH, TT, D_MODEL), lambda i: (0, i, 0)),
                   pl.BlockSpec((BATCH, TT, D_MODEL), lambda i: (0, i, 0))],
        out_shape=[jax.ShapeDtypeStruct((BATCH, L_ALL, D_MODEL), f32),
                   jax.ShapeDtypeStruct((BATCH, L_ALL, D_MODEL), bf16)],
        scratch_shapes=[pltpu.VMEM((S5_WIDTH // LANES, BATCH * TT, LANES), f32),
                        pltpu.VMEM((S5_LANE_GROUPS, BATCH * TT, LANES), f32),
                        pltpu.VMEM((BATCH * TT, S5_WIDTH), f32)],
        compiler_params=_cparams("parallel"),
        name="even_out_proj",
    )(u, p, y_blocks, o_f, o_b, rows, mods, gains, gains, mods, mods, d_skip, glu_b, glu_w, w_out, perm)


def _mlp_kernel(x_ref, h_ref, gt_ref, gpost_ref, w1_ref, w2_ref, o_ref, acc_ref):
    f = pl.program_id(1)

    @pl.when(f == 0)
    def _():
        acc_ref[...] = jnp.zeros_like(acc_ref)

    h = h_ref[...].reshape(BATCH * TT, D_MODEL)
    a = jnp.maximum(jnp.dot(h, w1_ref[...], preferred_element_type=f32), 0.0)
    acc_ref[...] += jnp.dot((a * a).astype(bf16), w2_ref[...], preferred_element_type=f32)

    @pl.when(f == pl.num_programs(1) - 1)
    def _():
        _gated_residual_rows(lambda b, r0, n: acc_ref[pl.ds(pl.multiple_of(b * TT + r0, n), n), :],
                             x_ref, gt_ref, gpost_ref, o_ref)


def _mlp(x, h, ctx_tiles, mods, gains, w1, w2, layer):
    tf = 1024
    n_tiles = x.shape[1] // TT
    tile = pl.BlockSpec((BATCH, TT, D_MODEL), lambda i, f: (0, i, 0))
    return pl.pallas_call(
        _mlp_kernel,
        grid=(n_tiles, FFN_HIDDEN // tf),
        in_specs=[tile, tile, _mod_spec(5, ctx_tiles), _gain_spec(3),
                  pl.BlockSpec((None, D_MODEL, tf), lambda i, f: (layer, 0, f)),
                  pl.BlockSpec((None, tf, D_MODEL), lambda i, f: (layer, f, 0))],
        out_specs=tile,
        out_shape=jax.ShapeDtypeStruct((BATCH, n_tiles * TT, D_MODEL), f32),
        scratch_shapes=[pltpu.VMEM((BATCH * TT, D_MODEL), f32)],
        compiler_params=_cparams("parallel", "arbitrary"),
        name="sq_relu_mlp",
    )(x, h, mods, gains, w1, w2)


def _rope_tables():
    rows = SEQ // GRID_W
    row = np.repeat(np.arange(rows, dtype=np.float64), GRID_W)
    col = np.tile(np.arange(GRID_W, dtype=np.float64), rows)
    n_freq = ATT_HD // 4
    inv_freq = ROPE_BASE ** (-np.arange(n_freq, dtype=np.float64) / n_freq)
    ang = np.concatenate([row[:, None] * inv_freq[None], col[:, None] * inv_freq[None]], axis=-1)
    cos, sin = np.cos(ang), np.sin(ang)
    cos, sin = np.tile(cos, (1, 4)), np.concatenate([-sin, sin, -sin, sin], axis=-1)
    return (jnp.asarray(np.stack([cos * Q_SCALE, cos, np.ones_like(cos)]), dtype=f32),
            jnp.asarray(np.stack([sin * Q_SCALE, sin, np.zeros_like(sin)]), dtype=f32))


ROPE_Q, ROPE_K, ROPE_NONE = 0, 1, 2


def _rope(x, cos, sin):
    half = ATT_HD // 2
    lane = lax.broadcasted_iota(jnp.int32, x.shape, x.ndim - 1)
    partner = jnp.where((lane & (ATT_HD - 1)) < half,
                        pltpu.roll(x, LANES - half, x.ndim - 1), pltpu.roll(x, half, x.ndim - 1))
    return x * cos + partner * sin


QKV_TILE = 2 * ATT_KVH * ATT_HD
QKV_TILES = ODD_IN // QKV_TILE


def _odd_in_kernel(x_ref, g_ref, sh_ref, sc_ref, w_ref, cos_q_ref, sin_q_ref, cos_k_ref, sin_k_ref, o_ref, h_ref):
    _norm_mod_rows(x_ref, g_ref, sc_ref, sh_ref, h_ref)
    kw = ATT_KVH * ATT_HD
    for jt in range(QKV_TILES):
        acc = jnp.dot(h_ref[...], w_ref[:, jt * QKV_TILE:(jt + 1) * QKV_TILE], preferred_element_type=f32)
        for c in range(QKV_TILE // LANES):
            cols = slice(c * LANES, (c + 1) * LANES)
            xc = acc[:, cols].reshape(BATCH, TT, LANES)
            if jt < QKV_TILES - 1:
                xc = _rope(xc, cos_q_ref[...], sin_q_ref[...])
            elif c * LANES < kw:
                xc = _rope(xc, cos_k_ref[...], sin_k_ref[...])
            o_ref[jt, :, :, cols] = xc.astype(o_ref.dtype)


def _odd_in_proj(x, gains, mods, w_in, cos, sin):
    def table(kind_of):
        return pl.BlockSpec((None, TT, LANES), lambda i: (kind_of(i), jnp.maximum(i - CTX_TILES, 0), 0))

    q_table = table(lambda i: ROPE_Q)
    k_table = table(lambda i: jnp.where(i >= CTX_TILES, ROPE_K, ROPE_NONE))
    return pl.pallas_call(
        _odd_in_kernel,
        grid=(N_TILES,),
        in_specs=[pl.BlockSpec((BATCH, TT, D_MODEL), lambda i: (0, i, 0)),
                  _gain_spec(0), _mod_spec(0, CTX_TILES), _mod_spec(1, CTX_TILES),
                  _const_spec((D_MODEL, ODD_IN)), q_table, q_table, k_table, k_table],
        out_specs=pl.BlockSpec((QKV_TILES, BATCH, TT, QKV_TILE), lambda i: (0, 0, i, 0)),
        out_shape=jax.ShapeDtypeStruct((QKV_TILES, BATCH, L_ALL, QKV_TILE), bf16),
        scratch_shapes=[pltpu.VMEM((BATCH * TT, D_MODEL), bf16)],
        compiler_params=_cparams("parallel"),
        name="odd_in_proj",
    )(x, gains, mods, mods, w_in, cos, sin, cos, sin)


def _attn_kernel(sink_ref, q_ref, kvp_ref, kvc_ref, kvn_ref, kvx_ref, o_ref):
    qb = pl.program_id(1)
    t = ATT_BLOCK
    kw = ATT_KVH * ATT_HD
    row = lax.broadcasted_iota(jnp.int32, (t, t), 0)
    col = lax.broadcasted_iota(jnp.int32, (t, t), 1)

    def band_valid(blk):
        off = col + blk * t
        kpos = qb * t - WINDOW + off
        return (jnp.abs(off - WINDOW - row) <= WINDOW) & (kpos >= 0) & (kpos < SEQ)

    n_ctx_chunks = CTX_LEN // LANES
    masks = {n_ctx_chunks: band_valid(0), n_ctx_chunks + 2: band_valid(2)}
    n_chunks = n_ctx_chunks + 3

    for kh in range(ATT_KVH):
        ks = slice(kh * ATT_HD, (kh + 1) * ATT_HD)
        vs = slice(kw + kh * ATT_HD, kw + (kh + 1) * ATT_HD)
        k_all = jnp.concatenate([kvx_ref[:, ks], kvp_ref[:, ks], kvc_ref[:, ks], kvn_ref[:, ks]], axis=0)
        v_all = jnp.concatenate([kvx_ref[:, vs], kvp_ref[:, vs], kvc_ref[:, vs], kvn_ref[:, vs]], axis=0)
        heads = [kh * ATT_GRP + g for g in range(ATT_GRP)]
        per_tile = QKV_TILE // ATT_HD
        q_all = jnp.concatenate([q_ref[h // per_tile, :, (h % per_tile) * ATT_HD:(h % per_tile + 1) * ATT_HD]
                                 for h in heads], axis=0)
        s_all = lax.dot_general(q_all, k_all, (((1,), (1,)), ((), ())), preferred_element_type=f32)
        probs, inv_den = [], []
        for g in range(ATT_GRP):
            sink = sink_ref[kh * ATT_GRP + g] * LOG2_E
            s = s_all[g * t:(g + 1) * t]
            s = jnp.concatenate([jnp.where(masks[c], s[:, c * LANES:(c + 1) * LANES], NEG_INF) if c in masks
                                 else s[:, c * LANES:(c + 1) * LANES] for c in range(n_chunks)], axis=-1)
            m = jnp.maximum(jnp.max(s, axis=-1, keepdims=True), sink)
            e = jnp.exp2(s - m)
            inv_den.append(1.0 / (jnp.sum(e, axis=-1, keepdims=True) + jnp.exp2(sink - m)))
            probs.append(e.astype(bf16))
        o_all = jnp.dot(jnp.concatenate(probs, axis=0), v_all, preferred_element_type=f32)
        for g in range(ATT_GRP):
            h = kh * ATT_GRP + g
            o_ref[:, h * ATT_HD:(h + 1) * ATT_HD] = (o_all[g * t:(g + 1) * t] * inv_den[g]).astype(o_ref.dtype)


def _attention(qkv, sink):
    t = ATT_BLOCK
    nb = SEQ // t
    off = CTX_LEN // t
    qw = ATT_H * ATT_HD
    kvw = QKV_TILE
    kv_tile = QKV_TILES - 1

    def kv_spec(rows, row_block):
        return pl.BlockSpec((None, None, rows, kvw), lambda b, i: (kv_tile, b, row_block(i), 0))

    return pl.pallas_call(
        _attn_kernel,
        grid=(BATCH, nb),
        in_specs=[pl.BlockSpec(memory_space=pltpu.SMEM),
                  pl.BlockSpec((kv_tile, None, t, QKV_TILE), lambda b, i: (0, b, off + i, 0)),
                  kv_spec(t, lambda i: off + jnp.maximum(i - 1, 0)),
                  kv_spec(t, lambda i: off + i),
                  kv_spec(t, lambda i: off + jnp.minimum(i + 1, nb - 1)),
                  kv_spec(CTX_LEN, lambda i: 0)],
        out_specs=pl.BlockSpec((None, t, qw), lambda b, i: (b, i, 0)),
        out_shape=jax.ShapeDtypeStruct((BATCH, SEQ, qw), bf16),
        compiler_params=_cparams("parallel", "parallel"),
        name="window_attention",
    )(sink, qkv, qkv, qkv, qkv, qkv)


def _odd_out_kernel(a_ref, x_ref, gt_ref, gpost_ref, g2_ref, sh2_ref, sc2_ref, wo_ref, o_ref, h_ref):
    a = a_ref[...].reshape(BATCH * TT, D_MODEL)
    out = jnp.dot(a, wo_ref[...], preferred_element_type=f32).reshape(BATCH, TT, D_MODEL)
    _mix_epilogue(out, x_ref, gt_ref, gpost_ref, g2_ref, sh2_ref, sc2_ref, o_ref, h_ref)


def _odd_out_proj(a, rows, mods, gains, w_out):
    tile = pl.BlockSpec((BATCH, TT, D_MODEL), lambda i: (0, i, 0))
    return pl.pallas_call(
        _odd_out_kernel,
        grid=(SEQ // TT,),
        in_specs=[tile,
                  pl.BlockSpec((BATCH, TT, D_MODEL), lambda i: (0, i + CTX_TILES, 0)),
                  _mod_spec(2, 0), _gain_spec(1), _gain_spec(2), _mod_spec(3, 0), _mod_spec(4, 0),
                  _const_spec((D_MODEL, D_MODEL))],
        out_specs=[tile, tile],
        out_shape=[jax.ShapeDtypeStruct((BATCH, SEQ, D_MODEL), f32),
                   jax.ShapeDtypeStruct((BATCH, SEQ, D_MODEL), bf16)],
        compiler_params=_cparams("parallel"),
        name="odd_out_proj",
    )(a, rows, mods, gains, gains, mods, mods, w_out)


def _layer_mods(m):
    m = m.reshape(MOD_ROWS, MOD_CHUNKS, D_MODEL)
    lat = m[:BATCH]
    ctx = jnp.broadcast_to(m[CTX_MOD_ROW:CTX_MOD_ROW + 1], lat.shape)
    return jnp.stack([lat, ctx], axis=0).transpose(2, 0, 1, 3)[:, :, :, None, :]


def kernel(x, c, ctx, c_ctx, mod_w, mod_b, norm_g, mlp_w1, mlp_w2, even_w_in, even_w_out, s5_lam_re, s5_lam_im, s5_log_dt, s5_b_re, s5_b_im, s5_c_re, s5_c_im, s5_d, s5_glu_w, s5_glu_b, odd_w_in, odd_w_out, odd_sink):
    c_rows = jnp.concatenate([c, c_ctx[None, :], jnp.zeros((MOD_ROWS - BATCH - 1, D_MODEL), f32)], axis=0)
    mods_all = _modulation(c_rows, mod_w, mod_b)
    gains_all = norm_g.reshape(DEPTH, 4, 1, D_MODEL)

    mods, gains = _layer_mods(mods_all[0]), gains_all[0]
    perm = _chunk_swap_matrix()
    w1, w2 = _to_bf16(mlp_w1), _to_bf16(mlp_w2)
    w_in = _to_bf16(even_w_in)[0]
    h, u, rows, u_blocks = _even_u_proj(x, ctx, gains, mods, w_in, perm)
    p = _even_rest_proj(h, w_in)
    tables = _s5_tables(s5_lam_re[0], s5_lam_im[0], s5_log_dt[0], s5_b_re[0], s5_b_im[0], s5_c_re[0], s5_c_im[0])
    y_blocks = _s5_mix(u_blocks, tables)
    o_f, o_b = _retention(p)
    rows, h = _even_out_proj(u, p, y_blocks, o_f, o_b, rows, mods, gains, s5_d[0].reshape(1, S5_WIDTH),
                             s5_glu_b[0].reshape(1, S5_WIDTH), _to_bf16(s5_glu_w)[0], _to_bf16(even_w_out)[0], perm)
    rows = _mlp(rows, h, CTX_TILES, mods, gains, w1, w2, 0)

    mods, gains = _layer_mods(mods_all[1]), gains_all[1]
    cos, sin = _rope_tables()
    qkv = _odd_in_proj(rows, gains, mods, _to_bf16(odd_w_in)[0], cos, sin)
    a = _attention(qkv, odd_sink[0].astype(f32))
    lat, h = _odd_out_proj(a, rows, mods, gains, _to_bf16(odd_w_out)[0])
    return _mlp(lat, h, 0, mods, gains, w1, w2, 1)
```

```python
import math

import jax
import jax.numpy as jnp
import numpy as np
from jax import lax
from jax.experimental import pallas as pl
from jax.experimental.pallas import tpu as pltpu

D_MODEL = 2048
BATCH = 4
SEQ = 4096
DEPTH = 2
GRID_W = 64
CTX_LEN = 256
MOD_CHUNKS = 6
FFN_HIDDEN = 4 * D_MODEL
NORM_EPS = 1e-6

S5_WIDTH = D_MODEL // 2
S5_P = 16
S5_G = S5_WIDTH // S5_P
S5_N = 64
LAMBDA_RE_MAX = -1e-4
RET_DK = 256
RET_H = (D_MODEL // 2) // RET_DK
RET_DV = (D_MODEL // 2) // RET_H
RET_DECAY_BASE = -5.0
EVEN_IN = S5_WIDTH + 2 * RET_H * RET_DK + 2 * RET_H * RET_DV

ATT_HD = 64
ATT_H = D_MODEL // ATT_HD
ATT_KVH = ATT_H // 8
ATT_GRP = ATT_H // ATT_KVH
ODD_IN = (ATT_H + 2 * ATT_KVH) * ATT_HD
WINDOW = 128
ATT_BLOCK = 128
ROPE_BASE = 10000.0
NEG_INF = -1e30
LOG2_E = math.log2(math.e)
Q_SCALE = ATT_HD ** -0.5 * LOG2_E

LANES = 128
SUBLANES = 8

L_ALL = CTX_LEN + SEQ
TT = 128
N_TILES = L_ALL // TT
CTX_TILES = CTX_LEN // TT
NORM_ROWS = 16
NORM_UNROLL = 4
MOD_ROWS = 8
CTX_MOD_ROW = BATCH

S5_T = 16
S5_TP = S5_T * S5_P
S5_NC = L_ALL // S5_T
S5_NC_CTX = CTX_LEN // S5_T
S5_BPAD = SUBLANES
S5_GB = 4
S5_GB_TABLES = 8
S5_SCAN_UNROLL = 4
S5_TILE_BLOCKS = TT // S5_T
S5_LANE_GROUPS = LANES // S5_P
RET_T = 256
RET_STEPS = L_ALL // RET_T
RET_CHAINS = 2 * RET_H

CAST_BLOCK_ELEMS = 2 * 1024 * 1024
VMEM_LIMIT = 56 * 1024 * 1024

f32 = jnp.float32
bf16 = jnp.bfloat16


def _cparams(*sem):
    return pltpu.CompilerParams(dimension_semantics=sem, vmem_limit_bytes=VMEM_LIMIT)


def _mod_spec(chunk, ctx_tiles):
    return pl.BlockSpec((None, None, BATCH, 1, D_MODEL),
                        lambda i, *_: (chunk, jnp.where(i < ctx_tiles, 1, 0), 0, 0, 0))


def _gain_spec(k):
    return pl.BlockSpec((None, 1, D_MODEL), lambda *_: (k, 0, 0))


def _const_spec(shape):
    return pl.BlockSpec(shape, lambda *_: (0,) * len(shape), pipeline_mode=pl.Buffered(1))


def _norm_mod_rows(x_ref, g_ref, sc_ref, sh_ref, h_ref, copy_ref=None):
    for b in range(BATCH):
        gain = g_ref[...] * (1.0 + sc_ref[b])
        shift = sh_ref[b]

        def body(r, carry):
            r0 = pl.multiple_of(r * NORM_ROWS, NORM_ROWS)
            xf = x_ref[b, pl.ds(r0, NORM_ROWS), :]
            if copy_ref is not None:
                copy_ref[b, pl.ds(r0, NORM_ROWS), :] = xf
            ms = jnp.mean(xf * xf, axis=-1, keepdims=True)
            h = (xf * lax.rsqrt(ms + NORM_EPS) * gain + shift).astype(h_ref.dtype)
            if len(h_ref.shape) == 3:
                h_ref[b, pl.ds(r0, NORM_ROWS), :] = h
            else:
                h_ref[pl.ds(pl.multiple_of(b * TT + r0, NORM_ROWS), NORM_ROWS), :] = h
            return carry

        lax.fori_loop(0, TT // NORM_ROWS, body, 0, unroll=NORM_UNROLL)


def _gated_residual(m, x, gate, g_post):
    ms = jnp.mean(m * m, axis=-1, keepdims=True)
    return x + m * lax.rsqrt(ms + NORM_EPS) * (g_post * gate)


def _mix_epilogue(out, x_ref, gt_ref, gpost_ref, g2_ref, sh2_ref, sc2_ref, o_ref, h_ref):
    x1 = _gated_residual(out, x_ref[...], gt_ref[...], gpost_ref[...])
    o_ref[...] = x1
    ms = jnp.mean(x1 * x1, axis=-1, keepdims=True)
    h_ref[...] = (x1 * lax.rsqrt(ms + NORM_EPS) * (g2_ref[...] * (1.0 + sc2_ref[...])) + sh2_ref[...]).astype(h_ref.dtype)


def _gated_residual_rows(load_rows, x_ref, gt_ref, gpost_ref, o_ref):
    for b in range(BATCH):
        gain = gpost_ref[...] * gt_ref[b]

        def body(r, carry):
            r0 = pl.multiple_of(r * NORM_ROWS, NORM_ROWS)
            m = load_rows(b, r0, NORM_ROWS)
            ms = jnp.mean(m * m, axis=-1, keepdims=True)
            o_ref[b, pl.ds(r0, NORM_ROWS), :] = x_ref[b, pl.ds(r0, NORM_ROWS), :] + m * lax.rsqrt(ms + NORM_EPS) * gain
            return carry

        lax.fori_loop(0, TT // NORM_ROWS, body, 0, unroll=NORM_UNROLL)


def _chunk_swap_matrix():
    n = S5_LANE_GROUPS * LANES
    src = np.arange(n)
    i, j, p = src // LANES, (src // S5_P) % S5_LANE_GROUPS, src % S5_P
    return jnp.asarray(np.arange(n)[None, :] == (j * LANES + i * S5_P + p)[:, None], dtype=bf16)


def _cast_kernel(w_ref, o_ref):
    o_ref[...] = w_ref[...].astype(o_ref.dtype)


def _to_bf16(w):
    n_layers, rows, cols = w.shape
    block_rows = min(rows, pl.next_power_of_2(CAST_BLOCK_ELEMS // cols + 1) // 2)
    spec = pl.BlockSpec((None, block_rows, cols), lambda l, r: (l, r, 0))
    return pl.pallas_call(
        _cast_kernel,
        grid=(n_layers, rows // block_rows),
        in_specs=[spec],
        out_specs=spec,
        out_shape=jax.ShapeDtypeStruct(w.shape, bf16),
        compiler_params=_cparams("parallel", "parallel"),
        name="to_bf16",
    )(w)


def _mod_kernel(c_ref, w_ref, b_ref, o_ref):
    c = c_ref[...]
    s = c * jax.nn.sigmoid(c)
    o_ref[...] = jnp.dot(s.astype(bf16), w_ref[...].astype(bf16), preferred_element_type=f32) + b_ref[...]


def _modulation(c_rows, mod_w, mod_b):
    tn = 1024
    n = MOD_CHUNKS * D_MODEL
    return pl.pallas_call(
        _mod_kernel,
        grid=(DEPTH, n // tn),
        in_specs=[pl.BlockSpec((MOD_ROWS, D_MODEL), lambda l, j: (0, 0)),
                  pl.BlockSpec((None, D_MODEL, tn), lambda l, j: (l, 0, j)),
                  pl.BlockSpec((None, 1, tn), lambda l, j: (l, 0, j))],
        out_specs=pl.BlockSpec((None, MOD_ROWS, tn), lambda l, j: (l, 0, j)),
        out_shape=jax.ShapeDtypeStruct((DEPTH, MOD_ROWS, n), f32),
        compiler_params=_cparams("parallel", "parallel"),
        name="ada_modulation",
    )(c_rows, mod_w, mod_b.reshape(DEPTH, 1, n))


def _even_u_kernel(x_ref, c_ref, g_ref, sh_ref, sc_ref, w_ref, perm_ref, h_ref, u_ref, rows_ref, ub_ref,
                   a_ref, xs_ref, st_ref):
    i = pl.program_id(0)

    @pl.when(i < CTX_TILES)
    def _():
        _norm_mod_rows(c_ref, g_ref, sc_ref, sh_ref, h_ref, copy_ref=rows_ref)

    @pl.when(i >= CTX_TILES)
    def _():
        _norm_mod_rows(x_ref, g_ref, sc_ref, sh_ref, h_ref, copy_ref=rows_ref)

    acc = jnp.dot(h_ref[...].reshape(BATCH * TT, D_MODEL), w_ref[...], preferred_element_type=f32)
    u_ref[...] = acc.reshape(BATCH, TT, S5_WIDTH).astype(u_ref.dtype)

    n_slab, n_half = S5_WIDTH // LANES, S5_TP // LANES
    for s in range(n_slab):
        a_ref[s] = acc[:, s * LANES:(s + 1) * LANES]
    for s in range(n_slab):
        for half in range(n_half):
            for b in range(BATCH):
                r0 = ((s * n_half + half) * BATCH + b) * S5_TILE_BLOCKS
                for tt in range(S5_LANE_GROUPS):
                    xs_ref[tt, r0:r0 + S5_TILE_BLOCKS, :] = a_ref[
                        s, pl.ds(b * TT + half * S5_LANE_GROUPS + tt, S5_TILE_BLOCKS, stride=S5_T), :]
    xs = jnp.concatenate([xs_ref[tt] for tt in range(S5_LANE_GROUPS)], axis=-1).astype(bf16)
    ys = jnp.dot(xs, perm_ref[...], preferred_element_type=f32)
    @pl.when(i == 0)
    def _():
        st_ref[...] = jnp.zeros_like(st_ref)
    for s in range(n_slab):
        for half in range(n_half):
            for b in range(BATCH):
                r0 = ((s * n_half + half) * BATCH + b) * S5_TILE_BLOCKS
                for gg in range(S5_LANE_GROUPS):
                    st_ref[s * S5_LANE_GROUPS + gg, half, pl.ds(b, S5_TILE_BLOCKS, stride=S5_BPAD), :] = (
                        ys[r0:r0 + S5_TILE_BLOCKS, gg * LANES:(gg + 1) * LANES])
    for g in range(S5_G):
        ub_ref[g] = jnp.concatenate([st_ref[g, 0], st_ref[g, 1]], axis=-1).astype(ub_ref.dtype)


def _even_u_proj(x, ctx, gains, mods, w_in, perm):
    blk_rows = S5_TILE_BLOCKS * S5_BPAD
    tile = lambda width: pl.BlockSpec((BATCH, TT, width), lambda i: (0, i, 0))
    return pl.pallas_call(
        _even_u_kernel,
        grid=(N_TILES,),
        in_specs=[pl.BlockSpec((BATCH, TT, D_MODEL), lambda i: (0, jnp.maximum(i - CTX_TILES, 0), 0)),
                  pl.BlockSpec((BATCH, TT, D_MODEL), lambda i: (0, jnp.minimum(i, CTX_TILES - 1), 0)),
                  _gain_spec(0), _mod_spec(0, CTX_TILES), _mod_spec(1, CTX_TILES),
                  pl.BlockSpec((D_MODEL, S5_WIDTH), lambda i: (0, 0), pipeline_mode=pl.Buffered(1)),
                  _const_spec(perm.shape)],
        out_specs=[tile(D_MODEL), tile(S5_WIDTH), tile(D_MODEL),
                   pl.BlockSpec((S5_G, blk_rows, S5_TP), lambda i: (0, i, 0))],
        out_shape=[jax.ShapeDtypeStruct((BATCH, L_ALL, D_MODEL), bf16),
                   jax.ShapeDtypeStruct((BATCH, L_ALL, S5_WIDTH), bf16),
                   jax.ShapeDtypeStruct((BATCH, L_ALL, D_MODEL), f32),
                   jax.ShapeDtypeStruct((S5_G, S5_NC * S5_BPAD, S5_TP), bf16)],
        scratch_shapes=[pltpu.VMEM((S5_WIDTH // LANES, BATCH * TT, LANES), f32),
                        pltpu.VMEM((S5_LANE_GROUPS, BATCH * TT, LANES), f32),
                        pltpu.VMEM((S5_G, S5_TP // LANES, blk_rows, LANES), f32)],
        compiler_params=_cparams("arbitrary"),
        name="even_u_proj",
    )(x, ctx, gains, mods, mods, w_in, perm)


EVEN_REST_TILES = (EVEN_IN - S5_WIDTH) // S5_WIDTH


def _cast_rider(ws, layer, n_blocks, block_of):
    in_specs, out_specs, out_shapes = [], [], []
    for w in ws:
        _, rows, cols = w.shape
        in_specs.append(pl.BlockSpec((None, rows // n_blocks, cols), lambda *g: (layer, block_of(*g), 0)))
        out_specs.append(pl.BlockSpec((rows // n_blocks, cols), lambda *g: (block_of(*g), 0)))
        out_shapes.append(jax.ShapeDtypeStruct((rows, cols), bf16))
    return in_specs, out_specs, out_shapes


def _even_rest_kernel(h_ref, *refs):
    w_refs, refs = refs[:EVEN_REST_TILES], refs[EVEN_REST_TILES:]
    w1f_ref, w2f_ref, o_ref, w1b_ref, w2b_ref = refs
    h = h_ref[...].reshape(BATCH * TT, D_MODEL)
    for jt, w_ref in enumerate(w_refs):
        acc = jnp.dot(h, w_ref[...], preferred_element_type=f32)
        o_ref[:, :, jt * S5_WIDTH:(jt + 1) * S5_WIDTH] = acc.reshape(BATCH, TT, S5_WIDTH).astype(o_ref.dtype)
    w1b_ref[...] = w1f_ref[...].astype(w1b_ref.dtype)
    w2b_ref[...] = w2f_ref[...].astype(w2b_ref.dtype)


def _even_rest_proj(h, w_in, mlp_w1, mlp_w2):
    width = EVEN_IN - S5_WIDTH
    w_specs = [pl.BlockSpec((D_MODEL, S5_WIDTH), lambda i, jt=jt: (0, 1 + jt), pipeline_mode=pl.Buffered(1))
               for jt in range(EVEN_REST_TILES)]
    n_blocks = 32
    c_in, c_out, c_shapes = _cast_rider((mlp_w1, mlp_w2), 0, n_blocks, lambda i: jnp.minimum(i, n_blocks - 1))
    return pl.pallas_call(
        _even_rest_kernel,
        grid=(N_TILES,),
        in_specs=[pl.BlockSpec((BATCH, TT, D_MODEL), lambda i: (0, i, 0))] + w_specs + c_in,
        out_specs=[pl.BlockSpec((BATCH, TT, width), lambda i: (0, i, 0))] + c_out,
        out_shape=[jax.ShapeDtypeStruct((BATCH, L_ALL, width), bf16)] + c_shapes,
        compiler_params=_cparams("arbitrary"),
        name="even_rest_proj",
    )(h, *([w_in] * EVEN_REST_TILES), mlp_w1, mlp_w2)


def _s5_tables(lam_re, lam_im, log_dt, b_re, b_im, c_re, c_im):
    lr = jnp.minimum(lam_re.astype(f32), LAMBDA_RE_MAX)
    li = lam_im.astype(f32)
    dt = jnp.exp(log_dt.astype(f32))[..., None]
    zr, zi = lr * dt, li * dt
    ab_mag = jnp.exp(zr)
    ab_re, ab_im = ab_mag * jnp.cos(zi), ab_mag * jnp.sin(zi)
    den = lr * lr + li * li
    nr = ab_re - 1.0
    f_re = (nr * lr + ab_im * li) / den
    f_im = (ab_im * lr - nr * li) / den
    br, bi = b_re.astype(f32), b_im.astype(f32)
    bb_re = f_re[..., None] * br - f_im[..., None] * bi
    bb_im = f_re[..., None] * bi + f_im[..., None] * br
    steps = jnp.arange(S5_T + 1, dtype=f32)[:, None]
    zr_k, zi_k = zr[:, :, None, :] * steps, zi[:, :, None, :] * steps
    pw_mag = jnp.exp(zr_k)
    pw_re, pw_im = pw_mag * jnp.cos(zi_k), pw_mag * jnp.sin(zi_k)
    cr, ci = c_re.astype(f32), c_im.astype(f32)
    bt_re, bt_im = bb_re.transpose(0, 1, 3, 2), bb_im.transpose(0, 1, 3, 2)

    kf, kb = slice(0, S5_T), slice(S5_T - 1, None, -1)
    ca_re = cr[:, :, None] * pw_re[:, :, :, None, :] - ci[:, :, None] * pw_im[:, :, :, None, :]
    ca_im = cr[:, :, None] * pw_im[:, :, :, None, :] + ci[:, :, None] * pw_re[:, :, :, None, :]
    ca_f = jnp.concatenate([ca_re[0, :, kf], ca_im[0, :, kf]], axis=-1).reshape(S5_G, S5_TP, 2 * S5_N)
    ca_b = jnp.concatenate([ca_re[1, :, kb], ca_im[1, :, kb]], axis=-1).reshape(S5_G, S5_TP, 2 * S5_N)
    bcat = jnp.concatenate([bt_re, -bt_im], axis=-1)
    w_intra = _s5_intra_matrix(bcat[0], bcat[1], ca_f, ca_b)

    def state_in(d, k):
        pr, pi = pw_re[d][:, k][:, :, None, :], pw_im[d][:, k][:, :, None, :]
        br_, bi_ = bt_re[d][:, None], bt_im[d][:, None]
        return ((pr * br_ - pi * bi_).reshape(S5_G, S5_TP, S5_N), (pr * bi_ + pi * br_).reshape(S5_G, S5_TP, S5_N))

    sf_re, sf_im = state_in(0, kb)
    sb_re, sb_im = state_in(1, kf)
    w_state = jnp.concatenate([sf_re, sb_re, sf_im, sb_im], axis=-1)

    ct_re, ct_im = cr.transpose(0, 1, 3, 2), ci.transpose(0, 1, 3, 2)
    pt_re, pt_im = pw_re.transpose(0, 1, 3, 2), pw_im.transpose(0, 1, 3, 2)

    def state_out(d, k):
        pr, pi = pt_re[d][:, :, k][..., None], pt_im[d][:, :, k][..., None]
        cr_, ci_ = ct_re[d][:, :, None, :], ct_im[d][:, :, None, :]
        return ((cr_ * pr - ci_ * pi).reshape(S5_G, S5_N, S5_TP), (cr_ * pi + ci_ * pr).reshape(S5_G, S5_N, S5_TP))

    xf_re, xf_im = state_out(0, slice(1, S5_T + 1))
    xb_re, xb_im = state_out(1, slice(S5_T, 0, -1))
    w_cross = jnp.concatenate([xf_re, xb_re, -xf_im, -xb_im], axis=1)
    a_re = jnp.concatenate([pw_re[0, :, S5_T], pw_re[1, :, S5_T]], axis=-1).reshape(S5_G, 1, 2 * S5_N)
    a_im = jnp.concatenate([pw_im[0, :, S5_T], pw_im[1, :, S5_T]], axis=-1).reshape(S5_G, 1, 2 * S5_N)
    return w_intra, w_state.astype(bf16), w_cross.astype(bf16), a_re, a_im


def _s5_intra_kernel(bf_ref, bb_ref, caf_ref, cab_ref, w_ref):
    lane = lax.broadcasted_iota(jnp.int32, (S5_P, S5_TP), 1)
    dn = (((1,), (1,)), ((), ()))
    for g in range(S5_GB_TABLES):
        tq_f = lax.dot_general(bf_ref[g], caf_ref[g], dn, precision=lax.Precision.HIGHEST, preferred_element_type=f32)
        tq_b = lax.dot_general(bb_ref[g], cab_ref[g], dn, precision=lax.Precision.HIGHEST, preferred_element_type=f32)
        for s in range(S5_T):
            sf, sb = S5_P * s, S5_P * (S5_T - 1 - s)
            row_f = tq_f if s == 0 else jnp.where(lane >= sf, pltpu.roll(tq_f, sf, 1), 0.0)
            row_b = tq_b if sb == 0 else jnp.where(lane < S5_TP - sb, pltpu.roll(tq_b, S5_TP - sb, 1), 0.0)
            w_ref[g, s * S5_P:(s + 1) * S5_P, :] = (row_f + row_b).astype(w_ref.dtype)


def _s5_intra_matrix(b_f, b_b, ca_f, ca_b):
    def spec(rows, cols):
        return pl.BlockSpec((S5_GB_TABLES, rows, cols), lambda i: (i, 0, 0))

    return pl.pallas_call(
        _s5_intra_kernel,
        grid=(S5_G // S5_GB_TABLES,),
        in_specs=[spec(S5_P, 2 * S5_N), spec(S5_P, 2 * S5_N), spec(S5_TP, 2 * S5_N), spec(S5_TP, 2 * S5_N)],
        out_specs=spec(S5_TP, S5_TP),
        out_shape=jax.ShapeDtypeStruct((S5_G, S5_TP, S5_TP), bf16),
        compiler_params=_cparams("parallel"),
        name="s5_intra_matrix",
    )(b_f, b_b, ca_f, ca_b)


def _s5_kernel(u_ref, wi_ref, ws_ref, wc_ref, are_ref, aim_ref, y_ref, s_ref):
    half = 2 * S5_N
    for g in range(S5_GB):
        s_ref[g] = jnp.dot(u_ref[g], ws_ref[g], preferred_element_type=f32)

    fwd_lane = lax.broadcasted_iota(jnp.int32, (S5_BPAD, half), 1) < S5_N
    a_re = [jnp.broadcast_to(are_ref[g], (S5_BPAD, half)) for g in range(S5_GB)]
    a_im = [jnp.broadcast_to(aim_ref[g], (S5_BPAD, half)) for g in range(S5_GB)]

    def step(i, carry):
        cb = jnp.where(i < S5_NC_CTX, S5_NC_CTX - 1 - i, S5_NC + S5_NC_CTX - 1 - i)
        rf = pl.multiple_of(i * S5_BPAD, S5_BPAD)
        rb = pl.multiple_of(cb * S5_BPAD, S5_BPAD)
        out = []
        for g in range(S5_GB):
            h_re, h_im = carry[2 * g], carry[2 * g + 1]
            l_re = jnp.where(fwd_lane, s_ref[g, pl.ds(rf, S5_BPAD), 0:half], s_ref[g, pl.ds(rb, S5_BPAD), 0:half])
            l_im = jnp.where(fwd_lane, s_ref[g, pl.ds(rf, S5_BPAD), half:2 * half],
                             s_ref[g, pl.ds(rb, S5_BPAD), half:2 * half])
            s_ref[g, pl.ds(rf, S5_BPAD), 0:S5_N] = h_re[:, 0:S5_N]
            s_ref[g, pl.ds(rb, S5_BPAD), S5_N:half] = h_re[:, S5_N:half]
            s_ref[g, pl.ds(rf, S5_BPAD), half:half + S5_N] = h_im[:, 0:S5_N]
            s_ref[g, pl.ds(rb, S5_BPAD), half + S5_N:2 * half] = h_im[:, S5_N:half]
            out.append(a_re[g] * h_re - a_im[g] * h_im + l_re)
            out.append(a_re[g] * h_im + a_im[g] * h_re + l_im)
        return tuple(out)

    zero = jnp.zeros((S5_BPAD, half), f32)
    lax.fori_loop(0, S5_NC, step, (zero,) * (2 * S5_GB), unroll=S5_SCAN_UNROLL)

    for g in range(S5_GB):
        y = jnp.dot(u_ref[g], wi_ref[g], preferred_element_type=f32)
        y = y + jnp.dot(s_ref[g].astype(bf16), wc_ref[g], preferred_element_type=f32)
        for hf in range(S5_TP // LANES):
            y_ref[g, hf] = y[:, hf * LANES:(hf + 1) * LANES]


def _s5_mix(u_blocks, tables):
    w_intra, w_state, w_cross, a_re, a_im = tables
    rows = S5_NC * S5_BPAD
    wsp = pl.BlockSpec((S5_GB, S5_TP, S5_TP), lambda g: (g, 0, 0))
    asp = pl.BlockSpec((S5_GB, 1, 2 * S5_N), lambda g: (g, 0, 0))
    return pl.pallas_call(
        _s5_kernel,
        grid=(S5_G // S5_GB,),
        in_specs=[pl.BlockSpec((S5_GB, rows, S5_TP), lambda g: (g, 0, 0)), wsp, wsp, wsp, asp, asp],
        out_specs=pl.BlockSpec((S5_GB, S5_TP // LANES, rows, LANES), lambda g: (g, 0, 0, 0)),
        out_shape=jax.ShapeDtypeStruct((S5_G, S5_TP // LANES, rows, LANES), f32),
        scratch_shapes=[pltpu.VMEM((S5_GB, rows, S5_TP), f32)],
        compiler_params=_cparams("parallel"),
        name="s5_mix",
    )(u_blocks, w_intra, w_state, w_cross, a_re, a_im)


def _ret_kernel(lg_ref, cd_ref, qf_ref, kf_ref, vf_ref, qb_ref, kb_ref, vb_ref, of_ref, ob_ref,
                r_ref, dec_ref, qdec_ref, kdec_ref):
    t = RET_T

    @pl.when(pl.program_id(1) == 0)
    def _():
        r_ref[...] = jnp.zeros_like(r_ref)
        n = lax.broadcasted_iota(jnp.int32, (t, t), 0)
        m = lax.broadcasted_iota(jnp.int32, (t, t), 1)
        pos = lax.broadcasted_iota(jnp.int32, (t, 1), 0)
        for d in range(2):
            diff = (n - m if d == 0 else m - n).astype(f32)
            p = (pos if d == 0 else t - 1 - pos).astype(f32)
            for h in range(RET_H):
                c = d * RET_H + h
                lg = lg_ref[c]
                dec_ref[c] = jnp.where(diff >= 0, jnp.exp(jnp.maximum(diff, 0.0) * lg), 0.0)
                qdec_ref[c] = jnp.exp((p + 1.0) * lg)
                kdec_ref[c] = jnp.exp((t - 1.0 - p) * lg)

    for d, (q_ref, k_ref, v_ref, o_ref) in enumerate(((qf_ref, kf_ref, vf_ref, of_ref),
                                                        (qb_ref, kb_ref, vb_ref, ob_ref))):
        for h in range(RET_H):
            c = d * RET_H + h
            cols = slice(h * RET_DK, (h + 1) * RET_DK)
            q = q_ref[:, cols]
            ks = k_ref[:, cols] * (RET_DK ** -0.5)
            v = v_ref[:, cols]
            r = r_ref[c]
            s = lax.dot_general(q, ks, (((1,), (1,)), ((), ())), preferred_element_type=f32) * dec_ref[c]
            inner = jnp.dot(s.astype(bf16), v, preferred_element_type=f32)
            cross = jnp.dot(q, r.astype(bf16), preferred_element_type=f32) * qdec_ref[c]
            o_ref[:, cols] = (inner + cross).astype(o_ref.dtype)
            kd = (ks.astype(f32) * kdec_ref[c]).astype(bf16)
            r_ref[c] = r * cd_ref[c] + lax.dot_general(kd, v, (((0,), (0,)), ((), ())), preferred_element_type=f32)


def _retention(p):
    e = RET_DECAY_BASE - (2.0 * np.arange(RET_H, dtype=np.float64)[None, :] + np.arange(2, dtype=np.float64)[:, None])
    log_g64 = np.log1p(-np.exp2(e)).reshape(RET_CHAINS)
    log_g = jnp.asarray(log_g64, dtype=f32)
    chunk_dec = jnp.asarray(np.exp(RET_T * log_g64), dtype=f32)
    width = RET_H * RET_DK
    col0 = 0

    def bwd_chunk(i):
        return jnp.where(i == 0, 0, RET_STEPS - i)

    def fwd(off):
        return pl.BlockSpec((None, RET_T, width), lambda b, i: (b, i, col0 + off))

    def bwd(off):
        return pl.BlockSpec((None, RET_T, width), lambda b, i: (b, bwd_chunk(i), col0 + off))

    smem = pl.BlockSpec(memory_space=pltpu.SMEM)
    out = jax.ShapeDtypeStruct((BATCH, L_ALL, RET_H * RET_DV), bf16)
    return pl.pallas_call(
        _ret_kernel,
        grid=(BATCH, RET_STEPS),
        in_specs=[smem, smem, fwd(0), fwd(1), fwd(2), bwd(0), bwd(1), bwd(2)],
        out_specs=[pl.BlockSpec((None, RET_T, RET_H * RET_DV), lambda b, i: (b, i, 0)),
                   pl.BlockSpec((None, RET_T, RET_H * RET_DV), lambda b, i: (b, bwd_chunk(i), 0))],
        out_shape=[out, out],
        scratch_shapes=[pltpu.VMEM((RET_CHAINS, RET_DK, RET_DV), f32),
                        pltpu.VMEM((RET_CHAINS, RET_T, RET_T), f32),
                        pltpu.VMEM((RET_CHAINS, RET_T, 1), f32),
                        pltpu.VMEM((RET_CHAINS, RET_T, 1), f32)],
        compiler_params=_cparams("parallel", "arbitrary"),
        name="retention",
    )(log_g, chunk_dec, p, p, p, p, p, p)


def _gelu_tanh(x):
    return 0.5 * x * (1.0 + jnp.tanh(math.sqrt(2.0 / math.pi) * (x + 0.044715 * (x * x * x))))


def _even_out_kernel(u_ref, gate_ref, y_ref, of_ref, ob_ref, x_ref, gt_ref, gpost_ref, g2_ref, sh2_ref, sc2_ref,
                     dsk_ref, glub_ref, gluw_ref, wo_ref, perm_ref, o_ref, h_ref, z_ref, xs_ref, yf_ref):
    rows = BATCH * TT
    n_slab, n_half = S5_WIDTH // LANES, S5_TP // LANES
    for s in range(n_slab):
        for half in range(n_half):
            for b in range(BATCH):
                r0 = ((s * n_half + half) * BATCH + b) * S5_TILE_BLOCKS
                for gg in range(S5_LANE_GROUPS):
                    xs_ref[gg, r0:r0 + S5_TILE_BLOCKS, :] = y_ref[
                        s * S5_LANE_GROUPS + gg, half, pl.ds(b, S5_TILE_BLOCKS, stride=S5_BPAD), :]
    xs = jnp.concatenate([xs_ref[gg] for gg in range(S5_LANE_GROUPS)], axis=-1).astype(bf16)
    ys = jnp.dot(xs, perm_ref[...], preferred_element_type=f32)
    for s in range(n_slab):
        for half in range(n_half):
            for b in range(BATCH):
                r0 = ((s * n_half + half) * BATCH + b) * S5_TILE_BLOCKS
                for tt in range(S5_LANE_GROUPS):
                    z_ref[s, pl.ds(b * TT + half * S5_LANE_GROUPS + tt, S5_TILE_BLOCKS, stride=S5_T), :] = (
                        ys[r0:r0 + S5_TILE_BLOCKS, tt * LANES:(tt + 1) * LANES])
    u = u_ref[...].reshape(rows, S5_WIDTH)
    for s in range(S5_WIDTH // LANES):
        cols = slice(s * LANES, (s + 1) * LANES)
        yf_ref[:, cols] = _gelu_tanh(z_ref[s] + dsk_ref[:, cols] * u[:, cols].astype(f32))
    y = yf_ref[...]
    z = jnp.dot(y.astype(bf16), gluw_ref[...], preferred_element_type=f32) + glub_ref[...]
    s5_out = y * jax.nn.sigmoid(z)
    r = (of_ref[...].astype(f32) + ob_ref[...].astype(f32)).reshape(rows, RET_H * RET_DV)
    heads = []
    for h in range(RET_H):
        rh = r[:, h * RET_DV:(h + 1) * RET_DV]
        heads.append(rh * lax.rsqrt(jnp.mean(rh * rh, axis=-1, keepdims=True) + NORM_EPS))
    g = gate_ref[...].reshape(rows, RET_H * RET_DV).astype(f32)
    ret_out = jnp.concatenate(heads, axis=-1) * (g * jax.nn.sigmoid(g))
    out = jnp.dot(s5_out.astype(bf16), wo_ref[0:S5_WIDTH, :], preferred_element_type=f32)
    out = out + jnp.dot(ret_out.astype(bf16), wo_ref[S5_WIDTH:, :], preferred_element_type=f32)
    _mix_epilogue(out.reshape(BATCH, TT, D_MODEL), x_ref, gt_ref, gpost_ref, g2_ref, sh2_ref, sc2_ref, o_ref, h_ref)


def _even_out_proj(u, p, y_blocks, o_f, o_b, rows, mods, gains, d_skip, glu_b, glu_w, w_out, perm):
    half = S5_WIDTH
    blk_rows = S5_TILE_BLOCKS * S5_BPAD
    tile = pl.BlockSpec((BATCH, TT, half), lambda i: (0, i, 0))
    return pl.pallas_call(
        _even_out_kernel,
        grid=(N_TILES,),
        in_specs=[tile,
                  pl.BlockSpec((BATCH, TT, half), lambda i: (0, i, EVEN_REST_TILES - 1)),
                  pl.BlockSpec((S5_G, S5_TP // LANES, blk_rows, LANES), lambda i: (0, 0, i, 0)),
                  tile, tile,
                  pl.BlockSpec((BATCH, TT, D_MODEL), lambda i: (0, i, 0)),
                  _mod_spec(2, CTX_TILES), _gain_spec(1),
                  _gain_spec(2), _mod_spec(3, CTX_TILES), _mod_spec(4, CTX_TILES),
                  _const_spec((1, half)), _const_spec((1, half)),
                  _const_spec((half, half)), _const_spec((D_MODEL, D_MODEL)), _const_spec(perm.shape)],
        out_specs=[pl.BlockSpec((BATCH, TT, D_MODEL), lambda i: (0, i, 0)),
                   pl.BlockSpec((BATCH, TT, D_MODEL), lambda i: (0, i, 0))],
        out_shape=[jax.ShapeDtypeStruct((BATCH, L_ALL, D_MODEL), f32),
                   jax.ShapeDtypeStruct((BATCH, L_ALL, D_MODEL), bf16)],
        scratch_shapes=[pltpu.VMEM((S5_WIDTH // LANES, BATCH * TT, LANES), f32),
                        pltpu.VMEM((S5_LANE_GROUPS, BATCH * TT, LANES), f32),
                        pltpu.VMEM((BATCH * TT, S5_WIDTH), f32)],
        compiler_params=_cparams("parallel"),
        name="even_out_proj",
    )(u, p, y_blocks, o_f, o_b, rows, mods, gains, gains, mods, mods, d_skip, glu_b, glu_w, w_out, perm)


def _mlp_kernel(x_ref, h_ref, gt_ref, gpost_ref, w1_ref, w2_ref, o_ref, acc_ref):
    f = pl.program_id(1)

    @pl.when(f == 0)
    def _():
        acc_ref[...] = jnp.zeros_like(acc_ref)

    h = h_ref[...].reshape(BATCH * TT, D_MODEL)
    a = jnp.maximum(jnp.dot(h, w1_ref[...], preferred_element_type=f32), 0.0)
    acc_ref[...] += jnp.dot((a * a).astype(bf16), w2_ref[...], preferred_element_type=f32)

    @pl.when(f == pl.num_programs(1) - 1)
    def _():
        _gated_residual_rows(lambda b, r0, n: acc_ref[pl.ds(pl.multiple_of(b * TT + r0, n), n), :],
                             x_ref, gt_ref, gpost_ref, o_ref)


def _mlp(x, h, ctx_tiles, mods, gains, w1, w2):
    tf = 1024
    n_tiles = x.shape[1] // TT
    tile = pl.BlockSpec((BATCH, TT, D_MODEL), lambda i, f: (0, i, 0))
    return pl.pallas_call(
        _mlp_kernel,
        grid=(n_tiles, FFN_HIDDEN // tf),
        in_specs=[tile, tile, _mod_spec(5, ctx_tiles), _gain_spec(3),
                  pl.BlockSpec((D_MODEL, tf), lambda i, f: (0, f)),
                  pl.BlockSpec((tf, D_MODEL), lambda i, f: (f, 0))],
        out_specs=tile,
        out_shape=jax.ShapeDtypeStruct((BATCH, n_tiles * TT, D_MODEL), f32),
        scratch_shapes=[pltpu.VMEM((BATCH * TT, D_MODEL), f32)],
        compiler_params=_cparams("parallel", "arbitrary"),
        name="sq_relu_mlp",
    )(x, h, mods, gains, w1, w2)


def _rope_tables():
    rows = SEQ // GRID_W
    row = np.repeat(np.arange(rows, dtype=np.float64), GRID_W)
    col = np.tile(np.arange(GRID_W, dtype=np.float64), rows)
    n_freq = ATT_HD // 4
    inv_freq = ROPE_BASE ** (-np.arange(n_freq, dtype=np.float64) / n_freq)
    ang = np.concatenate([row[:, None] * inv_freq[None], col[:, None] * inv_freq[None]], axis=-1)
    cos, sin = np.cos(ang), np.sin(ang)
    cos, sin = np.tile(cos, (1, 4)), np.concatenate([-sin, sin, -sin, sin], axis=-1)
    return (jnp.asarray(np.stack([cos * Q_SCALE, cos, np.ones_like(cos)]), dtype=f32),
            jnp.asarray(np.stack([sin * Q_SCALE, sin, np.zeros_like(sin)]), dtype=f32))


ROPE_Q, ROPE_K, ROPE_NONE = 0, 1, 2


def _rope(x, cos, sin):
    half = ATT_HD // 2
    lane = lax.broadcasted_iota(jnp.int32, x.shape, x.ndim - 1)
    partner = jnp.where((lane & (ATT_HD - 1)) < half,
                        pltpu.roll(x, LANES - half, x.ndim - 1), pltpu.roll(x, half, x.ndim - 1))
    return x * cos + partner * sin


QKV_TILE = 2 * ATT_KVH * ATT_HD
QKV_TILES = ODD_IN // QKV_TILE


def _odd_in_kernel(x_ref, g_ref, sh_ref, sc_ref, w_ref, cos_q_ref, sin_q_ref, cos_k_ref, sin_k_ref, o_ref, h_ref):
    _norm_mod_rows(x_ref, g_ref, sc_ref, sh_ref, h_ref)
    kw = ATT_KVH * ATT_HD
    for jt in range(QKV_TILES):
        acc = jnp.dot(h_ref[...], w_ref[:, jt * QKV_TILE:(jt + 1) * QKV_TILE], preferred_element_type=f32)
        for c in range(QKV_TILE // LANES):
            cols = slice(c * LANES, (c + 1) * LANES)
            xc = acc[:, cols].reshape(BATCH, TT, LANES)
            if jt < QKV_TILES - 1:
                xc = _rope(xc, cos_q_ref[...], sin_q_ref[...])
            elif c * LANES < kw:
                xc = _rope(xc, cos_k_ref[...], sin_k_ref[...])
            o_ref[jt, :, :, cols] = xc.astype(o_ref.dtype)


def _odd_in_proj(x, gains, mods, w_in, cos, sin):
    def table(kind_of):
        return pl.BlockSpec((None, TT, LANES), lambda i: (kind_of(i), jnp.maximum(i - CTX_TILES, 0), 0))

    q_table = table(lambda i: ROPE_Q)
    k_table = table(lambda i: jnp.where(i >= CTX_TILES, ROPE_K, ROPE_NONE))
    return pl.pallas_call(
        _odd_in_kernel,
        grid=(N_TILES,),
        in_specs=[pl.BlockSpec((BATCH, TT, D_MODEL), lambda i: (0, i, 0)),
                  _gain_spec(0), _mod_spec(0, CTX_TILES), _mod_spec(1, CTX_TILES),
                  _const_spec((D_MODEL, ODD_IN)), q_table, q_table, k_table, k_table],
        out_specs=pl.BlockSpec((QKV_TILES, BATCH, TT, QKV_TILE), lambda i: (0, 0, i, 0)),
        out_shape=jax.ShapeDtypeStruct((QKV_TILES, BATCH, L_ALL, QKV_TILE), bf16),
        scratch_shapes=[pltpu.VMEM((BATCH * TT, D_MODEL), bf16)],
        compiler_params=_cparams("parallel"),
        name="odd_in_proj",
    )(x, gains, mods, mods, w_in, cos, sin, cos, sin)


def _attn_kernel(sink_ref, q_ref, kvp_ref, kvc_ref, kvn_ref, kvx_ref, w1f_ref, w2f_ref, o_ref, w1b_ref, w2b_ref):
    w1b_ref[...] = w1f_ref[...].astype(w1b_ref.dtype)
    w2b_ref[...] = w2f_ref[...].astype(w2b_ref.dtype)
    qb = pl.program_id(1)
    t = ATT_BLOCK
    kw = ATT_KVH * ATT_HD
    row = lax.broadcasted_iota(jnp.int32, (t, t), 0)
    col = lax.broadcasted_iota(jnp.int32, (t, t), 1)

    def band_valid(blk):
        off = col + blk * t
        kpos = qb * t - WINDOW + off
        return (jnp.abs(off - WINDOW - row) <= WINDOW) & (kpos >= 0) & (kpos < SEQ)

    n_ctx_chunks = CTX_LEN // LANES
    masks = {n_ctx_chunks: band_valid(0), n_ctx_chunks + 2: band_valid(2)}
    n_chunks = n_ctx_chunks + 3

    for kh in range(ATT_KVH):
        ks = slice(kh * ATT_HD, (kh + 1) * ATT_HD)
        vs = slice(kw + kh * ATT_HD, kw + (kh + 1) * ATT_HD)
        k_all = jnp.concatenate([kvx_ref[:, ks], kvp_ref[:, ks], kvc_ref[:, ks], kvn_ref[:, ks]], axis=0)
        v_all = jnp.concatenate([kvx_ref[:, vs], kvp_ref[:, vs], kvc_ref[:, vs], kvn_ref[:, vs]], axis=0)
        heads = [kh * ATT_GRP + g for g in range(ATT_GRP)]
        per_tile = QKV_TILE // ATT_HD
        q_all = jnp.concatenate([q_ref[h // per_tile, :, (h % per_tile) * ATT_HD:(h % per_tile + 1) * ATT_HD]
                                 for h in heads], axis=0)
        s_all = lax.dot_general(q_all, k_all, (((1,), (1,)), ((), ())), preferred_element_type=f32)
        probs, inv_den = [], []
        for g in range(ATT_GRP):
            sink = sink_ref[kh * ATT_GRP + g] * LOG2_E
            s = s_all[g * t:(g + 1) * t]
            s = jnp.concatenate([jnp.where(masks[c], s[:, c * LANES:(c + 1) * LANES], NEG_INF) if c in masks
                                 else s[:, c * LANES:(c + 1) * LANES] for c in range(n_chunks)], axis=-1)
            m = jnp.maximum(jnp.max(s, axis=-1, keepdims=True), sink)
            e = jnp.exp2(s - m)
            inv_den.append(1.0 / (jnp.sum(e, axis=-1, keepdims=True) + jnp.exp2(sink - m)))
            probs.append(e.astype(bf16))
        o_all = jnp.dot(jnp.concatenate(probs, axis=0), v_all, preferred_element_type=f32)
        for g in range(ATT_GRP):
            h = kh * ATT_GRP + g
            o_ref[:, h * ATT_HD:(h + 1) * ATT_HD] = (o_all[g * t:(g + 1) * t] * inv_den[g]).astype(o_ref.dtype)


def _attention(qkv, sink, mlp_w1, mlp_w2):
    t = ATT_BLOCK
    nb = SEQ // t
    off = CTX_LEN // t
    qw = ATT_H * ATT_HD
    kvw = QKV_TILE
    kv_tile = QKV_TILES - 1

    def kv_spec(rows, row_block):
        return pl.BlockSpec((None, None, rows, kvw), lambda b, i: (kv_tile, b, row_block(i), 0))

    c_in, c_out, c_shapes = _cast_rider((mlp_w1, mlp_w2), DEPTH - 1, BATCH * nb, lambda b, i: b * nb + i)
    return pl.pallas_call(
        _attn_kernel,
        grid=(BATCH, nb),
        in_specs=[pl.BlockSpec(memory_space=pltpu.SMEM),
                  pl.BlockSpec((kv_tile, None, t, QKV_TILE), lambda b, i: (0, b, off + i, 0)),
                  kv_spec(t, lambda i: off + jnp.maximum(i - 1, 0)),
                  kv_spec(t, lambda i: off + i),
                  kv_spec(t, lambda i: off + jnp.minimum(i + 1, nb - 1)),
                  kv_spec(CTX_LEN, lambda i: 0)] + c_in,
        out_specs=[pl.BlockSpec((None, t, qw), lambda b, i: (b, i, 0))] + c_out,
        out_shape=[jax.ShapeDtypeStruct((BATCH, SEQ, qw), bf16)] + c_shapes,
        compiler_params=_cparams("parallel", "parallel"),
        name="window_attention",
    )(sink, qkv, qkv, qkv, qkv, qkv, mlp_w1, mlp_w2)


def _odd_out_kernel(a_ref, x_ref, gt_ref, gpost_ref, g2_ref, sh2_ref, sc2_ref, wo_ref, o_ref, h_ref):
    a = a_ref[...].reshape(BATCH * TT, D_MODEL)
    out = jnp.dot(a, wo_ref[...], preferred_element_type=f32).reshape(BATCH, TT, D_MODEL)
    _mix_epilogue(out, x_ref, gt_ref, gpost_ref, g2_ref, sh2_ref, sc2_ref, o_ref, h_ref)


def _odd_out_proj(a, rows, mods, gains, w_out):
    tile = pl.BlockSpec((BATCH, TT, D_MODEL), lambda i: (0, i, 0))
    return pl.pallas_call(
        _odd_out_kernel,
        grid=(SEQ // TT,),
        in_specs=[tile,
                  pl.BlockSpec((BATCH, TT, D_MODEL), lambda i: (0, i + CTX_TILES, 0)),
                  _mod_spec(2, 0), _gain_spec(1), _gain_spec(2), _mod_spec(3, 0), _mod_spec(4, 0),
                  _const_spec((D_MODEL, D_MODEL))],
        out_specs=[tile, tile],
        out_shape=[jax.ShapeDtypeStruct((BATCH, SEQ, D_MODEL), f32),
                   jax.ShapeDtypeStruct((BATCH, SEQ, D_MODEL), bf16)],
        compiler_params=_cparams("parallel"),
        name="odd_out_proj",
    )(a, rows, mods, gains, gains, mods, mods, w_out)


def _layer_mods(m):
    m = m.reshape(MOD_ROWS, MOD_CHUNKS, D_MODEL)
    lat = m[:BATCH]
    ctx = jnp.broadcast_to(m[CTX_MOD_ROW:CTX_MOD_ROW + 1], lat.shape)
    return jnp.stack([lat, ctx], axis=0).transpose(2, 0, 1, 3)[:, :, :, None, :]


def kernel(x, c, ctx, c_ctx, mod_w, mod_b, norm_g, mlp_w1, mlp_w2, even_w_in, even_w_out, s5_lam_re, s5_lam_im, s5_log_dt, s5_b_re, s5_b_im, s5_c_re, s5_c_im, s5_d, s5_glu_w, s5_glu_b, odd_w_in, odd_w_out, odd_sink):
    c_rows = jnp.concatenate([c, c_ctx[None, :], jnp.zeros((MOD_ROWS - BATCH - 1, D_MODEL), f32)], axis=0)
    mods_all = _modulation(c_rows, mod_w, mod_b)
    gains_all = norm_g.reshape(DEPTH, 4, 1, D_MODEL)

    mods, gains = _layer_mods(mods_all[0]), gains_all[0]
    perm = _chunk_swap_matrix()
    w_in = _to_bf16(even_w_in)[0]
    h, u, rows, u_blocks = _even_u_proj(x, ctx, gains, mods, w_in, perm)
    p, w1, w2 = _even_rest_proj(h, w_in, mlp_w1, mlp_w2)
    tables = _s5_tables(s5_lam_re[0], s5_lam_im[0], s5_log_dt[0], s5_b_re[0], s5_b_im[0], s5_c_re[0], s5_c_im[0])
    y_blocks = _s5_mix(u_blocks, tables)
    o_f, o_b = _retention(p)
    rows, h = _even_out_proj(u, p, y_blocks, o_f, o_b, rows, mods, gains, s5_d[0].reshape(1, S5_WIDTH),
                             s5_glu_b[0].reshape(1, S5_WIDTH), _to_bf16(s5_glu_w)[0], _to_bf16(even_w_out)[0], perm)
    rows = _mlp(rows, h, CTX_TILES, mods, gains, w1, w2)

    mods, gains = _layer_mods(mods_all[1]), gains_all[1]
    cos, sin = _rope_tables()
    qkv = _odd_in_proj(rows, gains, mods, _to_bf16(odd_w_in)[0], cos, sin)
    a, w1, w2 = _attention(qkv, odd_sink[0].astype(f32), mlp_w1, mlp_w2)
    lat, h = _odd_out_proj(a, rows, mods, gains, _to_bf16(odd_w_out)[0])
    return _mlp(lat, h, 0, mods, gains, w1, w2)
```

```python
import math

import jax
import jax.numpy as jnp
import numpy as np
from jax import lax
from jax.experimental import pallas as pl
from jax.experimental.pallas import tpu as pltpu

D_MODEL = 2048
BATCH = 4
SEQ = 4096
DEPTH = 2
GRID_W = 64
CTX_LEN = 256
MOD_CHUNKS = 6
FFN_HIDDEN = 4 * D_MODEL
NORM_EPS = 1e-6

S5_WIDTH = D_MODEL // 2
S5_P = 16
S5_G = S5_WIDTH // S5_P
S5_N = 64
LAMBDA_RE_MAX = -1e-4
RET_DK = 256
RET_H = (D_MODEL // 2) // RET_DK
RET_DV = (D_MODEL // 2) // RET_H
RET_DECAY_BASE = -5.0
EVEN_IN = S5_WIDTH + 2 * RET_H * RET_DK + 2 * RET_H * RET_DV

ATT_HD = 64
ATT_H = D_MODEL // ATT_HD
ATT_KVH = ATT_H // 8
ATT_GRP = ATT_H // ATT_KVH
ODD_IN = (ATT_H + 2 * ATT_KVH) * ATT_HD
WINDOW = 128
ATT_BLOCK = 128
ROPE_BASE = 10000.0
NEG_INF = -1e30
LOG2_E = math.log2(math.e)
Q_SCALE = ATT_HD ** -0.5 * LOG2_E

LANES = 128
SUBLANES = 8

L_ALL = CTX_LEN + SEQ
TT = 128
N_TILES = L_ALL // TT
CTX_TILES = CTX_LEN // TT
NORM_ROWS = 16
NORM_UNROLL = 4
MOD_ROWS = 8
CTX_MOD_ROW = BATCH

S5_T = 16
S5_TP = S5_T * S5_P
S5_NC = L_ALL // S5_T
S5_NC_CTX = CTX_LEN // S5_T
S5_BPAD = SUBLANES
S5_GB = 4
S5_GB_TABLES = 8
S5_SCAN_UNROLL = 4
S5_TILE_BLOCKS = TT // S5_T
S5_LANE_GROUPS = LANES // S5_P
RET_T = 256
RET_STEPS = L_ALL // RET_T
RET_CHAINS = 2 * RET_H

CAST_BLOCK_ELEMS = 2 * 1024 * 1024
VMEM_LIMIT = 56 * 1024 * 1024

f32 = jnp.float32
bf16 = jnp.bfloat16


def _cparams(*sem):
    return pltpu.CompilerParams(dimension_semantics=sem, vmem_limit_bytes=VMEM_LIMIT)


def _mod_spec(chunk, ctx_tiles):
    return pl.BlockSpec((None, None, BATCH, 1, D_MODEL),
                        lambda i, *_: (chunk, jnp.where(i < ctx_tiles, 1, 0), 0, 0, 0))


def _gain_spec(k):
    return pl.BlockSpec((None, 1, D_MODEL), lambda *_: (k, 0, 0))


def _const_spec(shape):
    return pl.BlockSpec(shape, lambda *_: (0,) * len(shape), pipeline_mode=pl.Buffered(1))


def _norm_mod_rows(x_ref, g_ref, sc_ref, sh_ref, h_ref, copy_ref=None):
    for b in range(BATCH):
        gain = g_ref[...] * (1.0 + sc_ref[b])
        shift = sh_ref[b]

        def body(r, carry):
            r0 = pl.multiple_of(r * NORM_ROWS, NORM_ROWS)
            xf = x_ref[b, pl.ds(r0, NORM_ROWS), :]
            if copy_ref is not None:
                copy_ref[b, pl.ds(r0, NORM_ROWS), :] = xf
            ms = jnp.mean(xf * xf, axis=-1, keepdims=True)
            h = (xf * lax.rsqrt(ms + NORM_EPS) * gain + shift).astype(h_ref.dtype)
            if len(h_ref.shape) == 3:
                h_ref[b, pl.ds(r0, NORM_ROWS), :] = h
            else:
                h_ref[pl.ds(pl.multiple_of(b * TT + r0, NORM_ROWS), NORM_ROWS), :] = h
            return carry

        lax.fori_loop(0, TT // NORM_ROWS, body, 0, unroll=NORM_UNROLL)


def _gated_residual(m, x, gate, g_post):
    ms = jnp.mean(m * m, axis=-1, keepdims=True)
    return x + m * lax.rsqrt(ms + NORM_EPS) * (g_post * gate)


def _mix_epilogue(out, x_ref, gt_ref, gpost_ref, g2_ref, sh2_ref, sc2_ref, o_ref, h_ref):
    x1 = _gated_residual(out, x_ref[...], gt_ref[...], gpost_ref[...])
    o_ref[...] = x1
    ms = jnp.mean(x1 * x1, axis=-1, keepdims=True)
    h_ref[...] = (x1 * lax.rsqrt(ms + NORM_EPS) * (g2_ref[...] * (1.0 + sc2_ref[...])) + sh2_ref[...]).astype(h_ref.dtype)


def _gated_residual_rows(load_rows, x_ref, gt_ref, gpost_ref, o_ref):
    for b in range(BATCH):
        gain = gpost_ref[...] * gt_ref[b]

        def body(r, carry):
            r0 = pl.multiple_of(r * NORM_ROWS, NORM_ROWS)
            m = load_rows(b, r0, NORM_ROWS)
            ms = jnp.mean(m * m, axis=-1, keepdims=True)
            o_ref[b, pl.ds(r0, NORM_ROWS), :] = x_ref[b, pl.ds(r0, NORM_ROWS), :] + m * lax.rsqrt(ms + NORM_EPS) * gain
            return carry

        lax.fori_loop(0, TT // NORM_ROWS, body, 0, unroll=NORM_UNROLL)


def _chunk_swap_matrix():
    n = S5_LANE_GROUPS * LANES
    src = np.arange(n)
    i, j, p = src // LANES, (src // S5_P) % S5_LANE_GROUPS, src % S5_P
    return jnp.asarray(np.arange(n)[None, :] == (j * LANES + i * S5_P + p)[:, None], dtype=bf16)


def _cast_kernel(w_ref, o_ref):
    o_ref[...] = w_ref[...].astype(o_ref.dtype)


def _to_bf16(w):
    n_layers, rows, cols = w.shape
    block_rows = min(rows, pl.next_power_of_2(CAST_BLOCK_ELEMS // cols + 1) // 2)
    spec = pl.BlockSpec((None, block_rows, cols), lambda l, r: (l, r, 0))
    return pl.pallas_call(
        _cast_kernel,
        grid=(n_layers, rows // block_rows),
        in_specs=[spec],
        out_specs=spec,
        out_shape=jax.ShapeDtypeStruct(w.shape, bf16),
        compiler_params=_cparams("parallel", "parallel"),
        name="to_bf16",
    )(w)


def _mod_kernel(c_ref, w_ref, b_ref, o_ref):
    c = c_ref[...]
    s = c * jax.nn.sigmoid(c)
    o_ref[...] = jnp.dot(s.astype(bf16), w_ref[...].astype(bf16), preferred_element_type=f32) + b_ref[...]


def _modulation(c_rows, mod_w, mod_b):
    tn = 1024
    n = MOD_CHUNKS * D_MODEL
    return pl.pallas_call(
        _mod_kernel,
        grid=(DEPTH, n // tn),
        in_specs=[pl.BlockSpec((MOD_ROWS, D_MODEL), lambda l, j: (0, 0)),
                  pl.BlockSpec((None, D_MODEL, tn), lambda l, j: (l, 0, j)),
                  pl.BlockSpec((None, 1, tn), lambda l, j: (l, 0, j))],
        out_specs=pl.BlockSpec((None, MOD_ROWS, tn), lambda l, j: (l, 0, j)),
        out_shape=jax.ShapeDtypeStruct((DEPTH, MOD_ROWS, n), f32),
        compiler_params=_cparams("parallel", "parallel"),
        name="ada_modulation",
    )(c_rows, mod_w, mod_b.reshape(DEPTH, 1, n))


def _even_u_kernel(x_ref, c_ref, g_ref, sh_ref, sc_ref, w_ref, perm_ref, h_ref, u_ref, rows_ref, ub_ref,
                   a_ref, xs_ref, st_ref):
    i = pl.program_id(0)

    @pl.when(i < CTX_TILES)
    def _():
        _norm_mod_rows(c_ref, g_ref, sc_ref, sh_ref, h_ref, copy_ref=rows_ref)

    @pl.when(i >= CTX_TILES)
    def _():
        _norm_mod_rows(x_ref, g_ref, sc_ref, sh_ref, h_ref, copy_ref=rows_ref)

    acc = jnp.dot(h_ref[...].reshape(BATCH * TT, D_MODEL), w_ref[...], preferred_element_type=f32)
    u_ref[...] = acc.reshape(BATCH, TT, S5_WIDTH).astype(u_ref.dtype)

    n_slab, n_half = S5_WIDTH // LANES, S5_TP // LANES
    for s in range(n_slab):
        a_ref[s] = acc[:, s * LANES:(s + 1) * LANES]
    for s in range(n_slab):
        for half in range(n_half):
            for b in range(BATCH):
                r0 = ((s * n_half + half) * BATCH + b) * S5_TILE_BLOCKS
                for tt in range(S5_LANE_GROUPS):
                    xs_ref[tt, r0:r0 + S5_TILE_BLOCKS, :] = a_ref[
                        s, pl.ds(b * TT + half * S5_LANE_GROUPS + tt, S5_TILE_BLOCKS, stride=S5_T), :]
    xs = jnp.concatenate([xs_ref[tt] for tt in range(S5_LANE_GROUPS)], axis=-1).astype(bf16)
    ys = jnp.dot(xs, perm_ref[...], preferred_element_type=f32)
    @pl.when(i == 0)
    def _():
        st_ref[...] = jnp.zeros_like(st_ref)
    for s in range(n_slab):
        for half in range(n_half):
            for b in range(BATCH):
                r0 = ((s * n_half + half) * BATCH + b) * S5_TILE_BLOCKS
                for gg in range(S5_LANE_GROUPS):
                    st_ref[s * S5_LANE_GROUPS + gg, half, pl.ds(b, S5_TILE_BLOCKS, stride=S5_BPAD), :] = (
                        ys[r0:r0 + S5_TILE_BLOCKS, gg * LANES:(gg + 1) * LANES])
    for g in range(S5_G):
        ub_ref[g] = jnp.concatenate([st_ref[g, 0], st_ref[g, 1]], axis=-1).astype(ub_ref.dtype)


def _even_u_proj(x, ctx, gains, mods, w_in, perm):
    blk_rows = S5_TILE_BLOCKS * S5_BPAD
    tile = lambda width: pl.BlockSpec((BATCH, TT, width), lambda i: (0, i, 0))
    return pl.pallas_call(
        _even_u_kernel,
        grid=(N_TILES,),
        in_specs=[pl.BlockSpec((BATCH, TT, D_MODEL), lambda i: (0, jnp.maximum(i - CTX_TILES, 0), 0)),
                  pl.BlockSpec((BATCH, TT, D_MODEL), lambda i: (0, jnp.minimum(i, CTX_TILES - 1), 0)),
                  _gain_spec(0), _mod_spec(0, CTX_TILES), _mod_spec(1, CTX_TILES),
                  pl.BlockSpec((D_MODEL, S5_WIDTH), lambda i: (0, 0), pipeline_mode=pl.Buffered(1)),
                  _const_spec(perm.shape)],
        out_specs=[tile(D_MODEL), tile(S5_WIDTH), tile(D_MODEL),
                   pl.BlockSpec((S5_G, blk_rows, S5_TP), lambda i: (0, i, 0))],
        out_shape=[jax.ShapeDtypeStruct((BATCH, L_ALL, D_MODEL), bf16),
                   jax.ShapeDtypeStruct((BATCH, L_ALL, S5_WIDTH), bf16),
                   jax.ShapeDtypeStruct((BATCH, L_ALL, D_MODEL), f32),
                   jax.ShapeDtypeStruct((S5_G, S5_NC * S5_BPAD, S5_TP), bf16)],
        scratch_shapes=[pltpu.VMEM((S5_WIDTH // LANES, BATCH * TT, LANES), f32),
                        pltpu.VMEM((S5_LANE_GROUPS, BATCH * TT, LANES), f32),
                        pltpu.VMEM((S5_G, S5_TP // LANES, blk_rows, LANES), f32)],
        compiler_params=_cparams("arbitrary"),
        name="even_u_proj",
    )(x, ctx, gains, mods, mods, w_in, perm)


EVEN_REST_TILES = (EVEN_IN - S5_WIDTH) // S5_WIDTH


def _cast_rider(ws, layer, n_blocks, block_of):
    in_specs, out_specs, out_shapes = [], [], []
    for w in ws:
        _, rows, cols = w.shape
        in_specs.append(pl.BlockSpec((None, rows // n_blocks, cols), lambda *g: (layer, block_of(*g), 0)))
        out_specs.append(pl.BlockSpec((rows // n_blocks, cols), lambda *g: (block_of(*g), 0)))
        out_shapes.append(jax.ShapeDtypeStruct((rows, cols), bf16))
    return in_specs, out_specs, out_shapes


def _even_rest_kernel(h_ref, *refs):
    w_refs, refs = refs[:EVEN_REST_TILES], refs[EVEN_REST_TILES:]
    n_riders = len(refs) // 2
    rider_in, o_ref, rider_out = refs[:n_riders], refs[n_riders], refs[n_riders + 1:]
    h = h_ref[...].reshape(BATCH * TT, D_MODEL)
    for jt, w_ref in enumerate(w_refs):
        acc = jnp.dot(h, w_ref[...], preferred_element_type=f32)
        o_ref[:, :, jt * S5_WIDTH:(jt + 1) * S5_WIDTH] = acc.reshape(BATCH, TT, S5_WIDTH).astype(o_ref.dtype)
    for src_ref, dst_ref in zip(rider_in, rider_out):
        dst_ref[...] = src_ref[...].astype(dst_ref.dtype)


def _even_rest_proj(h, w_in, layer0_weights, shared_weights):
    width = EVEN_IN - S5_WIDTH
    w_specs = [pl.BlockSpec((D_MODEL, S5_WIDTH), lambda i, jt=jt: (0, 1 + jt), pipeline_mode=pl.Buffered(1))
               for jt in range(EVEN_REST_TILES)]
    n_blocks = 32
    riders = tuple(layer0_weights) + tuple(shared_weights)
    c_in, c_out, c_shapes = _cast_rider(riders, 0, n_blocks, lambda i: jnp.minimum(i, n_blocks - 1))
    outs = pl.pallas_call(
        _even_rest_kernel,
        grid=(N_TILES,),
        in_specs=[pl.BlockSpec((BATCH, TT, D_MODEL), lambda i: (0, i, 0))] + w_specs + c_in,
        out_specs=[pl.BlockSpec((BATCH, TT, width), lambda i: (0, i, 0))] + c_out,
        out_shape=[jax.ShapeDtypeStruct((BATCH, L_ALL, width), bf16)] + c_shapes,
        compiler_params=_cparams("arbitrary"),
        name="even_rest_proj",
    )(h, *([w_in] * EVEN_REST_TILES), *riders)
    return outs[0], outs[1:]


def _s5_tables(lam_re, lam_im, log_dt, b_re, b_im, c_re, c_im):
    lr = jnp.minimum(lam_re.astype(f32), LAMBDA_RE_MAX)
    li = lam_im.astype(f32)
    dt = jnp.exp(log_dt.astype(f32))[..., None]
    zr, zi = lr * dt, li * dt
    ab_mag = jnp.exp(zr)
    ab_re, ab_im = ab_mag * jnp.cos(zi), ab_mag * jnp.sin(zi)
    den = lr * lr + li * li
    nr = ab_re - 1.0
    f_re = (nr * lr + ab_im * li) / den
    f_im = (ab_im * lr - nr * li) / den
    br, bi = b_re.astype(f32), b_im.astype(f32)
    bb_re = f_re[..., None] * br - f_im[..., None] * bi
    bb_im = f_re[..., None] * bi + f_im[..., None] * br
    cr, ci = c_re.astype(f32), c_im.astype(f32)
    bt_re, bt_im = bb_re.transpose(0, 1, 3, 2), bb_im.transpose(0, 1, 3, 2)
    asc = np.arange(S5_T, dtype=np.float32)
    desc = asc[::-1].copy()

    def powers(d, ks):
        k = jnp.asarray(ks, dtype=f32)[:, None]
        mag = jnp.exp(zr[d][:, None, :] * k)
        return mag * jnp.cos(zi[d][:, None, :] * k), mag * jnp.sin(zi[d][:, None, :] * k)

    def both(fwd, bwd):
        return jnp.concatenate([fwd, bwd], axis=-1)

    def c_times_powers(d, ks):
        pr, pi = powers(d, ks)
        ca_re = cr[d][:, None] * pr[:, :, None, :] - ci[d][:, None] * pi[:, :, None, :]
        ca_im = cr[d][:, None] * pi[:, :, None, :] + ci[d][:, None] * pr[:, :, None, :]
        return both(ca_re, ca_im).reshape(S5_G, S5_TP, 2 * S5_N)

    bcat = both(bt_re, -bt_im)
    w_intra = _s5_intra_matrix(bcat[0], bcat[1], c_times_powers(0, asc), c_times_powers(1, desc))

    (pf_re, pf_im), (pb_re, pb_im) = powers(0, desc), powers(1, asc)
    p_re, p_im = both(pf_re, pb_re)[:, :, None, :], both(pf_im, pb_im)[:, :, None, :]
    b_re2, b_im2 = both(bt_re[0], bt_re[1])[:, None], both(bt_im[0], bt_im[1])[:, None]
    w_state = both((p_re * b_re2 - p_im * b_im2).reshape(S5_G, S5_TP, 2 * S5_N),
                   (p_re * b_im2 + p_im * b_re2).reshape(S5_G, S5_TP, 2 * S5_N))

    def rows_fb(fwd, bwd):
        return jnp.concatenate([fwd.transpose(0, 2, 1), bwd.transpose(0, 2, 1)], axis=1)

    (qf_re, qf_im), (qb_re, qb_im) = powers(0, asc + 1.0), powers(1, desc + 1.0)
    q_re, q_im = rows_fb(qf_re, qb_re)[..., None], rows_fb(qf_im, qb_im)[..., None]
    c_re2, c_im2 = rows_fb(cr[0], cr[1])[:, :, None, :], rows_fb(ci[0], ci[1])[:, :, None, :]
    w_cross = jnp.concatenate([(c_re2 * q_re - c_im2 * q_im).reshape(S5_G, 2 * S5_N, S5_TP),
                               -(c_re2 * q_im + c_im2 * q_re).reshape(S5_G, 2 * S5_N, S5_TP)], axis=1)

    (af_re, af_im), (ab_re_t, ab_im_t) = powers(0, [float(S5_T)]), powers(1, [float(S5_T)])
    a_re, a_im = both(af_re, ab_re_t), both(af_im, ab_im_t)
    return w_intra, w_state.astype(bf16), w_cross.astype(bf16), a_re, a_im


def _s5_intra_kernel(bf_ref, bb_ref, caf_ref, cab_ref, w_ref):
    lane = lax.broadcasted_iota(jnp.int32, (S5_P, S5_TP), 1)
    dn = (((1,), (1,)), ((), ()))
    for g in range(S5_GB_TABLES):
        tq_f = lax.dot_general(bf_ref[g], caf_ref[g], dn, precision=lax.Precision.HIGHEST, preferred_element_type=f32)
        tq_b = lax.dot_general(bb_ref[g], cab_ref[g], dn, precision=lax.Precision.HIGHEST, preferred_element_type=f32)
        for s in range(S5_T):
            sf, sb = S5_P * s, S5_P * (S5_T - 1 - s)
            row_f = tq_f if s == 0 else jnp.where(lane >= sf, pltpu.roll(tq_f, sf, 1), 0.0)
            row_b = tq_b if sb == 0 else jnp.where(lane < S5_TP - sb, pltpu.roll(tq_b, S5_TP - sb, 1), 0.0)
            w_ref[g, s * S5_P:(s + 1) * S5_P, :] = (row_f + row_b).astype(w_ref.dtype)


def _s5_intra_matrix(b_f, b_b, ca_f, ca_b):
    def spec(rows, cols):
        return pl.BlockSpec((S5_GB_TABLES, rows, cols), lambda i: (i, 0, 0))

    return pl.pallas_call(
        _s5_intra_kernel,
        grid=(S5_G // S5_GB_TABLES,),
        in_specs=[spec(S5_P, 2 * S5_N), spec(S5_P, 2 * S5_N), spec(S5_TP, 2 * S5_N), spec(S5_TP, 2 * S5_N)],
        out_specs=spec(S5_TP, S5_TP),
        out_shape=jax.ShapeDtypeStruct((S5_G, S5_TP, S5_TP), bf16),
        compiler_params=_cparams("parallel"),
        name="s5_intra_matrix",
    )(b_f, b_b, ca_f, ca_b)


def _s5_kernel(u_ref, wi_ref, ws_ref, wc_ref, are_ref, aim_ref, y_ref, s_ref):
    half = 2 * S5_N
    for g in range(S5_GB):
        s_ref[g] = jnp.dot(u_ref[g], ws_ref[g], preferred_element_type=f32)

    fwd_lane = lax.broadcasted_iota(jnp.int32, (S5_BPAD, half), 1) < S5_N
    a_re = [jnp.broadcast_to(are_ref[g], (S5_BPAD, half)) for g in range(S5_GB)]
    a_im = [jnp.broadcast_to(aim_ref[g], (S5_BPAD, half)) for g in range(S5_GB)]

    def step(i, carry):
        cb = jnp.where(i < S5_NC_CTX, S5_NC_CTX - 1 - i, S5_NC + S5_NC_CTX - 1 - i)
        rf = pl.multiple_of(i * S5_BPAD, S5_BPAD)
        rb = pl.multiple_of(cb * S5_BPAD, S5_BPAD)
        out = []
        for g in range(S5_GB):
            h_re, h_im = carry[2 * g], carry[2 * g + 1]
            l_re = jnp.where(fwd_lane, s_ref[g, pl.ds(rf, S5_BPAD), 0:half], s_ref[g, pl.ds(rb, S5_BPAD), 0:half])
            l_im = jnp.where(fwd_lane, s_ref[g, pl.ds(rf, S5_BPAD), half:2 * half],
                             s_ref[g, pl.ds(rb, S5_BPAD), half:2 * half])
            s_ref[g, pl.ds(rf, S5_BPAD), 0:S5_N] = h_re[:, 0:S5_N]
            s_ref[g, pl.ds(rb, S5_BPAD), S5_N:half] = h_re[:, S5_N:half]
            s_ref[g, pl.ds(rf, S5_BPAD), half:half + S5_N] = h_im[:, 0:S5_N]
            s_ref[g, pl.ds(rb, S5_BPAD), half + S5_N:2 * half] = h_im[:, S5_N:half]
            out.append(a_re[g] * h_re - a_im[g] * h_im + l_re)
            out.append(a_re[g] * h_im + a_im[g] * h_re + l_im)
        return tuple(out)

    zero = jnp.zeros((S5_BPAD, half), f32)
    lax.fori_loop(0, S5_NC, step, (zero,) * (2 * S5_GB), unroll=S5_SCAN_UNROLL)

    for g in range(S5_GB):
        y = jnp.dot(u_ref[g], wi_ref[g], preferred_element_type=f32)
        y = y + jnp.dot(s_ref[g].astype(bf16), wc_ref[g], preferred_element_type=f32)
        for hf in range(S5_TP // LANES):
            y_ref[g, hf] = y[:, hf * LANES:(hf + 1) * LANES]


def _s5_mix(u_blocks, tables):
    w_intra, w_state, w_cross, a_re, a_im = tables
    rows = S5_NC * S5_BPAD
    wsp = pl.BlockSpec((S5_GB, S5_TP, S5_TP), lambda g: (g, 0, 0))
    asp = pl.BlockSpec((S5_GB, 1, 2 * S5_N), lambda g: (g, 0, 0))
    return pl.pallas_call(
        _s5_kernel,
        grid=(S5_G // S5_GB,),
        in_specs=[pl.BlockSpec((S5_GB, rows, S5_TP), lambda g: (g, 0, 0)), wsp, wsp, wsp, asp, asp],
        out_specs=pl.BlockSpec((S5_GB, S5_TP // LANES, rows, LANES), lambda g: (g, 0, 0, 0)),
        out_shape=jax.ShapeDtypeStruct((S5_G, S5_TP // LANES, rows, LANES), f32),
        scratch_shapes=[pltpu.VMEM((S5_GB, rows, S5_TP), f32)],
        compiler_params=_cparams("parallel"),
        name="s5_mix",
    )(u_blocks, w_intra, w_state, w_cross, a_re, a_im)


def _ret_kernel(lg_ref, cd_ref, qf_ref, kf_ref, vf_ref, qb_ref, kb_ref, vb_ref, of_ref, ob_ref,
                r_ref, dec_ref, qdec_ref, kdec_ref):
    t = RET_T

    @pl.when(pl.program_id(1) == 0)
    def _():
        r_ref[...] = jnp.zeros_like(r_ref)
        n = lax.broadcasted_iota(jnp.int32, (t, t), 0)
        m = lax.broadcasted_iota(jnp.int32, (t, t), 1)
        pos = lax.broadcasted_iota(jnp.int32, (t, 1), 0)
        for d in range(2):
            diff = (n - m if d == 0 else m - n).astype(f32)
            p = (pos if d == 0 else t - 1 - pos).astype(f32)
            for h in range(RET_H):
                c = d * RET_H + h
                lg = lg_ref[c]
                dec_ref[c] = jnp.where(diff >= 0, jnp.exp(jnp.maximum(diff, 0.0) * lg), 0.0)
                qdec_ref[c] = jnp.exp((p + 1.0) * lg)
                kdec_ref[c] = jnp.exp((t - 1.0 - p) * lg)

    for d, (q_ref, k_ref, v_ref, o_ref) in enumerate(((qf_ref, kf_ref, vf_ref, of_ref),
                                                        (qb_ref, kb_ref, vb_ref, ob_ref))):
        for h in range(RET_H):
            c = d * RET_H + h
            cols = slice(h * RET_DK, (h + 1) * RET_DK)
            q = q_ref[:, cols]
            ks = k_ref[:, cols] * (RET_DK ** -0.5)
            v = v_ref[:, cols]
            r = r_ref[c]
            s = lax.dot_general(q, ks, (((1,), (1,)), ((), ())), preferred_element_type=f32) * dec_ref[c]
            inner = jnp.dot(s.astype(bf16), v, preferred_element_type=f32)
            cross = jnp.dot(q, r.astype(bf16), preferred_element_type=f32) * qdec_ref[c]
            o_ref[:, cols] = (inner + cross).astype(o_ref.dtype)
            kd = (ks.astype(f32) * kdec_ref[c]).astype(bf16)
            r_ref[c] = r * cd_ref[c] + lax.dot_general(kd, v, (((0,), (0,)), ((), ())), preferred_element_type=f32)


def _retention(p):
    e = RET_DECAY_BASE - (2.0 * np.arange(RET_H, dtype=np.float64)[None, :] + np.arange(2, dtype=np.float64)[:, None])
    log_g64 = np.log1p(-np.exp2(e)).reshape(RET_CHAINS)
    log_g = jnp.asarray(log_g64, dtype=f32)
    chunk_dec = jnp.asarray(np.exp(RET_T * log_g64), dtype=f32)
    width = RET_H * RET_DK
    col0 = 0

    def bwd_chunk(i):
        return jnp.where(i == 0, 0, RET_STEPS - i)

    def fwd(off):
        return pl.BlockSpec((None, RET_T, width), lambda b, i: (b, i, col0 + off))

    def bwd(off):
        return pl.BlockSpec((None, RET_T, width), lambda b, i: (b, bwd_chunk(i), col0 + off))

    smem = pl.BlockSpec(memory_space=pltpu.SMEM)
    out = jax.ShapeDtypeStruct((BATCH, L_ALL, RET_H * RET_DV), bf16)
    return pl.pallas_call(
        _ret_kernel,
        grid=(BATCH, RET_STEPS),
        in_specs=[smem, smem, fwd(0), fwd(1), fwd(2), bwd(0), bwd(1), bwd(2)],
        out_specs=[pl.BlockSpec((None, RET_T, RET_H * RET_DV), lambda b, i: (b, i, 0)),
                   pl.BlockSpec((None, RET_T, RET_H * RET_DV), lambda b, i: (b, bwd_chunk(i), 0))],
        out_shape=[out, out],
        scratch_shapes=[pltpu.VMEM((RET_CHAINS, RET_DK, RET_DV), f32),
                        pltpu.VMEM((RET_CHAINS, RET_T, RET_T), f32),
                        pltpu.VMEM((RET_CHAINS, RET_T, 1), f32),
                        pltpu.VMEM((RET_CHAINS, RET_T, 1), f32)],
        compiler_params=_cparams("parallel", "arbitrary"),
        name="retention",
    )(log_g, chunk_dec, p, p, p, p, p, p)


def _gelu_tanh(x):
    return 0.5 * x * (1.0 + jnp.tanh(math.sqrt(2.0 / math.pi) * (x + 0.044715 * (x * x * x))))


def _even_out_kernel(u_ref, gate_ref, y_ref, of_ref, ob_ref, x_ref, gt_ref, gpost_ref, g2_ref, sh2_ref, sc2_ref,
                     dsk_ref, glub_ref, gluw_ref, wo_ref, perm_ref, o_ref, h_ref, z_ref, xs_ref, yf_ref):
    rows = BATCH * TT
    n_slab, n_half = S5_WIDTH // LANES, S5_TP // LANES
    for s in range(n_slab):
        for half in range(n_half):
            for b in range(BATCH):
                r0 = ((s * n_half + half) * BATCH + b) * S5_TILE_BLOCKS
                for gg in range(S5_LANE_GROUPS):
                    xs_ref[gg, r0:r0 + S5_TILE_BLOCKS, :] = y_ref[
                        s * S5_LANE_GROUPS + gg, half, pl.ds(b, S5_TILE_BLOCKS, stride=S5_BPAD), :]
    xs = jnp.concatenate([xs_ref[gg] for gg in range(S5_LANE_GROUPS)], axis=-1).astype(bf16)
    ys = jnp.dot(xs, perm_ref[...], preferred_element_type=f32)
    for s in range(n_slab):
        for half in range(n_half):
            for b in range(BATCH):
                r0 = ((s * n_half + half) * BATCH + b) * S5_TILE_BLOCKS
                for tt in range(S5_LANE_GROUPS):
                    z_ref[s, pl.ds(b * TT + half * S5_LANE_GROUPS + tt, S5_TILE_BLOCKS, stride=S5_T), :] = (
                        ys[r0:r0 + S5_TILE_BLOCKS, tt * LANES:(tt + 1) * LANES])
    u = u_ref[...].reshape(rows, S5_WIDTH)
    for s in range(S5_WIDTH // LANES):
        cols = slice(s * LANES, (s + 1) * LANES)
        yf_ref[:, cols] = _gelu_tanh(z_ref[s] + dsk_ref[:, cols] * u[:, cols].astype(f32))
    y = yf_ref[...]
    z = jnp.dot(y.astype(bf16), gluw_ref[...], preferred_element_type=f32) + glub_ref[...]
    s5_out = y * jax.nn.sigmoid(z)
    r = (of_ref[...].astype(f32) + ob_ref[...].astype(f32)).reshape(rows, RET_H * RET_DV)
    heads = []
    for h in range(RET_H):
        rh = r[:, h * RET_DV:(h + 1) * RET_DV]
        heads.append(rh * lax.rsqrt(jnp.mean(rh * rh, axis=-1, keepdims=True) + NORM_EPS))
    g = gate_ref[...].reshape(rows, RET_H * RET_DV).astype(f32)
    ret_out = jnp.concatenate(heads, axis=-1) * (g * jax.nn.sigmoid(g))
    out = jnp.dot(s5_out.astype(bf16), wo_ref[0:S5_WIDTH, :], preferred_element_type=f32)
    out = out + jnp.dot(ret_out.astype(bf16), wo_ref[S5_WIDTH:, :], preferred_element_type=f32)
    _mix_epilogue(out.reshape(BATCH, TT, D_MODEL), x_ref, gt_ref, gpost_ref, g2_ref, sh2_ref, sc2_ref, o_ref, h_ref)


def _even_out_proj(u, p, y_blocks, o_f, o_b, rows, mods, gains, d_skip, glu_b, glu_w, w_out, perm):
    half = S5_WIDTH
    blk_rows = S5_TILE_BLOCKS * S5_BPAD
    tile = pl.BlockSpec((BATCH, TT, half), lambda i: (0, i, 0))
    return pl.pallas_call(
        _even_out_kernel,
        grid=(N_TILES,),
        in_specs=[tile,
                  pl.BlockSpec((BATCH, TT, half), lambda i: (0, i, EVEN_REST_TILES - 1)),
                  pl.BlockSpec((S5_G, S5_TP // LANES, blk_rows, LANES), lambda i: (0, 0, i, 0)),
                  tile, tile,
                  pl.BlockSpec((BATCH, TT, D_MODEL), lambda i: (0, i, 0)),
                  _mod_spec(2, CTX_TILES), _gain_spec(1),
                  _gain_spec(2), _mod_spec(3, CTX_TILES), _mod_spec(4, CTX_TILES),
                  _const_spec((1, half)), _const_spec((1, half)),
                  _const_spec((half, half)), _const_spec((D_MODEL, D_MODEL)), _const_spec(perm.shape)],
        out_specs=[pl.BlockSpec((BATCH, TT, D_MODEL), lambda i: (0, i, 0)),
                   pl.BlockSpec((BATCH, TT, D_MODEL), lambda i: (0, i, 0))],
        out_shape=[jax.ShapeDtypeStruct((BATCH, L_ALL, D_MODEL), f32),
                   jax.ShapeDtypeStruct((BATCH, L_ALL, D_MODEL), bf16)],
        scratch_shapes=[pltpu.VMEM((S5_WIDTH // LANES, BATCH * TT, LANES), f32),
                        pltpu.VMEM((S5_LANE_GROUPS, BATCH * TT, LANES), f32),
                        pltpu.VMEM((BATCH * TT, S5_WIDTH), f32)],
        compiler_params=_cparams("parallel"),
        name="even_out_proj",
    )(u, p, y_blocks, o_f, o_b, rows, mods, gains, gains, mods, mods, d_skip, glu_b, glu_w, w_out, perm)


def _mlp_kernel(x_ref, h_ref, gt_ref, gpost_ref, w1_ref, w2_ref, o_ref, acc_ref):
    f = pl.program_id(1)

    @pl.when(f == 0)
    def _():
        acc_ref[...] = jnp.zeros_like(acc_ref)

    h = h_ref[...].reshape(BATCH * TT, D_MODEL)
    a = jnp.maximum(jnp.dot(h, w1_ref[...], preferred_element_type=f32), 0.0)
    acc_ref[...] += jnp.dot((a * a).astype(bf16), w2_ref[...], preferred_element_type=f32)

    @pl.when(f == pl.num_programs(1) - 1)
    def _():
        _gated_residual_rows(lambda b, r0, n: acc_ref[pl.ds(pl.multiple_of(b * TT + r0, n), n), :],
                             x_ref, gt_ref, gpost_ref, o_ref)


def _mlp(x, h, ctx_tiles, mods, gains, w1, w2):
    tf = 1024
    n_tiles = x.shape[1] // TT
    tile = pl.BlockSpec((BATCH, TT, D_MODEL), lambda i, f: (0, i, 0))
    return pl.pallas_call(
        _mlp_kernel,
        grid=(n_tiles, FFN_HIDDEN // tf),
        in_specs=[tile, tile, _mod_spec(5, ctx_tiles), _gain_spec(3),
                  pl.BlockSpec((D_MODEL, tf), lambda i, f: (0, f)),
                  pl.BlockSpec((tf, D_MODEL), lambda i, f: (f, 0))],
        out_specs=tile,
        out_shape=jax.ShapeDtypeStruct((BATCH, n_tiles * TT, D_MODEL), f32),
        scratch_shapes=[pltpu.VMEM((BATCH * TT, D_MODEL), f32)],
        compiler_params=_cparams("parallel", "arbitrary"),
        name="sq_relu_mlp",
    )(x, h, mods, gains, w1, w2)


def _rope_tables():
    rows = SEQ // GRID_W
    row = np.repeat(np.arange(rows, dtype=np.float64), GRID_W)
    col = np.tile(np.arange(GRID_W, dtype=np.float64), rows)
    n_freq = ATT_HD // 4
    inv_freq = ROPE_BASE ** (-np.arange(n_freq, dtype=np.float64) / n_freq)
    ang = np.concatenate([row[:, None] * inv_freq[None], col[:, None] * inv_freq[None]], axis=-1)
    cos, sin = np.cos(ang), np.sin(ang)
    cos, sin = np.tile(cos, (1, 4)), np.concatenate([-sin, sin, -sin, sin], axis=-1)
    return (jnp.asarray(np.stack([cos * Q_SCALE, cos, np.ones_like(cos)]), dtype=f32),
            jnp.asarray(np.stack([sin * Q_SCALE, sin, np.zeros_like(sin)]), dtype=f32))


ROPE_Q, ROPE_K, ROPE_NONE = 0, 1, 2


def _rope(x, cos, sin):
    half = ATT_HD // 2
    lane = lax.broadcasted_iota(jnp.int32, x.shape, x.ndim - 1)
    partner = jnp.where((lane & (ATT_HD - 1)) < half,
                        pltpu.roll(x, LANES - half, x.ndim - 1), pltpu.roll(x, half, x.ndim - 1))
    return x * cos + partner * sin


QKV_TILE = 2 * ATT_KVH * ATT_HD
QKV_TILES = ODD_IN // QKV_TILE


def _odd_in_kernel(x_ref, g_ref, sh_ref, sc_ref, w_ref, cos_q_ref, sin_q_ref, cos_k_ref, sin_k_ref, o_ref, h_ref):
    _norm_mod_rows(x_ref, g_ref, sc_ref, sh_ref, h_ref)
    kw = ATT_KVH * ATT_HD
    for jt in range(QKV_TILES):
        acc = jnp.dot(h_ref[...], w_ref[:, jt * QKV_TILE:(jt + 1) * QKV_TILE], preferred_element_type=f32)
        for c in range(QKV_TILE // LANES):
            cols = slice(c * LANES, (c + 1) * LANES)
            xc = acc[:, cols].reshape(BATCH, TT, LANES)
            if jt < QKV_TILES - 1:
                xc = _rope(xc, cos_q_ref[...], sin_q_ref[...])
            elif c * LANES < kw:
                xc = _rope(xc, cos_k_ref[...], sin_k_ref[...])
            o_ref[jt, :, :, cols] = xc.astype(o_ref.dtype)


def _odd_in_proj(x, gains, mods, w_in, cos, sin):
    def table(kind_of):
        return pl.BlockSpec((None, TT, LANES), lambda i: (kind_of(i), jnp.maximum(i - CTX_TILES, 0), 0))

    q_table = table(lambda i: ROPE_Q)
    k_table = table(lambda i: jnp.where(i >= CTX_TILES, ROPE_K, ROPE_NONE))
    return pl.pallas_call(
        _odd_in_kernel,
        grid=(N_TILES,),
        in_specs=[pl.BlockSpec((BATCH, TT, D_MODEL), lambda i: (0, i, 0)),
                  _gain_spec(0), _mod_spec(0, CTX_TILES), _mod_spec(1, CTX_TILES),
                  _const_spec((D_MODEL, ODD_IN)), q_table, q_table, k_table, k_table],
        out_specs=pl.BlockSpec((QKV_TILES, BATCH, TT, QKV_TILE), lambda i: (0, 0, i, 0)),
        out_shape=jax.ShapeDtypeStruct((QKV_TILES, BATCH, L_ALL, QKV_TILE), bf16),
        scratch_shapes=[pltpu.VMEM((BATCH * TT, D_MODEL), bf16)],
        compiler_params=_cparams("parallel"),
        name="odd_in_proj",
    )(x, gains, mods, mods, w_in, cos, sin, cos, sin)


def _attn_kernel(sink_ref, q_ref, kvp_ref, kvc_ref, kvn_ref, kvx_ref, w1f_ref, w2f_ref, o_ref, w1b_ref, w2b_ref):
    w1b_ref[...] = w1f_ref[...].astype(w1b_ref.dtype)
    w2b_ref[...] = w2f_ref[...].astype(w2b_ref.dtype)
    qb = pl.program_id(1)
    t = ATT_BLOCK
    kw = ATT_KVH * ATT_HD
    row = lax.broadcasted_iota(jnp.int32, (t, t), 0)
    col = lax.broadcasted_iota(jnp.int32, (t, t), 1)

    def band_valid(blk):
        off = col + blk * t
        kpos = qb * t - WINDOW + off
        return (jnp.abs(off - WINDOW - row) <= WINDOW) & (kpos >= 0) & (kpos < SEQ)

    n_ctx_chunks = CTX_LEN // LANES
    masks = {n_ctx_chunks: band_valid(0), n_ctx_chunks + 2: band_valid(2)}
    n_chunks = n_ctx_chunks + 3

    for kh in range(ATT_KVH):
        ks = slice(kh * ATT_HD, (kh + 1) * ATT_HD)
        vs = slice(kw + kh * ATT_HD, kw + (kh + 1) * ATT_HD)
        k_all = jnp.concatenate([kvx_ref[:, ks], kvp_ref[:, ks], kvc_ref[:, ks], kvn_ref[:, ks]], axis=0)
        v_all = jnp.concatenate([kvx_ref[:, vs], kvp_ref[:, vs], kvc_ref[:, vs], kvn_ref[:, vs]], axis=0)
        heads = [kh * ATT_GRP + g for g in range(ATT_GRP)]
        per_tile = QKV_TILE // ATT_HD
        q_all = jnp.concatenate([q_ref[h // per_tile, :, (h % per_tile) * ATT_HD:(h % per_tile + 1) * ATT_HD]
                                 for h in heads], axis=0)
        s_all = lax.dot_general(q_all, k_all, (((1,), (1,)), ((), ())), preferred_element_type=f32)
        probs, inv_den = [], []
        for g in range(ATT_GRP):
            sink = sink_ref[kh * ATT_GRP + g] * LOG2_E
            s = s_all[g * t:(g + 1) * t]
            s = jnp.concatenate([jnp.where(masks[c], s[:, c * LANES:(c + 1) * LANES], NEG_INF) if c in masks
                                 else s[:, c * LANES:(c + 1) * LANES] for c in range(n_chunks)], axis=-1)
            m = jnp.maximum(jnp.max(s, axis=-1, keepdims=True), sink)
            e = jnp.exp2(s - m)
            inv_den.append(1.0 / (jnp.sum(e, axis=-1, keepdims=True) + jnp.exp2(sink - m)))
            probs.append(e.astype(bf16))
        o_all = jnp.dot(jnp.concatenate(probs, axis=0), v_all, preferred_element_type=f32)
        for g in range(ATT_GRP):
            h = kh * ATT_GRP + g
            o_ref[:, h * ATT_HD:(h + 1) * ATT_HD] = (o_all[g * t:(g + 1) * t] * inv_den[g]).astype(o_ref.dtype)


def _attention(qkv, sink, mlp_w1, mlp_w2):
    t = ATT_BLOCK
    nb = SEQ // t
    off = CTX_LEN // t
    qw = ATT_H * ATT_HD
    kvw = QKV_TILE
    kv_tile = QKV_TILES - 1

    def kv_spec(rows, row_block):
        return pl.BlockSpec((None, None, rows, kvw), lambda b, i: (kv_tile, b, row_block(i), 0))

    c_in, c_out, c_shapes = _cast_rider((mlp_w1, mlp_w2), DEPTH - 1, BATCH * nb, lambda b, i: b * nb + i)
    return pl.pallas_call(
        _attn_kernel,
        grid=(BATCH, nb),
        in_specs=[pl.BlockSpec(memory_space=pltpu.SMEM),
                  pl.BlockSpec((kv_tile, None, t, QKV_TILE), lambda b, i: (0, b, off + i, 0)),
                  kv_spec(t, lambda i: off + jnp.maximum(i - 1, 0)),
                  kv_spec(t, lambda i: off + i),
                  kv_spec(t, lambda i: off + jnp.minimum(i + 1, nb - 1)),
                  kv_spec(CTX_LEN, lambda i: 0)] + c_in,
        out_specs=[pl.BlockSpec((None, t, qw), lambda b, i: (b, i, 0))] + c_out,
        out_shape=[jax.ShapeDtypeStruct((BATCH, SEQ, qw), bf16)] + c_shapes,
        compiler_params=_cparams("parallel", "parallel"),
        name="window_attention",
    )(sink, qkv, qkv, qkv, qkv, qkv, mlp_w1, mlp_w2)


def _odd_out_kernel(a_ref, x_ref, gt_ref, gpost_ref, g2_ref, sh2_ref, sc2_ref, wo_ref, o_ref, h_ref):
    a = a_ref[...].reshape(BATCH * TT, D_MODEL)
    out = jnp.dot(a, wo_ref[...], preferred_element_type=f32).reshape(BATCH, TT, D_MODEL)
    _mix_epilogue(out, x_ref, gt_ref, gpost_ref, g2_ref, sh2_ref, sc2_ref, o_ref, h_ref)


def _odd_out_proj(a, rows, mods, gains, w_out):
    tile = pl.BlockSpec((BATCH, TT, D_MODEL), lambda i: (0, i, 0))
    return pl.pallas_call(
        _odd_out_kernel,
        grid=(SEQ // TT,),
        in_specs=[tile,
                  pl.BlockSpec((BATCH, TT, D_MODEL), lambda i: (0, i + CTX_TILES, 0)),
                  _mod_spec(2, 0), _gain_spec(1), _gain_spec(2), _mod_spec(3, 0), _mod_spec(4, 0),
                  _const_spec((D_MODEL, D_MODEL))],
        out_specs=[tile, tile],
        out_shape=[jax.ShapeDtypeStruct((BATCH, SEQ, D_MODEL), f32),
                   jax.ShapeDtypeStruct((BATCH, SEQ, D_MODEL), bf16)],
        compiler_params=_cparams("parallel"),
        name="odd_out_proj",
    )(a, rows, mods, gains, gains, mods, mods, w_out)


def _layer_mods(m):
    m = m.reshape(MOD_ROWS, MOD_CHUNKS, D_MODEL)
    lat = m[:BATCH]
    ctx = jnp.broadcast_to(m[CTX_MOD_ROW:CTX_MOD_ROW + 1], lat.shape)
    return jnp.stack([lat, ctx], axis=0).transpose(2, 0, 1, 3)[:, :, :, None, :]


def kernel(x, c, ctx, c_ctx, mod_w, mod_b, norm_g, mlp_w1, mlp_w2, even_w_in, even_w_out, s5_lam_re, s5_lam_im, s5_log_dt, s5_b_re, s5_b_im, s5_c_re, s5_c_im, s5_d, s5_glu_w, s5_glu_b, odd_w_in, odd_w_out, odd_sink):
    c_rows = jnp.concatenate([c, c_ctx[None, :], jnp.zeros((MOD_ROWS - BATCH - 1, D_MODEL), f32)], axis=0)
    mods_all = _modulation(c_rows, mod_w, mod_b)
    gains_all = norm_g.reshape(DEPTH, 4, 1, D_MODEL)

    mods, gains = _layer_mods(mods_all[0]), gains_all[0]
    perm = _chunk_swap_matrix()
    w_in = _to_bf16(even_w_in)[0]
    h, u, rows, u_blocks = _even_u_proj(x, ctx, gains, mods, w_in, perm)
    p, (w1, w2, glu_w, even_out_w, odd_in_w, odd_out_w) = _even_rest_proj(
        h, w_in, (mlp_w1, mlp_w2), (s5_glu_w, even_w_out, odd_w_in, odd_w_out))
    tables = _s5_tables(s5_lam_re[0], s5_lam_im[0], s5_log_dt[0], s5_b_re[0], s5_b_im[0], s5_c_re[0], s5_c_im[0])
    y_blocks = _s5_mix(u_blocks, tables)
    o_f, o_b = _retention(p)
    rows, h = _even_out_proj(u, p, y_blocks, o_f, o_b, rows, mods, gains, s5_d[0].reshape(1, S5_WIDTH),
                             s5_glu_b[0].reshape(1, S5_WIDTH), glu_w, even_out_w, perm)
    rows = _mlp(rows, h, CTX_TILES, mods, gains, w1, w2)

    mods, gains = _layer_mods(mods_all[1]), gains_all[1]
    cos, sin = _rope_tables()
    qkv = _odd_in_proj(rows, gains, mods, odd_in_w, cos, sin)
    a, w1, w2 = _attention(qkv, odd_sink[0].astype(f32), mlp_w1, mlp_w2)
    lat, h = _odd_out_proj(a, rows, mods, gains, odd_out_w)
    return _mlp(lat, h, 0, mods, gains, w1, w2)
```

```python
import math

import jax
import jax.numpy as jnp
import numpy as np
from jax import lax
from jax.experimental import pallas as pl
from jax.experimental.pallas import tpu as pltpu

D_MODEL = 2048
BATCH = 4
SEQ = 4096
DEPTH = 2
GRID_W = 64
CTX_LEN = 256
MOD_CHUNKS = 6
FFN_HIDDEN = 4 * D_MODEL
NORM_EPS = 1e-6

S5_WIDTH = D_MODEL // 2
S5_P = 16
S5_G = S5_WIDTH // S5_P
S5_N = 64
LAMBDA_RE_MAX = -1e-4
RET_DK = 256
RET_H = (D_MODEL // 2) // RET_DK
RET_DV = (D_MODEL // 2) // RET_H
RET_DECAY_BASE = -5.0
EVEN_IN = S5_WIDTH + 2 * RET_H * RET_DK + 2 * RET_H * RET_DV

ATT_HD = 64
ATT_H = D_MODEL // ATT_HD
ATT_KVH = ATT_H // 8
ATT_GRP = ATT_H // ATT_KVH
ODD_IN = (ATT_H + 2 * ATT_KVH) * ATT_HD
WINDOW = 128
ATT_BLOCK = 128
ROPE_BASE = 10000.0
NEG_INF = -1e30
LOG2_E = math.log2(math.e)
Q_SCALE = ATT_HD ** -0.5 * LOG2_E

LANES = 128
SUBLANES = 8

L_ALL = CTX_LEN + SEQ
TT = 128
N_TILES = L_ALL // TT
CTX_TILES = CTX_LEN // TT
NORM_ROWS = 16
NORM_UNROLL = 4
MOD_ROWS = 8
CTX_MOD_ROW = BATCH

S5_T = 16
S5_TP = S5_T * S5_P
S5_NC = L_ALL // S5_T
S5_NC_CTX = CTX_LEN // S5_T
S5_BPAD = SUBLANES
S5_GB = 4
S5_GB_TABLES = 8
S5_SCAN_UNROLL = 8
S5_TILE_BLOCKS = TT // S5_T
S5_LANE_GROUPS = LANES // S5_P
RET_T = 256
RET_STEPS = L_ALL // RET_T
RET_CHAINS = 2 * RET_H

CAST_BLOCK_ELEMS = 2 * 1024 * 1024
VMEM_LIMIT = 56 * 1024 * 1024

f32 = jnp.float32
bf16 = jnp.bfloat16


def _cparams(*sem):
    return pltpu.CompilerParams(dimension_semantics=sem, vmem_limit_bytes=VMEM_LIMIT)


def _mod_spec(chunk, ctx_tiles):
    return pl.BlockSpec((None, None, BATCH, 1, D_MODEL),
                        lambda i, *_: (chunk, jnp.where(i < ctx_tiles, 1, 0), 0, 0, 0))


def _gain_spec(k):
    return pl.BlockSpec((None, 1, D_MODEL), lambda *_: (k, 0, 0))


def _const_spec(shape):
    return pl.BlockSpec(shape, lambda *_: (0,) * len(shape), pipeline_mode=pl.Buffered(1))


def _norm_mod_rows(x_ref, g_ref, sc_ref, sh_ref, h_ref, copy_ref=None):
    for b in range(BATCH):
        gain = g_ref[...] * (1.0 + sc_ref[b])
        shift = sh_ref[b]

        def body(r, carry):
            r0 = pl.multiple_of(r * NORM_ROWS, NORM_ROWS)
            xf = x_ref[b, pl.ds(r0, NORM_ROWS), :]
            if copy_ref is not None:
                copy_ref[b, pl.ds(r0, NORM_ROWS), :] = xf
            ms = jnp.mean(xf * xf, axis=-1, keepdims=True)
            h = (xf * lax.rsqrt(ms + NORM_EPS) * gain + shift).astype(h_ref.dtype)
            if len(h_ref.shape) == 3:
                h_ref[b, pl.ds(r0, NORM_ROWS), :] = h
            else:
                h_ref[pl.ds(pl.multiple_of(b * TT + r0, NORM_ROWS), NORM_ROWS), :] = h
            return carry

        lax.fori_loop(0, TT // NORM_ROWS, body, 0, unroll=NORM_UNROLL)


def _gated_residual(m, x, gate, g_post):
    ms = jnp.mean(m * m, axis=-1, keepdims=True)
    return x + m * lax.rsqrt(ms + NORM_EPS) * (g_post * gate)


def _mix_epilogue(out, x_ref, gt_ref, gpost_ref, g2_ref, sh2_ref, sc2_ref, o_ref, h_ref):
    x1 = _gated_residual(out, x_ref[...], gt_ref[...], gpost_ref[...])
    o_ref[...] = x1
    ms = jnp.mean(x1 * x1, axis=-1, keepdims=True)
    h_ref[...] = (x1 * lax.rsqrt(ms + NORM_EPS) * (g2_ref[...] * (1.0 + sc2_ref[...])) + sh2_ref[...]).astype(h_ref.dtype)


def _gated_residual_rows(load_rows, x_ref, gt_ref, gpost_ref, o_ref):
    for b in range(BATCH):
        gain = gpost_ref[...] * gt_ref[b]

        def body(r, carry):
            r0 = pl.multiple_of(r * NORM_ROWS, NORM_ROWS)
            m = load_rows(b, r0, NORM_ROWS)
            ms = jnp.mean(m * m, axis=-1, keepdims=True)
            o_ref[b, pl.ds(r0, NORM_ROWS), :] = x_ref[b, pl.ds(r0, NORM_ROWS), :] + m * lax.rsqrt(ms + NORM_EPS) * gain
            return carry

        lax.fori_loop(0, TT // NORM_ROWS, body, 0, unroll=NORM_UNROLL)


def _chunk_swap_matrix():
    n = S5_LANE_GROUPS * LANES
    src = np.arange(n)
    i, j, p = src // LANES, (src // S5_P) % S5_LANE_GROUPS, src % S5_P
    return jnp.asarray(np.arange(n)[None, :] == (j * LANES + i * S5_P + p)[:, None], dtype=bf16)


def _cast_kernel(w_ref, o_ref):
    o_ref[...] = w_ref[...].astype(o_ref.dtype)


def _to_bf16(w):
    n_layers, rows, cols = w.shape
    block_rows = min(rows, pl.next_power_of_2(CAST_BLOCK_ELEMS // cols + 1) // 2)
    spec = pl.BlockSpec((None, block_rows, cols), lambda l, r: (l, r, 0))
    return pl.pallas_call(
        _cast_kernel,
        grid=(n_layers, rows // block_rows),
        in_specs=[spec],
        out_specs=spec,
        out_shape=jax.ShapeDtypeStruct(w.shape, bf16),
        compiler_params=_cparams("parallel", "parallel"),
        name="to_bf16",
    )(w)


def _mod_kernel(c_ref, w_ref, b_ref, o_ref):
    c = c_ref[...]
    s = c * jax.nn.sigmoid(c)
    o_ref[...] = jnp.dot(s.astype(bf16), w_ref[...].astype(bf16), preferred_element_type=f32) + b_ref[...]


def _modulation(c_rows, mod_w, mod_b):
    tn = 1024
    n = MOD_CHUNKS * D_MODEL
    return pl.pallas_call(
        _mod_kernel,
        grid=(DEPTH, n // tn),
        in_specs=[pl.BlockSpec((MOD_ROWS, D_MODEL), lambda l, j: (0, 0)),
                  pl.BlockSpec((None, D_MODEL, tn), lambda l, j: (l, 0, j)),
                  pl.BlockSpec((None, 1, tn), lambda l, j: (l, 0, j))],
        out_specs=pl.BlockSpec((None, MOD_ROWS, tn), lambda l, j: (l, 0, j)),
        out_shape=jax.ShapeDtypeStruct((DEPTH, MOD_ROWS, n), f32),
        compiler_params=_cparams("parallel", "parallel"),
        name="ada_modulation",
    )(c_rows, mod_w, mod_b.reshape(DEPTH, 1, n))


def _even_u_kernel(x_ref, c_ref, g_ref, sh_ref, sc_ref, w_ref, perm_ref, h_ref, u_ref, rows_ref, ub_ref,
                   a_ref, xs_ref, st_ref):
    i = pl.program_id(0)

    @pl.when(i < CTX_TILES)
    def _():
        _norm_mod_rows(c_ref, g_ref, sc_ref, sh_ref, h_ref, copy_ref=rows_ref)

    @pl.when(i >= CTX_TILES)
    def _():
        _norm_mod_rows(x_ref, g_ref, sc_ref, sh_ref, h_ref, copy_ref=rows_ref)

    acc = jnp.dot(h_ref[...].reshape(BATCH * TT, D_MODEL), w_ref[...], preferred_element_type=f32)
    u_ref[...] = acc.reshape(BATCH, TT, S5_WIDTH).astype(u_ref.dtype)

    n_slab, n_half = S5_WIDTH // LANES, S5_TP // LANES
    for s in range(n_slab):
        a_ref[s] = acc[:, s * LANES:(s + 1) * LANES]
    for s in range(n_slab):
        for half in range(n_half):
            for b in range(BATCH):
                r0 = ((s * n_half + half) * BATCH + b) * S5_TILE_BLOCKS
                for tt in range(S5_LANE_GROUPS):
                    xs_ref[tt, r0:r0 + S5_TILE_BLOCKS, :] = a_ref[
                        s, pl.ds(b * TT + half * S5_LANE_GROUPS + tt, S5_TILE_BLOCKS, stride=S5_T), :]
    xs = jnp.concatenate([xs_ref[tt] for tt in range(S5_LANE_GROUPS)], axis=-1).astype(bf16)
    ys = jnp.dot(xs, perm_ref[...], preferred_element_type=f32)
    @pl.when(i == 0)
    def _():
        st_ref[...] = jnp.zeros_like(st_ref)
    for s in range(n_slab):
        for half in range(n_half):
            for b in range(BATCH):
                r0 = ((s * n_half + half) * BATCH + b) * S5_TILE_BLOCKS
                for gg in range(S5_LANE_GROUPS):
                    st_ref[s * S5_LANE_GROUPS + gg, half, pl.ds(b, S5_TILE_BLOCKS, stride=S5_BPAD), :] = (
                        ys[r0:r0 + S5_TILE_BLOCKS, gg * LANES:(gg + 1) * LANES])
    for g in range(S5_G):
        ub_ref[g] = jnp.concatenate([st_ref[g, 0], st_ref[g, 1]], axis=-1).astype(ub_ref.dtype)


def _even_u_proj(x, ctx, gains, mods, w_in, perm):
    blk_rows = S5_TILE_BLOCKS * S5_BPAD
    tile = lambda width: pl.BlockSpec((BATCH, TT, width), lambda i: (0, i, 0))
    return pl.pallas_call(
        _even_u_kernel,
        grid=(N_TILES,),
        in_specs=[pl.BlockSpec((BATCH, TT, D_MODEL), lambda i: (0, jnp.maximum(i - CTX_TILES, 0), 0)),
                  pl.BlockSpec((BATCH, TT, D_MODEL), lambda i: (0, jnp.minimum(i, CTX_TILES - 1), 0)),
                  _gain_spec(0), _mod_spec(0, CTX_TILES), _mod_spec(1, CTX_TILES),
                  pl.BlockSpec((D_MODEL, S5_WIDTH), lambda i: (0, 0), pipeline_mode=pl.Buffered(1)),
                  _const_spec(perm.shape)],
        out_specs=[tile(D_MODEL), tile(S5_WIDTH), tile(D_MODEL),
                   pl.BlockSpec((S5_G, blk_rows, S5_TP), lambda i: (0, i, 0))],
        out_shape=[jax.ShapeDtypeStruct((BATCH, L_ALL, D_MODEL), bf16),
                   jax.ShapeDtypeStruct((BATCH, L_ALL, S5_WIDTH), bf16),
                   jax.ShapeDtypeStruct((BATCH, L_ALL, D_MODEL), f32),
                   jax.ShapeDtypeStruct((S5_G, S5_NC * S5_BPAD, S5_TP), bf16)],
        scratch_shapes=[pltpu.VMEM((S5_WIDTH // LANES, BATCH * TT, LANES), f32),
                        pltpu.VMEM((S5_LANE_GROUPS, BATCH * TT, LANES), f32),
                        pltpu.VMEM((S5_G, S5_TP // LANES, blk_rows, LANES), f32)],
        compiler_params=_cparams("arbitrary"),
        name="even_u_proj",
    )(x, ctx, gains, mods, mods, w_in, perm)


EVEN_REST_TILES = (EVEN_IN - S5_WIDTH) // S5_WIDTH


def _cast_rider(ws, layer, n_blocks, block_of):
    in_specs, out_specs, out_shapes = [], [], []
    for w in ws:
        _, rows, cols = w.shape
        in_specs.append(pl.BlockSpec((None, rows // n_blocks, cols), lambda *g: (layer, block_of(*g), 0)))
        out_specs.append(pl.BlockSpec((rows // n_blocks, cols), lambda *g: (block_of(*g), 0)))
        out_shapes.append(jax.ShapeDtypeStruct((rows, cols), bf16))
    return in_specs, out_specs, out_shapes


def _even_rest_kernel(h_ref, *refs):
    w_refs, refs = refs[:EVEN_REST_TILES], refs[EVEN_REST_TILES:]
    n_riders = len(refs) // 2
    rider_in, o_ref, rider_out = refs[:n_riders], refs[n_riders], refs[n_riders + 1:]
    h = h_ref[...].reshape(BATCH * TT, D_MODEL)
    for jt, w_ref in enumerate(w_refs):
        acc = jnp.dot(h, w_ref[...], preferred_element_type=f32)
        o_ref[:, :, jt * S5_WIDTH:(jt + 1) * S5_WIDTH] = acc.reshape(BATCH, TT, S5_WIDTH).astype(o_ref.dtype)
    for src_ref, dst_ref in zip(rider_in, rider_out):
        dst_ref[...] = src_ref[...].astype(dst_ref.dtype)


def _even_rest_proj(h, w_in, layer0_weights, shared_weights):
    width = EVEN_IN - S5_WIDTH
    w_specs = [pl.BlockSpec((D_MODEL, S5_WIDTH), lambda i, jt=jt: (0, 1 + jt), pipeline_mode=pl.Buffered(1))
               for jt in range(EVEN_REST_TILES)]
    n_blocks = 32
    riders = tuple(layer0_weights) + tuple(shared_weights)
    c_in, c_out, c_shapes = _cast_rider(riders, 0, n_blocks, lambda i: jnp.minimum(i, n_blocks - 1))
    outs = pl.pallas_call(
        _even_rest_kernel,
        grid=(N_TILES,),
        in_specs=[pl.BlockSpec((BATCH, TT, D_MODEL), lambda i: (0, i, 0))] + w_specs + c_in,
        out_specs=[pl.BlockSpec((BATCH, TT, width), lambda i: (0, i, 0))] + c_out,
        out_shape=[jax.ShapeDtypeStruct((BATCH, L_ALL, width), bf16)] + c_shapes,
        compiler_params=_cparams("arbitrary"),
        name="even_rest_proj",
    )(h, *([w_in] * EVEN_REST_TILES), *riders)
    return outs[0], outs[1:]


def _s5_tables(lam_re, lam_im, log_dt, b_re, b_im, c_re, c_im):
    lr = jnp.minimum(lam_re.astype(f32), LAMBDA_RE_MAX)
    li = lam_im.astype(f32)
    dt = jnp.exp(log_dt.astype(f32))[..., None]
    zr, zi = lr * dt, li * dt
    ab_mag = jnp.exp(zr)
    ab_re, ab_im = ab_mag * jnp.cos(zi), ab_mag * jnp.sin(zi)
    den = lr * lr + li * li
    nr = ab_re - 1.0
    f_re = (nr * lr + ab_im * li) / den
    f_im = (ab_im * lr - nr * li) / den
    br, bi = b_re.astype(f32), b_im.astype(f32)
    bb_re = f_re[..., None] * br - f_im[..., None] * bi
    bb_im = f_re[..., None] * bi + f_im[..., None] * br
    cr, ci = c_re.astype(f32), c_im.astype(f32)
    bt_re, bt_im = bb_re.transpose(0, 1, 3, 2), bb_im.transpose(0, 1, 3, 2)
    asc = np.arange(S5_T, dtype=np.float32)
    desc = asc[::-1].copy()

    def powers(d, ks):
        k = jnp.asarray(ks, dtype=f32)[:, None]
        mag = jnp.exp(zr[d][:, None, :] * k)
        return mag * jnp.cos(zi[d][:, None, :] * k), mag * jnp.sin(zi[d][:, None, :] * k)

    def both(fwd, bwd):
        return jnp.concatenate([fwd, bwd], axis=-1)

    def c_times_powers(d, ks):
        pr, pi = powers(d, ks)
        ca_re = cr[d][:, None] * pr[:, :, None, :] - ci[d][:, None] * pi[:, :, None, :]
        ca_im = cr[d][:, None] * pi[:, :, None, :] + ci[d][:, None] * pr[:, :, None, :]
        return both(ca_re, ca_im).reshape(S5_G, S5_TP, 2 * S5_N)

    bcat = both(bt_re, -bt_im)
    w_intra = _s5_intra_matrix(bcat[0], bcat[1], c_times_powers(0, asc), c_times_powers(1, desc))

    (pf_re, pf_im), (pb_re, pb_im) = powers(0, desc), powers(1, asc)
    p_re, p_im = both(pf_re, pb_re)[:, :, None, :], both(pf_im, pb_im)[:, :, None, :]
    b_re2, b_im2 = both(bt_re[0], bt_re[1])[:, None], both(bt_im[0], bt_im[1])[:, None]
    w_state = jnp.stack([(p_re * b_re2 - p_im * b_im2).reshape(S5_G, S5_TP, 2 * S5_N),
                         (p_re * b_im2 + p_im * b_re2).reshape(S5_G, S5_TP, 2 * S5_N)])

    def rows_fb(fwd, bwd):
        return jnp.concatenate([fwd.transpose(0, 2, 1), bwd.transpose(0, 2, 1)], axis=1)

    (qf_re, qf_im), (qb_re, qb_im) = powers(0, asc + 1.0), powers(1, desc + 1.0)
    q_re, q_im = rows_fb(qf_re, qb_re)[..., None], rows_fb(qf_im, qb_im)[..., None]
    c_re2, c_im2 = rows_fb(cr[0], cr[1])[:, :, None, :], rows_fb(ci[0], ci[1])[:, :, None, :]
    w_cross = jnp.stack([(c_re2 * q_re - c_im2 * q_im).reshape(S5_G, 2 * S5_N, S5_TP),
                         -(c_re2 * q_im + c_im2 * q_re).reshape(S5_G, 2 * S5_N, S5_TP)])

    (af_re, af_im), (ab_re_t, ab_im_t) = powers(0, [float(S5_T)]), powers(1, [float(S5_T)])
    a_re, a_im = both(af_re, ab_re_t), both(af_im, ab_im_t)
    return w_intra, w_state.astype(bf16), w_cross.astype(bf16), a_re, a_im


def _s5_intra_kernel(bf_ref, bb_ref, caf_ref, cab_ref, w_ref):
    lane = lax.broadcasted_iota(jnp.int32, (S5_P, S5_TP), 1)
    dn = (((1,), (1,)), ((), ()))
    for g in range(S5_GB_TABLES):
        tq_f = lax.dot_general(bf_ref[g], caf_ref[g], dn, precision=lax.Precision.HIGHEST, preferred_element_type=f32)
        tq_b = lax.dot_general(bb_ref[g], cab_ref[g], dn, precision=lax.Precision.HIGHEST, preferred_element_type=f32)
        for s in range(S5_T):
            sf, sb = S5_P * s, S5_P * (S5_T - 1 - s)
            row_f = tq_f if s == 0 else jnp.where(lane >= sf, pltpu.roll(tq_f, sf, 1), 0.0)
            row_b = tq_b if sb == 0 else jnp.where(lane < S5_TP - sb, pltpu.roll(tq_b, S5_TP - sb, 1), 0.0)
            w_ref[g, s * S5_P:(s + 1) * S5_P, :] = (row_f + row_b).astype(w_ref.dtype)


def _s5_intra_matrix(b_f, b_b, ca_f, ca_b):
    def spec(rows, cols):
        return pl.BlockSpec((S5_GB_TABLES, rows, cols), lambda i: (i, 0, 0))

    return pl.pallas_call(
        _s5_intra_kernel,
        grid=(S5_G // S5_GB_TABLES,),
        in_specs=[spec(S5_P, 2 * S5_N), spec(S5_P, 2 * S5_N), spec(S5_TP, 2 * S5_N), spec(S5_TP, 2 * S5_N)],
        out_specs=spec(S5_TP, S5_TP),
        out_shape=jax.ShapeDtypeStruct((S5_G, S5_TP, S5_TP), bf16),
        compiler_params=_cparams("parallel"),
        name="s5_intra_matrix",
    )(b_f, b_b, ca_f, ca_b)


def _s5_kernel(u_ref, wi_ref, ws_ref, wc_ref, are_ref, aim_ref, y_ref, s_ref):
    half = 2 * S5_N
    for g in range(S5_GB):
        w_state = jnp.concatenate([ws_ref[0, g], ws_ref[1, g]], axis=-1)
        s_ref[g] = jnp.dot(u_ref[g], w_state, preferred_element_type=f32)

    fwd_lane = lax.broadcasted_iota(jnp.int32, (S5_BPAD, half), 1) < S5_N
    a_re = [jnp.broadcast_to(are_ref[g], (S5_BPAD, half)) for g in range(S5_GB)]
    a_im = [jnp.broadcast_to(aim_ref[g], (S5_BPAD, half)) for g in range(S5_GB)]

    def step(i, carry):
        cb = jnp.where(i < S5_NC_CTX, S5_NC_CTX - 1 - i, S5_NC + S5_NC_CTX - 1 - i)
        rf = pl.multiple_of(i * S5_BPAD, S5_BPAD)
        rb = pl.multiple_of(cb * S5_BPAD, S5_BPAD)
        out = []
        for g in range(S5_GB):
            h_re, h_im = carry[2 * g], carry[2 * g + 1]
            l_re = jnp.where(fwd_lane, s_ref[g, pl.ds(rf, S5_BPAD), 0:half], s_ref[g, pl.ds(rb, S5_BPAD), 0:half])
            l_im = jnp.where(fwd_lane, s_ref[g, pl.ds(rf, S5_BPAD), half:2 * half],
                             s_ref[g, pl.ds(rb, S5_BPAD), half:2 * half])
            s_ref[g, pl.ds(rf, S5_BPAD), 0:S5_N] = h_re[:, 0:S5_N]
            s_ref[g, pl.ds(rb, S5_BPAD), S5_N:half] = h_re[:, S5_N:half]
            s_ref[g, pl.ds(rf, S5_BPAD), half:half + S5_N] = h_im[:, 0:S5_N]
            s_ref[g, pl.ds(rb, S5_BPAD), half + S5_N:2 * half] = h_im[:, S5_N:half]
            out.append(a_re[g] * h_re - a_im[g] * h_im + l_re)
            out.append(a_re[g] * h_im + a_im[g] * h_re + l_im)
        return tuple(out)

    zero = jnp.zeros((S5_BPAD, half), f32)
    lax.fori_loop(0, S5_NC, step, (zero,) * (2 * S5_GB), unroll=S5_SCAN_UNROLL)

    for g in range(S5_GB):
        y = jnp.dot(u_ref[g], wi_ref[g], preferred_element_type=f32)
        w_cross = jnp.concatenate([wc_ref[0, g], wc_ref[1, g]], axis=0)
        y = y + jnp.dot(s_ref[g].astype(bf16), w_cross, preferred_element_type=f32)
        for hf in range(S5_TP // LANES):
            y_ref[g, hf] = y[:, hf * LANES:(hf + 1) * LANES]


def _s5_mix(u_blocks, tables):
    w_intra, w_state, w_cross, a_re, a_im = tables
    rows = S5_NC * S5_BPAD
    wsp = pl.BlockSpec((S5_GB, S5_TP, S5_TP), lambda g: (g, 0, 0))
    w_state_spec = pl.BlockSpec((2, S5_GB, S5_TP, 2 * S5_N), lambda g: (0, g, 0, 0))
    w_cross_spec = pl.BlockSpec((2, S5_GB, 2 * S5_N, S5_TP), lambda g: (0, g, 0, 0))
    asp = pl.BlockSpec((S5_GB, 1, 2 * S5_N), lambda g: (g, 0, 0))
    return pl.pallas_call(
        _s5_kernel,
        grid=(S5_G // S5_GB,),
        in_specs=[pl.BlockSpec((S5_GB, rows, S5_TP), lambda g: (g, 0, 0)), wsp, w_state_spec, w_cross_spec, asp, asp],
        out_specs=pl.BlockSpec((S5_GB, S5_TP // LANES, rows, LANES), lambda g: (g, 0, 0, 0)),
        out_shape=jax.ShapeDtypeStruct((S5_G, S5_TP // LANES, rows, LANES), f32),
        scratch_shapes=[pltpu.VMEM((S5_GB, rows, S5_TP), f32)],
        compiler_params=_cparams("parallel"),
        name="s5_mix",
    )(u_blocks, w_intra, w_state, w_cross, a_re, a_im)


def _ret_kernel(lg_ref, cd_ref, qf_ref, kf_ref, vf_ref, qb_ref, kb_ref, vb_ref, of_ref, ob_ref,
                r_ref, dec_ref, qdec_ref, kdec_ref):
    t = RET_T

    @pl.when(pl.program_id(1) == 0)
    def _():
        r_ref[...] = jnp.zeros_like(r_ref)
        n = lax.broadcasted_iota(jnp.int32, (t, t), 0)
        m = lax.broadcasted_iota(jnp.int32, (t, t), 1)
        pos = lax.broadcasted_iota(jnp.int32, (t, 1), 0)
        for d in range(2):
            diff = (n - m if d == 0 else m - n).astype(f32)
            p = (pos if d == 0 else t - 1 - pos).astype(f32)
            for h in range(RET_H):
                c = d * RET_H + h
                lg = lg_ref[c]
                dec_ref[c] = jnp.where(diff >= 0, jnp.exp(jnp.maximum(diff, 0.0) * lg), 0.0)
                qdec_ref[c] = jnp.exp((p + 1.0) * lg)
                kdec_ref[c] = jnp.exp((t - 1.0 - p) * lg)

    for d, (q_ref, k_ref, v_ref, o_ref) in enumerate(((qf_ref, kf_ref, vf_ref, of_ref),
                                                        (qb_ref, kb_ref, vb_ref, ob_ref))):
        for h in range(RET_H):
            c = d * RET_H + h
            cols = slice(h * RET_DK, (h + 1) * RET_DK)
            q = q_ref[:, cols]
            ks = k_ref[:, cols] * (RET_DK ** -0.5)
            v = v_ref[:, cols]
            r = r_ref[c]
            s = lax.dot_general(q, ks, (((1,), (1,)), ((), ())), preferred_element_type=f32) * dec_ref[c]
            inner = jnp.dot(s.astype(bf16), v, preferred_element_type=f32)
            cross = jnp.dot(q, r.astype(bf16), preferred_element_type=f32) * qdec_ref[c]
            o_ref[:, cols] = (inner + cross).astype(o_ref.dtype)
            kd = (ks.astype(f32) * kdec_ref[c]).astype(bf16)
            r_ref[c] = r * cd_ref[c] + lax.dot_general(kd, v, (((0,), (0,)), ((), ())), preferred_element_type=f32)


def _retention(p):
    e = RET_DECAY_BASE - (2.0 * np.arange(RET_H, dtype=np.float64)[None, :] + np.arange(2, dtype=np.float64)[:, None])
    log_g64 = np.log1p(-np.exp2(e)).reshape(RET_CHAINS)
    log_g = jnp.asarray(log_g64, dtype=f32)
    chunk_dec = jnp.asarray(np.exp(RET_T * log_g64), dtype=f32)
    width = RET_H * RET_DK
    col0 = 0

    def bwd_chunk(i):
        return jnp.where(i == 0, 0, RET_STEPS - i)

    def fwd(off):
        return pl.BlockSpec((None, RET_T, width), lambda b, i: (b, i, col0 + off))

    def bwd(off):
        return pl.BlockSpec((None, RET_T, width), lambda b, i: (b, bwd_chunk(i), col0 + off))

    smem = pl.BlockSpec(memory_space=pltpu.SMEM)
    out = jax.ShapeDtypeStruct((BATCH, L_ALL, RET_H * RET_DV), bf16)
    return pl.pallas_call(
        _ret_kernel,
        grid=(BATCH, RET_STEPS),
        in_specs=[smem, smem, fwd(0), fwd(1), fwd(2), bwd(0), bwd(1), bwd(2)],
        out_specs=[pl.BlockSpec((None, RET_T, RET_H * RET_DV), lambda b, i: (b, i, 0)),
                   pl.BlockSpec((None, RET_T, RET_H * RET_DV), lambda b, i: (b, bwd_chunk(i), 0))],
        out_shape=[out, out],
        scratch_shapes=[pltpu.VMEM((RET_CHAINS, RET_DK, RET_DV), f32),
                        pltpu.VMEM((RET_CHAINS, RET_T, RET_T), f32),
                        pltpu.VMEM((RET_CHAINS, RET_T, 1), f32),
                        pltpu.VMEM((RET_CHAINS, RET_T, 1), f32)],
        compiler_params=_cparams("parallel", "arbitrary"),
        name="retention",
    )(log_g, chunk_dec, p, p, p, p, p, p)


def _gelu_tanh(x):
    return 0.5 * x * (1.0 + jnp.tanh(math.sqrt(2.0 / math.pi) * (x + 0.044715 * (x * x * x))))


def _even_out_kernel(u_ref, gate_ref, y_ref, of_ref, ob_ref, x_ref, gt_ref, gpost_ref, g2_ref, sh2_ref, sc2_ref,
                     dsk_ref, glub_ref, gluw_ref, wo_ref, perm_ref, o_ref, h_ref, z_ref, xs_ref, yf_ref):
    rows = BATCH * TT
    n_slab, n_half = S5_WIDTH // LANES, S5_TP // LANES
    for s in range(n_slab):
        for half in range(n_half):
            for b in range(BATCH):
                r0 = ((s * n_half + half) * BATCH + b) * S5_TILE_BLOCKS
                for gg in range(S5_LANE_GROUPS):
                    xs_ref[gg, r0:r0 + S5_TILE_BLOCKS, :] = y_ref[
                        s * S5_LANE_GROUPS + gg, half, pl.ds(b, S5_TILE_BLOCKS, stride=S5_BPAD), :]
    xs = jnp.concatenate([xs_ref[gg] for gg in range(S5_LANE_GROUPS)], axis=-1).astype(bf16)
    ys = jnp.dot(xs, perm_ref[...], preferred_element_type=f32)
    for s in range(n_slab):
        for half in range(n_half):
            for b in range(BATCH):
                r0 = ((s * n_half + half) * BATCH + b) * S5_TILE_BLOCKS
                for tt in range(S5_LANE_GROUPS):
                    z_ref[s, pl.ds(b * TT + half * S5_LANE_GROUPS + tt, S5_TILE_BLOCKS, stride=S5_T), :] = (
                        ys[r0:r0 + S5_TILE_BLOCKS, tt * LANES:(tt + 1) * LANES])
    u = u_ref[...].reshape(rows, S5_WIDTH)
    for s in range(S5_WIDTH // LANES):
        cols = slice(s * LANES, (s + 1) * LANES)
        yf_ref[:, cols] = _gelu_tanh(z_ref[s] + dsk_ref[:, cols] * u[:, cols].astype(f32))
    y = yf_ref[...]
    z = jnp.dot(y.astype(bf16), gluw_ref[...], preferred_element_type=f32) + glub_ref[...]
    s5_out = y * jax.nn.sigmoid(z)
    r = (of_ref[...].astype(f32) + ob_ref[...].astype(f32)).reshape(rows, RET_H * RET_DV)
    heads = []
    for h in range(RET_H):
        rh = r[:, h * RET_DV:(h + 1) * RET_DV]
        heads.append(rh * lax.rsqrt(jnp.mean(rh * rh, axis=-1, keepdims=True) + NORM_EPS))
    g = gate_ref[...].reshape(rows, RET_H * RET_DV).astype(f32)
    ret_out = jnp.concatenate(heads, axis=-1) * (g * jax.nn.sigmoid(g))
    out = jnp.dot(s5_out.astype(bf16), wo_ref[0:S5_WIDTH, :], preferred_element_type=f32)
    out = out + jnp.dot(ret_out.astype(bf16), wo_ref[S5_WIDTH:, :], preferred_element_type=f32)
    _mix_epilogue(out.reshape(BATCH, TT, D_MODEL), x_ref, gt_ref, gpost_ref, g2_ref, sh2_ref, sc2_ref, o_ref, h_ref)


def _even_out_proj(u, p, y_blocks, o_f, o_b, rows, mods, gains, d_skip, glu_b, glu_w, w_out, perm):
    half = S5_WIDTH
    blk_rows = S5_TILE_BLOCKS * S5_BPAD
    tile = pl.BlockSpec((BATCH, TT, half), lambda i: (0, i, 0))
    return pl.pallas_call(
        _even_out_kernel,
        grid=(N_TILES,),
        in_specs=[tile,
                  pl.BlockSpec((BATCH, TT, half), lambda i: (0, i, EVEN_REST_TILES - 1)),
                  pl.BlockSpec((S5_G, S5_TP // LANES, blk_rows, LANES), lambda i: (0, 0, i, 0)),
                  tile, tile,
                  pl.BlockSpec((BATCH, TT, D_MODEL), lambda i: (0, i, 0)),
                  _mod_spec(2, CTX_TILES), _gain_spec(1),
                  _gain_spec(2), _mod_spec(3, CTX_TILES), _mod_spec(4, CTX_TILES),
                  _const_spec((1, half)), _const_spec((1, half)),
                  _const_spec((half, half)), _const_spec((D_MODEL, D_MODEL)), _const_spec(perm.shape)],
        out_specs=[pl.BlockSpec((BATCH, TT, D_MODEL), lambda i: (0, i, 0)),
                   pl.BlockSpec((BATCH, TT, D_MODEL), lambda i: (0, i, 0))],
        out_shape=[jax.ShapeDtypeStruct((BATCH, L_ALL, D_MODEL), f32),
                   jax.ShapeDtypeStruct((BATCH, L_ALL, D_MODEL), bf16)],
        scratch_shapes=[pltpu.VMEM((S5_WIDTH // LANES, BATCH * TT, LANES), f32),
                        pltpu.VMEM((S5_LANE_GROUPS, BATCH * TT, LANES), f32),
                        pltpu.VMEM((BATCH * TT, S5_WIDTH), f32)],
        compiler_params=_cparams("parallel"),
        name="even_out_proj",
    )(u, p, y_blocks, o_f, o_b, rows, mods, gains, gains, mods, mods, d_skip, glu_b, glu_w, w_out, perm)


def _mlp_kernel(x_ref, h_ref, gt_ref, gpost_ref, w1_ref, w2_ref, o_ref, acc_ref):
    f = pl.program_id(1)

    @pl.when(f == 0)
    def _():
        acc_ref[...] = jnp.zeros_like(acc_ref)

    h = h_ref[...].reshape(BATCH * TT, D_MODEL)
    a = jnp.maximum(jnp.dot(h, w1_ref[...], preferred_element_type=f32), 0.0)
    acc_ref[...] += jnp.dot((a * a).astype(bf16), w2_ref[...], preferred_element_type=f32)

    @pl.when(f == pl.num_programs(1) - 1)
    def _():
        _gated_residual_rows(lambda b, r0, n: acc_ref[pl.ds(pl.multiple_of(b * TT + r0, n), n), :],
                             x_ref, gt_ref, gpost_ref, o_ref)


def _mlp(x, h, ctx_tiles, mods, gains, w1, w2):
    tf = 1024
    n_tiles = x.shape[1] // TT
    tile = pl.BlockSpec((BATCH, TT, D_MODEL), lambda i, f: (0, i, 0))
    return pl.pallas_call(
        _mlp_kernel,
        grid=(n_tiles, FFN_HIDDEN // tf),
        in_specs=[tile, tile, _mod_spec(5, ctx_tiles), _gain_spec(3),
                  pl.BlockSpec((D_MODEL, tf), lambda i, f: (0, f)),
                  pl.BlockSpec((tf, D_MODEL), lambda i, f: (f, 0))],
        out_specs=tile,
        out_shape=jax.ShapeDtypeStruct((BATCH, n_tiles * TT, D_MODEL), f32),
        scratch_shapes=[pltpu.VMEM((BATCH * TT, D_MODEL), f32)],
        compiler_params=_cparams("parallel", "arbitrary"),
        name="sq_relu_mlp",
    )(x, h, mods, gains, w1, w2)


def _rope_tables():
    rows = SEQ // GRID_W
    row = np.repeat(np.arange(rows, dtype=np.float64), GRID_W)
    col = np.tile(np.arange(GRID_W, dtype=np.float64), rows)
    n_freq = ATT_HD // 4
    inv_freq = ROPE_BASE ** (-np.arange(n_freq, dtype=np.float64) / n_freq)
    ang = np.concatenate([row[:, None] * inv_freq[None], col[:, None] * inv_freq[None]], axis=-1)
    cos, sin = np.cos(ang), np.sin(ang)
    cos, sin = np.tile(cos, (1, 4)), np.concatenate([-sin, sin, -sin, sin], axis=-1)
    return (jnp.asarray(np.stack([cos * Q_SCALE, cos, np.ones_like(cos)]), dtype=f32),
            jnp.asarray(np.stack([sin * Q_SCALE, sin, np.zeros_like(sin)]), dtype=f32))


ROPE_Q, ROPE_K, ROPE_NONE = 0, 1, 2


def _rope(x, cos, sin):
    half = ATT_HD // 2
    lane = lax.broadcasted_iota(jnp.int32, x.shape, x.ndim - 1)
    partner = jnp.where((lane & (ATT_HD - 1)) < half,
                        pltpu.roll(x, LANES - half, x.ndim - 1), pltpu.roll(x, half, x.ndim - 1))
    return x * cos + partner * sin


QKV_TILE = 2 * ATT_KVH * ATT_HD
QKV_TILES = ODD_IN // QKV_TILE


def _odd_in_kernel(x_ref, g_ref, sh_ref, sc_ref, w_ref, cos_q_ref, sin_q_ref, cos_k_ref, sin_k_ref, o_ref, h_ref):
    _norm_mod_rows(x_ref, g_ref, sc_ref, sh_ref, h_ref)
    kw = ATT_KVH * ATT_HD
    for jt in range(QKV_TILES):
        acc = jnp.dot(h_ref[...], w_ref[:, jt * QKV_TILE:(jt + 1) * QKV_TILE], preferred_element_type=f32)
        for c in range(QKV_TILE // LANES):
            cols = slice(c * LANES, (c + 1) * LANES)
            xc = acc[:, cols].reshape(BATCH, TT, LANES)
            if jt < QKV_TILES - 1:
                xc = _rope(xc, cos_q_ref[...], sin_q_ref[...])
            elif c * LANES < kw:
                xc = _rope(xc, cos_k_ref[...], sin_k_ref[...])
            o_ref[jt, :, :, cols] = xc.astype(o_ref.dtype)


def _odd_in_proj(x, gains, mods, w_in, cos, sin):
    def table(kind_of):
        return pl.BlockSpec((None, TT, LANES), lambda i: (kind_of(i), jnp.maximum(i - CTX_TILES, 0), 0))

    q_table = table(lambda i: ROPE_Q)
    k_table = table(lambda i: jnp.where(i >= CTX_TILES, ROPE_K, ROPE_NONE))
    return pl.pallas_call(
        _odd_in_kernel,
        grid=(N_TILES,),
        in_specs=[pl.BlockSpec((BATCH, TT, D_MODEL), lambda i: (0, i, 0)),
                  _gain_spec(0), _mod_spec(0, CTX_TILES), _mod_spec(1, CTX_TILES),
                  _const_spec((D_MODEL, ODD_IN)), q_table, q_table, k_table, k_table],
        out_specs=pl.BlockSpec((QKV_TILES, BATCH, TT, QKV_TILE), lambda i: (0, 0, i, 0)),
        out_shape=jax.ShapeDtypeStruct((QKV_TILES, BATCH, L_ALL, QKV_TILE), bf16),
        scratch_shapes=[pltpu.VMEM((BATCH * TT, D_MODEL), bf16)],
        compiler_params=_cparams("parallel"),
        name="odd_in_proj",
    )(x, gains, mods, mods, w_in, cos, sin, cos, sin)


def _attn_kernel(sink_ref, q_ref, kvp_ref, kvc_ref, kvn_ref, kvx_ref, w1f_ref, w2f_ref, o_ref, w1b_ref, w2b_ref):
    w1b_ref[...] = w1f_ref[...].astype(w1b_ref.dtype)
    w2b_ref[...] = w2f_ref[...].astype(w2b_ref.dtype)
    qb = pl.program_id(1)
    t = ATT_BLOCK
    kw = ATT_KVH * ATT_HD
    row = lax.broadcasted_iota(jnp.int32, (t, t), 0)
    col = lax.broadcasted_iota(jnp.int32, (t, t), 1)

    def band_valid(blk):
        off = col + blk * t
        kpos = qb * t - WINDOW + off
        return (jnp.abs(off - WINDOW - row) <= WINDOW) & (kpos >= 0) & (kpos < SEQ)

    n_ctx_chunks = CTX_LEN // LANES
    masks = {n_ctx_chunks: band_valid(0), n_ctx_chunks + 2: band_valid(2)}
    n_chunks = n_ctx_chunks + 3

    for kh in range(ATT_KVH):
        ks = slice(kh * ATT_HD, (kh + 1) * ATT_HD)
        vs = slice(kw + kh * ATT_HD, kw + (kh + 1) * ATT_HD)
        k_all = jnp.concatenate([kvx_ref[:, ks], kvp_ref[:, ks], kvc_ref[:, ks], kvn_ref[:, ks]], axis=0)
        v_all = jnp.concatenate([kvx_ref[:, vs], kvp_ref[:, vs], kvc_ref[:, vs], kvn_ref[:, vs]], axis=0)
        heads = [kh * ATT_GRP + g for g in range(ATT_GRP)]
        per_tile = QKV_TILE // ATT_HD
        q_all = jnp.concatenate([q_ref[h // per_tile, :, (h % per_tile) * ATT_HD:(h % per_tile + 1) * ATT_HD]
                                 for h in heads], axis=0)
        s_all = lax.dot_general(q_all, k_all, (((1,), (1,)), ((), ())), preferred_element_type=f32)
        probs, inv_den = [], []
        for g in range(ATT_GRP):
            sink = sink_ref[kh * ATT_GRP + g] * LOG2_E
            s = s_all[g * t:(g + 1) * t]
            s = jnp.concatenate([jnp.where(masks[c], s[:, c * LANES:(c + 1) * LANES], NEG_INF) if c in masks
                                 else s[:, c * LANES:(c + 1) * LANES] for c in range(n_chunks)], axis=-1)
            m = jnp.maximum(jnp.max(s, axis=-1, keepdims=True), sink)
            e = jnp.exp2(s - m)
            inv_den.append(1.0 / (jnp.sum(e, axis=-1, keepdims=True) + jnp.exp2(sink - m)))
            probs.append(e.astype(bf16))
        o_all = jnp.dot(jnp.concatenate(probs, axis=0), v_all, preferred_element_type=f32)
        for g in range(ATT_GRP):
            h = kh * ATT_GRP + g
            o_ref[:, h * ATT_HD:(h + 1) * ATT_HD] = (o_all[g * t:(g + 1) * t] * inv_den[g]).astype(o_ref.dtype)


def _attention(qkv, sink, mlp_w1, mlp_w2):
    t = ATT_BLOCK
    nb = SEQ // t
    off = CTX_LEN // t
    qw = ATT_H * ATT_HD
    kvw = QKV_TILE
    kv_tile = QKV_TILES - 1

    def kv_spec(rows, row_block):
        return pl.BlockSpec((None, None, rows, kvw), lambda b, i: (kv_tile, b, row_block(i), 0))

    c_in, c_out, c_shapes = _cast_rider((mlp_w1, mlp_w2), DEPTH - 1, BATCH * nb, lambda b, i: b * nb + i)
    return pl.pallas_call(
        _attn_kernel,
        grid=(BATCH, nb),
        in_specs=[pl.BlockSpec(memory_space=pltpu.SMEM),
                  pl.BlockSpec((kv_tile, None, t, QKV_TILE), lambda b, i: (0, b, off + i, 0)),
                  kv_spec(t, lambda i: off + jnp.maximum(i - 1, 0)),
                  kv_spec(t, lambda i: off + i),
                  kv_spec(t, lambda i: off + jnp.minimum(i + 1, nb - 1)),
                  kv_spec(CTX_LEN, lambda i: 0)] + c_in,
        out_specs=[pl.BlockSpec((None, t, qw), lambda b, i: (b, i, 0))] + c_out,
        out_shape=[jax.ShapeDtypeStruct((BATCH, SEQ, qw), bf16)] + c_shapes,
        compiler_params=_cparams("parallel", "parallel"),
        name="window_attention",
    )(sink, qkv, qkv, qkv, qkv, qkv, mlp_w1, mlp_w2)


def _odd_out_kernel(a_ref, x_ref, gt_ref, gpost_ref, g2_ref, sh2_ref, sc2_ref, wo_ref, o_ref, h_ref):
    a = a_ref[...].reshape(BATCH * TT, D_MODEL)
    out = jnp.dot(a, wo_ref[...], preferred_element_type=f32).reshape(BATCH, TT, D_MODEL)
    _mix_epilogue(out, x_ref, gt_ref, gpost_ref, g2_ref, sh2_ref, sc2_ref, o_ref, h_ref)


def _odd_out_proj(a, rows, mods, gains, w_out):
    tile = pl.BlockSpec((BATCH, TT, D_MODEL), lambda i: (0, i, 0))
    return pl.pallas_call(
        _odd_out_kernel,
        grid=(SEQ // TT,),
        in_specs=[tile,
                  pl.BlockSpec((BATCH, TT, D_MODEL), lambda i: (0, i + CTX_TILES, 0)),
                  _mod_spec(2, 0), _gain_spec(1), _gain_spec(2), _mod_spec(3, 0), _mod_spec(4, 0),
                  _const_spec((D_MODEL, D_MODEL))],
        out_specs=[tile, tile],
        out_shape=[jax.ShapeDtypeStruct((BATCH, SEQ, D_MODEL), f32),
                   jax.ShapeDtypeStruct((BATCH, SEQ, D_MODEL), bf16)],
        compiler_params=_cparams("parallel"),
        name="odd_out_proj",
    )(a, rows, mods, gains, gains, mods, mods, w_out)


def _layer_mods(m):
    m = m.reshape(MOD_ROWS, MOD_CHUNKS, D_MODEL)
    lat = m[:BATCH]
    ctx = jnp.broadcast_to(m[CTX_MOD_ROW:CTX_MOD_ROW + 1], lat.shape)
    return jnp.stack([lat, ctx], axis=0).transpose(2, 0, 1, 3)[:, :, :, None, :]


def kernel(x, c, ctx, c_ctx, mod_w, mod_b, norm_g, mlp_w1, mlp_w2, even_w_in, even_w_out, s5_lam_re, s5_lam_im, s5_log_dt, s5_b_re, s5_b_im, s5_c_re, s5_c_im, s5_d, s5_glu_w, s5_glu_b, odd_w_in, odd_w_out, odd_sink):
    c_rows = jnp.concatenate([c, c_ctx[None, :], jnp.zeros((MOD_ROWS - BATCH - 1, D_MODEL), f32)], axis=0)
    mods_all = _modulation(c_rows, mod_w, mod_b)
    gains_all = norm_g.reshape(DEPTH, 4, 1, D_MODEL)

    mods, gains = _layer_mods(mods_all[0]), gains_all[0]
    perm = _chunk_swap_matrix()
    w_in = _to_bf16(even_w_in)[0]
    h, u, rows, u_blocks = _even_u_proj(x, ctx, gains, mods, w_in, perm)
    p, (w1, w2, glu_w, even_out_w, odd_in_w, odd_out_w) = _even_rest_proj(
        h, w_in, (mlp_w1, mlp_w2), (s5_glu_w, even_w_out, odd_w_in, odd_w_out))
    tables = _s5_tables(s5_lam_re[0], s5_lam_im[0], s5_log_dt[0], s5_b_re[0], s5_b_im[0], s5_c_re[0], s5_c_im[0])
    y_blocks = _s5_mix(u_blocks, tables)
    o_f, o_b = _retention(p)
    rows, h = _even_out_proj(u, p, y_blocks, o_f, o_b, rows, mods, gains, s5_d[0].reshape(1, S5_WIDTH),
                             s5_glu_b[0].reshape(1, S5_WIDTH), glu_w, even_out_w, perm)
    rows = _mlp(rows, h, CTX_TILES, mods, gains, w1, w2)

    mods, gains = _layer_mods(mods_all[1]), gains_all[1]
    cos, sin = _rope_tables()
    qkv = _odd_in_proj(rows, gains, mods, odd_in_w, cos, sin)
    a, w1, w2 = _attention(qkv, odd_sink[0].astype(f32), mlp_w1, mlp_w2)
    lat, h = _odd_out_proj(a, rows, mods, gains, odd_out_w)
    return _mlp(lat, h, 0, mods, gains, w1, w2)
```

```python
import math

import jax
import jax.numpy as jnp
import numpy as np
from jax import lax
from jax.experimental import pallas as pl
from jax.experimental.pallas import tpu as pltpu

D_MODEL = 2048
BATCH = 4
SEQ = 4096
DEPTH = 2
GRID_W = 64
CTX_LEN = 256
MOD_CHUNKS = 6
FFN_HIDDEN = 4 * D_MODEL
NORM_EPS = 1e-6

S5_WIDTH = D_MODEL // 2
S5_P = 16
S5_G = S5_WIDTH // S5_P
S5_N = 64
LAMBDA_RE_MAX = -1e-4
RET_DK = 256
RET_H = (D_MODEL // 2) // RET_DK
RET_DV = (D_MODEL // 2) // RET_H
RET_DECAY_BASE = -5.0
EVEN_IN = S5_WIDTH + 2 * RET_H * RET_DK + 2 * RET_H * RET_DV

ATT_HD = 64
ATT_H = D_MODEL // ATT_HD
ATT_KVH = ATT_H // 8
ATT_GRP = ATT_H // ATT_KVH
ODD_IN = (ATT_H + 2 * ATT_KVH) * ATT_HD
WINDOW = 128
ATT_BLOCK = 128
ROPE_BASE = 10000.0
NEG_INF = -1e30
LOG2_E = math.log2(math.e)
Q_SCALE = ATT_HD ** -0.5 * LOG2_E

LANES = 128
SUBLANES = 8

L_ALL = CTX_LEN + SEQ
TT = 128
N_TILES = L_ALL // TT
CTX_TILES = CTX_LEN // TT
NORM_ROWS = 16
NORM_UNROLL = 4
MOD_ROWS = 8
CTX_MOD_ROW = BATCH

S5_T = 16
S5_TP = S5_T * S5_P
S5_NC = L_ALL // S5_T
S5_NC_CTX = CTX_LEN // S5_T
S5_BPAD = SUBLANES
S5_GB = 4
S5_GB_TABLES = 8
S5_SCAN_UNROLL = 8
S5_TILE_BLOCKS = TT // S5_T
S5_LANE_GROUPS = LANES // S5_P
RET_T = 256
RET_STEPS = L_ALL // RET_T
RET_CHAINS = 2 * RET_H

CAST_BLOCK_ELEMS = 2 * 1024 * 1024
VMEM_LIMIT = 56 * 1024 * 1024

f32 = jnp.float32
bf16 = jnp.bfloat16


def _cparams(*sem):
    return pltpu.CompilerParams(dimension_semantics=sem, vmem_limit_bytes=VMEM_LIMIT)


def _mod_spec(chunk, ctx_tiles):
    return pl.BlockSpec((None, None, BATCH, 1, D_MODEL),
                        lambda i, *_: (chunk, jnp.where(i < ctx_tiles, 1, 0), 0, 0, 0))


def _gain_spec(k):
    return pl.BlockSpec((None, 1, D_MODEL), lambda *_: (k, 0, 0))


def _const_spec(shape):
    return pl.BlockSpec(shape, lambda *_: (0,) * len(shape), pipeline_mode=pl.Buffered(1))


def _norm_mod_rows(x_ref, g_ref, sc_ref, sh_ref, h_ref, copy_ref=None):
    for b in range(BATCH):
        gain = g_ref[...] * (1.0 + sc_ref[b])
        shift = sh_ref[b]

        def body(r, carry):
            r0 = pl.multiple_of(r * NORM_ROWS, NORM_ROWS)
            xf = x_ref[b, pl.ds(r0, NORM_ROWS), :]
            if copy_ref is not None:
                copy_ref[b, pl.ds(r0, NORM_ROWS), :] = xf
            ms = jnp.mean(xf * xf, axis=-1, keepdims=True)
            h = (xf * lax.rsqrt(ms + NORM_EPS) * gain + shift).astype(h_ref.dtype)
            if len(h_ref.shape) == 3:
                h_ref[b, pl.ds(r0, NORM_ROWS), :] = h
            else:
                h_ref[pl.ds(pl.multiple_of(b * TT + r0, NORM_ROWS), NORM_ROWS), :] = h
            return carry

        lax.fori_loop(0, TT // NORM_ROWS, body, 0, unroll=NORM_UNROLL)


def _gated_residual(m, x, gate, g_post):
    ms = jnp.mean(m * m, axis=-1, keepdims=True)
    return x + m * lax.rsqrt(ms + NORM_EPS) * (g_post * gate)


def _mix_epilogue(out, x_ref, gt_ref, gpost_ref, g2_ref, sh2_ref, sc2_ref, o_ref, h_ref):
    x1 = _gated_residual(out, x_ref[...], gt_ref[...], gpost_ref[...])
    o_ref[...] = x1
    ms = jnp.mean(x1 * x1, axis=-1, keepdims=True)
    h_ref[...] = (x1 * lax.rsqrt(ms + NORM_EPS) * (g2_ref[...] * (1.0 + sc2_ref[...])) + sh2_ref[...]).astype(h_ref.dtype)


def _gated_residual_rows(load_rows, x_ref, gt_ref, gpost_ref, o_ref):
    for b in range(BATCH):
        gain = gpost_ref[...] * gt_ref[b]

        def body(r, carry):
            r0 = pl.multiple_of(r * NORM_ROWS, NORM_ROWS)
            m = load_rows(b, r0, NORM_ROWS)
            ms = jnp.mean(m * m, axis=-1, keepdims=True)
            o_ref[b, pl.ds(r0, NORM_ROWS), :] = x_ref[b, pl.ds(r0, NORM_ROWS), :] + m * lax.rsqrt(ms + NORM_EPS) * gain
            return carry

        lax.fori_loop(0, TT // NORM_ROWS, body, 0, unroll=NORM_UNROLL)


def _chunk_swap_matrix():
    n = S5_LANE_GROUPS * LANES
    src = np.arange(n)
    i, j, p = src // LANES, (src // S5_P) % S5_LANE_GROUPS, src % S5_P
    return jnp.asarray(np.arange(n)[None, :] == (j * LANES + i * S5_P + p)[:, None], dtype=bf16)


def _cast_kernel(w_ref, o_ref):
    o_ref[...] = w_ref[...].astype(o_ref.dtype)


def _to_bf16(w):
    n_layers, rows, cols = w.shape
    block_rows = min(rows, pl.next_power_of_2(CAST_BLOCK_ELEMS // cols + 1) // 2)
    spec = pl.BlockSpec((None, block_rows, cols), lambda l, r: (l, r, 0))
    return pl.pallas_call(
        _cast_kernel,
        grid=(n_layers, rows // block_rows),
        in_specs=[spec],
        out_specs=spec,
        out_shape=jax.ShapeDtypeStruct(w.shape, bf16),
        compiler_params=_cparams("parallel", "parallel"),
        name="to_bf16",
    )(w)


def _mod_kernel(c_ref, w_ref, b_ref, o_ref):
    c = c_ref[...]
    s = c * jax.nn.sigmoid(c)
    o_ref[...] = jnp.dot(s.astype(bf16), w_ref[...].astype(bf16), preferred_element_type=f32) + b_ref[...]


def _modulation(c_rows, mod_w, mod_b):
    tn = 1024
    n = MOD_CHUNKS * D_MODEL
    return pl.pallas_call(
        _mod_kernel,
        grid=(DEPTH, n // tn),
        in_specs=[pl.BlockSpec((MOD_ROWS, D_MODEL), lambda l, j: (0, 0)),
                  pl.BlockSpec((None, D_MODEL, tn), lambda l, j: (l, 0, j)),
                  pl.BlockSpec((None, 1, tn), lambda l, j: (l, 0, j))],
        out_specs=pl.BlockSpec((None, MOD_ROWS, tn), lambda l, j: (l, 0, j)),
        out_shape=jax.ShapeDtypeStruct((DEPTH, MOD_ROWS, n), f32),
        compiler_params=_cparams("parallel", "parallel"),
        name="ada_modulation",
    )(c_rows, mod_w, mod_b.reshape(DEPTH, 1, n))


def _even_u_kernel(x_ref, c_ref, g_ref, sh_ref, sc_ref, w_ref, perm_ref, h_ref, u_ref, rows_ref, ub_ref,
                   a_ref, xs_ref, st_ref):
    i = pl.program_id(0)

    @pl.when(i < CTX_TILES)
    def _():
        _norm_mod_rows(c_ref, g_ref, sc_ref, sh_ref, h_ref, copy_ref=rows_ref)

    @pl.when(i >= CTX_TILES)
    def _():
        _norm_mod_rows(x_ref, g_ref, sc_ref, sh_ref, h_ref, copy_ref=rows_ref)

    acc = jnp.dot(h_ref[...].reshape(BATCH * TT, D_MODEL), w_ref[...], preferred_element_type=f32)
    u_ref[...] = acc.reshape(BATCH, TT, S5_WIDTH).astype(u_ref.dtype)

    n_slab, n_half = S5_WIDTH // LANES, S5_TP // LANES
    for s in range(n_slab):
        a_ref[s] = acc[:, s * LANES:(s + 1) * LANES]
    for s in range(n_slab):
        for half in range(n_half):
            for b in range(BATCH):
                r0 = ((s * n_half + half) * BATCH + b) * S5_TILE_BLOCKS
                for tt in range(S5_LANE_GROUPS):
                    xs_ref[tt, r0:r0 + S5_TILE_BLOCKS, :] = a_ref[
                        s, pl.ds(b * TT + half * S5_LANE_GROUPS + tt, S5_TILE_BLOCKS, stride=S5_T), :]
    xs = jnp.concatenate([xs_ref[tt] for tt in range(S5_LANE_GROUPS)], axis=-1).astype(bf16)
    ys = jnp.dot(xs, perm_ref[...], preferred_element_type=f32)
    @pl.when(i == 0)
    def _():
        st_ref[...] = jnp.zeros_like(st_ref)
    for s in range(n_slab):
        for half in range(n_half):
            for b in range(BATCH):
                r0 = ((s * n_half + half) * BATCH + b) * S5_TILE_BLOCKS
                for gg in range(S5_LANE_GROUPS):
                    st_ref[s * S5_LANE_GROUPS + gg, half, pl.ds(b, S5_TILE_BLOCKS, stride=S5_BPAD), :] = (
                        ys[r0:r0 + S5_TILE_BLOCKS, gg * LANES:(gg + 1) * LANES])
    for g in range(S5_G):
        ub_ref[g] = jnp.concatenate([st_ref[g, 0], st_ref[g, 1]], axis=-1).astype(ub_ref.dtype)


def _even_u_proj(x, ctx, gains, mods, w_in, perm):
    blk_rows = S5_TILE_BLOCKS * S5_BPAD
    tile = lambda width: pl.BlockSpec((BATCH, TT, width), lambda i: (0, i, 0))
    return pl.pallas_call(
        _even_u_kernel,
        grid=(N_TILES,),
        in_specs=[pl.BlockSpec((BATCH, TT, D_MODEL), lambda i: (0, jnp.maximum(i - CTX_TILES, 0), 0)),
                  pl.BlockSpec((BATCH, TT, D_MODEL), lambda i: (0, jnp.minimum(i, CTX_TILES - 1), 0)),
                  _gain_spec(0), _mod_spec(0, CTX_TILES), _mod_spec(1, CTX_TILES),
                  pl.BlockSpec((D_MODEL, S5_WIDTH), lambda i: (0, 0), pipeline_mode=pl.Buffered(1)),
                  _const_spec(perm.shape)],
        out_specs=[tile(D_MODEL), tile(S5_WIDTH), tile(D_MODEL),
                   pl.BlockSpec((S5_G, blk_rows, S5_TP), lambda i: (0, i, 0))],
        out_shape=[jax.ShapeDtypeStruct((BATCH, L_ALL, D_MODEL), bf16),
                   jax.ShapeDtypeStruct((BATCH, L_ALL, S5_WIDTH), bf16),
                   jax.ShapeDtypeStruct((BATCH, L_ALL, D_MODEL), f32),
                   jax.ShapeDtypeStruct((S5_G, S5_NC * S5_BPAD, S5_TP), bf16)],
        scratch_shapes=[pltpu.VMEM((S5_WIDTH // LANES, BATCH * TT, LANES), f32),
                        pltpu.VMEM((S5_LANE_GROUPS, BATCH * TT, LANES), f32),
                        pltpu.VMEM((S5_G, S5_TP // LANES, blk_rows, LANES), f32)],
        compiler_params=_cparams("arbitrary"),
        name="even_u_proj",
    )(x, ctx, gains, mods, mods, w_in, perm)


EVEN_REST_TILES = (EVEN_IN - S5_WIDTH) // S5_WIDTH


def _cast_rider(ws, layer, n_blocks, block_of):
    in_specs, out_specs, out_shapes = [], [], []
    for w in ws:
        _, rows, cols = w.shape
        in_specs.append(pl.BlockSpec((None, rows // n_blocks, cols), lambda *g: (layer, block_of(*g), 0)))
        out_specs.append(pl.BlockSpec((rows // n_blocks, cols), lambda *g: (block_of(*g), 0)))
        out_shapes.append(jax.ShapeDtypeStruct((rows, cols), bf16))
    return in_specs, out_specs, out_shapes


def _even_rest_kernel(h_ref, *refs):
    w_refs, refs = refs[:EVEN_REST_TILES], refs[EVEN_REST_TILES:]
    n_riders = len(refs) // 2
    rider_in, o_ref, rider_out = refs[:n_riders], refs[n_riders], refs[n_riders + 1:]
    h = h_ref[...].reshape(BATCH * TT, D_MODEL)
    for jt, w_ref in enumerate(w_refs):
        acc = jnp.dot(h, w_ref[...], preferred_element_type=f32)
        o_ref[:, :, jt * S5_WIDTH:(jt + 1) * S5_WIDTH] = acc.reshape(BATCH, TT, S5_WIDTH).astype(o_ref.dtype)
    for src_ref, dst_ref in zip(rider_in, rider_out):
        dst_ref[...] = src_ref[...].astype(dst_ref.dtype)


def _even_rest_proj(h, w_in, layer0_weights, shared_weights):
    width = EVEN_IN - S5_WIDTH
    w_specs = [pl.BlockSpec((D_MODEL, S5_WIDTH), lambda i, jt=jt: (0, 1 + jt), pipeline_mode=pl.Buffered(1))
               for jt in range(EVEN_REST_TILES)]
    n_blocks = 32
    riders = tuple(layer0_weights) + tuple(shared_weights)
    c_in, c_out, c_shapes = _cast_rider(riders, 0, n_blocks, lambda i: jnp.minimum(i, n_blocks - 1))
    outs = pl.pallas_call(
        _even_rest_kernel,
        grid=(N_TILES,),
        in_specs=[pl.BlockSpec((BATCH, TT, D_MODEL), lambda i: (0, i, 0))] + w_specs + c_in,
        out_specs=[pl.BlockSpec((BATCH, TT, width), lambda i: (0, i, 0))] + c_out,
        out_shape=[jax.ShapeDtypeStruct((BATCH, L_ALL, width), bf16)] + c_shapes,
        compiler_params=_cparams("arbitrary"),
        name="even_rest_proj",
    )(h, *([w_in] * EVEN_REST_TILES), *riders)
    return outs[0], outs[1:]


def _s5_tables(lam_re, lam_im, log_dt, b_re, b_im, c_re, c_im):
    lr = jnp.minimum(lam_re.astype(f32), LAMBDA_RE_MAX)
    li = lam_im.astype(f32)
    dt = jnp.exp(log_dt.astype(f32))[..., None]
    zr, zi = lr * dt, li * dt
    ab_mag = jnp.exp(zr)
    ab_re, ab_im = ab_mag * jnp.cos(zi), ab_mag * jnp.sin(zi)
    den = lr * lr + li * li
    nr = ab_re - 1.0
    f_re = (nr * lr + ab_im * li) / den
    f_im = (ab_im * lr - nr * li) / den
    br, bi = b_re.astype(f32), b_im.astype(f32)
    bb_re = f_re[..., None] * br - f_im[..., None] * bi
    bb_im = f_re[..., None] * bi + f_im[..., None] * br
    cr, ci = c_re.astype(f32), c_im.astype(f32)
    bt_re, bt_im = bb_re.transpose(0, 1, 3, 2), bb_im.transpose(0, 1, 3, 2)
    asc = np.arange(S5_T, dtype=np.float32)
    desc = asc[::-1].copy()

    def powers(d, ks):
        k = jnp.asarray(ks, dtype=f32)[:, None]
        mag = jnp.exp(zr[d][:, None, :] * k)
        return mag * jnp.cos(zi[d][:, None, :] * k), mag * jnp.sin(zi[d][:, None, :] * k)

    def both(fwd, bwd):
        return jnp.concatenate([fwd, bwd], axis=-1)

    def c_times_powers(d, ks):
        pr, pi = powers(d, ks)
        ca_re = cr[d][:, None] * pr[:, :, None, :] - ci[d][:, None] * pi[:, :, None, :]
        ca_im = cr[d][:, None] * pi[:, :, None, :] + ci[d][:, None] * pr[:, :, None, :]
        return both(ca_re, ca_im).reshape(S5_G, S5_TP, 2 * S5_N)

    bcat = both(bt_re, -bt_im)
    w_intra = _s5_intra_matrix(bcat[0], bcat[1], c_times_powers(0, asc), c_times_powers(1, desc))

    (pf_re, pf_im), (pb_re, pb_im) = powers(0, desc), powers(1, asc)
    p_re, p_im = both(pf_re, pb_re)[:, :, None, :], both(pf_im, pb_im)[:, :, None, :]
    b_re2, b_im2 = both(bt_re[0], bt_re[1])[:, None], both(bt_im[0], bt_im[1])[:, None]
    w_state = jnp.stack([(p_re * b_re2 - p_im * b_im2).reshape(S5_G, S5_TP, 2 * S5_N),
                         (p_re * b_im2 + p_im * b_re2).reshape(S5_G, S5_TP, 2 * S5_N)])

    def rows_fb(fwd, bwd):
        return jnp.concatenate([fwd.transpose(0, 2, 1), bwd.transpose(0, 2, 1)], axis=1)

    (qf_re, qf_im), (qb_re, qb_im) = powers(0, asc + 1.0), powers(1, desc + 1.0)
    q_re, q_im = rows_fb(qf_re, qb_re)[..., None], rows_fb(qf_im, qb_im)[..., None]
    c_re2, c_im2 = rows_fb(cr[0], cr[1])[:, :, None, :], rows_fb(ci[0], ci[1])[:, :, None, :]
    w_cross = jnp.stack([(c_re2 * q_re - c_im2 * q_im).reshape(S5_G, 2 * S5_N, S5_TP),
                         -(c_re2 * q_im + c_im2 * q_re).reshape(S5_G, 2 * S5_N, S5_TP)])

    (af_re, af_im), (ab_re_t, ab_im_t) = powers(0, [float(S5_T)]), powers(1, [float(S5_T)])
    a_re, a_im = both(af_re, ab_re_t), both(af_im, ab_im_t)
    return w_intra, w_state.astype(bf16), w_cross.astype(bf16), a_re, a_im


def _s5_intra_kernel(bf_ref, bb_ref, caf_ref, cab_ref, w_ref):
    lane = lax.broadcasted_iota(jnp.int32, (S5_P, S5_TP), 1)
    dn = (((1,), (1,)), ((), ()))
    for g in range(S5_GB_TABLES):
        tq_f = lax.dot_general(bf_ref[g], caf_ref[g], dn, precision=lax.Precision.HIGHEST, preferred_element_type=f32)
        tq_b = lax.dot_general(bb_ref[g], cab_ref[g], dn, precision=lax.Precision.HIGHEST, preferred_element_type=f32)
        for s in range(S5_T):
            sf, sb = S5_P * s, S5_P * (S5_T - 1 - s)
            row_f = tq_f if s == 0 else jnp.where(lane >= sf, pltpu.roll(tq_f, sf, 1), 0.0)
            row_b = tq_b if sb == 0 else jnp.where(lane < S5_TP - sb, pltpu.roll(tq_b, S5_TP - sb, 1), 0.0)
            w_ref[g, s * S5_P:(s + 1) * S5_P, :] = (row_f + row_b).astype(w_ref.dtype)


def _s5_intra_matrix(b_f, b_b, ca_f, ca_b):
    def spec(rows, cols):
        return pl.BlockSpec((S5_GB_TABLES, rows, cols), lambda i: (i, 0, 0))

    return pl.pallas_call(
        _s5_intra_kernel,
        grid=(S5_G // S5_GB_TABLES,),
        in_specs=[spec(S5_P, 2 * S5_N), spec(S5_P, 2 * S5_N), spec(S5_TP, 2 * S5_N), spec(S5_TP, 2 * S5_N)],
        out_specs=spec(S5_TP, S5_TP),
        out_shape=jax.ShapeDtypeStruct((S5_G, S5_TP, S5_TP), bf16),
        compiler_params=_cparams("parallel"),
        name="s5_intra_matrix",
    )(b_f, b_b, ca_f, ca_b)


def _s5_kernel(u_ref, wi_ref, ws_ref, wc_ref, are_ref, aim_ref, y_ref, s_ref):
    half = 2 * S5_N
    for g in range(S5_GB):
        w_state = jnp.concatenate([ws_ref[0, g], ws_ref[1, g]], axis=-1)
        s_ref[g] = jnp.dot(u_ref[g], w_state, preferred_element_type=f32)

    fwd_lane = lax.broadcasted_iota(jnp.int32, (S5_BPAD, half), 1) < S5_N
    a_re = [jnp.broadcast_to(are_ref[g], (S5_BPAD, half)) for g in range(S5_GB)]
    a_im = [jnp.broadcast_to(aim_ref[g], (S5_BPAD, half)) for g in range(S5_GB)]

    def step(i, carry):
        cb = jnp.where(i < S5_NC_CTX, S5_NC_CTX - 1 - i, S5_NC + S5_NC_CTX - 1 - i)
        rf = pl.multiple_of(i * S5_BPAD, S5_BPAD)
        rb = pl.multiple_of(cb * S5_BPAD, S5_BPAD)
        out = []
        for g in range(S5_GB):
            h_re, h_im = carry[2 * g], carry[2 * g + 1]
            l_re = jnp.where(fwd_lane, s_ref[g, pl.ds(rf, S5_BPAD), 0:half], s_ref[g, pl.ds(rb, S5_BPAD), 0:half])
            l_im = jnp.where(fwd_lane, s_ref[g, pl.ds(rf, S5_BPAD), half:2 * half],
                             s_ref[g, pl.ds(rb, S5_BPAD), half:2 * half])
            s_ref[g, pl.ds(rf, S5_BPAD), 0:S5_N] = h_re[:, 0:S5_N]
            s_ref[g, pl.ds(rb, S5_BPAD), S5_N:half] = h_re[:, S5_N:half]
            s_ref[g, pl.ds(rf, S5_BPAD), half:half + S5_N] = h_im[:, 0:S5_N]
            s_ref[g, pl.ds(rb, S5_BPAD), half + S5_N:2 * half] = h_im[:, S5_N:half]
            out.append(a_re[g] * h_re - a_im[g] * h_im + l_re)
            out.append(a_re[g] * h_im + a_im[g] * h_re + l_im)
        return tuple(out)

    zero = jnp.zeros((S5_BPAD, half), f32)
    lax.fori_loop(0, S5_NC, step, (zero,) * (2 * S5_GB), unroll=S5_SCAN_UNROLL)

    for g in range(S5_GB):
        y = jnp.dot(u_ref[g], wi_ref[g], preferred_element_type=f32)
        w_cross = jnp.concatenate([wc_ref[0, g], wc_ref[1, g]], axis=0)
        y = y + jnp.dot(s_ref[g].astype(bf16), w_cross, preferred_element_type=f32)
        for hf in range(S5_TP // LANES):
            y_ref[g, hf] = y[:, hf * LANES:(hf + 1) * LANES]


def _s5_mix(u_blocks, tables):
    w_intra, w_state, w_cross, a_re, a_im = tables
    rows = S5_NC * S5_BPAD
    wsp = pl.BlockSpec((S5_GB, S5_TP, S5_TP), lambda g: (g, 0, 0))
    w_state_spec = pl.BlockSpec((2, S5_GB, S5_TP, 2 * S5_N), lambda g: (0, g, 0, 0))
    w_cross_spec = pl.BlockSpec((2, S5_GB, 2 * S5_N, S5_TP), lambda g: (0, g, 0, 0))
    asp = pl.BlockSpec((S5_GB, 1, 2 * S5_N), lambda g: (g, 0, 0))
    return pl.pallas_call(
        _s5_kernel,
        grid=(S5_G // S5_GB,),
        in_specs=[pl.BlockSpec((S5_GB, rows, S5_TP), lambda g: (g, 0, 0)), wsp, w_state_spec, w_cross_spec, asp, asp],
        out_specs=pl.BlockSpec((S5_GB, S5_TP // LANES, rows, LANES), lambda g: (g, 0, 0, 0)),
        out_shape=jax.ShapeDtypeStruct((S5_G, S5_TP // LANES, rows, LANES), f32),
        scratch_shapes=[pltpu.VMEM((S5_GB, rows, S5_TP), f32)],
        compiler_params=_cparams("parallel"),
        name="s5_mix",
    )(u_blocks, w_intra, w_state, w_cross, a_re, a_im)


def _ret_kernel(lg_ref, cd_ref, qf_ref, kf_ref, vf_ref, qb_ref, kb_ref, vb_ref, of_ref, ob_ref,
                r_ref, dec_ref, qdec_ref, kdec_ref):
    t = RET_T

    @pl.when(pl.program_id(1) == 0)
    def _():
        r_ref[...] = jnp.zeros_like(r_ref)
        n = lax.broadcasted_iota(jnp.int32, (t, t), 0)
        m = lax.broadcasted_iota(jnp.int32, (t, t), 1)
        pos = lax.broadcasted_iota(jnp.int32, (t, 1), 0)
        for d in range(2):
            diff = (n - m if d == 0 else m - n).astype(f32)
            p = (pos if d == 0 else t - 1 - pos).astype(f32)
            for h in range(RET_H):
                c = d * RET_H + h
                lg = lg_ref[c]
                dec_ref[c] = jnp.where(diff >= 0, jnp.exp(jnp.maximum(diff, 0.0) * lg), 0.0)
                qdec_ref[c] = jnp.exp((p + 1.0) * lg)
                kdec_ref[c] = jnp.exp((t - 1.0 - p) * lg)

    for d, (q_ref, k_ref, v_ref, o_ref) in enumerate(((qf_ref, kf_ref, vf_ref, of_ref),
                                                        (qb_ref, kb_ref, vb_ref, ob_ref))):
        for h in range(RET_H):
            c = d * RET_H + h
            cols = slice(h * RET_DK, (h + 1) * RET_DK)
            q = q_ref[:, cols]
            ks = k_ref[:, cols] * (RET_DK ** -0.5)
            v = v_ref[:, cols]
            r = r_ref[c]
            s = lax.dot_general(q, ks, (((1,), (1,)), ((), ())), preferred_element_type=f32) * dec_ref[c]
            inner = jnp.dot(s.astype(bf16), v, preferred_element_type=f32)
            cross = jnp.dot(q, r.astype(bf16), preferred_element_type=f32) * qdec_ref[c]
            o_ref[:, cols] = (inner + cross).astype(o_ref.dtype)
            kd = (ks.astype(f32) * kdec_ref[c]).astype(bf16)
            r_ref[c] = r * cd_ref[c] + lax.dot_general(kd, v, (((0,), (0,)), ((), ())), preferred_element_type=f32)


def _retention(p):
    e = RET_DECAY_BASE - (2.0 * np.arange(RET_H, dtype=np.float64)[None, :] + np.arange(2, dtype=np.float64)[:, None])
    log_g64 = np.log1p(-np.exp2(e)).reshape(RET_CHAINS)
    log_g = jnp.asarray(log_g64, dtype=f32)
    chunk_dec = jnp.asarray(np.exp(RET_T * log_g64), dtype=f32)
    width = RET_H * RET_DK
    col0 = 0

    def bwd_chunk(i):
        return jnp.where(i == 0, 0, RET_STEPS - i)

    def fwd(off):
        return pl.BlockSpec((None, RET_T, width), lambda b, i: (b, i, col0 + off))

    def bwd(off):
        return pl.BlockSpec((None, RET_T, width), lambda b, i: (b, bwd_chunk(i), col0 + off))

    smem = pl.BlockSpec(memory_space=pltpu.SMEM)
    out = jax.ShapeDtypeStruct((BATCH, L_ALL, RET_H * RET_DV), bf16)
    return pl.pallas_call(
        _ret_kernel,
        grid=(BATCH, RET_STEPS),
        in_specs=[smem, smem, fwd(0), fwd(1), fwd(2), bwd(0), bwd(1), bwd(2)],
        out_specs=[pl.BlockSpec((None, RET_T, RET_H * RET_DV), lambda b, i: (b, i, 0)),
                   pl.BlockSpec((None, RET_T, RET_H * RET_DV), lambda b, i: (b, bwd_chunk(i), 0))],
        out_shape=[out, out],
        scratch_shapes=[pltpu.VMEM((RET_CHAINS, RET_DK, RET_DV), f32),
                        pltpu.VMEM((RET_CHAINS, RET_T, RET_T), f32),
                        pltpu.VMEM((RET_CHAINS, RET_T, 1), f32),
                        pltpu.VMEM((RET_CHAINS, RET_T, 1), f32)],
        compiler_params=_cparams("parallel", "arbitrary"),
        name="retention",
    )(log_g, chunk_dec, p, p, p, p, p, p)


def _gelu_tanh(x):
    return 0.5 * x * (1.0 + jnp.tanh(math.sqrt(2.0 / math.pi) * (x + 0.044715 * (x * x * x))))


def _even_out_kernel(u_ref, gate_ref, y_ref, of_ref, ob_ref, x_ref, gt_ref, gpost_ref, g2_ref, sh2_ref, sc2_ref,
                     dsk_ref, glub_ref, gluw_ref, wo_ref, perm_ref, o_ref, h_ref, z_ref, xs_ref, yf_ref):
    rows = BATCH * TT
    n_slab, n_half = S5_WIDTH // LANES, S5_TP // LANES
    for s in range(n_slab):
        for half in range(n_half):
            for b in range(BATCH):
                r0 = ((s * n_half + half) * BATCH + b) * S5_TILE_BLOCKS
                for gg in range(S5_LANE_GROUPS):
                    xs_ref[gg, r0:r0 + S5_TILE_BLOCKS, :] = y_ref[
                        s * S5_LANE_GROUPS + gg, half, pl.ds(b, S5_TILE_BLOCKS, stride=S5_BPAD), :]
    xs = jnp.concatenate([xs_ref[gg] for gg in range(S5_LANE_GROUPS)], axis=-1).astype(bf16)
    ys = jnp.dot(xs, perm_ref[...], preferred_element_type=f32)
    for s in range(n_slab):
        for half in range(n_half):
            for b in range(BATCH):
                r0 = ((s * n_half + half) * BATCH + b) * S5_TILE_BLOCKS
                for tt in range(S5_LANE_GROUPS):
                    z_ref[s, pl.ds(b * TT + half * S5_LANE_GROUPS + tt, S5_TILE_BLOCKS, stride=S5_T), :] = (
                        ys[r0:r0 + S5_TILE_BLOCKS, tt * LANES:(tt + 1) * LANES])
    u = u_ref[...].reshape(rows, S5_WIDTH)
    for s in range(S5_WIDTH // LANES):
        cols = slice(s * LANES, (s + 1) * LANES)
        yf_ref[:, cols] = _gelu_tanh(z_ref[s] + dsk_ref[:, cols] * u[:, cols].astype(f32))
    y = yf_ref[...]
    z = jnp.dot(y.astype(bf16), gluw_ref[...], preferred_element_type=f32) + glub_ref[...]
    s5_out = y * jax.nn.sigmoid(z)
    r = (of_ref[...].astype(f32) + ob_ref[...].astype(f32)).reshape(rows, RET_H * RET_DV)
    heads = []
    for h in range(RET_H):
        rh = r[:, h * RET_DV:(h + 1) * RET_DV]
        heads.append(rh * lax.rsqrt(jnp.mean(rh * rh, axis=-1, keepdims=True) + NORM_EPS))
    g = gate_ref[...].reshape(rows, RET_H * RET_DV).astype(f32)
    ret_out = jnp.concatenate(heads, axis=-1) * (g * jax.nn.sigmoid(g))
    out = jnp.dot(s5_out.astype(bf16), wo_ref[0:S5_WIDTH, :], preferred_element_type=f32)
    out = out + jnp.dot(ret_out.astype(bf16), wo_ref[S5_WIDTH:, :], preferred_element_type=f32)
    _mix_epilogue(out.reshape(BATCH, TT, D_MODEL), x_ref, gt_ref, gpost_ref, g2_ref, sh2_ref, sc2_ref, o_ref, h_ref)


def _even_out_proj(u, p, y_blocks, o_f, o_b, rows, mods, gains, d_skip, glu_b, glu_w, w_out, perm):
    half = S5_WIDTH
    blk_rows = S5_TILE_BLOCKS * S5_BPAD
    tile = pl.BlockSpec((BATCH, TT, half), lambda i: (0, i, 0))
    return pl.pallas_call(
        _even_out_kernel,
        grid=(N_TILES,),
        in_specs=[tile,
                  pl.BlockSpec((BATCH, TT, half), lambda i: (0, i, EVEN_REST_TILES - 1)),
                  pl.BlockSpec((S5_G, S5_TP // LANES, blk_rows, LANES), lambda i: (0, 0, i, 0)),
                  tile, tile,
                  pl.BlockSpec((BATCH, TT, D_MODEL), lambda i: (0, i, 0)),
                  _mod_spec(2, CTX_TILES), _gain_spec(1),
                  _gain_spec(2), _mod_spec(3, CTX_TILES), _mod_spec(4, CTX_TILES),
                  _const_spec((1, half)), _const_spec((1, half)),
                  _const_spec((half, half)), _const_spec((D_MODEL, D_MODEL)), _const_spec(perm.shape)],
        out_specs=[pl.BlockSpec((BATCH, TT, D_MODEL), lambda i: (0, i, 0)),
                   pl.BlockSpec((BATCH, TT, D_MODEL), lambda i: (0, i, 0))],
        out_shape=[jax.ShapeDtypeStruct((BATCH, L_ALL, D_MODEL), f32),
                   jax.ShapeDtypeStruct((BATCH, L_ALL, D_MODEL), bf16)],
        scratch_shapes=[pltpu.VMEM((S5_WIDTH // LANES, BATCH * TT, LANES), f32),
                        pltpu.VMEM((S5_LANE_GROUPS, BATCH * TT, LANES), f32),
                        pltpu.VMEM((BATCH * TT, S5_WIDTH), f32)],
        compiler_params=_cparams("parallel"),
        name="even_out_proj",
    )(u, p, y_blocks, o_f, o_b, rows, mods, gains, gains, mods, mods, d_skip, glu_b, glu_w, w_out, perm)


def _mlp_kernel(x_ref, h_ref, gt_ref, gpost_ref, w1_ref, w2_ref, o_ref, acc_ref):
    f = pl.program_id(1)
    last = pl.num_programs(1) - 1

    def partial_out():
        h = h_ref[...].reshape(BATCH * TT, D_MODEL)
        a = jnp.maximum(jnp.dot(h, w1_ref[...], preferred_element_type=f32), 0.0)
        return jnp.dot((a * a).astype(bf16), w2_ref[...], preferred_element_type=f32)

    @pl.when(f == 0)
    def _():
        acc_ref[...] = partial_out()

    @pl.when(jnp.logical_and(f > 0, f < last))
    def _():
        acc_ref[...] += partial_out()

    @pl.when(f == last)
    def _():
        m = (acc_ref[...] + partial_out()).reshape(BATCH, TT, D_MODEL)
        o_ref[...] = _gated_residual(m, x_ref[...], gt_ref[...], gpost_ref[...])


def _mlp(x, h, ctx_tiles, mods, gains, w1, w2):
    tf = 1024
    n_tiles = x.shape[1] // TT
    tile = pl.BlockSpec((BATCH, TT, D_MODEL), lambda i, f: (0, i, 0))
    return pl.pallas_call(
        _mlp_kernel,
        grid=(n_tiles, FFN_HIDDEN // tf),
        in_specs=[tile, tile, _mod_spec(5, ctx_tiles), _gain_spec(3),
                  pl.BlockSpec((D_MODEL, tf), lambda i, f: (0, f)),
                  pl.BlockSpec((tf, D_MODEL), lambda i, f: (f, 0))],
        out_specs=tile,
        out_shape=jax.ShapeDtypeStruct((BATCH, n_tiles * TT, D_MODEL), f32),
        scratch_shapes=[pltpu.VMEM((BATCH * TT, D_MODEL), f32)],
        compiler_params=_cparams("parallel", "arbitrary"),
        name="sq_relu_mlp",
    )(x, h, mods, gains, w1, w2)


def _rope_tables():
    rows = SEQ // GRID_W
    row = np.repeat(np.arange(rows, dtype=np.float64), GRID_W)
    col = np.tile(np.arange(GRID_W, dtype=np.float64), rows)
    n_freq = ATT_HD // 4
    inv_freq = ROPE_BASE ** (-np.arange(n_freq, dtype=np.float64) / n_freq)
    ang = np.concatenate([row[:, None] * inv_freq[None], col[:, None] * inv_freq[None]], axis=-1)
    cos, sin = np.cos(ang), np.sin(ang)
    cos, sin = np.tile(cos, (1, 4)), np.concatenate([-sin, sin, -sin, sin], axis=-1)
    return (jnp.asarray(np.stack([cos * Q_SCALE, cos, np.ones_like(cos)]), dtype=f32),
            jnp.asarray(np.stack([sin * Q_SCALE, sin, np.zeros_like(sin)]), dtype=f32))


ROPE_Q, ROPE_K, ROPE_NONE = 0, 1, 2


def _rope(x, cos, sin):
    half = ATT_HD // 2
    lane = lax.broadcasted_iota(jnp.int32, x.shape, x.ndim - 1)
    partner = jnp.where((lane & (ATT_HD - 1)) < half,
                        pltpu.roll(x, LANES - half, x.ndim - 1), pltpu.roll(x, half, x.ndim - 1))
    return x * cos + partner * sin


QKV_TILE = 2 * ATT_KVH * ATT_HD
QKV_TILES = ODD_IN // QKV_TILE


def _odd_in_kernel(x_ref, g_ref, sh_ref, sc_ref, w_ref, cos_q_ref, sin_q_ref, cos_k_ref, sin_k_ref, o_ref, h_ref):
    _norm_mod_rows(x_ref, g_ref, sc_ref, sh_ref, h_ref)
    kw = ATT_KVH * ATT_HD
    for jt in range(QKV_TILES):
        acc = jnp.dot(h_ref[...], w_ref[:, jt * QKV_TILE:(jt + 1) * QKV_TILE], preferred_element_type=f32)
        for c in range(QKV_TILE // LANES):
            cols = slice(c * LANES, (c + 1) * LANES)
            xc = acc[:, cols].reshape(BATCH, TT, LANES)
            if jt < QKV_TILES - 1:
                xc = _rope(xc, cos_q_ref[...], sin_q_ref[...])
            elif c * LANES < kw:
                xc = _rope(xc, cos_k_ref[...], sin_k_ref[...])
            o_ref[jt, :, :, cols] = xc.astype(o_ref.dtype)


def _odd_in_proj(x, gains, mods, w_in, cos, sin):
    def table(kind_of):
        return pl.BlockSpec((None, TT, LANES), lambda i: (kind_of(i), jnp.maximum(i - CTX_TILES, 0), 0))

    q_table = table(lambda i: ROPE_Q)
    k_table = table(lambda i: jnp.where(i >= CTX_TILES, ROPE_K, ROPE_NONE))
    return pl.pallas_call(
        _odd_in_kernel,
        grid=(N_TILES,),
        in_specs=[pl.BlockSpec((BATCH, TT, D_MODEL), lambda i: (0, i, 0)),
                  _gain_spec(0), _mod_spec(0, CTX_TILES), _mod_spec(1, CTX_TILES),
                  _const_spec((D_MODEL, ODD_IN)), q_table, q_table, k_table, k_table],
        out_specs=pl.BlockSpec((QKV_TILES, BATCH, TT, QKV_TILE), lambda i: (0, 0, i, 0)),
        out_shape=jax.ShapeDtypeStruct((QKV_TILES, BATCH, L_ALL, QKV_TILE), bf16),
        scratch_shapes=[pltpu.VMEM((BATCH * TT, D_MODEL), bf16)],
        compiler_params=_cparams("parallel"),
        name="odd_in_proj",
    )(x, gains, mods, mods, w_in, cos, sin, cos, sin)


def _attn_kernel(sink_ref, q_ref, kvp_ref, kvc_ref, kvn_ref, kvx_ref, w1f_ref, w2f_ref, o_ref, w1b_ref, w2b_ref):
    w1b_ref[...] = w1f_ref[...].astype(w1b_ref.dtype)
    w2b_ref[...] = w2f_ref[...].astype(w2b_ref.dtype)
    qb = pl.program_id(1)
    t = ATT_BLOCK
    kw = ATT_KVH * ATT_HD
    row = lax.broadcasted_iota(jnp.int32, (t, t), 0)
    col = lax.broadcasted_iota(jnp.int32, (t, t), 1)

    def band_valid(blk):
        off = col + blk * t
        kpos = qb * t - WINDOW + off
        return (jnp.abs(off - WINDOW - row) <= WINDOW) & (kpos >= 0) & (kpos < SEQ)

    n_ctx_chunks = CTX_LEN // LANES
    masks = {n_ctx_chunks: band_valid(0), n_ctx_chunks + 2: band_valid(2)}
    n_chunks = n_ctx_chunks + 3

    for kh in range(ATT_KVH):
        ks = slice(kh * ATT_HD, (kh + 1) * ATT_HD)
        vs = slice(kw + kh * ATT_HD, kw + (kh + 1) * ATT_HD)
        k_all = jnp.concatenate([kvx_ref[:, ks], kvp_ref[:, ks], kvc_ref[:, ks], kvn_ref[:, ks]], axis=0)
        v_all = jnp.concatenate([kvx_ref[:, vs], kvp_ref[:, vs], kvc_ref[:, vs], kvn_ref[:, vs]], axis=0)
        heads = [kh * ATT_GRP + g for g in range(ATT_GRP)]
        per_tile = QKV_TILE // ATT_HD
        q_all = jnp.concatenate([q_ref[h // per_tile, :, (h % per_tile) * ATT_HD:(h % per_tile + 1) * ATT_HD]
                                 for h in heads], axis=0)
        s_all = lax.dot_general(q_all, k_all, (((1,), (1,)), ((), ())), preferred_element_type=f32)
        probs, inv_den = [], []
        for g in range(ATT_GRP):
            sink = sink_ref[kh * ATT_GRP + g] * LOG2_E
            s = s_all[g * t:(g + 1) * t]
            s = jnp.concatenate([jnp.where(masks[c], s[:, c * LANES:(c + 1) * LANES], NEG_INF) if c in masks
                                 else s[:, c * LANES:(c + 1) * LANES] for c in range(n_chunks)], axis=-1)
            m = jnp.maximum(jnp.max(s, axis=-1, keepdims=True), sink)
            e = jnp.exp2(s - m)
            inv_den.append(1.0 / (jnp.sum(e, axis=-1, keepdims=True) + jnp.exp2(sink - m)))
            probs.append(e.astype(bf16))
        o_all = jnp.dot(jnp.concatenate(probs, axis=0), v_all, preferred_element_type=f32)
        for g in range(ATT_GRP):
            h = kh * ATT_GRP + g
            o_ref[:, h * ATT_HD:(h + 1) * ATT_HD] = (o_all[g * t:(g + 1) * t] * inv_den[g]).astype(o_ref.dtype)


def _attention(qkv, sink, mlp_w1, mlp_w2):
    t = ATT_BLOCK
    nb = SEQ // t
    off = CTX_LEN // t
    qw = ATT_H * ATT_HD
    kvw = QKV_TILE
    kv_tile = QKV_TILES - 1

    def kv_spec(rows, row_block):
        return pl.BlockSpec((None, None, rows, kvw), lambda b, i: (kv_tile, b, row_block(i), 0))

    c_in, c_out, c_shapes = _cast_rider((mlp_w1, mlp_w2), DEPTH - 1, BATCH * nb, lambda b, i: b * nb + i)
    return pl.pallas_call(
        _attn_kernel,
        grid=(BATCH, nb),
        in_specs=[pl.BlockSpec(memory_space=pltpu.SMEM),
                  pl.BlockSpec((kv_tile, None, t, QKV_TILE), lambda b, i: (0, b, off + i, 0)),
                  kv_spec(t, lambda i: off + jnp.maximum(i - 1, 0)),
                  kv_spec(t, lambda i: off + i),
                  kv_spec(t, lambda i: off + jnp.minimum(i + 1, nb - 1)),
                  kv_spec(CTX_LEN, lambda i: 0)] + c_in,
        out_specs=[pl.BlockSpec((None, t, qw), lambda b, i: (b, i, 0))] + c_out,
        out_shape=[jax.ShapeDtypeStruct((BATCH, SEQ, qw), bf16)] + c_shapes,
        compiler_params=_cparams("parallel", "parallel"),
        name="window_attention",
    )(sink, qkv, qkv, qkv, qkv, qkv, mlp_w1, mlp_w2)


def _odd_out_kernel(a_ref, x_ref, gt_ref, gpost_ref, g2_ref, sh2_ref, sc2_ref, wo_ref, o_ref, h_ref):
    a = a_ref[...].reshape(BATCH * TT, D_MODEL)
    out = jnp.dot(a, wo_ref[...], preferred_element_type=f32).reshape(BATCH, TT, D_MODEL)
    _mix_epilogue(out, x_ref, gt_ref, gpost_ref, g2_ref, sh2_ref, sc2_ref, o_ref, h_ref)


def _odd_out_proj(a, rows, mods, gains, w_out):
    tile = pl.BlockSpec((BATCH, TT, D_MODEL), lambda i: (0, i, 0))
    return pl.pallas_call(
        _odd_out_kernel,
        grid=(SEQ // TT,),
        in_specs=[tile,
                  pl.BlockSpec((BATCH, TT, D_MODEL), lambda i: (0, i + CTX_TILES, 0)),
                  _mod_spec(2, 0), _gain_spec(1), _gain_spec(2), _mod_spec(3, 0), _mod_spec(4, 0),
                  _const_spec((D_MODEL, D_MODEL))],
        out_specs=[tile, tile],
        out_shape=[jax.ShapeDtypeStruct((BATCH, SEQ, D_MODEL), f32),
                   jax.ShapeDtypeStruct((BATCH, SEQ, D_MODEL), bf16)],
        compiler_params=_cparams("parallel"),
        name="odd_out_proj",
    )(a, rows, mods, gains, gains, mods, mods, w_out)


def _layer_mods(m):
    m = m.reshape(MOD_ROWS, MOD_CHUNKS, D_MODEL)
    lat = m[:BATCH]
    ctx = jnp.broadcast_to(m[CTX_MOD_ROW:CTX_MOD_ROW + 1], lat.shape)
    return jnp.stack([lat, ctx], axis=0).transpose(2, 0, 1, 3)[:, :, :, None, :]


def kernel(x, c, ctx, c_ctx, mod_w, mod_b, norm_g, mlp_w1, mlp_w2, even_w_in, even_w_out, s5_lam_re, s5_lam_im, s5_log_dt, s5_b_re, s5_b_im, s5_c_re, s5_c_im, s5_d, s5_glu_w, s5_glu_b, odd_w_in, odd_w_out, odd_sink):
    c_rows = jnp.concatenate([c, c_ctx[None, :], jnp.zeros((MOD_ROWS - BATCH - 1, D_MODEL), f32)], axis=0)
    mods_all = _modulation(c_rows, mod_w, mod_b)
    gains_all = norm_g.reshape(DEPTH, 4, 1, D_MODEL)

    mods, gains = _layer_mods(mods_all[0]), gains_all[0]
    perm = _chunk_swap_matrix()
    w_in = _to_bf16(even_w_in)[0]
    h, u, rows, u_blocks = _even_u_proj(x, ctx, gains, mods, w_in, perm)
    p, (w1, w2, glu_w, even_out_w, odd_in_w, odd_out_w) = _even_rest_proj(
        h, w_in, (mlp_w1, mlp_w2), (s5_glu_w, even_w_out, odd_w_in, odd_w_out))
    tables = _s5_tables(s5_lam_re[0], s5_lam_im[0], s5_log_dt[0], s5_b_re[0], s5_b_im[0], s5_c_re[0], s5_c_im[0])
    y_blocks = _s5_mix(u_blocks, tables)
    o_f, o_b = _retention(p)
    rows, h = _even_out_proj(u, p, y_blocks, o_f, o_b, rows, mods, gains, s5_d[0].reshape(1, S5_WIDTH),
                             s5_glu_b[0].reshape(1, S5_WIDTH), glu_w, even_out_w, perm)
    rows = _mlp(rows, h, CTX_TILES, mods, gains, w1, w2)

    mods, gains = _layer_mods(mods_all[1]), gains_all[1]
    cos, sin = _rope_tables()
    qkv = _odd_in_proj(rows, gains, mods, odd_in_w, cos, sin)
    a, w1, w2 = _attention(qkv, odd_sink[0].astype(f32), mlp_w1, mlp_w2)
    lat, h = _odd_out_proj(a, rows, mods, gains, odd_out_w)
    return _mlp(lat, h, 0, mods, gains, w1, w2)
```

```python
import math

import jax
import jax.numpy as jnp
import numpy as np
from jax import lax
from jax.experimental import pallas as pl
from jax.experimental.pallas import tpu as pltpu

D_MODEL = 2048
BATCH = 4
SEQ = 4096
DEPTH = 2
GRID_W = 64
CTX_LEN = 256
MOD_CHUNKS = 6
FFN_HIDDEN = 4 * D_MODEL
NORM_EPS = 1e-6

S5_WIDTH = D_MODEL // 2
S5_P = 16
S5_G = S5_WIDTH // S5_P
S5_N = 64
LAMBDA_RE_MAX = -1e-4
RET_DK = 256
RET_H = (D_MODEL // 2) // RET_DK
RET_DV = (D_MODEL // 2) // RET_H
RET_DECAY_BASE = -5.0
EVEN_IN = S5_WIDTH + 2 * RET_H * RET_DK + 2 * RET_H * RET_DV

ATT_HD = 64
ATT_H = D_MODEL // ATT_HD
ATT_KVH = ATT_H // 8
ATT_GRP = ATT_H // ATT_KVH
ODD_IN = (ATT_H + 2 * ATT_KVH) * ATT_HD
WINDOW = 128
ATT_BLOCK = 128
ROPE_BASE = 10000.0
NEG_INF = -1e30
LOG2_E = math.log2(math.e)
Q_SCALE = ATT_HD ** -0.5 * LOG2_E

LANES = 128
SUBLANES = 8

L_ALL = CTX_LEN + SEQ
TT = 128
N_TILES = L_ALL // TT
CTX_TILES = CTX_LEN // TT
NORM_ROWS = 16
NORM_UNROLL = 4
MOD_ROWS = 8
CTX_MOD_ROW = BATCH

S5_T = 16
S5_TP = S5_T * S5_P
S5_NC = L_ALL // S5_T
S5_NC_CTX = CTX_LEN // S5_T
S5_BPAD = SUBLANES
S5_GB = 4
S5_GB_TABLES = 8
S5_SCAN_UNROLL = 8
S5_TILE_BLOCKS = TT // S5_T
S5_LANE_GROUPS = LANES // S5_P
RET_T = 256
RET_STEPS = L_ALL // RET_T
RET_CHAINS = 2 * RET_H

CAST_BLOCK_ELEMS = 2 * 1024 * 1024
VMEM_LIMIT = 56 * 1024 * 1024

f32 = jnp.float32
bf16 = jnp.bfloat16


def _cparams(*sem):
    return pltpu.CompilerParams(dimension_semantics=sem, vmem_limit_bytes=VMEM_LIMIT)


def _mod_spec(chunk, ctx_tiles):
    return pl.BlockSpec((None, None, BATCH, 1, D_MODEL),
                        lambda i, *_: (chunk, jnp.where(i < ctx_tiles, 1, 0), 0, 0, 0))


def _gain_spec(k):
    return pl.BlockSpec((None, 1, D_MODEL), lambda *_: (k, 0, 0))


def _const_spec(shape):
    return pl.BlockSpec(shape, lambda *_: (0,) * len(shape), pipeline_mode=pl.Buffered(1))


def _norm_mod_rows(x_ref, g_ref, sc_ref, sh_ref, h_ref, copy_ref=None):
    for b in range(BATCH):
        gain = g_ref[...] * (1.0 + sc_ref[b])
        shift = sh_ref[b]

        def body(r, carry):
            r0 = pl.multiple_of(r * NORM_ROWS, NORM_ROWS)
            xf = x_ref[b, pl.ds(r0, NORM_ROWS), :]
            if copy_ref is not None:
                copy_ref[b, pl.ds(r0, NORM_ROWS), :] = xf
            ms = jnp.mean(xf * xf, axis=-1, keepdims=True)
            h = (xf * lax.rsqrt(ms + NORM_EPS) * gain + shift).astype(h_ref.dtype)
            if len(h_ref.shape) == 3:
                h_ref[b, pl.ds(r0, NORM_ROWS), :] = h
            else:
                h_ref[pl.ds(pl.multiple_of(b * TT + r0, NORM_ROWS), NORM_ROWS), :] = h
            return carry

        lax.fori_loop(0, TT // NORM_ROWS, body, 0, unroll=NORM_UNROLL)


def _gated_residual(m, x, gate, g_post):
    ms = jnp.mean(m * m, axis=-1, keepdims=True)
    return x + m * lax.rsqrt(ms + NORM_EPS) * (g_post * gate)


def _mix_epilogue(out, x_ref, gt_ref, gpost_ref, g2_ref, sh2_ref, sc2_ref, o_ref, h_ref):
    x1 = _gated_residual(out, x_ref[...], gt_ref[...], gpost_ref[...])
    o_ref[...] = x1
    ms = jnp.mean(x1 * x1, axis=-1, keepdims=True)
    h_ref[...] = (x1 * lax.rsqrt(ms + NORM_EPS) * (g2_ref[...] * (1.0 + sc2_ref[...])) + sh2_ref[...]).astype(h_ref.dtype)


def _gated_residual_rows(load_rows, x_ref, gt_ref, gpost_ref, o_ref):
    for b in range(BATCH):
        gain = gpost_ref[...] * gt_ref[b]

        def body(r, carry):
            r0 = pl.multiple_of(r * NORM_ROWS, NORM_ROWS)
            m = load_rows(b, r0, NORM_ROWS)
            ms = jnp.mean(m * m, axis=-1, keepdims=True)
            o_ref[b, pl.ds(r0, NORM_ROWS), :] = x_ref[b, pl.ds(r0, NORM_ROWS), :] + m * lax.rsqrt(ms + NORM_EPS) * gain
            return carry

        lax.fori_loop(0, TT // NORM_ROWS, body, 0, unroll=NORM_UNROLL)


def _chunk_swap_matrix():
    n = S5_LANE_GROUPS * LANES
    src = np.arange(n)
    i, j, p = src // LANES, (src // S5_P) % S5_LANE_GROUPS, src % S5_P
    return jnp.asarray(np.arange(n)[None, :] == (j * LANES + i * S5_P + p)[:, None], dtype=bf16)


def _cast_kernel(w_ref, o_ref):
    o_ref[...] = w_ref[...].astype(o_ref.dtype)


def _to_bf16(w):
    n_layers, rows, cols = w.shape
    block_rows = min(rows, pl.next_power_of_2(CAST_BLOCK_ELEMS // cols + 1) // 2)
    spec = pl.BlockSpec((None, block_rows, cols), lambda l, r: (l, r, 0))
    return pl.pallas_call(
        _cast_kernel,
        grid=(n_layers, rows // block_rows),
        in_specs=[spec],
        out_specs=spec,
        out_shape=jax.ShapeDtypeStruct(w.shape, bf16),
        compiler_params=_cparams("parallel", "parallel"),
        name="to_bf16",
    )(w)


def _mod_kernel(c_ref, w_ref, b_ref, o_ref):
    c = c_ref[...]
    s = c * jax.nn.sigmoid(c)
    o_ref[...] = jnp.dot(s.astype(bf16), w_ref[...].astype(bf16), preferred_element_type=f32) + b_ref[...]


def _modulation(c_rows, mod_w, mod_b):
    tn = 1024
    n = MOD_CHUNKS * D_MODEL
    return pl.pallas_call(
        _mod_kernel,
        grid=(DEPTH, n // tn),
        in_specs=[pl.BlockSpec((MOD_ROWS, D_MODEL), lambda l, j: (0, 0)),
                  pl.BlockSpec((None, D_MODEL, tn), lambda l, j: (l, 0, j)),
                  pl.BlockSpec((None, 1, tn), lambda l, j: (l, 0, j))],
        out_specs=pl.BlockSpec((None, MOD_ROWS, tn), lambda l, j: (l, 0, j)),
        out_shape=jax.ShapeDtypeStruct((DEPTH, MOD_ROWS, n), f32),
        compiler_params=_cparams("parallel", "parallel"),
        name="ada_modulation",
    )(c_rows, mod_w, mod_b.reshape(DEPTH, 1, n))


def _even_u_kernel(x_ref, c_ref, g_ref, sh_ref, sc_ref, w_ref, perm_ref, h_ref, u_ref, rows_ref, ub_ref,
                   a_ref, xs_ref, st_ref):
    i = pl.program_id(0)

    @pl.when(i < CTX_TILES)
    def _():
        _norm_mod_rows(c_ref, g_ref, sc_ref, sh_ref, h_ref, copy_ref=rows_ref)

    @pl.when(i >= CTX_TILES)
    def _():
        _norm_mod_rows(x_ref, g_ref, sc_ref, sh_ref, h_ref, copy_ref=rows_ref)

    acc = jnp.dot(h_ref[...].reshape(BATCH * TT, D_MODEL), w_ref[...], preferred_element_type=f32)
    u_ref[...] = acc.reshape(BATCH, TT, S5_WIDTH).astype(u_ref.dtype)

    n_slab, n_half = S5_WIDTH // LANES, S5_TP // LANES
    for s in range(n_slab):
        a_ref[s] = acc[:, s * LANES:(s + 1) * LANES]
    for s in range(n_slab):
        for half in range(n_half):
            for b in range(BATCH):
                r0 = ((s * n_half + half) * BATCH + b) * S5_TILE_BLOCKS
                for tt in range(S5_LANE_GROUPS):
                    xs_ref[tt, r0:r0 + S5_TILE_BLOCKS, :] = a_ref[
                        s, pl.ds(b * TT + half * S5_LANE_GROUPS + tt, S5_TILE_BLOCKS, stride=S5_T), :]
    xs = jnp.concatenate([xs_ref[tt] for tt in range(S5_LANE_GROUPS)], axis=-1).astype(bf16)
    ys = jnp.dot(xs, perm_ref[...], preferred_element_type=f32)
    @pl.when(i == 0)
    def _():
        st_ref[...] = jnp.zeros_like(st_ref)
    for s in range(n_slab):
        for half in range(n_half):
            for b in range(BATCH):
                r0 = ((s * n_half + half) * BATCH + b) * S5_TILE_BLOCKS
                for gg in range(S5_LANE_GROUPS):
                    st_ref[s * S5_LANE_GROUPS + gg, half, pl.ds(b, S5_TILE_BLOCKS, stride=S5_BPAD), :] = (
                        ys[r0:r0 + S5_TILE_BLOCKS, gg * LANES:(gg + 1) * LANES])
    for g in range(S5_G):
        ub_ref[g] = jnp.concatenate([st_ref[g, 0], st_ref[g, 1]], axis=-1).astype(ub_ref.dtype)


def _even_u_proj(x, ctx, gains, mods, w_in, perm):
    blk_rows = S5_TILE_BLOCKS * S5_BPAD
    tile = lambda width: pl.BlockSpec((BATCH, TT, width), lambda i: (0, i, 0))
    return pl.pallas_call(
        _even_u_kernel,
        grid=(N_TILES,),
        in_specs=[pl.BlockSpec((BATCH, TT, D_MODEL), lambda i: (0, jnp.maximum(i - CTX_TILES, 0), 0)),
                  pl.BlockSpec((BATCH, TT, D_MODEL), lambda i: (0, jnp.minimum(i, CTX_TILES - 1), 0)),
                  _gain_spec(0), _mod_spec(0, CTX_TILES), _mod_spec(1, CTX_TILES),
                  pl.BlockSpec((D_MODEL, S5_WIDTH), lambda i: (0, 0), pipeline_mode=pl.Buffered(1)),
                  _const_spec(perm.shape)],
        out_specs=[tile(D_MODEL), tile(S5_WIDTH), tile(D_MODEL),
                   pl.BlockSpec((S5_G, blk_rows, S5_TP), lambda i: (0, i, 0))],
        out_shape=[jax.ShapeDtypeStruct((BATCH, L_ALL, D_MODEL), bf16),
                   jax.ShapeDtypeStruct((BATCH, L_ALL, S5_WIDTH), bf16),
                   jax.ShapeDtypeStruct((BATCH, L_ALL, D_MODEL), f32),
                   jax.ShapeDtypeStruct((S5_G, S5_NC * S5_BPAD, S5_TP), bf16)],
        scratch_shapes=[pltpu.VMEM((S5_WIDTH // LANES, BATCH * TT, LANES), f32),
                        pltpu.VMEM((S5_LANE_GROUPS, BATCH * TT, LANES), f32),
                        pltpu.VMEM((S5_G, S5_TP // LANES, blk_rows, LANES), f32)],
        compiler_params=_cparams("arbitrary"),
        name="even_u_proj",
    )(x, ctx, gains, mods, mods, w_in, perm)


EVEN_REST_TILES = (EVEN_IN - S5_WIDTH) // S5_WIDTH


def _cast_rider(ws, layer, n_blocks, block_of):
    in_specs, out_specs, out_shapes = [], [], []
    for w in ws:
        _, rows, cols = w.shape
        in_specs.append(pl.BlockSpec((None, rows // n_blocks, cols), lambda *g: (layer, block_of(*g), 0)))
        out_specs.append(pl.BlockSpec((rows // n_blocks, cols), lambda *g: (block_of(*g), 0)))
        out_shapes.append(jax.ShapeDtypeStruct((rows, cols), bf16))
    return in_specs, out_specs, out_shapes


def _even_rest_kernel(h_ref, *refs):
    w_refs, refs = refs[:EVEN_REST_TILES], refs[EVEN_REST_TILES:]
    n_riders = len(refs) // 2
    rider_in, o_ref, rider_out = refs[:n_riders], refs[n_riders], refs[n_riders + 1:]
    h = h_ref[...].reshape(BATCH * TT, D_MODEL)
    for jt, w_ref in enumerate(w_refs):
        acc = jnp.dot(h, w_ref[...], preferred_element_type=f32)
        o_ref[:, :, jt * S5_WIDTH:(jt + 1) * S5_WIDTH] = acc.reshape(BATCH, TT, S5_WIDTH).astype(o_ref.dtype)
    for src_ref, dst_ref in zip(rider_in, rider_out):
        dst_ref[...] = src_ref[...].astype(dst_ref.dtype)


def _even_rest_proj(h, w_in, layer0_weights, shared_weights):
    width = EVEN_IN - S5_WIDTH
    w_specs = [pl.BlockSpec((D_MODEL, S5_WIDTH), lambda i, jt=jt: (0, 1 + jt), pipeline_mode=pl.Buffered(1))
               for jt in range(EVEN_REST_TILES)]
    n_blocks = 32
    riders = tuple(layer0_weights) + tuple(shared_weights)
    c_in, c_out, c_shapes = _cast_rider(riders, 0, n_blocks, lambda i: jnp.minimum(i, n_blocks - 1))
    outs = pl.pallas_call(
        _even_rest_kernel,
        grid=(N_TILES,),
        in_specs=[pl.BlockSpec((BATCH, TT, D_MODEL), lambda i: (0, i, 0))] + w_specs + c_in,
        out_specs=[pl.BlockSpec((BATCH, TT, width), lambda i: (0, i, 0))] + c_out,
        out_shape=[jax.ShapeDtypeStruct((BATCH, L_ALL, width), bf16)] + c_shapes,
        compiler_params=_cparams("arbitrary"),
        name="even_rest_proj",
    )(h, *([w_in] * EVEN_REST_TILES), *riders)
    return outs[0], outs[1:]


def _s5_tables(lam_re, lam_im, log_dt, b_re, b_im, c_re, c_im):
    lr = jnp.minimum(lam_re.astype(f32), LAMBDA_RE_MAX)
    li = lam_im.astype(f32)
    dt = jnp.exp(log_dt.astype(f32))[..., None]
    zr, zi = lr * dt, li * dt
    ab_mag = jnp.exp(zr)
    ab_re, ab_im = ab_mag * jnp.cos(zi), ab_mag * jnp.sin(zi)
    den = lr * lr + li * li
    nr = ab_re - 1.0
    f_re = (nr * lr + ab_im * li) / den
    f_im = (ab_im * lr - nr * li) / den
    br, bi = b_re.astype(f32), b_im.astype(f32)
    bb_re = f_re[..., None] * br - f_im[..., None] * bi
    bb_im = f_re[..., None] * bi + f_im[..., None] * br
    cr, ci = c_re.astype(f32), c_im.astype(f32)
    bt_re, bt_im = bb_re.transpose(0, 1, 3, 2), bb_im.transpose(0, 1, 3, 2)
    asc = np.arange(S5_T, dtype=np.float32)
    desc = asc[::-1].copy()

    def powers(d, ks):
        k = jnp.asarray(ks, dtype=f32)[:, None]
        mag = jnp.exp(zr[d][:, None, :] * k)
        return mag * jnp.cos(zi[d][:, None, :] * k), mag * jnp.sin(zi[d][:, None, :] * k)

    def both(fwd, bwd):
        return jnp.concatenate([fwd, bwd], axis=-1)

    def c_times_powers(d, ks):
        pr, pi = powers(d, ks)
        ca_re = cr[d][:, None] * pr[:, :, None, :] - ci[d][:, None] * pi[:, :, None, :]
        ca_im = cr[d][:, None] * pi[:, :, None, :] + ci[d][:, None] * pr[:, :, None, :]
        return both(ca_re, ca_im).reshape(S5_G, S5_TP, 2 * S5_N)

    bcat = both(bt_re, -bt_im)
    w_intra = _s5_intra_matrix(bcat[0], bcat[1], c_times_powers(0, asc), c_times_powers(1, desc))

    (pf_re, pf_im), (pb_re, pb_im) = powers(0, desc), powers(1, asc)
    p_re, p_im = both(pf_re, pb_re)[:, :, None, :], both(pf_im, pb_im)[:, :, None, :]
    b_re2, b_im2 = both(bt_re[0], bt_re[1])[:, None], both(bt_im[0], bt_im[1])[:, None]
    w_state = jnp.stack([(p_re * b_re2 - p_im * b_im2).reshape(S5_G, S5_TP, 2 * S5_N),
                         (p_re * b_im2 + p_im * b_re2).reshape(S5_G, S5_TP, 2 * S5_N)])

    def rows_fb(fwd, bwd):
        return jnp.concatenate([fwd.transpose(0, 2, 1), bwd.transpose(0, 2, 1)], axis=1)

    (qf_re, qf_im), (qb_re, qb_im) = powers(0, asc + 1.0), powers(1, desc + 1.0)
    q_re, q_im = rows_fb(qf_re, qb_re)[..., None], rows_fb(qf_im, qb_im)[..., None]
    c_re2, c_im2 = rows_fb(cr[0], cr[1])[:, :, None, :], rows_fb(ci[0], ci[1])[:, :, None, :]
    w_cross = jnp.stack([(c_re2 * q_re - c_im2 * q_im).reshape(S5_G, 2 * S5_N, S5_TP),
                         -(c_re2 * q_im + c_im2 * q_re).reshape(S5_G, 2 * S5_N, S5_TP)])

    (af_re, af_im), (ab_re_t, ab_im_t) = powers(0, [float(S5_T)]), powers(1, [float(S5_T)])
    a_re, a_im = both(af_re, ab_re_t), both(af_im, ab_im_t)
    return w_intra, w_state.astype(bf16), w_cross.astype(bf16), a_re, a_im


def _s5_intra_kernel(bf_ref, bb_ref, caf_ref, cab_ref, w_ref):
    lane = lax.broadcasted_iota(jnp.int32, (S5_P, S5_TP), 1)
    dn = (((1,), (1,)), ((), ()))
    for g in range(S5_GB_TABLES):
        tq_f = lax.dot_general(bf_ref[g], caf_ref[g], dn, precision=lax.Precision.HIGHEST, preferred_element_type=f32)
        tq_b = lax.dot_general(bb_ref[g], cab_ref[g], dn, precision=lax.Precision.HIGHEST, preferred_element_type=f32)
        for s in range(S5_T):
            sf, sb = S5_P * s, S5_P * (S5_T - 1 - s)
            row_f = tq_f if s == 0 else jnp.where(lane >= sf, pltpu.roll(tq_f, sf, 1), 0.0)
            row_b = tq_b if sb == 0 else jnp.where(lane < S5_TP - sb, pltpu.roll(tq_b, S5_TP - sb, 1), 0.0)
            w_ref[g, s * S5_P:(s + 1) * S5_P, :] = (row_f + row_b).astype(w_ref.dtype)


def _s5_intra_matrix(b_f, b_b, ca_f, ca_b):
    def spec(rows, cols):
        return pl.BlockSpec((S5_GB_TABLES, rows, cols), lambda i: (i, 0, 0))

    return pl.pallas_call(
        _s5_intra_kernel,
        grid=(S5_G // S5_GB_TABLES,),
        in_specs=[spec(S5_P, 2 * S5_N), spec(S5_P, 2 * S5_N), spec(S5_TP, 2 * S5_N), spec(S5_TP, 2 * S5_N)],
        out_specs=spec(S5_TP, S5_TP),
        out_shape=jax.ShapeDtypeStruct((S5_G, S5_TP, S5_TP), bf16),
        compiler_params=_cparams("parallel"),
        name="s5_intra_matrix",
    )(b_f, b_b, ca_f, ca_b)


def _s5_kernel(u_ref, wi_ref, ws_ref, wc_ref, are_ref, aim_ref, y_ref, s_ref):
    half = 2 * S5_N
    for g in range(S5_GB):
        w_state = jnp.concatenate([ws_ref[0, g], ws_ref[1, g]], axis=-1)
        s_ref[g] = jnp.dot(u_ref[g], w_state, preferred_element_type=f32)

    fwd_lane = lax.broadcasted_iota(jnp.int32, (S5_BPAD, half), 1) < S5_N
    a_re = [jnp.broadcast_to(are_ref[g], (S5_BPAD, half)) for g in range(S5_GB)]
    a_im = [jnp.broadcast_to(aim_ref[g], (S5_BPAD, half)) for g in range(S5_GB)]

    def step(i, carry):
        cb = jnp.where(i < S5_NC_CTX, S5_NC_CTX - 1 - i, S5_NC + S5_NC_CTX - 1 - i)
        rf = pl.multiple_of(i * S5_BPAD, S5_BPAD)
        rb = pl.multiple_of(cb * S5_BPAD, S5_BPAD)
        out = []
        for g in range(S5_GB):
            h_re, h_im = carry[2 * g], carry[2 * g + 1]
            l_re = jnp.where(fwd_lane, s_ref[g, pl.ds(rf, S5_BPAD), 0:half], s_ref[g, pl.ds(rb, S5_BPAD), 0:half])
            l_im = jnp.where(fwd_lane, s_ref[g, pl.ds(rf, S5_BPAD), half:2 * half],
                             s_ref[g, pl.ds(rb, S5_BPAD), half:2 * half])
            s_ref[g, pl.ds(rf, S5_BPAD), 0:S5_N] = h_re[:, 0:S5_N]
            s_ref[g, pl.ds(rb, S5_BPAD), S5_N:half] = h_re[:, S5_N:half]
            s_ref[g, pl.ds(rf, S5_BPAD), half:half + S5_N] = h_im[:, 0:S5_N]
            s_ref[g, pl.ds(rb, S5_BPAD), half + S5_N:2 * half] = h_im[:, S5_N:half]
            out.append(a_re[g] * h_re - a_im[g] * h_im + l_re)
            out.append(a_re[g] * h_im + a_im[g] * h_re + l_im)
        return tuple(out)

    zero = jnp.zeros((S5_BPAD, half), f32)
    lax.fori_loop(0, S5_NC, step, (zero,) * (2 * S5_GB), unroll=S5_SCAN_UNROLL)

    for g in range(S5_GB):
        y = jnp.dot(u_ref[g], wi_ref[g], preferred_element_type=f32)
        w_cross = jnp.concatenate([wc_ref[0, g], wc_ref[1, g]], axis=0)
        y = y + jnp.dot(s_ref[g].astype(bf16), w_cross, preferred_element_type=f32)
        for hf in range(S5_TP // LANES):
            y_ref[g, hf] = y[:, hf * LANES:(hf + 1) * LANES]


def _s5_mix(u_blocks, tables):
    w_intra, w_state, w_cross, a_re, a_im = tables
    rows = S5_NC * S5_BPAD
    wsp = pl.BlockSpec((S5_GB, S5_TP, S5_TP), lambda g: (g, 0, 0))
    w_state_spec = pl.BlockSpec((2, S5_GB, S5_TP, 2 * S5_N), lambda g: (0, g, 0, 0))
    w_cross_spec = pl.BlockSpec((2, S5_GB, 2 * S5_N, S5_TP), lambda g: (0, g, 0, 0))
    asp = pl.BlockSpec((S5_GB, 1, 2 * S5_N), lambda g: (g, 0, 0))
    return pl.pallas_call(
        _s5_kernel,
        grid=(S5_G // S5_GB,),
        in_specs=[pl.BlockSpec((S5_GB, rows, S5_TP), lambda g: (g, 0, 0)), wsp, w_state_spec, w_cross_spec, asp, asp],
        out_specs=pl.BlockSpec((S5_GB, S5_TP // LANES, rows, LANES), lambda g: (g, 0, 0, 0)),
        out_shape=jax.ShapeDtypeStruct((S5_G, S5_TP // LANES, rows, LANES), f32),
        scratch_shapes=[pltpu.VMEM((S5_GB, rows, S5_TP), f32)],
        compiler_params=_cparams("parallel"),
        name="s5_mix",
    )(u_blocks, w_intra, w_state, w_cross, a_re, a_im)


def _ret_kernel(lg_ref, cd_ref, qf_ref, kf_ref, vf_ref, qb_ref, kb_ref, vb_ref, of_ref, ob_ref,
                r_ref, dec_ref, qdec_ref, kdec_ref):
    t = RET_T

    @pl.when(pl.program_id(1) == 0)
    def _():
        r_ref[...] = jnp.zeros_like(r_ref)
        n = lax.broadcasted_iota(jnp.int32, (t, t), 0)
        m = lax.broadcasted_iota(jnp.int32, (t, t), 1)
        pos = lax.broadcasted_iota(jnp.int32, (t, 1), 0)
        for d in range(2):
            diff = (n - m if d == 0 else m - n).astype(f32)
            p = (pos if d == 0 else t - 1 - pos).astype(f32)
            for h in range(RET_H):
                c = d * RET_H + h
                lg = lg_ref[c]
                dec_ref[c] = jnp.where(diff >= 0, jnp.exp(jnp.maximum(diff, 0.0) * lg), 0.0)
                qdec_ref[c] = jnp.exp((p + 1.0) * lg)
                kdec_ref[c] = jnp.exp((t - 1.0 - p) * lg)

    for d, (q_ref, k_ref, v_ref, o_ref) in enumerate(((qf_ref, kf_ref, vf_ref, of_ref),
                                                        (qb_ref, kb_ref, vb_ref, ob_ref))):
        for h in range(RET_H):
            c = d * RET_H + h
            cols = slice(h * RET_DK, (h + 1) * RET_DK)
            q = q_ref[:, cols]
            ks = k_ref[:, cols] * (RET_DK ** -0.5)
            v = v_ref[:, cols]
            r = r_ref[c]
            s = lax.dot_general(q, ks, (((1,), (1,)), ((), ())), preferred_element_type=f32) * dec_ref[c]
            inner = jnp.dot(s.astype(bf16), v, preferred_element_type=f32)
            cross = jnp.dot(q, r.astype(bf16), preferred_element_type=f32) * qdec_ref[c]
            o_ref[:, cols] = (inner + cross).astype(o_ref.dtype)
            kd = (ks.astype(f32) * kdec_ref[c]).astype(bf16)
            r_ref[c] = r * cd_ref[c] + lax.dot_general(kd, v, (((0,), (0,)), ((), ())), preferred_element_type=f32)


def _retention(p):
    e = RET_DECAY_BASE - (2.0 * np.arange(RET_H, dtype=np.float64)[None, :] + np.arange(2, dtype=np.float64)[:, None])
    log_g64 = np.log1p(-np.exp2(e)).reshape(RET_CHAINS)
    log_g = jnp.asarray(log_g64, dtype=f32)
    chunk_dec = jnp.asarray(np.exp(RET_T * log_g64), dtype=f32)
    width = RET_H * RET_DK
    col0 = 0

    def bwd_chunk(i):
        return jnp.where(i == 0, 0, RET_STEPS - i)

    def fwd(off):
        return pl.BlockSpec((None, RET_T, width), lambda b, i: (b, i, col0 + off))

    def bwd(off):
        return pl.BlockSpec((None, RET_T, width), lambda b, i: (b, bwd_chunk(i), col0 + off))

    smem = pl.BlockSpec(memory_space=pltpu.SMEM)
    out = jax.ShapeDtypeStruct((BATCH, L_ALL, RET_H * RET_DV), bf16)
    return pl.pallas_call(
        _ret_kernel,
        grid=(BATCH, RET_STEPS),
        in_specs=[smem, smem, fwd(0), fwd(1), fwd(2), bwd(0), bwd(1), bwd(2)],
        out_specs=[pl.BlockSpec((None, RET_T, RET_H * RET_DV), lambda b, i: (b, i, 0)),
                   pl.BlockSpec((None, RET_T, RET_H * RET_DV), lambda b, i: (b, bwd_chunk(i), 0))],
        out_shape=[out, out],
        scratch_shapes=[pltpu.VMEM((RET_CHAINS, RET_DK, RET_DV), f32),
                        pltpu.VMEM((RET_CHAINS, RET_T, RET_T), f32),
                        pltpu.VMEM((RET_CHAINS, RET_T, 1), f32),
                        pltpu.VMEM((RET_CHAINS, RET_T, 1), f32)],
        compiler_params=_cparams("parallel", "arbitrary"),
        name="retention",
    )(log_g, chunk_dec, p, p, p, p, p, p)


def _gelu_tanh(x):
    return 0.5 * x * (1.0 + jnp.tanh(math.sqrt(2.0 / math.pi) * (x + 0.044715 * (x * x * x))))


def _even_out_kernel(u_ref, gate_ref, y_ref, of_ref, ob_ref, x_ref, gt_ref, gpost_ref, g2_ref, sh2_ref, sc2_ref,
                     dsk_ref, glub_ref, gluw_ref, wo_ref, perm_ref, o_ref, h_ref, z_ref, xs_ref, yf_ref):
    rows = BATCH * TT
    n_slab, n_half = S5_WIDTH // LANES, S5_TP // LANES
    for s in range(n_slab):
        for half in range(n_half):
            for b in range(BATCH):
                r0 = ((s * n_half + half) * BATCH + b) * S5_TILE_BLOCKS
                for gg in range(S5_LANE_GROUPS):
                    xs_ref[gg, r0:r0 + S5_TILE_BLOCKS, :] = y_ref[
                        s * S5_LANE_GROUPS + gg, half, pl.ds(b, S5_TILE_BLOCKS, stride=S5_BPAD), :]
    xs = jnp.concatenate([xs_ref[gg] for gg in range(S5_LANE_GROUPS)], axis=-1).astype(bf16)
    ys = jnp.dot(xs, perm_ref[...], preferred_element_type=f32)
    for s in range(n_slab):
        for half in range(n_half):
            for b in range(BATCH):
                r0 = ((s * n_half + half) * BATCH + b) * S5_TILE_BLOCKS
                for tt in range(S5_LANE_GROUPS):
                    z_ref[s, pl.ds(b * TT + half * S5_LANE_GROUPS + tt, S5_TILE_BLOCKS, stride=S5_T), :] = (
                        ys[r0:r0 + S5_TILE_BLOCKS, tt * LANES:(tt + 1) * LANES])
    u = u_ref[...].reshape(rows, S5_WIDTH)
    for s in range(S5_WIDTH // LANES):
        cols = slice(s * LANES, (s + 1) * LANES)
        yf_ref[:, cols] = _gelu_tanh(z_ref[s] + dsk_ref[:, cols] * u[:, cols].astype(f32))
    y = yf_ref[...]
    z = jnp.dot(y.astype(bf16), gluw_ref[...], preferred_element_type=f32) + glub_ref[...]
    s5_out = y * jax.nn.sigmoid(z)
    r = (of_ref[...].astype(f32) + ob_ref[...].astype(f32)).reshape(rows, RET_H * RET_DV)
    heads = []
    for h in range(RET_H):
        rh = r[:, h * RET_DV:(h + 1) * RET_DV]
        heads.append(rh * lax.rsqrt(jnp.mean(rh * rh, axis=-1, keepdims=True) + NORM_EPS))
    g = gate_ref[...].reshape(rows, RET_H * RET_DV).astype(f32)
    ret_out = jnp.concatenate(heads, axis=-1) * (g * jax.nn.sigmoid(g))
    out = jnp.dot(s5_out.astype(bf16), wo_ref[0:S5_WIDTH, :], preferred_element_type=f32)
    out = out + jnp.dot(ret_out.astype(bf16), wo_ref[S5_WIDTH:, :], preferred_element_type=f32)
    _mix_epilogue(out.reshape(BATCH, TT, D_MODEL), x_ref, gt_ref, gpost_ref, g2_ref, sh2_ref, sc2_ref, o_ref, h_ref)


def _even_out_proj(u, p, y_blocks, o_f, o_b, rows, mods, gains, d_skip, glu_b, glu_w, w_out, perm):
    half = S5_WIDTH
    blk_rows = S5_TILE_BLOCKS * S5_BPAD
    tile = pl.BlockSpec((BATCH, TT, half), lambda i: (0, i, 0))
    return pl.pallas_call(
        _even_out_kernel,
        grid=(N_TILES,),
        in_specs=[tile,
                  pl.BlockSpec((BATCH, TT, half), lambda i: (0, i, EVEN_REST_TILES - 1)),
                  pl.BlockSpec((S5_G, S5_TP // LANES, blk_rows, LANES), lambda i: (0, 0, i, 0)),
                  tile, tile,
                  pl.BlockSpec((BATCH, TT, D_MODEL), lambda i: (0, i, 0)),
                  _mod_spec(2, CTX_TILES), _gain_spec(1),
                  _gain_spec(2), _mod_spec(3, CTX_TILES), _mod_spec(4, CTX_TILES),
                  _const_spec((1, half)), _const_spec((1, half)),
                  _const_spec((half, half)), _const_spec((D_MODEL, D_MODEL)), _const_spec(perm.shape)],
        out_specs=[pl.BlockSpec((BATCH, TT, D_MODEL), lambda i: (0, i, 0)),
                   pl.BlockSpec((BATCH, TT, D_MODEL), lambda i: (0, i, 0))],
        out_shape=[jax.ShapeDtypeStruct((BATCH, L_ALL, D_MODEL), f32),
                   jax.ShapeDtypeStruct((BATCH, L_ALL, D_MODEL), bf16)],
        scratch_shapes=[pltpu.VMEM((S5_WIDTH // LANES, BATCH * TT, LANES), f32),
                        pltpu.VMEM((S5_LANE_GROUPS, BATCH * TT, LANES), f32),
                        pltpu.VMEM((BATCH * TT, S5_WIDTH), f32)],
        compiler_params=_cparams("parallel"),
        name="even_out_proj",
    )(u, p, y_blocks, o_f, o_b, rows, mods, gains, gains, mods, mods, d_skip, glu_b, glu_w, w_out, perm)


def _mlp_kernel(x_ref, h_ref, gt_ref, gpost_ref, *refs):
    if len(refs) == 4:
        w1_ref, w2_ref, o_ref, acc_ref = refs
        next_norm = None
    else:
        gn_ref, shn_ref, scn_ref, w1_ref, w2_ref, o_ref, hn_ref, acc_ref = refs
        next_norm = (gn_ref, shn_ref, scn_ref, hn_ref)
    f = pl.program_id(1)
    last = pl.num_programs(1) - 1

    def partial_out():
        h = h_ref[...].reshape(BATCH * TT, D_MODEL)
        a = jnp.maximum(jnp.dot(h, w1_ref[...], preferred_element_type=f32), 0.0)
        return jnp.dot((a * a).astype(bf16), w2_ref[...], preferred_element_type=f32)

    @pl.when(f == 0)
    def _():
        acc_ref[...] = partial_out()

    @pl.when(jnp.logical_and(f > 0, f < last))
    def _():
        acc_ref[...] += partial_out()

    @pl.when(f == last)
    def _():
        m = (acc_ref[...] + partial_out()).reshape(BATCH, TT, D_MODEL)
        if next_norm is None:
            o_ref[...] = _gated_residual(m, x_ref[...], gt_ref[...], gpost_ref[...])
        else:
            _mix_epilogue(m, x_ref, gt_ref, gpost_ref, gn_ref, shn_ref, scn_ref, o_ref, hn_ref)


def _mlp(x, h, ctx_tiles, mods, gains, w1, w2, next_layer=None):
    tf = 1024
    n_tiles = x.shape[1] // TT
    tile = pl.BlockSpec((BATCH, TT, D_MODEL), lambda i, f: (0, i, 0))
    rows = jax.ShapeDtypeStruct((BATCH, n_tiles * TT, D_MODEL), f32)
    weights = [pl.BlockSpec((D_MODEL, tf), lambda i, f: (0, f)), pl.BlockSpec((tf, D_MODEL), lambda i, f: (f, 0))]
    if next_layer is None:
        extra_specs, extra_args, out_specs, out_shape = [], (), tile, rows
    else:
        next_mods, next_gains = next_layer
        extra_specs = [_gain_spec(0), _mod_spec(0, ctx_tiles), _mod_spec(1, ctx_tiles)]
        extra_args = (next_gains, next_mods, next_mods)
        out_specs, out_shape = [tile, tile], [rows, jax.ShapeDtypeStruct(rows.shape, bf16)]
    return pl.pallas_call(
        _mlp_kernel,
        grid=(n_tiles, FFN_HIDDEN // tf),
        in_specs=[tile, tile, _mod_spec(5, ctx_tiles), _gain_spec(3)] + extra_specs + weights,
        out_specs=out_specs,
        out_shape=out_shape,
        scratch_shapes=[pltpu.VMEM((BATCH * TT, D_MODEL), f32)],
        compiler_params=_cparams("parallel", "arbitrary"),
        name="sq_relu_mlp",
    )(x, h, mods, gains, *extra_args, w1, w2)


def _rope_tables():
    rows = SEQ // GRID_W
    row = np.repeat(np.arange(rows, dtype=np.float64), GRID_W)
    col = np.tile(np.arange(GRID_W, dtype=np.float64), rows)
    n_freq = ATT_HD // 4
    inv_freq = ROPE_BASE ** (-np.arange(n_freq, dtype=np.float64) / n_freq)
    ang = np.concatenate([row[:, None] * inv_freq[None], col[:, None] * inv_freq[None]], axis=-1)
    cos, sin = np.cos(ang), np.sin(ang)
    cos, sin = np.tile(cos, (1, 4)), np.concatenate([-sin, sin, -sin, sin], axis=-1)
    return (jnp.asarray(np.stack([cos * Q_SCALE, cos, np.ones_like(cos)]), dtype=f32),
            jnp.asarray(np.stack([sin * Q_SCALE, sin, np.zeros_like(sin)]), dtype=f32))


ROPE_Q, ROPE_K, ROPE_NONE = 0, 1, 2


def _rope(x, cos, sin):
    half = ATT_HD // 2
    lane = lax.broadcasted_iota(jnp.int32, x.shape, x.ndim - 1)
    partner = jnp.where((lane & (ATT_HD - 1)) < half,
                        pltpu.roll(x, LANES - half, x.ndim - 1), pltpu.roll(x, half, x.ndim - 1))
    return x * cos + partner * sin


QKV_TILE = 2 * ATT_KVH * ATT_HD
QKV_TILES = ODD_IN // QKV_TILE


def _odd_in_kernel(h_ref, w_ref, cos_q_ref, sin_q_ref, cos_k_ref, sin_k_ref, o_ref):
    h = h_ref[...].reshape(BATCH * TT, D_MODEL)
    kw = ATT_KVH * ATT_HD
    for jt in range(QKV_TILES):
        acc = jnp.dot(h, w_ref[:, jt * QKV_TILE:(jt + 1) * QKV_TILE], preferred_element_type=f32)
        for c in range(QKV_TILE // LANES):
            cols = slice(c * LANES, (c + 1) * LANES)
            xc = acc[:, cols].reshape(BATCH, TT, LANES)
            if jt < QKV_TILES - 1:
                xc = _rope(xc, cos_q_ref[...], sin_q_ref[...])
            elif c * LANES < kw:
                xc = _rope(xc, cos_k_ref[...], sin_k_ref[...])
            o_ref[jt, :, :, cols] = xc.astype(o_ref.dtype)


def _odd_in_proj(h, w_in, cos, sin):
    def table(kind_of):
        return pl.BlockSpec((None, TT, LANES), lambda i: (kind_of(i), jnp.maximum(i - CTX_TILES, 0), 0))

    q_table = table(lambda i: ROPE_Q)
    k_table = table(lambda i: jnp.where(i >= CTX_TILES, ROPE_K, ROPE_NONE))
    return pl.pallas_call(
        _odd_in_kernel,
        grid=(N_TILES,),
        in_specs=[pl.BlockSpec((BATCH, TT, D_MODEL), lambda i: (0, i, 0)),
                  _const_spec((D_MODEL, ODD_IN)), q_table, q_table, k_table, k_table],
        out_specs=pl.BlockSpec((QKV_TILES, BATCH, TT, QKV_TILE), lambda i: (0, 0, i, 0)),
        out_shape=jax.ShapeDtypeStruct((QKV_TILES, BATCH, L_ALL, QKV_TILE), bf16),
        compiler_params=_cparams("parallel"),
        name="odd_in_proj",
    )(h, w_in, cos, sin, cos, sin)


def _attn_kernel(sink_ref, q_ref, kvp_ref, kvc_ref, kvn_ref, kvx_ref, w1f_ref, w2f_ref, o_ref, w1b_ref, w2b_ref):
    w1b_ref[...] = w1f_ref[...].astype(w1b_ref.dtype)
    w2b_ref[...] = w2f_ref[...].astype(w2b_ref.dtype)
    qb = pl.program_id(1)
    t = ATT_BLOCK
    kw = ATT_KVH * ATT_HD
    row = lax.broadcasted_iota(jnp.int32, (t, t), 0)
    col = lax.broadcasted_iota(jnp.int32, (t, t), 1)

    def band_valid(blk):
        off = col + blk * t
        kpos = qb * t - WINDOW + off
        return (jnp.abs(off - WINDOW - row) <= WINDOW) & (kpos >= 0) & (kpos < SEQ)

    n_ctx_chunks = CTX_LEN // LANES
    masks = {n_ctx_chunks: band_valid(0), n_ctx_chunks + 2: band_valid(2)}
    n_chunks = n_ctx_chunks + 3

    for kh in range(ATT_KVH):
        ks = slice(kh * ATT_HD, (kh + 1) * ATT_HD)
        vs = slice(kw + kh * ATT_HD, kw + (kh + 1) * ATT_HD)
        k_all = jnp.concatenate([kvx_ref[:, ks], kvp_ref[:, ks], kvc_ref[:, ks], kvn_ref[:, ks]], axis=0)
        v_all = jnp.concatenate([kvx_ref[:, vs], kvp_ref[:, vs], kvc_ref[:, vs], kvn_ref[:, vs]], axis=0)
        heads = [kh * ATT_GRP + g for g in range(ATT_GRP)]
        per_tile = QKV_TILE // ATT_HD
        q_all = jnp.concatenate([q_ref[h // per_tile, :, (h % per_tile) * ATT_HD:(h % per_tile + 1) * ATT_HD]
                                 for h in heads], axis=0)
        s_all = lax.dot_general(q_all, k_all, (((1,), (1,)), ((), ())), preferred_element_type=f32)
        probs, inv_den = [], []
        for g in range(ATT_GRP):
            sink = sink_ref[kh * ATT_GRP + g] * LOG2_E
            s = s_all[g * t:(g + 1) * t]
            s = jnp.concatenate([jnp.where(masks[c], s[:, c * LANES:(c + 1) * LANES], NEG_INF) if c in masks
                                 else s[:, c * LANES:(c + 1) * LANES] for c in range(n_chunks)], axis=-1)
            m = jnp.maximum(jnp.max(s, axis=-1, keepdims=True), sink)
            e = jnp.exp2(s - m)
            inv_den.append(1.0 / (jnp.sum(e, axis=-1, keepdims=True) + jnp.exp2(sink - m)))
            probs.append(e.astype(bf16))
        o_all = jnp.dot(jnp.concatenate(probs, axis=0), v_all, preferred_element_type=f32)
        for g in range(ATT_GRP):
            h = kh * ATT_GRP + g
            o_ref[:, h * ATT_HD:(h + 1) * ATT_HD] = (o_all[g * t:(g + 1) * t] * inv_den[g]).astype(o_ref.dtype)


def _attention(qkv, sink, mlp_w1, mlp_w2):
    t = ATT_BLOCK
    nb = SEQ // t
    off = CTX_LEN // t
    qw = ATT_H * ATT_HD
    kvw = QKV_TILE
    kv_tile = QKV_TILES - 1

    def kv_spec(rows, row_block):
        return pl.BlockSpec((None, None, rows, kvw), lambda b, i: (kv_tile, b, row_block(i), 0))

    c_in, c_out, c_shapes = _cast_rider((mlp_w1, mlp_w2), DEPTH - 1, BATCH * nb, lambda b, i: b * nb + i)
    return pl.pallas_call(
        _attn_kernel,
        grid=(BATCH, nb),
        in_specs=[pl.BlockSpec(memory_space=pltpu.SMEM),
                  pl.BlockSpec((kv_tile, None, t, QKV_TILE), lambda b, i: (0, b, off + i, 0)),
                  kv_spec(t, lambda i: off + jnp.maximum(i - 1, 0)),
                  kv_spec(t, lambda i: off + i),
                  kv_spec(t, lambda i: off + jnp.minimum(i + 1, nb - 1)),
                  kv_spec(CTX_LEN, lambda i: 0)] + c_in,
        out_specs=[pl.BlockSpec((None, t, qw), lambda b, i: (b, i, 0))] + c_out,
        out_shape=[jax.ShapeDtypeStruct((BATCH, SEQ, qw), bf16)] + c_shapes,
        compiler_params=_cparams("parallel", "parallel"),
        name="window_attention",
    )(sink, qkv, qkv, qkv, qkv, qkv, mlp_w1, mlp_w2)


def _odd_out_kernel(a_ref, x_ref, gt_ref, gpost_ref, g2_ref, sh2_ref, sc2_ref, wo_ref, o_ref, h_ref):
    a = a_ref[...].reshape(BATCH * TT, D_MODEL)
    out = jnp.dot(a, wo_ref[...], preferred_element_type=f32).reshape(BATCH, TT, D_MODEL)
    _mix_epilogue(out, x_ref, gt_ref, gpost_ref, g2_ref, sh2_ref, sc2_ref, o_ref, h_ref)


def _odd_out_proj(a, rows, mods, gains, w_out):
    tile = pl.BlockSpec((BATCH, TT, D_MODEL), lambda i: (0, i, 0))
    return pl.pallas_call(
        _odd_out_kernel,
        grid=(SEQ // TT,),
        in_specs=[tile,
                  pl.BlockSpec((BATCH, TT, D_MODEL), lambda i: (0, i + CTX_TILES, 0)),
                  _mod_spec(2, 0), _gain_spec(1), _gain_spec(2), _mod_spec(3, 0), _mod_spec(4, 0),
                  _const_spec((D_MODEL, D_MODEL))],
        out_specs=[tile, tile],
        out_shape=[jax.ShapeDtypeStruct((BATCH, SEQ, D_MODEL), f32),
                   jax.ShapeDtypeStruct((BATCH, SEQ, D_MODEL), bf16)],
        compiler_params=_cparams("parallel"),
        name="odd_out_proj",
    )(a, rows, mods, gains, gains, mods, mods, w_out)


def _layer_mods(m):
    m = m.reshape(MOD_ROWS, MOD_CHUNKS, D_MODEL)
    lat = m[:BATCH]
    ctx = jnp.broadcast_to(m[CTX_MOD_ROW:CTX_MOD_ROW + 1], lat.shape)
    return jnp.stack([lat, ctx], axis=0).transpose(2, 0, 1, 3)[:, :, :, None, :]


def kernel(x, c, ctx, c_ctx, mod_w, mod_b, norm_g, mlp_w1, mlp_w2, even_w_in, even_w_out, s5_lam_re, s5_lam_im, s5_log_dt, s5_b_re, s5_b_im, s5_c_re, s5_c_im, s5_d, s5_glu_w, s5_glu_b, odd_w_in, odd_w_out, odd_sink):
    c_rows = jnp.concatenate([c, c_ctx[None, :], jnp.zeros((MOD_ROWS - BATCH - 1, D_MODEL), f32)], axis=0)
    mods_all = _modulation(c_rows, mod_w, mod_b)
    gains_all = norm_g.reshape(DEPTH, 4, 1, D_MODEL)

    mods, gains = _layer_mods(mods_all[0]), gains_all[0]
    perm = _chunk_swap_matrix()
    w_in = _to_bf16(even_w_in)[0]
    h, u, rows, u_blocks = _even_u_proj(x, ctx, gains, mods, w_in, perm)
    p, (w1, w2, glu_w, even_out_w, odd_in_w, odd_out_w) = _even_rest_proj(
        h, w_in, (mlp_w1, mlp_w2), (s5_glu_w, even_w_out, odd_w_in, odd_w_out))
    tables = _s5_tables(s5_lam_re[0], s5_lam_im[0], s5_log_dt[0], s5_b_re[0], s5_b_im[0], s5_c_re[0], s5_c_im[0])
    y_blocks = _s5_mix(u_blocks, tables)
    o_f, o_b = _retention(p)
    rows, h = _even_out_proj(u, p, y_blocks, o_f, o_b, rows, mods, gains, s5_d[0].reshape(1, S5_WIDTH),
                             s5_glu_b[0].reshape(1, S5_WIDTH), glu_w, even_out_w, perm)
    next_mods, next_gains = _layer_mods(mods_all[1]), gains_all[1]
    rows, h = _mlp(rows, h, CTX_TILES, mods, gains, w1, w2, next_layer=(next_mods, next_gains))

    mods, gains = next_mods, next_gains
    cos, sin = _rope_tables()
    qkv = _odd_in_proj(h, odd_in_w, cos, sin)
    a, w1, w2 = _attention(qkv, odd_sink[0].astype(f32), mlp_w1, mlp_w2)
    lat, h = _odd_out_proj(a, rows, mods, gains, odd_out_w)
    return _mlp(lat, h, 0, mods, gains, w1, w2)
```

```python
import math

import jax
import jax.numpy as jnp
import numpy as np
from jax import lax
from jax.experimental import pallas as pl
from jax.experimental.pallas import tpu as pltpu

D_MODEL = 2048
BATCH = 4
SEQ = 4096
DEPTH = 2
GRID_W = 64
CTX_LEN = 256
MOD_CHUNKS = 6
FFN_HIDDEN = 4 * D_MODEL
NORM_EPS = 1e-6

S5_WIDTH = D_MODEL // 2
S5_P = 16
S5_G = S5_WIDTH // S5_P
S5_N = 64
LAMBDA_RE_MAX = -1e-4
RET_DK = 256
RET_H = (D_MODEL // 2) // RET_DK
RET_DV = (D_MODEL // 2) // RET_H
RET_DECAY_BASE = -5.0
EVEN_IN = S5_WIDTH + 2 * RET_H * RET_DK + 2 * RET_H * RET_DV

ATT_HD = 64
ATT_H = D_MODEL // ATT_HD
ATT_KVH = ATT_H // 8
ATT_GRP = ATT_H // ATT_KVH
ODD_IN = (ATT_H + 2 * ATT_KVH) * ATT_HD
WINDOW = 128
ATT_BLOCK = 128
ROPE_BASE = 10000.0
NEG_INF = -1e30
LOG2_E = math.log2(math.e)
Q_SCALE = ATT_HD ** -0.5 * LOG2_E

LANES = 128
SUBLANES = 8

L_ALL = CTX_LEN + SEQ
TT = 128
N_TILES = L_ALL // TT
CTX_TILES = CTX_LEN // TT
NORM_ROWS = 16
NORM_UNROLL = 4
MOD_ROWS = 8
CTX_MOD_ROW = BATCH

S5_T = 16
S5_TP = S5_T * S5_P
S5_NC = L_ALL // S5_T
S5_NC_CTX = CTX_LEN // S5_T
S5_BPAD = SUBLANES
S5_GB = 4
S5_GB_TABLES = 8
S5_SCAN_UNROLL = 8
S5_TILE_BLOCKS = TT // S5_T
S5_LANE_GROUPS = LANES // S5_P
RET_T = 256
RET_STEPS = L_ALL // RET_T
RET_CHAINS = 2 * RET_H

CAST_BLOCK_ELEMS = 2 * 1024 * 1024
VMEM_LIMIT = 56 * 1024 * 1024

f32 = jnp.float32
bf16 = jnp.bfloat16


def _cparams(*sem):
    return pltpu.CompilerParams(dimension_semantics=sem, vmem_limit_bytes=VMEM_LIMIT)


def _mod_spec(chunk, ctx_tiles):
    return pl.BlockSpec((None, None, BATCH, 1, D_MODEL),
                        lambda i, *_: (chunk, jnp.where(i < ctx_tiles, 1, 0), 0, 0, 0))


def _gain_spec(k):
    return pl.BlockSpec((None, 1, D_MODEL), lambda *_: (k, 0, 0))


def _const_spec(shape):
    return pl.BlockSpec(shape, lambda *_: (0,) * len(shape), pipeline_mode=pl.Buffered(1))


def _norm_mod_rows(x_ref, g_ref, sc_ref, sh_ref, h_ref, copy_ref=None):
    for b in range(BATCH):
        gain = g_ref[...] * (1.0 + sc_ref[b])
        shift = sh_ref[b]

        def body(r, carry):
            r0 = pl.multiple_of(r * NORM_ROWS, NORM_ROWS)
            xf = x_ref[b, pl.ds(r0, NORM_ROWS), :]
            if copy_ref is not None:
                copy_ref[b, pl.ds(r0, NORM_ROWS), :] = xf
            ms = jnp.mean(xf * xf, axis=-1, keepdims=True)
            h = (xf * lax.rsqrt(ms + NORM_EPS) * gain + shift).astype(h_ref.dtype)
            if len(h_ref.shape) == 3:
                h_ref[b, pl.ds(r0, NORM_ROWS), :] = h
            else:
                h_ref[pl.ds(pl.multiple_of(b * TT + r0, NORM_ROWS), NORM_ROWS), :] = h
            return carry

        lax.fori_loop(0, TT // NORM_ROWS, body, 0, unroll=NORM_UNROLL)


def _gated_residual(m, x, gate, g_post):
    ms = jnp.mean(m * m, axis=-1, keepdims=True)
    return x + m * lax.rsqrt(ms + NORM_EPS) * (g_post * gate)


def _mix_epilogue(out, x_ref, gt_ref, gpost_ref, g2_ref, sh2_ref, sc2_ref, o_ref, h_ref):
    x1 = _gated_residual(out, x_ref[...], gt_ref[...], gpost_ref[...])
    o_ref[...] = x1
    ms = jnp.mean(x1 * x1, axis=-1, keepdims=True)
    h_ref[...] = (x1 * lax.rsqrt(ms + NORM_EPS) * (g2_ref[...] * (1.0 + sc2_ref[...])) + sh2_ref[...]).astype(h_ref.dtype)


def _gated_residual_rows(load_rows, x_ref, gt_ref, gpost_ref, o_ref):
    for b in range(BATCH):
        gain = gpost_ref[...] * gt_ref[b]

        def body(r, carry):
            r0 = pl.multiple_of(r * NORM_ROWS, NORM_ROWS)
            m = load_rows(b, r0, NORM_ROWS)
            ms = jnp.mean(m * m, axis=-1, keepdims=True)
            o_ref[b, pl.ds(r0, NORM_ROWS), :] = x_ref[b, pl.ds(r0, NORM_ROWS), :] + m * lax.rsqrt(ms + NORM_EPS) * gain
            return carry

        lax.fori_loop(0, TT // NORM_ROWS, body, 0, unroll=NORM_UNROLL)


def _chunk_swap_matrix():
    n = S5_LANE_GROUPS * LANES
    src = np.arange(n)
    i, j, p = src // LANES, (src // S5_P) % S5_LANE_GROUPS, src % S5_P
    return jnp.asarray(np.arange(n)[None, :] == (j * LANES + i * S5_P + p)[:, None], dtype=bf16)


def _cast_kernel(w_ref, o_ref):
    o_ref[...] = w_ref[...].astype(o_ref.dtype)


def _to_bf16(w):
    n_layers, rows, cols = w.shape
    block_rows = min(rows, pl.next_power_of_2(CAST_BLOCK_ELEMS // cols + 1) // 2)
    spec = pl.BlockSpec((None, block_rows, cols), lambda l, r: (l, r, 0))
    return pl.pallas_call(
        _cast_kernel,
        grid=(n_layers, rows // block_rows),
        in_specs=[spec],
        out_specs=spec,
        out_shape=jax.ShapeDtypeStruct(w.shape, bf16),
        compiler_params=_cparams("parallel", "parallel"),
        name="to_bf16",
    )(w)


def _mod_kernel(c_ref, w_ref, b_ref, o_ref):
    c = c_ref[...]
    s = c * jax.nn.sigmoid(c)
    o_ref[...] = jnp.dot(s.astype(bf16), w_ref[...].astype(bf16), preferred_element_type=f32) + b_ref[...]


def _modulation(c_rows, mod_w, mod_b):
    tn = 1024
    n = MOD_CHUNKS * D_MODEL
    return pl.pallas_call(
        _mod_kernel,
        grid=(DEPTH, n // tn),
        in_specs=[pl.BlockSpec((MOD_ROWS, D_MODEL), lambda l, j: (0, 0)),
                  pl.BlockSpec((None, D_MODEL, tn), lambda l, j: (l, 0, j)),
                  pl.BlockSpec((None, 1, tn), lambda l, j: (l, 0, j))],
        out_specs=pl.BlockSpec((None, MOD_ROWS, tn), lambda l, j: (l, 0, j)),
        out_shape=jax.ShapeDtypeStruct((DEPTH, MOD_ROWS, n), f32),
        compiler_params=_cparams("parallel", "parallel"),
        name="ada_modulation",
    )(c_rows, mod_w, mod_b.reshape(DEPTH, 1, n))


def _even_u_kernel(x_ref, c_ref, g_ref, sh_ref, sc_ref, w_ref, perm_ref, h_ref, u_ref, rows_ref, ub_ref,
                   a_ref, xs_ref, st_ref):
    i = pl.program_id(0)

    @pl.when(i < CTX_TILES)
    def _():
        _norm_mod_rows(c_ref, g_ref, sc_ref, sh_ref, h_ref, copy_ref=rows_ref)

    @pl.when(i >= CTX_TILES)
    def _():
        _norm_mod_rows(x_ref, g_ref, sc_ref, sh_ref, h_ref, copy_ref=rows_ref)

    acc = jnp.dot(h_ref[...].reshape(BATCH * TT, D_MODEL), w_ref[...], preferred_element_type=f32)
    u_ref[...] = acc.reshape(BATCH, TT, S5_WIDTH).astype(u_ref.dtype)

    n_slab, n_half = S5_WIDTH // LANES, S5_TP // LANES
    for s in range(n_slab):
        a_ref[s] = acc[:, s * LANES:(s + 1) * LANES]
    for s in range(n_slab):
        for half in range(n_half):
            for b in range(BATCH):
                r0 = ((s * n_half + half) * BATCH + b) * S5_TILE_BLOCKS
                for tt in range(S5_LANE_GROUPS):
                    xs_ref[tt, r0:r0 + S5_TILE_BLOCKS, :] = a_ref[
                        s, pl.ds(b * TT + half * S5_LANE_GROUPS + tt, S5_TILE_BLOCKS, stride=S5_T), :]
    xs = jnp.concatenate([xs_ref[tt] for tt in range(S5_LANE_GROUPS)], axis=-1).astype(bf16)
    ys = jnp.dot(xs, perm_ref[...], preferred_element_type=f32)
    @pl.when(i == 0)
    def _():
        st_ref[...] = jnp.zeros_like(st_ref)
    for s in range(n_slab):
        for half in range(n_half):
            for b in range(BATCH):
                r0 = ((s * n_half + half) * BATCH + b) * S5_TILE_BLOCKS
                for gg in range(S5_LANE_GROUPS):
                    st_ref[s * S5_LANE_GROUPS + gg, half, pl.ds(b, S5_TILE_BLOCKS, stride=S5_BPAD), :] = (
                        ys[r0:r0 + S5_TILE_BLOCKS, gg * LANES:(gg + 1) * LANES])
    for g in range(S5_G):
        ub_ref[g] = jnp.concatenate([st_ref[g, 0], st_ref[g, 1]], axis=-1).astype(ub_ref.dtype)


def _even_u_proj(x, ctx, gains, mods, w_in, perm):
    blk_rows = S5_TILE_BLOCKS * S5_BPAD
    tile = lambda width: pl.BlockSpec((BATCH, TT, width), lambda i: (0, i, 0))
    return pl.pallas_call(
        _even_u_kernel,
        grid=(N_TILES,),
        in_specs=[pl.BlockSpec((BATCH, TT, D_MODEL), lambda i: (0, jnp.maximum(i - CTX_TILES, 0), 0)),
                  pl.BlockSpec((BATCH, TT, D_MODEL), lambda i: (0, jnp.minimum(i, CTX_TILES - 1), 0)),
                  _gain_spec(0), _mod_spec(0, CTX_TILES), _mod_spec(1, CTX_TILES),
                  pl.BlockSpec((D_MODEL, S5_WIDTH), lambda i: (0, 0), pipeline_mode=pl.Buffered(1)),
                  _const_spec(perm.shape)],
        out_specs=[tile(D_MODEL), tile(S5_WIDTH), tile(D_MODEL),
                   pl.BlockSpec((S5_G, blk_rows, S5_TP), lambda i: (0, i, 0))],
        out_shape=[jax.ShapeDtypeStruct((BATCH, L_ALL, D_MODEL), bf16),
                   jax.ShapeDtypeStruct((BATCH, L_ALL, S5_WIDTH), bf16),
                   jax.ShapeDtypeStruct((BATCH, L_ALL, D_MODEL), f32),
                   jax.ShapeDtypeStruct((S5_G, S5_NC * S5_BPAD, S5_TP), bf16)],
        scratch_shapes=[pltpu.VMEM((S5_WIDTH // LANES, BATCH * TT, LANES), f32),
                        pltpu.VMEM((S5_LANE_GROUPS, BATCH * TT, LANES), f32),
                        pltpu.VMEM((S5_G, S5_TP // LANES, blk_rows, LANES), f32)],
        compiler_params=_cparams("arbitrary"),
        name="even_u_proj",
    )(x, ctx, gains, mods, mods, w_in, perm)


EVEN_REST_TILES = (EVEN_IN - S5_WIDTH) // S5_WIDTH


def _cast_rider(ws, layer, n_blocks, block_of):
    in_specs, out_specs, out_shapes = [], [], []
    for w in ws:
        _, rows, cols = w.shape
        in_specs.append(pl.BlockSpec((None, rows // n_blocks, cols), lambda *g: (layer, block_of(*g), 0)))
        out_specs.append(pl.BlockSpec((rows // n_blocks, cols), lambda *g: (block_of(*g), 0)))
        out_shapes.append(jax.ShapeDtypeStruct((rows, cols), bf16))
    return in_specs, out_specs, out_shapes


def _even_rest_kernel(h_ref, *refs):
    w_refs, refs = refs[:EVEN_REST_TILES], refs[EVEN_REST_TILES:]
    n_riders = len(refs) // 2
    rider_in, o_ref, rider_out = refs[:n_riders], refs[n_riders], refs[n_riders + 1:]
    h = h_ref[...].reshape(BATCH * TT, D_MODEL)
    for jt, w_ref in enumerate(w_refs):
        acc = jnp.dot(h, w_ref[...], preferred_element_type=f32)
        o_ref[:, :, jt * S5_WIDTH:(jt + 1) * S5_WIDTH] = acc.reshape(BATCH, TT, S5_WIDTH).astype(o_ref.dtype)
    for src_ref, dst_ref in zip(rider_in, rider_out):
        dst_ref[...] = src_ref[...].astype(dst_ref.dtype)


def _even_rest_proj(h, w_in, layer0_weights, shared_weights):
    width = EVEN_IN - S5_WIDTH
    w_specs = [pl.BlockSpec((D_MODEL, S5_WIDTH), lambda i, jt=jt: (0, 1 + jt), pipeline_mode=pl.Buffered(1))
               for jt in range(EVEN_REST_TILES)]
    n_blocks = 32
    riders = tuple(layer0_weights) + tuple(shared_weights)
    c_in, c_out, c_shapes = _cast_rider(riders, 0, n_blocks, lambda i: jnp.minimum(i, n_blocks - 1))
    outs = pl.pallas_call(
        _even_rest_kernel,
        grid=(N_TILES,),
        in_specs=[pl.BlockSpec((BATCH, TT, D_MODEL), lambda i: (0, i, 0))] + w_specs + c_in,
        out_specs=[pl.BlockSpec((BATCH, TT, width), lambda i: (0, i, 0))] + c_out,
        out_shape=[jax.ShapeDtypeStruct((BATCH, L_ALL, width), bf16)] + c_shapes,
        compiler_params=_cparams("arbitrary"),
        name="even_rest_proj",
    )(h, *([w_in] * EVEN_REST_TILES), *riders)
    return outs[0], outs[1:]


def _s5_tables(lam_re, lam_im, log_dt, b_re, b_im, c_re, c_im):
    lr = jnp.minimum(lam_re.astype(f32), LAMBDA_RE_MAX)
    li = lam_im.astype(f32)
    dt = jnp.exp(log_dt.astype(f32))[..., None]
    zr, zi = lr * dt, li * dt
    ab_mag = jnp.exp(zr)
    ab_re, ab_im = ab_mag * jnp.cos(zi), ab_mag * jnp.sin(zi)
    den = lr * lr + li * li
    nr = ab_re - 1.0
    f_re = (nr * lr + ab_im * li) / den
    f_im = (ab_im * lr - nr * li) / den
    br, bi = b_re.astype(f32), b_im.astype(f32)
    bb_re = f_re[..., None] * br - f_im[..., None] * bi
    bb_im = f_re[..., None] * bi + f_im[..., None] * br
    cr, ci = c_re.astype(f32), c_im.astype(f32)
    bt_re, bt_im = bb_re.transpose(0, 1, 3, 2), bb_im.transpose(0, 1, 3, 2)
    asc = np.arange(S5_T, dtype=np.float32)
    desc = asc[::-1].copy()

    def powers(d, ks):
        k = jnp.asarray(ks, dtype=f32)[:, None]
        mag = jnp.exp(zr[d][:, None, :] * k)
        return mag * jnp.cos(zi[d][:, None, :] * k), mag * jnp.sin(zi[d][:, None, :] * k)

    def both(fwd, bwd):
        return jnp.concatenate([fwd, bwd], axis=-1)

    def c_times_powers(d, ks):
        pr, pi = powers(d, ks)
        ca_re = cr[d][:, None] * pr[:, :, None, :] - ci[d][:, None] * pi[:, :, None, :]
        ca_im = cr[d][:, None] * pi[:, :, None, :] + ci[d][:, None] * pr[:, :, None, :]
        return both(ca_re, ca_im).reshape(S5_G, S5_TP, 2 * S5_N)

    bcat = both(bt_re, -bt_im)
    w_intra = _s5_intra_matrix(bcat[0], bcat[1], c_times_powers(0, asc), c_times_powers(1, desc))

    (pf_re, pf_im), (pb_re, pb_im) = powers(0, desc), powers(1, asc)
    p_re, p_im = both(pf_re, pb_re)[:, :, None, :], both(pf_im, pb_im)[:, :, None, :]
    b_re2, b_im2 = both(bt_re[0], bt_re[1])[:, None], both(bt_im[0], bt_im[1])[:, None]
    w_state = jnp.stack([(p_re * b_re2 - p_im * b_im2).reshape(S5_G, S5_TP, 2 * S5_N),
                         (p_re * b_im2 + p_im * b_re2).reshape(S5_G, S5_TP, 2 * S5_N)])

    def rows_fb(fwd, bwd):
        return jnp.concatenate([fwd.transpose(0, 2, 1), bwd.transpose(0, 2, 1)], axis=1)

    (qf_re, qf_im), (qb_re, qb_im) = powers(0, asc + 1.0), powers(1, desc + 1.0)
    q_re, q_im = rows_fb(qf_re, qb_re)[..., None], rows_fb(qf_im, qb_im)[..., None]
    c_re2, c_im2 = rows_fb(cr[0], cr[1])[:, :, None, :], rows_fb(ci[0], ci[1])[:, :, None, :]
    w_cross = jnp.stack([(c_re2 * q_re - c_im2 * q_im).reshape(S5_G, 2 * S5_N, S5_TP),
                         -(c_re2 * q_im + c_im2 * q_re).reshape(S5_G, 2 * S5_N, S5_TP)])

    (af_re, af_im), (ab_re_t, ab_im_t) = powers(0, [float(S5_T)]), powers(1, [float(S5_T)])
    a_re, a_im = both(af_re, ab_re_t), both(af_im, ab_im_t)
    return w_intra, w_state.astype(bf16), w_cross.astype(bf16), a_re, a_im


def _s5_intra_kernel(bf_ref, bb_ref, caf_ref, cab_ref, w_ref):
    lane = lax.broadcasted_iota(jnp.int32, (S5_P, S5_TP), 1)
    dn = (((1,), (1,)), ((), ()))
    for g in range(S5_GB_TABLES):
        tq_f = lax.dot_general(bf_ref[g], caf_ref[g], dn, precision=lax.Precision.HIGHEST, preferred_element_type=f32)
        tq_b = lax.dot_general(bb_ref[g], cab_ref[g], dn, precision=lax.Precision.HIGHEST, preferred_element_type=f32)
        for s in range(S5_T):
            sf, sb = S5_P * s, S5_P * (S5_T - 1 - s)
            row_f = tq_f if s == 0 else jnp.where(lane >= sf, pltpu.roll(tq_f, sf, 1), 0.0)
            row_b = tq_b if sb == 0 else jnp.where(lane < S5_TP - sb, pltpu.roll(tq_b, S5_TP - sb, 1), 0.0)
            w_ref[g, s * S5_P:(s + 1) * S5_P, :] = (row_f + row_b).astype(w_ref.dtype)


def _s5_intra_matrix(b_f, b_b, ca_f, ca_b):
    def spec(rows, cols):
        return pl.BlockSpec((S5_GB_TABLES, rows, cols), lambda i: (i, 0, 0))

    return pl.pallas_call(
        _s5_intra_kernel,
        grid=(S5_G // S5_GB_TABLES,),
        in_specs=[spec(S5_P, 2 * S5_N), spec(S5_P, 2 * S5_N), spec(S5_TP, 2 * S5_N), spec(S5_TP, 2 * S5_N)],
        out_specs=spec(S5_TP, S5_TP),
        out_shape=jax.ShapeDtypeStruct((S5_G, S5_TP, S5_TP), bf16),
        compiler_params=_cparams("parallel"),
        name="s5_intra_matrix",
    )(b_f, b_b, ca_f, ca_b)


def _s5_kernel(u_ref, wi_ref, ws_ref, wc_ref, are_ref, aim_ref, y_ref, s_ref):
    half = 2 * S5_N
    for g in range(S5_GB):
        w_state = jnp.concatenate([ws_ref[0, g], ws_ref[1, g]], axis=-1)
        s_ref[g] = jnp.dot(u_ref[g], w_state, preferred_element_type=f32)

    fwd_lane = lax.broadcasted_iota(jnp.int32, (S5_BPAD, half), 1) < S5_N
    a_re = [jnp.broadcast_to(are_ref[g], (S5_BPAD, half)) for g in range(S5_GB)]
    a_im = [jnp.broadcast_to(aim_ref[g], (S5_BPAD, half)) for g in range(S5_GB)]

    def step(i, carry):
        cb = jnp.where(i < S5_NC_CTX, S5_NC_CTX - 1 - i, S5_NC + S5_NC_CTX - 1 - i)
        rf = pl.multiple_of(i * S5_BPAD, S5_BPAD)
        rb = pl.multiple_of(cb * S5_BPAD, S5_BPAD)
        out = []
        for g in range(S5_GB):
            h_re, h_im = carry[2 * g], carry[2 * g + 1]
            l_re = jnp.where(fwd_lane, s_ref[g, pl.ds(rf, S5_BPAD), 0:half], s_ref[g, pl.ds(rb, S5_BPAD), 0:half])
            l_im = jnp.where(fwd_lane, s_ref[g, pl.ds(rf, S5_BPAD), half:2 * half],
                             s_ref[g, pl.ds(rb, S5_BPAD), half:2 * half])
            s_ref[g, pl.ds(rf, S5_BPAD), 0:S5_N] = h_re[:, 0:S5_N]
            s_ref[g, pl.ds(rb, S5_BPAD), S5_N:half] = h_re[:, S5_N:half]
            s_ref[g, pl.ds(rf, S5_BPAD), half:half + S5_N] = h_im[:, 0:S5_N]
            s_ref[g, pl.ds(rb, S5_BPAD), half + S5_N:2 * half] = h_im[:, S5_N:half]
            out.append(a_re[g] * h_re - a_im[g] * h_im + l_re)
            out.append(a_re[g] * h_im + a_im[g] * h_re + l_im)
        return tuple(out)

    zero = jnp.zeros((S5_BPAD, half), f32)
    lax.fori_loop(0, S5_NC, step, (zero,) * (2 * S5_GB), unroll=S5_SCAN_UNROLL)

    for g in range(S5_GB):
        y = jnp.dot(u_ref[g], wi_ref[g], preferred_element_type=f32)
        w_cross = jnp.concatenate([wc_ref[0, g], wc_ref[1, g]], axis=0)
        y = y + jnp.dot(s_ref[g].astype(bf16), w_cross, preferred_element_type=f32)
        for hf in range(S5_TP // LANES):
            y_ref[g, hf] = y[:, hf * LANES:(hf + 1) * LANES]


def _s5_mix(u_blocks, tables):
    w_intra, w_state, w_cross, a_re, a_im = tables
    rows = S5_NC * S5_BPAD
    wsp = pl.BlockSpec((S5_GB, S5_TP, S5_TP), lambda g: (g, 0, 0))
    w_state_spec = pl.BlockSpec((2, S5_GB, S5_TP, 2 * S5_N), lambda g: (0, g, 0, 0))
    w_cross_spec = pl.BlockSpec((2, S5_GB, 2 * S5_N, S5_TP), lambda g: (0, g, 0, 0))
    asp = pl.BlockSpec((S5_GB, 1, 2 * S5_N), lambda g: (g, 0, 0))
    return pl.pallas_call(
        _s5_kernel,
        grid=(S5_G // S5_GB,),
        in_specs=[pl.BlockSpec((S5_GB, rows, S5_TP), lambda g: (g, 0, 0)), wsp, w_state_spec, w_cross_spec, asp, asp],
        out_specs=pl.BlockSpec((S5_GB, S5_TP // LANES, rows, LANES), lambda g: (g, 0, 0, 0)),
        out_shape=jax.ShapeDtypeStruct((S5_G, S5_TP // LANES, rows, LANES), f32),
        scratch_shapes=[pltpu.VMEM((S5_GB, rows, S5_TP), f32)],
        compiler_params=_cparams("parallel"),
        name="s5_mix",
    )(u_blocks, w_intra, w_state, w_cross, a_re, a_im)


def _ret_kernel(lg_ref, cd_ref, qf_ref, kf_ref, vf_ref, qb_ref, kb_ref, vb_ref, of_ref, ob_ref,
                r_ref, dec_ref, qdec_ref, kdec_ref):
    t = RET_T
    k_scale = RET_DK ** -0.5

    @pl.when(pl.program_id(1) == 0)
    def _():
        r_ref[...] = jnp.zeros_like(r_ref)
        n = lax.broadcasted_iota(jnp.int32, (t, t), 0)
        m = lax.broadcasted_iota(jnp.int32, (t, t), 1)
        pos = lax.broadcasted_iota(jnp.int32, (t, 1), 0)
        for d in range(2):
            diff = (n - m if d == 0 else m - n).astype(f32)
            p = (pos if d == 0 else t - 1 - pos).astype(f32)
            for h in range(RET_H):
                c = d * RET_H + h
                lg = lg_ref[c]
                dec_ref[c] = jnp.where(diff >= 0, jnp.exp(jnp.maximum(diff, 0.0) * lg), 0.0) * k_scale
                qdec_ref[c] = jnp.broadcast_to(jnp.exp((p + 1.0) * lg), (t, RET_DV))
                kdec_ref[c] = jnp.broadcast_to(jnp.exp((t - 1.0 - p) * lg) * k_scale, (t, RET_DK))

    for d, (q_ref, k_ref, v_ref, o_ref) in enumerate(((qf_ref, kf_ref, vf_ref, of_ref),
                                                        (qb_ref, kb_ref, vb_ref, ob_ref))):
        for h in range(RET_H):
            c = d * RET_H + h
            cols = slice(h * RET_DK, (h + 1) * RET_DK)
            q = q_ref[:, cols]
            k = k_ref[:, cols]
            v = v_ref[:, cols]
            r = r_ref[c]
            s = lax.dot_general(q, k, (((1,), (1,)), ((), ())), preferred_element_type=f32) * dec_ref[c]
            inner = jnp.dot(s.astype(bf16), v, preferred_element_type=f32)
            cross = jnp.dot(q, r.astype(bf16), preferred_element_type=f32) * qdec_ref[c]
            o_ref[:, cols] = (inner + cross).astype(o_ref.dtype)
            kd = (k.astype(f32) * kdec_ref[c]).astype(bf16)
            r_ref[c] = r * cd_ref[c] + lax.dot_general(kd, v, (((0,), (0,)), ((), ())), preferred_element_type=f32)


def _retention(p):
    e = RET_DECAY_BASE - (2.0 * np.arange(RET_H, dtype=np.float64)[None, :] + np.arange(2, dtype=np.float64)[:, None])
    log_g64 = np.log1p(-np.exp2(e)).reshape(RET_CHAINS)
    log_g = jnp.asarray(log_g64, dtype=f32)
    chunk_dec = jnp.asarray(np.exp(RET_T * log_g64), dtype=f32)
    width = RET_H * RET_DK
    col0 = 0

    def bwd_chunk(i):
        return jnp.where(i == 0, 0, RET_STEPS - i)

    def fwd(off):
        return pl.BlockSpec((None, RET_T, width), lambda b, i: (b, i, col0 + off))

    def bwd(off):
        return pl.BlockSpec((None, RET_T, width), lambda b, i: (b, bwd_chunk(i), col0 + off))

    smem = pl.BlockSpec(memory_space=pltpu.SMEM)
    out = jax.ShapeDtypeStruct((BATCH, L_ALL, RET_H * RET_DV), bf16)
    return pl.pallas_call(
        _ret_kernel,
        grid=(BATCH, RET_STEPS),
        in_specs=[smem, smem, fwd(0), fwd(1), fwd(2), bwd(0), bwd(1), bwd(2)],
        out_specs=[pl.BlockSpec((None, RET_T, RET_H * RET_DV), lambda b, i: (b, i, 0)),
                   pl.BlockSpec((None, RET_T, RET_H * RET_DV), lambda b, i: (b, bwd_chunk(i), 0))],
        out_shape=[out, out],
        scratch_shapes=[pltpu.VMEM((RET_CHAINS, RET_DK, RET_DV), f32),
                        pltpu.VMEM((RET_CHAINS, RET_T, RET_T), f32),
                        pltpu.VMEM((RET_CHAINS, RET_T, RET_DV), f32),
                        pltpu.VMEM((RET_CHAINS, RET_T, RET_DK), f32)],
        compiler_params=_cparams("parallel", "arbitrary"),
        name="retention",
    )(log_g, chunk_dec, p, p, p, p, p, p)


def _gelu_tanh(x):
    return 0.5 * x * (1.0 + jnp.tanh(math.sqrt(2.0 / math.pi) * (x + 0.044715 * (x * x * x))))


def _even_out_kernel(u_ref, gate_ref, y_ref, of_ref, ob_ref, x_ref, gt_ref, gpost_ref, g2_ref, sh2_ref, sc2_ref,
                     dsk_ref, glub_ref, gluw_ref, wo_ref, perm_ref, o_ref, h_ref, z_ref, xs_ref, yf_ref):
    rows = BATCH * TT
    n_slab, n_half = S5_WIDTH // LANES, S5_TP // LANES
    for s in range(n_slab):
        for half in range(n_half):
            for b in range(BATCH):
                r0 = ((s * n_half + half) * BATCH + b) * S5_TILE_BLOCKS
                for gg in range(S5_LANE_GROUPS):
                    xs_ref[gg, r0:r0 + S5_TILE_BLOCKS, :] = y_ref[
                        s * S5_LANE_GROUPS + gg, half, pl.ds(b, S5_TILE_BLOCKS, stride=S5_BPAD), :]
    xs = jnp.concatenate([xs_ref[gg] for gg in range(S5_LANE_GROUPS)], axis=-1).astype(bf16)
    ys = jnp.dot(xs, perm_ref[...], preferred_element_type=f32)
    for s in range(n_slab):
        for half in range(n_half):
            for b in range(BATCH):
                r0 = ((s * n_half + half) * BATCH + b) * S5_TILE_BLOCKS
                for tt in range(S5_LANE_GROUPS):
                    z_ref[s, pl.ds(b * TT + half * S5_LANE_GROUPS + tt, S5_TILE_BLOCKS, stride=S5_T), :] = (
                        ys[r0:r0 + S5_TILE_BLOCKS, tt * LANES:(tt + 1) * LANES])
    u = u_ref[...].reshape(rows, S5_WIDTH)
    for s in range(S5_WIDTH // LANES):
        cols = slice(s * LANES, (s + 1) * LANES)
        yf_ref[:, cols] = _gelu_tanh(z_ref[s] + dsk_ref[:, cols] * u[:, cols].astype(f32))
    y = yf_ref[...]
    z = jnp.dot(y.astype(bf16), gluw_ref[...], preferred_element_type=f32) + glub_ref[...]
    s5_out = y * jax.nn.sigmoid(z)
    r = (of_ref[...].astype(f32) + ob_ref[...].astype(f32)).reshape(rows, RET_H * RET_DV)
    heads = []
    for h in range(RET_H):
        rh = r[:, h * RET_DV:(h + 1) * RET_DV]
        heads.append(rh * lax.rsqrt(jnp.mean(rh * rh, axis=-1, keepdims=True) + NORM_EPS))
    g = gate_ref[...].reshape(rows, RET_H * RET_DV).astype(f32)
    ret_out = jnp.concatenate(heads, axis=-1) * (g * jax.nn.sigmoid(g))
    out = jnp.dot(s5_out.astype(bf16), wo_ref[0:S5_WIDTH, :], preferred_element_type=f32)
    out = out + jnp.dot(ret_out.astype(bf16), wo_ref[S5_WIDTH:, :], preferred_element_type=f32)
    _mix_epilogue(out.reshape(BATCH, TT, D_MODEL), x_ref, gt_ref, gpost_ref, g2_ref, sh2_ref, sc2_ref, o_ref, h_ref)


def _even_out_proj(u, p, y_blocks, o_f, o_b, rows, mods, gains, d_skip, glu_b, glu_w, w_out, perm):
    half = S5_WIDTH
    blk_rows = S5_TILE_BLOCKS * S5_BPAD
    tile = pl.BlockSpec((BATCH, TT, half), lambda i: (0, i, 0))
    return pl.pallas_call(
        _even_out_kernel,
        grid=(N_TILES,),
        in_specs=[tile,
                  pl.BlockSpec((BATCH, TT, half), lambda i: (0, i, EVEN_REST_TILES - 1)),
                  pl.BlockSpec((S5_G, S5_TP // LANES, blk_rows, LANES), lambda i: (0, 0, i, 0)),
                  tile, tile,
                  pl.BlockSpec((BATCH, TT, D_MODEL), lambda i: (0, i, 0)),
                  _mod_spec(2, CTX_TILES), _gain_spec(1),
                  _gain_spec(2), _mod_spec(3, CTX_TILES), _mod_spec(4, CTX_TILES),
                  _const_spec((1, half)), _const_spec((1, half)),
                  _const_spec((half, half)), _const_spec((D_MODEL, D_MODEL)), _const_spec(perm.shape)],
        out_specs=[pl.BlockSpec((BATCH, TT, D_MODEL), lambda i: (0, i, 0)),
                   pl.BlockSpec((BATCH, TT, D_MODEL), lambda i: (0, i, 0))],
        out_shape=[jax.ShapeDtypeStruct((BATCH, L_ALL, D_MODEL), f32),
                   jax.ShapeDtypeStruct((BATCH, L_ALL, D_MODEL), bf16)],
        scratch_shapes=[pltpu.VMEM((S5_WIDTH // LANES, BATCH * TT, LANES), f32),
                        pltpu.VMEM((S5_LANE_GROUPS, BATCH * TT, LANES), f32),
                        pltpu.VMEM((BATCH * TT, S5_WIDTH), f32)],
        compiler_params=_cparams("parallel"),
        name="even_out_proj",
    )(u, p, y_blocks, o_f, o_b, rows, mods, gains, gains, mods, mods, d_skip, glu_b, glu_w, w_out, perm)


def _mlp_kernel(x_ref, h_ref, gt_ref, gpost_ref, w1_ref, w2_ref, o_ref, acc_ref):
    f = pl.program_id(1)
    last = pl.num_programs(1) - 1

    def partial_out():
        h = h_ref[...].reshape(BATCH * TT, D_MODEL)
        a = jnp.maximum(jnp.dot(h, w1_ref[...], preferred_element_type=f32), 0.0)
        return jnp.dot((a * a).astype(bf16), w2_ref[...], preferred_element_type=f32)

    @pl.when(f == 0)
    def _():
        acc_ref[...] = partial_out()

    @pl.when(jnp.logical_and(f > 0, f < last))
    def _():
        acc_ref[...] += partial_out()

    @pl.when(f == last)
    def _():
        m = (acc_ref[...] + partial_out()).reshape(BATCH, TT, D_MODEL)
        o_ref[...] = _gated_residual(m, x_ref[...], gt_ref[...], gpost_ref[...])


def _mlp(x, h, ctx_tiles, mods, gains, w1, w2):
    tf = 1024
    n_tiles = x.shape[1] // TT
    tile = pl.BlockSpec((BATCH, TT, D_MODEL), lambda i, f: (0, i, 0))
    return pl.pallas_call(
        _mlp_kernel,
        grid=(n_tiles, FFN_HIDDEN // tf),
        in_specs=[tile, tile, _mod_spec(5, ctx_tiles), _gain_spec(3),
                  pl.BlockSpec((D_MODEL, tf), lambda i, f: (0, f)),
                  pl.BlockSpec((tf, D_MODEL), lambda i, f: (f, 0))],
        out_specs=tile,
        out_shape=jax.ShapeDtypeStruct((BATCH, n_tiles * TT, D_MODEL), f32),
        scratch_shapes=[pltpu.VMEM((BATCH * TT, D_MODEL), f32)],
        compiler_params=_cparams("parallel", "arbitrary"),
        name="sq_relu_mlp",
    )(x, h, mods, gains, w1, w2)


def _rope_tables():
    rows = SEQ // GRID_W
    row = np.repeat(np.arange(rows, dtype=np.float64), GRID_W)
    col = np.tile(np.arange(GRID_W, dtype=np.float64), rows)
    n_freq = ATT_HD // 4
    inv_freq = ROPE_BASE ** (-np.arange(n_freq, dtype=np.float64) / n_freq)
    ang = np.concatenate([row[:, None] * inv_freq[None], col[:, None] * inv_freq[None]], axis=-1)
    cos, sin = np.cos(ang), np.sin(ang)
    cos, sin = np.tile(cos, (1, 4)), np.concatenate([-sin, sin, -sin, sin], axis=-1)
    return (jnp.asarray(np.stack([cos * Q_SCALE, cos, np.ones_like(cos)]), dtype=f32),
            jnp.asarray(np.stack([sin * Q_SCALE, sin, np.zeros_like(sin)]), dtype=f32))


ROPE_Q, ROPE_K, ROPE_NONE = 0, 1, 2


def _rope(x, cos, sin):
    half = ATT_HD // 2
    lane = lax.broadcasted_iota(jnp.int32, x.shape, x.ndim - 1)
    partner = jnp.where((lane & (ATT_HD - 1)) < half,
                        pltpu.roll(x, LANES - half, x.ndim - 1), pltpu.roll(x, half, x.ndim - 1))
    return x * cos + partner * sin


QKV_TILE = 2 * ATT_KVH * ATT_HD
QKV_TILES = ODD_IN // QKV_TILE


def _odd_in_kernel(x_ref, g_ref, sh_ref, sc_ref, w_ref, cos_q_ref, sin_q_ref, cos_k_ref, sin_k_ref, o_ref, h_ref):
    _norm_mod_rows(x_ref, g_ref, sc_ref, sh_ref, h_ref)
    kw = ATT_KVH * ATT_HD
    for jt in range(QKV_TILES):
        acc = jnp.dot(h_ref[...], w_ref[:, jt * QKV_TILE:(jt + 1) * QKV_TILE], preferred_element_type=f32)
        for c in range(QKV_TILE // LANES):
            cols = slice(c * LANES, (c + 1) * LANES)
            xc = acc[:, cols].reshape(BATCH, TT, LANES)
            if jt < QKV_TILES - 1:
                xc = _rope(xc, cos_q_ref[...], sin_q_ref[...])
            elif c * LANES < kw:
                xc = _rope(xc, cos_k_ref[...], sin_k_ref[...])
            o_ref[jt, :, :, cols] = xc.astype(o_ref.dtype)


def _odd_in_proj(x, gains, mods, w_in, cos, sin):
    def table(kind_of):
        return pl.BlockSpec((None, TT, LANES), lambda i: (kind_of(i), jnp.maximum(i - CTX_TILES, 0), 0))

    q_table = table(lambda i: ROPE_Q)
    k_table = table(lambda i: jnp.where(i >= CTX_TILES, ROPE_K, ROPE_NONE))
    return pl.pallas_call(
        _odd_in_kernel,
        grid=(N_TILES,),
        in_specs=[pl.BlockSpec((BATCH, TT, D_MODEL), lambda i: (0, i, 0)),
                  _gain_spec(0), _mod_spec(0, CTX_TILES), _mod_spec(1, CTX_TILES),
                  _const_spec((D_MODEL, ODD_IN)), q_table, q_table, k_table, k_table],
        out_specs=pl.BlockSpec((QKV_TILES, BATCH, TT, QKV_TILE), lambda i: (0, 0, i, 0)),
        out_shape=jax.ShapeDtypeStruct((QKV_TILES, BATCH, L_ALL, QKV_TILE), bf16),
        scratch_shapes=[pltpu.VMEM((BATCH * TT, D_MODEL), bf16)],
        compiler_params=_cparams("parallel"),
        name="odd_in_proj",
    )(x, gains, mods, mods, w_in, cos, sin, cos, sin)


def _attn_kernel(sink_ref, q_ref, kvp_ref, kvc_ref, kvn_ref, kvx_ref, w1f_ref, w2f_ref, o_ref, w1b_ref, w2b_ref):
    w1b_ref[...] = w1f_ref[...].astype(w1b_ref.dtype)
    w2b_ref[...] = w2f_ref[...].astype(w2b_ref.dtype)
    qb = pl.program_id(1)
    t = ATT_BLOCK
    kw = ATT_KVH * ATT_HD
    row = lax.broadcasted_iota(jnp.int32, (t, t), 0)
    col = lax.broadcasted_iota(jnp.int32, (t, t), 1)

    def band_valid(blk):
        off = col + blk * t
        kpos = qb * t - WINDOW + off
        return (jnp.abs(off - WINDOW - row) <= WINDOW) & (kpos >= 0) & (kpos < SEQ)

    n_ctx_chunks = CTX_LEN // LANES
    masks = {n_ctx_chunks: band_valid(0), n_ctx_chunks + 2: band_valid(2)}
    n_chunks = n_ctx_chunks + 3

    for kh in range(ATT_KVH):
        ks = slice(kh * ATT_HD, (kh + 1) * ATT_HD)
        vs = slice(kw + kh * ATT_HD, kw + (kh + 1) * ATT_HD)
        k_all = jnp.concatenate([kvx_ref[:, ks], kvp_ref[:, ks], kvc_ref[:, ks], kvn_ref[:, ks]], axis=0)
        v_all = jnp.concatenate([kvx_ref[:, vs], kvp_ref[:, vs], kvc_ref[:, vs], kvn_ref[:, vs]], axis=0)
        heads = [kh * ATT_GRP + g for g in range(ATT_GRP)]
        per_tile = QKV_TILE // ATT_HD
        q_all = jnp.concatenate([q_ref[h // per_tile, :, (h % per_tile) * ATT_HD:(h % per_tile + 1) * ATT_HD]
                                 for h in heads], axis=0)
        s_all = lax.dot_general(q_all, k_all, (((1,), (1,)), ((), ())), preferred_element_type=f32)
        probs, inv_den = [], []
        for g in range(ATT_GRP):
            sink = sink_ref[kh * ATT_GRP + g] * LOG2_E
            s = s_all[g * t:(g + 1) * t]
            s = jnp.concatenate([jnp.where(masks[c], s[:, c * LANES:(c + 1) * LANES], NEG_INF) if c in masks
                                 else s[:, c * LANES:(c + 1) * LANES] for c in range(n_chunks)], axis=-1)
            m = jnp.maximum(jnp.max(s, axis=-1, keepdims=True), sink)
            e = jnp.exp2(s - m)
            inv_den.append(1.0 / (jnp.sum(e, axis=-1, keepdims=True) + jnp.exp2(sink - m)))
            probs.append(e.astype(bf16))
        o_all = jnp.dot(jnp.concatenate(probs, axis=0), v_all, preferred_element_type=f32)
        for g in range(ATT_GRP):
            h = kh * ATT_GRP + g
            o_ref[:, h * ATT_HD:(h + 1) * ATT_HD] = (o_all[g * t:(g + 1) * t] * inv_den[g]).astype(o_ref.dtype)


def _attention(qkv, sink, mlp_w1, mlp_w2):
    t = ATT_BLOCK
    nb = SEQ // t
    off = CTX_LEN // t
    qw = ATT_H * ATT_HD
    kvw = QKV_TILE
    kv_tile = QKV_TILES - 1

    def kv_spec(rows, row_block):
        return pl.BlockSpec((None, None, rows, kvw), lambda b, i: (kv_tile, b, row_block(i), 0))

    c_in, c_out, c_shapes = _cast_rider((mlp_w1, mlp_w2), DEPTH - 1, BATCH * nb, lambda b, i: b * nb + i)
    return pl.pallas_call(
        _attn_kernel,
        grid=(BATCH, nb),
        in_specs=[pl.BlockSpec(memory_space=pltpu.SMEM),
                  pl.BlockSpec((kv_tile, None, t, QKV_TILE), lambda b, i: (0, b, off + i, 0)),
                  kv_spec(t, lambda i: off + jnp.maximum(i - 1, 0)),
                  kv_spec(t, lambda i: off + i),
                  kv_spec(t, lambda i: off + jnp.minimum(i + 1, nb - 1)),
                  kv_spec(CTX_LEN, lambda i: 0)] + c_in,
        out_specs=[pl.BlockSpec((None, t, qw), lambda b, i: (b, i, 0))] + c_out,
        out_shape=[jax.ShapeDtypeStruct((BATCH, SEQ, qw), bf16)] + c_shapes,
        compiler_params=_cparams("parallel", "parallel"),
        name="window_attention",
    )(sink, qkv, qkv, qkv, qkv, qkv, mlp_w1, mlp_w2)


def _odd_out_kernel(a_ref, x_ref, gt_ref, gpost_ref, g2_ref, sh2_ref, sc2_ref, wo_ref, o_ref, h_ref):
    a = a_ref[...].reshape(BATCH * TT, D_MODEL)
    out = jnp.dot(a, wo_ref[...], preferred_element_type=f32).reshape(BATCH, TT, D_MODEL)
    _mix_epilogue(out, x_ref, gt_ref, gpost_ref, g2_ref, sh2_ref, sc2_ref, o_ref, h_ref)


def _odd_out_proj(a, rows, mods, gains, w_out):
    tile = pl.BlockSpec((BATCH, TT, D_MODEL), lambda i: (0, i, 0))
    return pl.pallas_call(
        _odd_out_kernel,
        grid=(SEQ // TT,),
        in_specs=[tile,
                  pl.BlockSpec((BATCH, TT, D_MODEL), lambda i: (0, i + CTX_TILES, 0)),
                  _mod_spec(2, 0), _gain_spec(1), _gain_spec(2), _mod_spec(3, 0), _mod_spec(4, 0),
                  _const_spec((D_MODEL, D_MODEL))],
        out_specs=[tile, tile],
        out_shape=[jax.ShapeDtypeStruct((BATCH, SEQ, D_MODEL), f32),
                   jax.ShapeDtypeStruct((BATCH, SEQ, D_MODEL), bf16)],
        compiler_params=_cparams("parallel"),
        name="odd_out_proj",
    )(a, rows, mods, gains, gains, mods, mods, w_out)


def _layer_mods(m):
    m = m.reshape(MOD_ROWS, MOD_CHUNKS, D_MODEL)
    lat = m[:BATCH]
    ctx = jnp.broadcast_to(m[CTX_MOD_ROW:CTX_MOD_ROW + 1], lat.shape)
    return jnp.stack([lat, ctx], axis=0).transpose(2, 0, 1, 3)[:, :, :, None, :]


def kernel(x, c, ctx, c_ctx, mod_w, mod_b, norm_g, mlp_w1, mlp_w2, even_w_in, even_w_out, s5_lam_re, s5_lam_im, s5_log_dt, s5_b_re, s5_b_im, s5_c_re, s5_c_im, s5_d, s5_glu_w, s5_glu_b, odd_w_in, odd_w_out, odd_sink):
    c_rows = jnp.concatenate([c, c_ctx[None, :], jnp.zeros((MOD_ROWS - BATCH - 1, D_MODEL), f32)], axis=0)
    mods_all = _modulation(c_rows, mod_w, mod_b)
    gains_all = norm_g.reshape(DEPTH, 4, 1, D_MODEL)

    mods, gains = _layer_mods(mods_all[0]), gains_all[0]
    perm = _chunk_swap_matrix()
    w_in = _to_bf16(even_w_in)[0]
    h, u, rows, u_blocks = _even_u_proj(x, ctx, gains, mods, w_in, perm)
    p, (w1, w2, glu_w, even_out_w, odd_in_w, odd_out_w) = _even_rest_proj(
        h, w_in, (mlp_w1, mlp_w2), (s5_glu_w, even_w_out, odd_w_in, odd_w_out))
    tables = _s5_tables(s5_lam_re[0], s5_lam_im[0], s5_log_dt[0], s5_b_re[0], s5_b_im[0], s5_c_re[0], s5_c_im[0])
    y_blocks = _s5_mix(u_blocks, tables)
    o_f, o_b = _retention(p)
    rows, h = _even_out_proj(u, p, y_blocks, o_f, o_b, rows, mods, gains, s5_d[0].reshape(1, S5_WIDTH),
                             s5_glu_b[0].reshape(1, S5_WIDTH), glu_w, even_out_w, perm)
    rows = _mlp(rows, h, CTX_TILES, mods, gains, w1, w2)

    mods, gains = _layer_mods(mods_all[1]), gains_all[1]
    cos, sin = _rope_tables()
    qkv = _odd_in_proj(rows, gains, mods, odd_in_w, cos, sin)
    a, w1, w2 = _attention(qkv, odd_sink[0].astype(f32), mlp_w1, mlp_w2)
    lat, h = _odd_out_proj(a, rows, mods, gains, odd_out_w)
    return _mlp(lat, h, 0, mods, gains, w1, w2)
```

```python
import math

import jax
import jax.numpy as jnp
import numpy as np
from jax import lax
from jax.experimental import pallas as pl
from jax.experimental.pallas import tpu as pltpu

D_MODEL = 2048
BATCH = 4
SEQ = 4096
DEPTH = 2
GRID_W = 64
CTX_LEN = 256
MOD_CHUNKS = 6
FFN_HIDDEN = 4 * D_MODEL
NORM_EPS = 1e-6

S5_WIDTH = D_MODEL // 2
S5_P = 16
S5_G = S5_WIDTH // S5_P
S5_N = 64
LAMBDA_RE_MAX = -1e-4
RET_DK = 256
RET_H = (D_MODEL // 2) // RET_DK
RET_DV = (D_MODEL // 2) // RET_H
RET_DECAY_BASE = -5.0
EVEN_IN = S5_WIDTH + 2 * RET_H * RET_DK + 2 * RET_H * RET_DV

ATT_HD = 64
ATT_H = D_MODEL // ATT_HD
ATT_KVH = ATT_H // 8
ATT_GRP = ATT_H // ATT_KVH
ODD_IN = (ATT_H + 2 * ATT_KVH) * ATT_HD
WINDOW = 128
ATT_BLOCK = 128
ROPE_BASE = 10000.0
NEG_INF = -1e30
LOG2_E = math.log2(math.e)
Q_SCALE = ATT_HD ** -0.5 * LOG2_E

LANES = 128
SUBLANES = 8

L_ALL = CTX_LEN + SEQ
TT = 128
N_TILES = L_ALL // TT
CTX_TILES = CTX_LEN // TT
NORM_ROWS = 16
NORM_UNROLL = 4
MOD_ROWS = 8
CTX_MOD_ROW = BATCH

S5_T = 16
S5_TP = S5_T * S5_P
S5_NC = L_ALL // S5_T
S5_NC_CTX = CTX_LEN // S5_T
S5_BPAD = SUBLANES
S5_GB = 4
S5_GB_TABLES = 8
S5_SCAN_UNROLL = 8
S5_TILE_BLOCKS = TT // S5_T
S5_LANE_GROUPS = LANES // S5_P
RET_T = 256
RET_STEPS = L_ALL // RET_T
RET_CHAINS = 2 * RET_H
RET_BB = 2

CAST_BLOCK_ELEMS = 2 * 1024 * 1024
VMEM_LIMIT = 56 * 1024 * 1024

f32 = jnp.float32
bf16 = jnp.bfloat16


def _cparams(*sem):
    return pltpu.CompilerParams(dimension_semantics=sem, vmem_limit_bytes=VMEM_LIMIT)


def _mod_spec(chunk, ctx_tiles):
    return pl.BlockSpec((None, None, BATCH, 1, D_MODEL),
                        lambda i, *_: (chunk, jnp.where(i < ctx_tiles, 1, 0), 0, 0, 0))


def _gain_spec(k):
    return pl.BlockSpec((None, 1, D_MODEL), lambda *_: (k, 0, 0))


def _const_spec(shape):
    return pl.BlockSpec(shape, lambda *_: (0,) * len(shape), pipeline_mode=pl.Buffered(1))


def _norm_mod_rows(x_ref, g_ref, sc_ref, sh_ref, h_ref, copy_ref=None):
    for b in range(BATCH):
        gain = g_ref[...] * (1.0 + sc_ref[b])
        shift = sh_ref[b]

        def body(r, carry):
            r0 = pl.multiple_of(r * NORM_ROWS, NORM_ROWS)
            xf = x_ref[b, pl.ds(r0, NORM_ROWS), :]
            if copy_ref is not None:
                copy_ref[b, pl.ds(r0, NORM_ROWS), :] = xf
            ms = jnp.mean(xf * xf, axis=-1, keepdims=True)
            h = (xf * lax.rsqrt(ms + NORM_EPS) * gain + shift).astype(h_ref.dtype)
            if len(h_ref.shape) == 3:
                h_ref[b, pl.ds(r0, NORM_ROWS), :] = h
            else:
                h_ref[pl.ds(pl.multiple_of(b * TT + r0, NORM_ROWS), NORM_ROWS), :] = h
            return carry

        lax.fori_loop(0, TT // NORM_ROWS, body, 0, unroll=NORM_UNROLL)


def _gated_residual(m, x, gate, g_post):
    ms = jnp.mean(m * m, axis=-1, keepdims=True)
    return x + m * lax.rsqrt(ms + NORM_EPS) * (g_post * gate)


def _mix_epilogue(out, x_ref, gt_ref, gpost_ref, g2_ref, sh2_ref, sc2_ref, o_ref, h_ref):
    x1 = _gated_residual(out, x_ref[...], gt_ref[...], gpost_ref[...])
    o_ref[...] = x1
    ms = jnp.mean(x1 * x1, axis=-1, keepdims=True)
    h_ref[...] = (x1 * lax.rsqrt(ms + NORM_EPS) * (g2_ref[...] * (1.0 + sc2_ref[...])) + sh2_ref[...]).astype(h_ref.dtype)


def _gated_residual_rows(load_rows, x_ref, gt_ref, gpost_ref, o_ref):
    for b in range(BATCH):
        gain = gpost_ref[...] * gt_ref[b]

        def body(r, carry):
            r0 = pl.multiple_of(r * NORM_ROWS, NORM_ROWS)
            m = load_rows(b, r0, NORM_ROWS)
            ms = jnp.mean(m * m, axis=-1, keepdims=True)
            o_ref[b, pl.ds(r0, NORM_ROWS), :] = x_ref[b, pl.ds(r0, NORM_ROWS), :] + m * lax.rsqrt(ms + NORM_EPS) * gain
            return carry

        lax.fori_loop(0, TT // NORM_ROWS, body, 0, unroll=NORM_UNROLL)


def _chunk_swap_matrix():
    n = S5_LANE_GROUPS * LANES
    src = np.arange(n)
    i, j, p = src // LANES, (src // S5_P) % S5_LANE_GROUPS, src % S5_P
    return jnp.asarray(np.arange(n)[None, :] == (j * LANES + i * S5_P + p)[:, None], dtype=bf16)


def _cast_kernel(w_ref, o_ref):
    o_ref[...] = w_ref[...].astype(o_ref.dtype)


def _to_bf16(w):
    n_layers, rows, cols = w.shape
    block_rows = min(rows, pl.next_power_of_2(CAST_BLOCK_ELEMS // cols + 1) // 2)
    spec = pl.BlockSpec((None, block_rows, cols), lambda l, r: (l, r, 0))
    return pl.pallas_call(
        _cast_kernel,
        grid=(n_layers, rows // block_rows),
        in_specs=[spec],
        out_specs=spec,
        out_shape=jax.ShapeDtypeStruct(w.shape, bf16),
        compiler_params=_cparams("parallel", "parallel"),
        name="to_bf16",
    )(w)


def _mod_kernel(c_ref, w_ref, b_ref, o_ref):
    c = c_ref[...]
    s = c * jax.nn.sigmoid(c)
    o_ref[...] = jnp.dot(s.astype(bf16), w_ref[...].astype(bf16), preferred_element_type=f32) + b_ref[...]


def _modulation(c_rows, mod_w, mod_b):
    tn = 1024
    n = MOD_CHUNKS * D_MODEL
    return pl.pallas_call(
        _mod_kernel,
        grid=(DEPTH, n // tn),
        in_specs=[pl.BlockSpec((MOD_ROWS, D_MODEL), lambda l, j: (0, 0)),
                  pl.BlockSpec((None, D_MODEL, tn), lambda l, j: (l, 0, j)),
                  pl.BlockSpec((None, 1, tn), lambda l, j: (l, 0, j))],
        out_specs=pl.BlockSpec((None, MOD_ROWS, tn), lambda l, j: (l, 0, j)),
        out_shape=jax.ShapeDtypeStruct((DEPTH, MOD_ROWS, n), f32),
        compiler_params=_cparams("parallel", "parallel"),
        name="ada_modulation",
    )(c_rows, mod_w, mod_b.reshape(DEPTH, 1, n))


def _even_u_kernel(x_ref, c_ref, g_ref, sh_ref, sc_ref, w_ref, perm_ref, h_ref, u_ref, rows_ref, ub_ref,
                   a_ref, xs_ref, st_ref):
    i = pl.program_id(0)

    @pl.when(i < CTX_TILES)
    def _():
        _norm_mod_rows(c_ref, g_ref, sc_ref, sh_ref, h_ref, copy_ref=rows_ref)

    @pl.when(i >= CTX_TILES)
    def _():
        _norm_mod_rows(x_ref, g_ref, sc_ref, sh_ref, h_ref, copy_ref=rows_ref)

    acc = jnp.dot(h_ref[...].reshape(BATCH * TT, D_MODEL), w_ref[...], preferred_element_type=f32)
    u_ref[...] = acc.reshape(BATCH, TT, S5_WIDTH).astype(u_ref.dtype)

    n_slab, n_half = S5_WIDTH // LANES, S5_TP // LANES
    for s in range(n_slab):
        a_ref[s] = acc[:, s * LANES:(s + 1) * LANES]
    for s in range(n_slab):
        for half in range(n_half):
            for b in range(BATCH):
                r0 = ((s * n_half + half) * BATCH + b) * S5_TILE_BLOCKS
                for tt in range(S5_LANE_GROUPS):
                    xs_ref[tt, r0:r0 + S5_TILE_BLOCKS, :] = a_ref[
                        s, pl.ds(b * TT + half * S5_LANE_GROUPS + tt, S5_TILE_BLOCKS, stride=S5_T), :]
    xs = jnp.concatenate([xs_ref[tt] for tt in range(S5_LANE_GROUPS)], axis=-1).astype(bf16)
    ys = jnp.dot(xs, perm_ref[...], preferred_element_type=f32)
    @pl.when(i == 0)
    def _():
        st_ref[...] = jnp.zeros_like(st_ref)
    for s in range(n_slab):
        for half in range(n_half):
            for b in range(BATCH):
                r0 = ((s * n_half + half) * BATCH + b) * S5_TILE_BLOCKS
                for gg in range(S5_LANE_GROUPS):
                    st_ref[s * S5_LANE_GROUPS + gg, half, pl.ds(b, S5_TILE_BLOCKS, stride=S5_BPAD), :] = (
                        ys[r0:r0 + S5_TILE_BLOCKS, gg * LANES:(gg + 1) * LANES])
    for g in range(S5_G):
        ub_ref[g] = jnp.concatenate([st_ref[g, 0], st_ref[g, 1]], axis=-1).astype(ub_ref.dtype)


def _even_u_proj(x, ctx, gains, mods, w_in, perm):
    blk_rows = S5_TILE_BLOCKS * S5_BPAD
    tile = lambda width: pl.BlockSpec((BATCH, TT, width), lambda i: (0, i, 0))
    return pl.pallas_call(
        _even_u_kernel,
        grid=(N_TILES,),
        in_specs=[pl.BlockSpec((BATCH, TT, D_MODEL), lambda i: (0, jnp.maximum(i - CTX_TILES, 0), 0)),
                  pl.BlockSpec((BATCH, TT, D_MODEL), lambda i: (0, jnp.minimum(i, CTX_TILES - 1), 0)),
                  _gain_spec(0), _mod_spec(0, CTX_TILES), _mod_spec(1, CTX_TILES),
                  pl.BlockSpec((D_MODEL, S5_WIDTH), lambda i: (0, 0), pipeline_mode=pl.Buffered(1)),
                  _const_spec(perm.shape)],
        out_specs=[tile(D_MODEL), tile(S5_WIDTH), tile(D_MODEL),
                   pl.BlockSpec((S5_G, blk_rows, S5_TP), lambda i: (0, i, 0))],
        out_shape=[jax.ShapeDtypeStruct((BATCH, L_ALL, D_MODEL), bf16),
                   jax.ShapeDtypeStruct((BATCH, L_ALL, S5_WIDTH), bf16),
                   jax.ShapeDtypeStruct((BATCH, L_ALL, D_MODEL), f32),
                   jax.ShapeDtypeStruct((S5_G, S5_NC * S5_BPAD, S5_TP), bf16)],
        scratch_shapes=[pltpu.VMEM((S5_WIDTH // LANES, BATCH * TT, LANES), f32),
                        pltpu.VMEM((S5_LANE_GROUPS, BATCH * TT, LANES), f32),
                        pltpu.VMEM((S5_G, S5_TP // LANES, blk_rows, LANES), f32)],
        compiler_params=_cparams("arbitrary"),
        name="even_u_proj",
    )(x, ctx, gains, mods, mods, w_in, perm)


EVEN_REST_TILES = (EVEN_IN - S5_WIDTH) // S5_WIDTH


def _cast_rider(ws, layer, n_blocks, block_of):
    in_specs, out_specs, out_shapes = [], [], []
    for w in ws:
        _, rows, cols = w.shape
        in_specs.append(pl.BlockSpec((None, rows // n_blocks, cols), lambda *g: (layer, block_of(*g), 0)))
        out_specs.append(pl.BlockSpec((rows // n_blocks, cols), lambda *g: (block_of(*g), 0)))
        out_shapes.append(jax.ShapeDtypeStruct((rows, cols), bf16))
    return in_specs, out_specs, out_shapes


def _even_rest_kernel(h_ref, *refs):
    w_refs, refs = refs[:EVEN_REST_TILES], refs[EVEN_REST_TILES:]
    n_riders = len(refs) // 2
    rider_in, o_ref, rider_out = refs[:n_riders], refs[n_riders], refs[n_riders + 1:]
    h = h_ref[...].reshape(BATCH * TT, D_MODEL)
    for jt, w_ref in enumerate(w_refs):
        acc = jnp.dot(h, w_ref[...], preferred_element_type=f32)
        o_ref[:, :, jt * S5_WIDTH:(jt + 1) * S5_WIDTH] = acc.reshape(BATCH, TT, S5_WIDTH).astype(o_ref.dtype)
    for src_ref, dst_ref in zip(rider_in, rider_out):
        dst_ref[...] = src_ref[...].astype(dst_ref.dtype)


def _even_rest_proj(h, w_in, layer0_weights, shared_weights):
    width = EVEN_IN - S5_WIDTH
    w_specs = [pl.BlockSpec((D_MODEL, S5_WIDTH), lambda i, jt=jt: (0, 1 + jt), pipeline_mode=pl.Buffered(1))
               for jt in range(EVEN_REST_TILES)]
    n_blocks = 32
    riders = tuple(layer0_weights) + tuple(shared_weights)
    c_in, c_out, c_shapes = _cast_rider(riders, 0, n_blocks, lambda i: jnp.minimum(i, n_blocks - 1))
    outs = pl.pallas_call(
        _even_rest_kernel,
        grid=(N_TILES,),
        in_specs=[pl.BlockSpec((BATCH, TT, D_MODEL), lambda i: (0, i, 0))] + w_specs + c_in,
        out_specs=[pl.BlockSpec((BATCH, TT, width), lambda i: (0, i, 0))] + c_out,
        out_shape=[jax.ShapeDtypeStruct((BATCH, L_ALL, width), bf16)] + c_shapes,
        compiler_params=_cparams("arbitrary"),
        name="even_rest_proj",
    )(h, *([w_in] * EVEN_REST_TILES), *riders)
    return outs[0], outs[1:]


def _s5_tables(lam_re, lam_im, log_dt, b_re, b_im, c_re, c_im):
    lr = jnp.minimum(lam_re.astype(f32), LAMBDA_RE_MAX)
    li = lam_im.astype(f32)
    dt = jnp.exp(log_dt.astype(f32))[..., None]
    zr, zi = lr * dt, li * dt
    ab_mag = jnp.exp(zr)
    ab_re, ab_im = ab_mag * jnp.cos(zi), ab_mag * jnp.sin(zi)
    den = lr * lr + li * li
    nr = ab_re - 1.0
    f_re = (nr * lr + ab_im * li) / den
    f_im = (ab_im * lr - nr * li) / den
    br, bi = b_re.astype(f32), b_im.astype(f32)
    bb_re = f_re[..., None] * br - f_im[..., None] * bi
    bb_im = f_re[..., None] * bi + f_im[..., None] * br
    cr, ci = c_re.astype(f32), c_im.astype(f32)
    bt_re, bt_im = bb_re.transpose(0, 1, 3, 2), bb_im.transpose(0, 1, 3, 2)
    asc = np.arange(S5_T, dtype=np.float32)
    desc = asc[::-1].copy()

    def powers(d, ks):
        k = jnp.asarray(ks, dtype=f32)[:, None]
        mag = jnp.exp(zr[d][:, None, :] * k)
        return mag * jnp.cos(zi[d][:, None, :] * k), mag * jnp.sin(zi[d][:, None, :] * k)

    def both(fwd, bwd):
        return jnp.concatenate([fwd, bwd], axis=-1)

    def c_times_powers(d, ks):
        pr, pi = powers(d, ks)
        ca_re = cr[d][:, None] * pr[:, :, None, :] - ci[d][:, None] * pi[:, :, None, :]
        ca_im = cr[d][:, None] * pi[:, :, None, :] + ci[d][:, None] * pr[:, :, None, :]
        return both(ca_re, ca_im).reshape(S5_G, S5_TP, 2 * S5_N)

    bcat = both(bt_re, -bt_im)
    w_intra = _s5_intra_matrix(bcat[0], bcat[1], c_times_powers(0, asc), c_times_powers(1, desc))

    (pf_re, pf_im), (pb_re, pb_im) = powers(0, desc), powers(1, asc)
    p_re, p_im = both(pf_re, pb_re)[:, :, None, :], both(pf_im, pb_im)[:, :, None, :]
    b_re2, b_im2 = both(bt_re[0], bt_re[1])[:, None], both(bt_im[0], bt_im[1])[:, None]
    w_state = jnp.stack([(p_re * b_re2 - p_im * b_im2).reshape(S5_G, S5_TP, 2 * S5_N),
                         (p_re * b_im2 + p_im * b_re2).reshape(S5_G, S5_TP, 2 * S5_N)])

    def rows_fb(fwd, bwd):
        return jnp.concatenate([fwd.transpose(0, 2, 1), bwd.transpose(0, 2, 1)], axis=1)

    (qf_re, qf_im), (qb_re, qb_im) = powers(0, asc + 1.0), powers(1, desc + 1.0)
    q_re, q_im = rows_fb(qf_re, qb_re)[..., None], rows_fb(qf_im, qb_im)[..., None]
    c_re2, c_im2 = rows_fb(cr[0], cr[1])[:, :, None, :], rows_fb(ci[0], ci[1])[:, :, None, :]
    w_cross = jnp.stack([(c_re2 * q_re - c_im2 * q_im).reshape(S5_G, 2 * S5_N, S5_TP),
                         -(c_re2 * q_im + c_im2 * q_re).reshape(S5_G, 2 * S5_N, S5_TP)])

    (af_re, af_im), (ab_re_t, ab_im_t) = powers(0, [float(S5_T)]), powers(1, [float(S5_T)])
    a_re, a_im = both(af_re, ab_re_t), both(af_im, ab_im_t)
    return w_intra, w_state.astype(bf16), w_cross.astype(bf16), a_re, a_im


def _s5_intra_kernel(bf_ref, bb_ref, caf_ref, cab_ref, w_ref):
    lane = lax.broadcasted_iota(jnp.int32, (S5_P, S5_TP), 1)
    dn = (((1,), (1,)), ((), ()))
    for g in range(S5_GB_TABLES):
        tq_f = lax.dot_general(bf_ref[g], caf_ref[g], dn, precision=lax.Precision.HIGHEST, preferred_element_type=f32)
        tq_b = lax.dot_general(bb_ref[g], cab_ref[g], dn, precision=lax.Precision.HIGHEST, preferred_element_type=f32)
        for s in range(S5_T):
            sf, sb = S5_P * s, S5_P * (S5_T - 1 - s)
            row_f = tq_f if s == 0 else jnp.where(lane >= sf, pltpu.roll(tq_f, sf, 1), 0.0)
            row_b = tq_b if sb == 0 else jnp.where(lane < S5_TP - sb, pltpu.roll(tq_b, S5_TP - sb, 1), 0.0)
            w_ref[g, s * S5_P:(s + 1) * S5_P, :] = (row_f + row_b).astype(w_ref.dtype)


def _s5_intra_matrix(b_f, b_b, ca_f, ca_b):
    def spec(rows, cols):
        return pl.BlockSpec((S5_GB_TABLES, rows, cols), lambda i: (i, 0, 0))

    return pl.pallas_call(
        _s5_intra_kernel,
        grid=(S5_G // S5_GB_TABLES,),
        in_specs=[spec(S5_P, 2 * S5_N), spec(S5_P, 2 * S5_N), spec(S5_TP, 2 * S5_N), spec(S5_TP, 2 * S5_N)],
        out_specs=spec(S5_TP, S5_TP),
        out_shape=jax.ShapeDtypeStruct((S5_G, S5_TP, S5_TP), bf16),
        compiler_params=_cparams("parallel"),
        name="s5_intra_matrix",
    )(b_f, b_b, ca_f, ca_b)


def _s5_kernel(u_ref, wi_ref, ws_ref, wc_ref, are_ref, aim_ref, y_ref, s_ref):
    half = 2 * S5_N
    for g in range(S5_GB):
        w_state = jnp.concatenate([ws_ref[0, g], ws_ref[1, g]], axis=-1)
        s_ref[g] = jnp.dot(u_ref[g], w_state, preferred_element_type=f32)

    fwd_lane = lax.broadcasted_iota(jnp.int32, (S5_BPAD, half), 1) < S5_N
    a_re = [jnp.broadcast_to(are_ref[g], (S5_BPAD, half)) for g in range(S5_GB)]
    a_im = [jnp.broadcast_to(aim_ref[g], (S5_BPAD, half)) for g in range(S5_GB)]

    def step(i, carry):
        cb = jnp.where(i < S5_NC_CTX, S5_NC_CTX - 1 - i, S5_NC + S5_NC_CTX - 1 - i)
        rf = pl.multiple_of(i * S5_BPAD, S5_BPAD)
        rb = pl.multiple_of(cb * S5_BPAD, S5_BPAD)
        out = []
        for g in range(S5_GB):
            h_re, h_im = carry[2 * g], carry[2 * g + 1]
            l_re = jnp.where(fwd_lane, s_ref[g, pl.ds(rf, S5_BPAD), 0:half], s_ref[g, pl.ds(rb, S5_BPAD), 0:half])
            l_im = jnp.where(fwd_lane, s_ref[g, pl.ds(rf, S5_BPAD), half:2 * half],
                             s_ref[g, pl.ds(rb, S5_BPAD), half:2 * half])
            s_ref[g, pl.ds(rf, S5_BPAD), 0:S5_N] = h_re[:, 0:S5_N]
            s_ref[g, pl.ds(rb, S5_BPAD), S5_N:half] = h_re[:, S5_N:half]
            s_ref[g, pl.ds(rf, S5_BPAD), half:half + S5_N] = h_im[:, 0:S5_N]
            s_ref[g, pl.ds(rb, S5_BPAD), half + S5_N:2 * half] = h_im[:, S5_N:half]
            out.append(a_re[g] * h_re - a_im[g] * h_im + l_re)
            out.append(a_re[g] * h_im + a_im[g] * h_re + l_im)
        return tuple(out)

    zero = jnp.zeros((S5_BPAD, half), f32)
    lax.fori_loop(0, S5_NC, step, (zero,) * (2 * S5_GB), unroll=S5_SCAN_UNROLL)

    for g in range(S5_GB):
        y = jnp.dot(u_ref[g], wi_ref[g], preferred_element_type=f32)
        w_cross = jnp.concatenate([wc_ref[0, g], wc_ref[1, g]], axis=0)
        y = y + jnp.dot(s_ref[g].astype(bf16), w_cross, preferred_element_type=f32)
        for hf in range(S5_TP // LANES):
            y_ref[g, hf] = y[:, hf * LANES:(hf + 1) * LANES]


def _s5_mix(u_blocks, tables):
    w_intra, w_state, w_cross, a_re, a_im = tables
    rows = S5_NC * S5_BPAD
    wsp = pl.BlockSpec((S5_GB, S5_TP, S5_TP), lambda g: (g, 0, 0))
    w_state_spec = pl.BlockSpec((2, S5_GB, S5_TP, 2 * S5_N), lambda g: (0, g, 0, 0))
    w_cross_spec = pl.BlockSpec((2, S5_GB, 2 * S5_N, S5_TP), lambda g: (0, g, 0, 0))
    asp = pl.BlockSpec((S5_GB, 1, 2 * S5_N), lambda g: (g, 0, 0))
    return pl.pallas_call(
        _s5_kernel,
        grid=(S5_G // S5_GB,),
        in_specs=[pl.BlockSpec((S5_GB, rows, S5_TP), lambda g: (g, 0, 0)), wsp, w_state_spec, w_cross_spec, asp, asp],
        out_specs=pl.BlockSpec((S5_GB, S5_TP // LANES, rows, LANES), lambda g: (g, 0, 0, 0)),
        out_shape=jax.ShapeDtypeStruct((S5_G, S5_TP // LANES, rows, LANES), f32),
        scratch_shapes=[pltpu.VMEM((S5_GB, rows, S5_TP), f32)],
        compiler_params=_cparams("parallel"),
        name="s5_mix",
    )(u_blocks, w_intra, w_state, w_cross, a_re, a_im)


def _ret_kernel(lg_ref, cd_ref, qf_ref, kf_ref, vf_ref, qb_ref, kb_ref, vb_ref, of_ref, ob_ref,
                r_ref, dec_ref, qdec_ref, kdec_ref):
    t = RET_T
    k_scale = RET_DK ** -0.5

    @pl.when(pl.program_id(1) == 0)
    def _():
        r_ref[...] = jnp.zeros_like(r_ref)
        n = lax.broadcasted_iota(jnp.int32, (t, t), 0)
        m = lax.broadcasted_iota(jnp.int32, (t, t), 1)
        pos = lax.broadcasted_iota(jnp.int32, (t, 1), 0)
        for d in range(2):
            diff = (n - m if d == 0 else m - n).astype(f32)
            p = (pos if d == 0 else t - 1 - pos).astype(f32)
            for h in range(RET_H):
                c = d * RET_H + h
                lg = lg_ref[c]
                dec_ref[c] = jnp.where(diff >= 0, jnp.exp(jnp.maximum(diff, 0.0) * lg), 0.0) * k_scale
                qdec_ref[c] = jnp.broadcast_to(jnp.exp((p + 1.0) * lg), (t, RET_DV))
                kdec_ref[c] = jnp.broadcast_to(jnp.exp((t - 1.0 - p) * lg) * k_scale, (t, RET_DK))

    for bi in range(RET_BB):
        for d, (q_ref, k_ref, v_ref, o_ref) in enumerate(((qf_ref, kf_ref, vf_ref, of_ref),
                                                            (qb_ref, kb_ref, vb_ref, ob_ref))):
            for h in range(RET_H):
                c = d * RET_H + h
                rc = bi * RET_CHAINS + c
                cols = slice(h * RET_DK, (h + 1) * RET_DK)
                q = q_ref[bi, :, cols]
                k = k_ref[bi, :, cols]
                v = v_ref[bi, :, cols]
                r = r_ref[rc]
                s = lax.dot_general(q, k, (((1,), (1,)), ((), ())), preferred_element_type=f32) * dec_ref[c]
                inner = jnp.dot(s.astype(bf16), v, preferred_element_type=f32)
                cross = jnp.dot(q, r.astype(bf16), preferred_element_type=f32) * qdec_ref[c]
                o_ref[bi, :, cols] = (inner + cross).astype(o_ref.dtype)
                kd = (k.astype(f32) * kdec_ref[c]).astype(bf16)
                r_ref[rc] = r * cd_ref[c] + lax.dot_general(kd, v, (((0,), (0,)), ((), ())),
                                                            preferred_element_type=f32)


def _retention(p):
    e = RET_DECAY_BASE - (2.0 * np.arange(RET_H, dtype=np.float64)[None, :] + np.arange(2, dtype=np.float64)[:, None])
    log_g64 = np.log1p(-np.exp2(e)).reshape(RET_CHAINS)
    log_g = jnp.asarray(log_g64, dtype=f32)
    chunk_dec = jnp.asarray(np.exp(RET_T * log_g64), dtype=f32)
    width = RET_H * RET_DK
    col0 = 0

    def bwd_chunk(i):
        return jnp.where(i == 0, 0, RET_STEPS - i)

    def fwd(off):
        return pl.BlockSpec((RET_BB, RET_T, width), lambda b, i: (b, i, col0 + off))

    def bwd(off):
        return pl.BlockSpec((RET_BB, RET_T, width), lambda b, i: (b, bwd_chunk(i), col0 + off))

    smem = pl.BlockSpec(memory_space=pltpu.SMEM)
    out = jax.ShapeDtypeStruct((BATCH, L_ALL, RET_H * RET_DV), bf16)
    return pl.pallas_call(
        _ret_kernel,
        grid=(BATCH // RET_BB, RET_STEPS),
        in_specs=[smem, smem, fwd(0), fwd(1), fwd(2), bwd(0), bwd(1), bwd(2)],
        out_specs=[pl.BlockSpec((RET_BB, RET_T, RET_H * RET_DV), lambda b, i: (b, i, 0)),
                   pl.BlockSpec((RET_BB, RET_T, RET_H * RET_DV), lambda b, i: (b, bwd_chunk(i), 0))],
        out_shape=[out, out],
        scratch_shapes=[pltpu.VMEM((RET_BB * RET_CHAINS, RET_DK, RET_DV), f32),
                        pltpu.VMEM((RET_CHAINS, RET_T, RET_T), f32),
                        pltpu.VMEM((RET_CHAINS, RET_T, RET_DV), f32),
                        pltpu.VMEM((RET_CHAINS, RET_T, RET_DK), f32)],
        compiler_params=_cparams("parallel", "arbitrary"),
        name="retention",
    )(log_g, chunk_dec, p, p, p, p, p, p)


def _gelu_tanh(x):
    return 0.5 * x * (1.0 + jnp.tanh(math.sqrt(2.0 / math.pi) * (x + 0.044715 * (x * x * x))))


def _even_out_kernel(u_ref, gate_ref, y_ref, of_ref, ob_ref, x_ref, gt_ref, gpost_ref, g2_ref, sh2_ref, sc2_ref,
                     dsk_ref, glub_ref, gluw_ref, wo_ref, perm_ref, o_ref, h_ref, z_ref, xs_ref, yf_ref):
    rows = BATCH * TT
    n_slab, n_half = S5_WIDTH // LANES, S5_TP // LANES
    for s in range(n_slab):
        for half in range(n_half):
            for b in range(BATCH):
                r0 = ((s * n_half + half) * BATCH + b) * S5_TILE_BLOCKS
                for gg in range(S5_LANE_GROUPS):
                    xs_ref[gg, r0:r0 + S5_TILE_BLOCKS, :] = y_ref[
                        s * S5_LANE_GROUPS + gg, half, pl.ds(b, S5_TILE_BLOCKS, stride=S5_BPAD), :]
    xs = jnp.concatenate([xs_ref[gg] for gg in range(S5_LANE_GROUPS)], axis=-1).astype(bf16)
    ys = jnp.dot(xs, perm_ref[...], preferred_element_type=f32)
    for s in range(n_slab):
        for half in range(n_half):
            for b in range(BATCH):
                r0 = ((s * n_half + half) * BATCH + b) * S5_TILE_BLOCKS
                for tt in range(S5_LANE_GROUPS):
                    z_ref[s, pl.ds(b * TT + half * S5_LANE_GROUPS + tt, S5_TILE_BLOCKS, stride=S5_T), :] = (
                        ys[r0:r0 + S5_TILE_BLOCKS, tt * LANES:(tt + 1) * LANES])
    u = u_ref[...].reshape(rows, S5_WIDTH)
    for s in range(S5_WIDTH // LANES):
        cols = slice(s * LANES, (s + 1) * LANES)
        yf_ref[:, cols] = _gelu_tanh(z_ref[s] + dsk_ref[:, cols] * u[:, cols].astype(f32))
    y = yf_ref[...]
    z = jnp.dot(y.astype(bf16), gluw_ref[...], preferred_element_type=f32) + glub_ref[...]
    s5_out = y * jax.nn.sigmoid(z)
    r = (of_ref[...].astype(f32) + ob_ref[...].astype(f32)).reshape(rows, RET_H * RET_DV)
    heads = []
    for h in range(RET_H):
        rh = r[:, h * RET_DV:(h + 1) * RET_DV]
        heads.append(rh * lax.rsqrt(jnp.mean(rh * rh, axis=-1, keepdims=True) + NORM_EPS))
    g = gate_ref[...].reshape(rows, RET_H * RET_DV).astype(f32)
    ret_out = jnp.concatenate(heads, axis=-1) * (g * jax.nn.sigmoid(g))
    out = jnp.dot(s5_out.astype(bf16), wo_ref[0:S5_WIDTH, :], preferred_element_type=f32)
    out = out + jnp.dot(ret_out.astype(bf16), wo_ref[S5_WIDTH:, :], preferred_element_type=f32)
    _mix_epilogue(out.reshape(BATCH, TT, D_MODEL), x_ref, gt_ref, gpost_ref, g2_ref, sh2_ref, sc2_ref, o_ref, h_ref)


def _even_out_proj(u, p, y_blocks, o_f, o_b, rows, mods, gains, d_skip, glu_b, glu_w, w_out, perm):
    half = S5_WIDTH
    blk_rows = S5_TILE_BLOCKS * S5_BPAD
    tile = pl.BlockSpec((BATCH, TT, half), lambda i: (0, i, 0))
    return pl.pallas_call(
        _even_out_kernel,
        grid=(N_TILES,),
        in_specs=[tile,
                  pl.BlockSpec((BATCH, TT, half), lambda i: (0, i, EVEN_REST_TILES - 1)),
                  pl.BlockSpec((S5_G, S5_TP // LANES, blk_rows, LANES), lambda i: (0, 0, i, 0)),
                  tile, tile,
                  pl.BlockSpec((BATCH, TT, D_MODEL), lambda i: (0, i, 0)),
                  _mod_spec(2, CTX_TILES), _gain_spec(1),
                  _gain_spec(2), _mod_spec(3, CTX_TILES), _mod_spec(4, CTX_TILES),
                  _const_spec((1, half)), _const_spec((1, half)),
                  _const_spec((half, half)), _const_spec((D_MODEL, D_MODEL)), _const_spec(perm.shape)],
        out_specs=[pl.BlockSpec((BATCH, TT, D_MODEL), lambda i: (0, i, 0)),
                   pl.BlockSpec((BATCH, TT, D_MODEL), lambda i: (0, i, 0))],
        out_shape=[jax.ShapeDtypeStruct((BATCH, L_ALL, D_MODEL), f32),
                   jax.ShapeDtypeStruct((BATCH, L_ALL, D_MODEL), bf16)],
        scratch_shapes=[pltpu.VMEM((S5_WIDTH // LANES, BATCH * TT, LANES), f32),
                        pltpu.VMEM((S5_LANE_GROUPS, BATCH * TT, LANES), f32),
                        pltpu.VMEM((BATCH * TT, S5_WIDTH), f32)],
        compiler_params=_cparams("parallel"),
        name="even_out_proj",
    )(u, p, y_blocks, o_f, o_b, rows, mods, gains, gains, mods, mods, d_skip, glu_b, glu_w, w_out, perm)


def _mlp_kernel(x_ref, h_ref, gt_ref, gpost_ref, w1_ref, w2_ref, o_ref, acc_ref):
    f = pl.program_id(1)
    last = pl.num_programs(1) - 1

    def partial_out():
        h = h_ref[...].reshape(BATCH * TT, D_MODEL)
        a = jnp.maximum(jnp.dot(h, w1_ref[...], preferred_element_type=f32), 0.0)
        return jnp.dot((a * a).astype(bf16), w2_ref[...], preferred_element_type=f32)

    @pl.when(f == 0)
    def _():
        acc_ref[...] = partial_out()

    @pl.when(jnp.logical_and(f > 0, f < last))
    def _():
        acc_ref[...] += partial_out()

    @pl.when(f == last)
    def _():
        m = (acc_ref[...] + partial_out()).reshape(BATCH, TT, D_MODEL)
        o_ref[...] = _gated_residual(m, x_ref[...], gt_ref[...], gpost_ref[...])


def _mlp(x, h, ctx_tiles, mods, gains, w1, w2):
    tf = 1024
    n_tiles = x.shape[1] // TT
    tile = pl.BlockSpec((BATCH, TT, D_MODEL), lambda i, f: (0, i, 0))
    return pl.pallas_call(
        _mlp_kernel,
        grid=(n_tiles, FFN_HIDDEN // tf),
        in_specs=[tile, tile, _mod_spec(5, ctx_tiles), _gain_spec(3),
                  pl.BlockSpec((D_MODEL, tf), lambda i, f: (0, f)),
                  pl.BlockSpec((tf, D_MODEL), lambda i, f: (f, 0))],
        out_specs=tile,
        out_shape=jax.ShapeDtypeStruct((BATCH, n_tiles * TT, D_MODEL), f32),
        scratch_shapes=[pltpu.VMEM((BATCH * TT, D_MODEL), f32)],
        compiler_params=_cparams("parallel", "arbitrary"),
        name="sq_relu_mlp",
    )(x, h, mods, gains, w1, w2)


def _rope_tables():
    rows = SEQ // GRID_W
    row = np.repeat(np.arange(rows, dtype=np.float64), GRID_W)
    col = np.tile(np.arange(GRID_W, dtype=np.float64), rows)
    n_freq = ATT_HD // 4
    inv_freq = ROPE_BASE ** (-np.arange(n_freq, dtype=np.float64) / n_freq)
    ang = np.concatenate([row[:, None] * inv_freq[None], col[:, None] * inv_freq[None]], axis=-1)
    cos, sin = np.cos(ang), np.sin(ang)
    cos, sin = np.tile(cos, (1, 4)), np.concatenate([-sin, sin, -sin, sin], axis=-1)
    return (jnp.asarray(np.stack([cos * Q_SCALE, cos, np.ones_like(cos)]), dtype=f32),
            jnp.asarray(np.stack([sin * Q_SCALE, sin, np.zeros_like(sin)]), dtype=f32))


ROPE_Q, ROPE_K, ROPE_NONE = 0, 1, 2


def _rope(x, cos, sin):
    half = ATT_HD // 2
    lane = lax.broadcasted_iota(jnp.int32, x.shape, x.ndim - 1)
    partner = jnp.where((lane & (ATT_HD - 1)) < half,
                        pltpu.roll(x, LANES - half, x.ndim - 1), pltpu.roll(x, half, x.ndim - 1))
    return x * cos + partner * sin


QKV_TILE = 2 * ATT_KVH * ATT_HD
QKV_TILES = ODD_IN // QKV_TILE


def _odd_in_kernel(x_ref, g_ref, sh_ref, sc_ref, w_ref, cos_q_ref, sin_q_ref, cos_k_ref, sin_k_ref, o_ref, h_ref):
    _norm_mod_rows(x_ref, g_ref, sc_ref, sh_ref, h_ref)
    kw = ATT_KVH * ATT_HD
    for jt in range(QKV_TILES):
        acc = jnp.dot(h_ref[...], w_ref[:, jt * QKV_TILE:(jt + 1) * QKV_TILE], preferred_element_type=f32)
        for c in range(QKV_TILE // LANES):
            cols = slice(c * LANES, (c + 1) * LANES)
            xc = acc[:, cols].reshape(BATCH, TT, LANES)
            if jt < QKV_TILES - 1:
                xc = _rope(xc, cos_q_ref[...], sin_q_ref[...])
            elif c * LANES < kw:
                xc = _rope(xc, cos_k_ref[...], sin_k_ref[...])
            o_ref[jt, :, :, cols] = xc.astype(o_ref.dtype)


def _odd_in_proj(x, gains, mods, w_in, cos, sin):
    def table(kind_of):
        return pl.BlockSpec((None, TT, LANES), lambda i: (kind_of(i), jnp.maximum(i - CTX_TILES, 0), 0))

    q_table = table(lambda i: ROPE_Q)
    k_table = table(lambda i: jnp.where(i >= CTX_TILES, ROPE_K, ROPE_NONE))
    return pl.pallas_call(
        _odd_in_kernel,
        grid=(N_TILES,),
        in_specs=[pl.BlockSpec((BATCH, TT, D_MODEL), lambda i: (0, i, 0)),
                  _gain_spec(0), _mod_spec(0, CTX_TILES), _mod_spec(1, CTX_TILES),
                  _const_spec((D_MODEL, ODD_IN)), q_table, q_table, k_table, k_table],
        out_specs=pl.BlockSpec((QKV_TILES, BATCH, TT, QKV_TILE), lambda i: (0, 0, i, 0)),
        out_shape=jax.ShapeDtypeStruct((QKV_TILES, BATCH, L_ALL, QKV_TILE), bf16),
        scratch_shapes=[pltpu.VMEM((BATCH * TT, D_MODEL), bf16)],
        compiler_params=_cparams("parallel"),
        name="odd_in_proj",
    )(x, gains, mods, mods, w_in, cos, sin, cos, sin)


def _attn_kernel(sink_ref, q_ref, kvp_ref, kvc_ref, kvn_ref, kvx_ref, w1f_ref, w2f_ref, o_ref, w1b_ref, w2b_ref):
    w1b_ref[...] = w1f_ref[...].astype(w1b_ref.dtype)
    w2b_ref[...] = w2f_ref[...].astype(w2b_ref.dtype)
    qb = pl.program_id(1)
    t = ATT_BLOCK
    kw = ATT_KVH * ATT_HD
    row = lax.broadcasted_iota(jnp.int32, (t, t), 0)
    col = lax.broadcasted_iota(jnp.int32, (t, t), 1)

    def band_valid(blk):
        off = col + blk * t
        kpos = qb * t - WINDOW + off
        return (jnp.abs(off - WINDOW - row) <= WINDOW) & (kpos >= 0) & (kpos < SEQ)

    n_ctx_chunks = CTX_LEN // LANES
    masks = {n_ctx_chunks: band_valid(0), n_ctx_chunks + 2: band_valid(2)}
    n_chunks = n_ctx_chunks + 3

    for kh in range(ATT_KVH):
        ks = slice(kh * ATT_HD, (kh + 1) * ATT_HD)
        vs = slice(kw + kh * ATT_HD, kw + (kh + 1) * ATT_HD)
        k_all = jnp.concatenate([kvx_ref[:, ks], kvp_ref[:, ks], kvc_ref[:, ks], kvn_ref[:, ks]], axis=0)
        v_all = jnp.concatenate([kvx_ref[:, vs], kvp_ref[:, vs], kvc_ref[:, vs], kvn_ref[:, vs]], axis=0)
        heads = [kh * ATT_GRP + g for g in range(ATT_GRP)]
        per_tile = QKV_TILE // ATT_HD
        q_all = jnp.concatenate([q_ref[h // per_tile, :, (h % per_tile) * ATT_HD:(h % per_tile + 1) * ATT_HD]
                                 for h in heads], axis=0)
        s_all = lax.dot_general(q_all, k_all, (((1,), (1,)), ((), ())), preferred_element_type=f32)
        probs, inv_den = [], []
        for g in range(ATT_GRP):
            sink = sink_ref[kh * ATT_GRP + g] * LOG2_E
            s = s_all[g * t:(g + 1) * t]
            s = jnp.concatenate([jnp.where(masks[c], s[:, c * LANES:(c + 1) * LANES], NEG_INF) if c in masks
                                 else s[:, c * LANES:(c + 1) * LANES] for c in range(n_chunks)], axis=-1)
            m = jnp.maximum(jnp.max(s, axis=-1, keepdims=True), sink)
            e = jnp.exp2(s - m)
            inv_den.append(1.0 / (jnp.sum(e, axis=-1, keepdims=True) + jnp.exp2(sink - m)))
            probs.append(e.astype(bf16))
        o_all = jnp.dot(jnp.concatenate(probs, axis=0), v_all, preferred_element_type=f32)
        for g in range(ATT_GRP):
            h = kh * ATT_GRP + g
            o_ref[:, h * ATT_HD:(h + 1) * ATT_HD] = (o_all[g * t:(g + 1) * t] * inv_den[g]).astype(o_ref.dtype)


def _attention(qkv, sink, mlp_w1, mlp_w2):
    t = ATT_BLOCK
    nb = SEQ // t
    off = CTX_LEN // t
    qw = ATT_H * ATT_HD
    kvw = QKV_TILE
    kv_tile = QKV_TILES - 1

    def kv_spec(rows, row_block):
        return pl.BlockSpec((None, None, rows, kvw), lambda b, i: (kv_tile, b, row_block(i), 0))

    c_in, c_out, c_shapes = _cast_rider((mlp_w1, mlp_w2), DEPTH - 1, BATCH * nb, lambda b, i: b * nb + i)
    return pl.pallas_call(
        _attn_kernel,
        grid=(BATCH, nb),
        in_specs=[pl.BlockSpec(memory_space=pltpu.SMEM),
                  pl.BlockSpec((kv_tile, None, t, QKV_TILE), lambda b, i: (0, b, off + i, 0)),
                  kv_spec(t, lambda i: off + jnp.maximum(i - 1, 0)),
                  kv_spec(t, lambda i: off + i),
                  kv_spec(t, lambda i: off + jnp.minimum(i + 1, nb - 1)),
                  kv_spec(CTX_LEN, lambda i: 0)] + c_in,
        out_specs=[pl.BlockSpec((None, t, qw), lambda b, i: (b, i, 0))] + c_out,
        out_shape=[jax.ShapeDtypeStruct((BATCH, SEQ, qw), bf16)] + c_shapes,
        compiler_params=_cparams("parallel", "parallel"),
        name="window_attention",
    )(sink, qkv, qkv, qkv, qkv, qkv, mlp_w1, mlp_w2)


def _odd_out_kernel(a_ref, x_ref, gt_ref, gpost_ref, g2_ref, sh2_ref, sc2_ref, wo_ref, o_ref, h_ref):
    a = a_ref[...].reshape(BATCH * TT, D_MODEL)
    out = jnp.dot(a, wo_ref[...], preferred_element_type=f32).reshape(BATCH, TT, D_MODEL)
    _mix_epilogue(out, x_ref, gt_ref, gpost_ref, g2_ref, sh2_ref, sc2_ref, o_ref, h_ref)


def _odd_out_proj(a, rows, mods, gains, w_out):
    tile = pl.BlockSpec((BATCH, TT, D_MODEL), lambda i: (0, i, 0))
    return pl.pallas_call(
        _odd_out_kernel,
        grid=(SEQ // TT,),
        in_specs=[tile,
                  pl.BlockSpec((BATCH, TT, D_MODEL), lambda i: (0, i + CTX_TILES, 0)),
                  _mod_spec(2, 0), _gain_spec(1), _gain_spec(2), _mod_spec(3, 0), _mod_spec(4, 0),
                  _const_spec((D_MODEL, D_MODEL))],
        out_specs=[tile, tile],
        out_shape=[jax.ShapeDtypeStruct((BATCH, SEQ, D_MODEL), f32),
                   jax.ShapeDtypeStruct((BATCH, SEQ, D_MODEL), bf16)],
        compiler_params=_cparams("parallel"),
        name="odd_out_proj",
    )(a, rows, mods, gains, gains, mods, mods, w_out)


def _layer_mods(m):
    m = m.reshape(MOD_ROWS, MOD_CHUNKS, D_MODEL)
    lat = m[:BATCH]
    ctx = jnp.broadcast_to(m[CTX_MOD_ROW:CTX_MOD_ROW + 1], lat.shape)
    return jnp.stack([lat, ctx], axis=0).transpose(2, 0, 1, 3)[:, :, :, None, :]


def kernel(x, c, ctx, c_ctx, mod_w, mod_b, norm_g, mlp_w1, mlp_w2, even_w_in, even_w_out, s5_lam_re, s5_lam_im, s5_log_dt, s5_b_re, s5_b_im, s5_c_re, s5_c_im, s5_d, s5_glu_w, s5_glu_b, odd_w_in, odd_w_out, odd_sink):
    c_rows = jnp.concatenate([c, c_ctx[None, :], jnp.zeros((MOD_ROWS - BATCH - 1, D_MODEL), f32)], axis=0)
    mods_all = _modulation(c_rows, mod_w, mod_b)
    gains_all = norm_g.reshape(DEPTH, 4, 1, D_MODEL)

    mods, gains = _layer_mods(mods_all[0]), gains_all[0]
    perm = _chunk_swap_matrix()
    w_in = _to_bf16(even_w_in)[0]
    h, u, rows, u_blocks = _even_u_proj(x, ctx, gains, mods, w_in, perm)
    p, (w1, w2, glu_w, even_out_w, odd_in_w, odd_out_w) = _even_rest_proj(
        h, w_in, (mlp_w1, mlp_w2), (s5_glu_w, even_w_out, odd_w_in, odd_w_out))
    tables = _s5_tables(s5_lam_re[0], s5_lam_im[0], s5_log_dt[0], s5_b_re[0], s5_b_im[0], s5_c_re[0], s5_c_im[0])
    y_blocks = _s5_mix(u_blocks, tables)
    o_f, o_b = _retention(p)
    rows, h = _even_out_proj(u, p, y_blocks, o_f, o_b, rows, mods, gains, s5_d[0].reshape(1, S5_WIDTH),
                             s5_glu_b[0].reshape(1, S5_WIDTH), glu_w, even_out_w, perm)
    rows = _mlp(rows, h, CTX_TILES, mods, gains, w1, w2)

    mods, gains = _layer_mods(mods_all[1]), gains_all[1]
    cos, sin = _rope_tables()
    qkv = _odd_in_proj(rows, gains, mods, odd_in_w, cos, sin)
    a, w1, w2 = _attention(qkv, odd_sink[0].astype(f32), mlp_w1, mlp_w2)
    lat, h = _odd_out_proj(a, rows, mods, gains, odd_out_w)
    return _mlp(lat, h, 0, mods, gains, w1, w2)
```
